```python
import jax, jax.numpy as jnp
from jax import lax
import numpy as np

D_MODEL = 2048
BATCH = 4
SEQ = 4096
DEPTH = 2

N_MIXERS = 2
LRU_WIDTH = D_MODEL
LRU_BLOCKS = 8
LRU_BLOCK_W = LRU_WIDTH // LRU_BLOCKS
CONV_W = 4
LRU_C = 8.0
N_HEADS = 16
HEAD_DIM = D_MODEL // N_HEADS
MOBA_BLOCK = 256
MOBA_TOPK = 3
Q_BLOCK = 128
ROPE_THETA = 10000.0
N_GROUPS = 4
EXPERTS_PER_GROUP = 8
N_EXPERTS = N_GROUPS * EXPERTS_PER_GROUP
EXPERT_TOPK = 2
D_EXPERT = D_MODEL // 2
ROW_BLOCK = 256
NORM_EPS = 1e-6
NEG_INF = -1e30

kernel_name = 'hybrid_rglru_moba_hmoe'


def rmsnorm(x, g):
    xf = x.astype(jnp.float32)
    y = xf * lax.rsqrt(jnp.mean(xf * xf, axis=-1, keepdims=True) + NORM_EPS)
    return (y * g.astype(jnp.float32)).astype(x.dtype)


def rope_tables(seq):
    inv = 1.0 / (ROPE_THETA ** (jnp.arange(0, HEAD_DIM, 2, dtype=jnp.float32) / HEAD_DIM))
    ang = jnp.arange(seq, dtype=jnp.float32)[:, None] * inv[None, :]
    return jnp.cos(ang), jnp.sin(ang)


def apply_rope(x, cos, sin):
    x1, x2 = jnp.split(x.astype(jnp.float32), 2, axis=-1)
    c = cos[None, :, None, :]
    s = sin[None, :, None, :]
    return jnp.concatenate([x1 * c - x2 * s, x2 * c + x1 * s], axis=-1).astype(x.dtype)


def _lin_rec_combine(left, right):
    a1, b1 = left
    a2, b2 = right
    return a1 * a2, a2 * b1 + b2


def rglru_mixer(h, w_in, b_in, conv_w, conv_b, w_r, b_r, w_i, b_i, lam, w_out, b_out):
    bsz, seq, _ = h.shape
    f32 = jnp.float32
    u = h @ w_in + b_in
    xb, yb = jnp.split(u, 2, axis=-1)
    gate = jax.nn.gelu(yb, approximate=True)
    xc = lax.conv_general_dilated(
        xb, conv_w[:, None, :], window_strides=(1,), padding=[(CONV_W - 1, 0)],
        dimension_numbers=('NWC', 'WIO', 'NWC'), feature_group_count=LRU_WIDTH) + conv_b
    xg = xc.reshape(bsz, seq, LRU_BLOCKS, LRU_BLOCK_W)
    r = jax.nn.sigmoid((jnp.einsum('bsgi,gij->bsgj', xg, w_r).reshape(bsz, seq, LRU_WIDTH) + b_r).astype(f32))
    i = jax.nn.sigmoid((jnp.einsum('bsgi,gij->bsgj', xg, w_i).reshape(bsz, seq, LRU_WIDTH) + b_i).astype(f32))
    log_a = -LRU_C * r * jax.nn.softplus(-lam.astype(f32))
    a = jnp.exp(log_a)
    mult = jnp.sqrt(-jnp.expm1(2.0 * log_a))
    b = mult * i * xc.astype(f32)
    _, hs = lax.associative_scan(_lin_rec_combine, (a, b), axis=1)
    return (hs.astype(h.dtype) * gate) @ w_out + b_out


def moba_mixer(h, w_qkv, w_o, cos, sin):
    bsz, seq, _ = h.shape
    f32 = jnp.float32
    qkv = (h @ w_qkv).reshape(bsz, seq, 3, N_HEADS, HEAD_DIM)
    q = apply_rope(qkv[:, :, 0], cos, sin)
    k = apply_rope(qkv[:, :, 1], cos, sin)
    v = qkv[:, :, 2]
    n_blk = -(-seq // MOBA_BLOCK)
    pad = n_blk * MOBA_BLOCK - seq
    k = jnp.pad(k, ((0, 0), (0, pad), (0, 0), (0, 0)))
    v = jnp.pad(v, ((0, 0), (0, pad), (0, 0), (0, 0)))
    kb = k.reshape(bsz, n_blk, MOBA_BLOCK, N_HEADS, HEAD_DIM).transpose(0, 3, 1, 2, 4)
    vb = v.reshape(bsz, n_blk, MOBA_BLOCK, N_HEADS, HEAD_DIM).transpose(0, 3, 1, 2, 4)
    k_mean = jnp.mean(kb.astype(f32), axis=3)
    n_q = seq // Q_BLOCK
    qc = q.reshape(bsz, n_q, Q_BLOCK, N_HEADS, HEAD_DIM).transpose(0, 1, 3, 2, 4)
    qc = qc.reshape(bsz * n_q, N_HEADS, Q_BLOCK, HEAD_DIM)
    flat_ix = jnp.arange(bsz * n_q, dtype=jnp.int32)
    b_idx = flat_ix // n_q
    c_idx = flat_ix % n_q
    n_sel = min(MOBA_TOPK, n_blk)
    scale = HEAD_DIM ** -0.5
    head_ix = jnp.arange(N_HEADS)[:, None]
    blk_ix = jnp.arange(n_blk)
    key_off = jnp.arange(MOBA_BLOCK)

    def attend_chunk(args):
        qq, b, c = args
        kb_b, vb_b, km_b = kb[b], vb[b], k_mean[b]
        q_pos = c * Q_BLOCK + jnp.arange(Q_BLOCK)
        own = (c * Q_BLOCK) // MOBA_BLOCK
        gate = jnp.einsum('hqd,hnd->hqn', qq.astype(f32), km_b)
        gate = jnp.where(blk_ix < own, gate, NEG_INF)
        _, sel = lax.top_k(gate, n_sel)
        valid = sel < own
        k_own = kb_b[:, own]
        v_own = vb_b[:, own]
        s_own = jnp.einsum('hqd,hkd->hqk', qq, k_own, preferred_element_type=f32) * scale
        s_own = jnp.where(own * MOBA_BLOCK + key_off[None, :] <= q_pos[:, None], s_own, NEG_INF)
        scores = [s_own]
        for r in range(n_sel):
            k_r = kb_b[head_ix, sel[..., r]]
            s_r = jnp.einsum('hqd,hqkd->hqk', qq, k_r, preferred_element_type=f32) * scale
            scores.append(jnp.where(valid[..., r:r + 1], s_r, NEG_INF))
        p = jax.nn.softmax(jnp.concatenate(scores, axis=-1), axis=-1).astype(vb_b.dtype)
        o = jnp.einsum('hqk,hkd->hqd', p[..., :MOBA_BLOCK], v_own, preferred_element_type=f32)
        for r in range(n_sel):
            v_r = vb_b[head_ix, sel[..., r]]
            p_r = p[..., (r + 1) * MOBA_BLOCK:(r + 2) * MOBA_BLOCK]
            o = o + jnp.einsum('hqk,hqkd->hqd', p_r, v_r, preferred_element_type=f32)
        return o.astype(qq.dtype)

    o = lax.map(attend_chunk, (qc, b_idx, c_idx))
    o = o.reshape(bsz, n_q, N_HEADS, Q_BLOCK, HEAD_DIM).transpose(0, 1, 3, 2, 4).reshape(bsz, seq, D_MODEL)
    return o @ w_o


def hier_moe(h, w_grp, b_grp, w_exp, b_exp, w_gate, w_up, w_down):
    bsz, seq, dm = h.shape
    f32 = jnp.float32
    n_tok = bsz * seq
    x = h.reshape(n_tok, dm)
    lg = (x @ w_grp).astype(f32) + b_grp
    pg = jax.nn.softmax(lg, axis=-1)
    g = jnp.argmax(lg, axis=-1)
    pg_sel = jnp.take_along_axis(pg, g[:, None], axis=-1)
    le = ((x @ w_exp).astype(f32) + b_exp).reshape(n_tok, N_GROUPS, EXPERTS_PER_GROUP)
    le = jnp.take_along_axis(le, g[:, None, None], axis=1)[:, 0]
    pe = jax.nn.softmax(le, axis=-1)
    top_v, top_i = lax.top_k(pe, EXPERT_TOPK)
    gate_w = pg_sel * top_v / jnp.sum(top_v, axis=-1, keepdims=True)
    eid = (g[:, None] * EXPERTS_PER_GROUP + top_i).astype(jnp.int32)
    n_asg = n_tok * EXPERT_TOPK
    n_rb = -(-n_asg // ROW_BLOCK) + N_EXPERTS
    n_rows = n_rb * ROW_BLOCK
    flat_e = eid.reshape(-1)
    flat_w = gate_w.reshape(-1)
    order = jnp.argsort(flat_e)
    se = flat_e[order]
    counts = jnp.bincount(flat_e, length=N_EXPERTS)
    starts = jnp.cumsum(counts) - counts
    padded = (counts + ROW_BLOCK - 1) // ROW_BLOCK * ROW_BLOCK
    pad_ends = jnp.cumsum(padded)
    pad_starts = pad_ends - padded
    dest = pad_starts[se] + jnp.arange(n_asg, dtype=jnp.int32) - starts[se]
    row_tok = jnp.zeros((n_rows,), jnp.int32).at[dest].set((order // EXPERT_TOPK).astype(jnp.int32))
    row_w = jnp.zeros((n_rows,), f32).at[dest].set(flat_w[order])
    blk_e = jnp.minimum(jnp.searchsorted(pad_ends, jnp.arange(n_rb) * ROW_BLOCK, side='right'), N_EXPERTS - 1)
    xs = x[row_tok].reshape(n_rb, ROW_BLOCK, dm)

    def expert_block(args):
        xblk, e = args
        hid = jax.nn.silu(xblk @ w_gate[e]) * (xblk @ w_up[e])
        return hid @ w_down[e]

    ys = lax.map(expert_block, (xs, blk_e)).reshape(n_rows, dm)
    y = jnp.zeros((n_tok, dm), h.dtype).at[row_tok].add(ys * row_w[:, None].astype(ys.dtype))
    return y.reshape(bsz, seq, dm)


def setup_inputs(seed: int = 0) -> dict:
    key = jax.random.key(seed)
    k = jax.random.split(key, 32)
    f32 = jnp.float32
    n_lru = (DEPTH + N_MIXERS - 1) // N_MIXERS
    n_att = DEPTH // N_MIXERS

    def nrm(kk, shape, scale):
        return jax.random.normal(kk, shape, f32) * scale

    u = jax.random.uniform(k[10], (n_lru, LRU_WIDTH), f32, 0.9, 0.999)
    s = u ** (1.0 / LRU_C)
    lam = jnp.log(s) - jnp.log1p(-s)
    return {
        'x': jax.random.normal(k[0], (BATCH, SEQ, D_MODEL), f32),
        'lru_norm': 1.0 + nrm(k[1], (n_lru, D_MODEL), 0.01),
        'lru_w_in': nrm(k[2], (n_lru, D_MODEL, 2 * LRU_WIDTH), D_MODEL ** -0.5),
        'lru_b_in': nrm(k[3], (n_lru, 2 * LRU_WIDTH), 0.01),
        'lru_conv_w': nrm(k[4], (n_lru, CONV_W, LRU_WIDTH), CONV_W ** -0.5),
        'lru_conv_b': nrm(k[5], (n_lru, LRU_WIDTH), 0.01),
        'lru_w_r': nrm(k[6], (n_lru, LRU_BLOCKS, LRU_BLOCK_W, LRU_BLOCK_W), LRU_BLOCK_W ** -0.5),
        'lru_b_r': nrm(k[7], (n_lru, LRU_WIDTH), 0.01),
        'lru_w_i': nrm(k[8], (n_lru, LRU_BLOCKS, LRU_BLOCK_W, LRU_BLOCK_W), LRU_BLOCK_W ** -0.5),
        'lru_b_i': nrm(k[9], (n_lru, LRU_WIDTH), 0.01),
        'lru_lambda': lam,
        'lru_w_out': nrm(k[11], (n_lru, LRU_WIDTH, D_MODEL), LRU_WIDTH ** -0.5),
        'lru_b_out': nrm(k[12], (n_lru, D_MODEL), 0.01),
        'att_norm': 1.0 + nrm(k[13], (n_att, D_MODEL), 0.01),
        'att_w_qkv': nrm(k[14], (n_att, D_MODEL, 3 * D_MODEL), D_MODEL ** -0.5),
        'att_w_o': nrm(k[15], (n_att, D_MODEL, D_MODEL), D_MODEL ** -0.5),
        'ffn_norm': 1.0 + nrm(k[16], (DEPTH, D_MODEL), 0.01),
        'moe_w_grp': nrm(k[17], (DEPTH, D_MODEL, N_GROUPS), D_MODEL ** -0.5),
        'moe_b_grp': nrm(k[18], (DEPTH, N_GROUPS), 0.01),
        'moe_w_exp': nrm(k[19], (DEPTH, D_MODEL, N_EXPERTS), D_MODEL ** -0.5),
        'moe_b_exp': nrm(k[20], (DEPTH, N_EXPERTS), 0.01),
        'moe_w_gate': nrm(k[21], (DEPTH, N_EXPERTS, D_MODEL, D_EXPERT), D_MODEL ** -0.5),
        'moe_w_up': nrm(k[22], (DEPTH, N_EXPERTS, D_MODEL, D_EXPERT), D_MODEL ** -0.5),
        'moe_w_down': nrm(k[23], (DEPTH, N_EXPERTS, D_EXPERT, D_MODEL), D_EXPERT ** -0.5),
        'final_norm': 1.0 + nrm(k[24], (D_MODEL,), 0.01),
    }


def reference(x, lru_norm, lru_w_in, lru_b_in, lru_conv_w, lru_conv_b, lru_w_r, lru_b_r,
              lru_w_i, lru_b_i, lru_lambda, lru_w_out, lru_b_out, att_norm, att_w_qkv, att_w_o,
              ffn_norm, moe_w_grp, moe_b_grp, moe_w_exp, moe_b_exp, moe_w_gate, moe_w_up,
              moe_w_down, final_norm):
    cos, sin = rope_tables(x.shape[1])
    h = x
    for layer in range(DEPTH):
        j = layer // N_MIXERS
        if layer % N_MIXERS == 0:
            h = h + rglru_mixer(rmsnorm(h, lru_norm[j]), lru_w_in[j], lru_b_in[j], lru_conv_w[j],
                                lru_conv_b[j], lru_w_r[j], lru_b_r[j], lru_w_i[j], lru_b_i[j],
                                lru_lambda[j], lru_w_out[j], lru_b_out[j])
        else:
            h = h + moba_mixer(rmsnorm(h, att_norm[j]), att_w_qkv[j], att_w_o[j], cos, sin)
        h = h + hier_moe(rmsnorm(h, ffn_norm[layer]), moe_w_grp[layer], moe_b_grp[layer],
                         moe_w_exp[layer], moe_b_exp[layer], moe_w_gate[layer], moe_w_up[layer],
                         moe_w_down[layer])
    return rmsnorm(h, final_norm)
```

```python
import functools

import jax
import jax.numpy as jnp
from jax import lax
from jax.experimental import pallas as pl
from jax.experimental.pallas import tpu as pltpu

F32 = jnp.float32
BF16 = jnp.bfloat16

NORM_EPS = 1e-6
NEG_INF = -1e30
LRU_C = 8.0
N_HEADS = 16
MOBA_BLOCK = 256
MOBA_TOPK = 3
ROPE_THETA = 10000.0
EXPERT_TOPK = 2
ROW_BLOCK = 256
LANES = 128
SUBLANES = 8
VMEM_LIMIT_BYTES = 56 * 1024 * 1024


def _tile(n, pref):
    t = min(n, pref)
    while n % t:
        t //= 2
    return t


def _cparams(sem):
    return pltpu.CompilerParams(dimension_semantics=sem, vmem_limit_bytes=VMEM_LIMIT_BYTES)


def _rmsnorm_rows(x, g):
    ms = jnp.mean(x * x, axis=-1, keepdims=True)
    return x * lax.rsqrt(ms + NORM_EPS) * g


def _lru_in_kernel(x_ref, g_ref, w_ref, b_ref, o_ref, xn_ref, *, n_plain):
    j = pl.program_id(1)

    @pl.when(j == 0)
    def _():
        xn_ref[...] = _rmsnorm_rows(x_ref[...], g_ref[...]).astype(BF16)

    acc = jnp.dot(xn_ref[...], w_ref[...], preferred_element_type=F32) + b_ref[...]

    @pl.when(j < n_plain)
    def _():
        o_ref[...] = acc

    @pl.when(j >= n_plain)
    def _():
        o_ref[...] = jax.nn.gelu(acc, approximate=True)


def _lru_in(h, g, w, b):
    t, d = h.shape
    n = w.shape[1]
    tm, tn = _tile(t, 1024), _tile(n // 2, 1024)
    return pl.pallas_call(
        functools.partial(_lru_in_kernel, n_plain=(n // 2) // tn),
        grid=(t // tm, n // tn),
        in_specs=[pl.BlockSpec((tm, d), lambda i, j: (i, 0)),
                  pl.BlockSpec((1, d), lambda i, j: (0, 0)),
                  pl.BlockSpec((d, tn), lambda i, j: (0, j)),
                  pl.BlockSpec((1, tn), lambda i, j: (0, j))],
        out_specs=pl.BlockSpec((tm, tn), lambda i, j: (i, j)),
        out_shape=jax.ShapeDtypeStruct((t, n), F32),
        scratch_shapes=[pltpu.VMEM((tm, d), BF16)],
        compiler_params=_cparams(("parallel", "arbitrary")),
        name="lru_in",
    )(h, g.reshape(1, d), w, b.reshape(1, n))


def _qkv_kernel(x_ref, g_ref, w_ref, cc_ref, ss_ref, o_ref, xn_ref, *, n_rope, heads_per_tile, dh):
    j = pl.program_id(1)

    @pl.when(j == 0)
    def _():
        xn_ref[...] = _rmsnorm_rows(x_ref[...], g_ref[...]).astype(BF16)

    acc = jnp.dot(xn_ref[...], w_ref[...], preferred_element_type=F32)

    @pl.when(j < n_rope)
    def _():
        cc = cc_ref[...]
        ss = ss_ref[...]
        for hh in range(heads_per_tile):
            a = acc[:, hh * dh:(hh + 1) * dh]
            o_ref[:, hh * dh:(hh + 1) * dh] = (a * cc + pltpu.roll(a, dh // 2, 1) * ss).astype(o_ref.dtype)

    @pl.when(j >= n_rope)
    def _():
        o_ref[...] = acc.astype(o_ref.dtype)


def _qkv_rope(h, g, w, cc, ss, seq):
    t, d = h.shape
    n = w.shape[1]
    dh = d // N_HEADS
    tm, tn = _tile(seq, 1024), _tile(d, 1024)
    s_tiles = seq // tm
    return pl.pallas_call(
        functools.partial(_qkv_kernel, n_rope=(2 * d) // tn, heads_per_tile=tn // dh, dh=dh),
        grid=(t // tm, n // tn),
        in_specs=[pl.BlockSpec((tm, d), lambda i, j: (i, 0)),
                  pl.BlockSpec((1, d), lambda i, j: (0, 0)),
                  pl.BlockSpec((d, tn), lambda i, j: (0, j)),
                  pl.BlockSpec((tm, dh), lambda i, j: (i % s_tiles, 0)),
                  pl.BlockSpec((tm, dh), lambda i, j: (i % s_tiles, 0))],
        out_specs=pl.BlockSpec((tm, tn), lambda i, j: (i, j)),
        out_shape=jax.ShapeDtypeStruct((t, n), BF16),
        scratch_shapes=[pltpu.VMEM((tm, d), BF16)],
        compiler_params=_cparams(("parallel", "arbitrary")),
        name="qkv_rope",
    )(h, g.reshape(1, d), w, cc, ss)


def _mm_res_kernel(a_ref, w_ref, *rest, has_bias):
    if has_bias:
        b_ref, r_ref, o_ref = rest
    else:
        r_ref, o_ref = rest
    y = jnp.dot(a_ref[...], w_ref[...], preferred_element_type=F32)
    if has_bias:
        y = y + b_ref[...]
    o_ref[...] = r_ref[...] + y


def _matmul_residual(a, w, b, res):
    t, k = a.shape
    n = w.shape[1]
    tm, tn = _tile(t, 1024), _tile(n, 1024)
    has_bias = b is not None
    in_specs = [pl.BlockSpec((tm, k), lambda i, j: (i, 0)),
                pl.BlockSpec((k, tn), lambda i, j: (0, j))]
    args = [a, w]
    if has_bias:
        in_specs.append(pl.BlockSpec((1, tn), lambda i, j: (0, j)))
        args.append(b.reshape(1, n))
    in_specs.append(pl.BlockSpec((tm, tn), lambda i, j: (i, j)))
    args.append(res)
    return pl.pallas_call(
        functools.partial(_mm_res_kernel, has_bias=has_bias),
        grid=(t // tm, n // tn),
        in_specs=in_specs,
        out_specs=pl.BlockSpec((tm, tn), lambda i, j: (i, j)),
        out_shape=jax.ShapeDtypeStruct((t, n), F32),
        compiler_params=_cparams(("parallel", "parallel")),
        name="matmul_residual",
    )(*args)


def _lru_scan_kernel(xb_ref, gate_ref, cw_ref, cb_ref, wr_ref, br_ref, wi_ref, bi_ref, lam_ref,
                     o_ref, ext_ref, a_ref, b_ref, h_ref, *, ts, n_grp, gw, conv_w):
    s = pl.program_id(1)
    w = xb_ref.shape[1]

    @pl.when(s == 0)
    def _():
        ext_ref[0:SUBLANES, :] = jnp.zeros((SUBLANES, w), F32)
        h_ref[...] = jnp.zeros_like(h_ref)

    ext_ref[SUBLANES:SUBLANES + ts, :] = xb_ref[...]
    xc = cb_ref[...] + cw_ref[conv_w - 1:conv_w, :] * xb_ref[...]
    for k in range(1, conv_w):
        xc = xc + cw_ref[conv_w - 1 - k:conv_w - k, :] * ext_ref[SUBLANES - k:SUBLANES - k + ts, :]
    ext_ref[0:SUBLANES, :] = ext_ref[ts:ts + SUBLANES, :]

    z = -lam_ref[...]
    softplus = jnp.maximum(z, 0.0) + jnp.log1p(jnp.exp(-jnp.abs(z)))
    c = -LRU_C * softplus
    for g in range(n_grp):
        sl = slice(g * gw, (g + 1) * gw)
        xg = xc[:, sl]
        xg16 = xg.astype(BF16)
        r = jax.nn.sigmoid(jnp.dot(xg16, wr_ref[g], preferred_element_type=F32) + br_ref[:, sl])
        ig = jax.nn.sigmoid(jnp.dot(xg16, wi_ref[g], preferred_element_type=F32) + bi_ref[:, sl])
        log_a = c[:, sl] * r
        a_ref[:, sl] = jnp.exp(log_a)
        th = jnp.tanh(log_a)
        b_ref[:, sl] = jnp.sqrt(-2.0 * th / (1.0 - th)) * ig * xg

    rows = lax.broadcasted_iota(jnp.int32, (SUBLANES, w), 0)

    def chunk(ci, h):
        r0 = pl.multiple_of(ci * SUBLANES, SUBLANES)
        a = a_ref[pl.ds(r0, SUBLANES), :]
        b = b_ref[pl.ds(r0, SUBLANES), :]
        for sh in (1, 2, 4):
            keep = rows >= sh
            a_sh = jnp.where(keep, pltpu.roll(a, sh, 0), 1.0)
            b_sh = jnp.where(keep, pltpu.roll(b, sh, 0), 0.0)
            b = a * b_sh + b
            a = a * a_sh
        hc = a * h + b
        b_ref[pl.ds(r0, SUBLANES), :] = hc * gate_ref[pl.ds(r0, SUBLANES), :]
        return jnp.broadcast_to(hc[SUBLANES - 1:SUBLANES, :], (SUBLANES, w))

    h_ref[...] = lax.fori_loop(0, ts // SUBLANES, chunk, h_ref[...])
    o_ref[...] = b_ref[...].astype(o_ref.dtype)


def _lru_scan(u, conv_w, conv_b, w_r, b_r, w_i, b_i, lam, bsz, seq):
    t = u.shape[0]
    w = u.shape[1] // 2
    n_grp, gw = w_r.shape[0], w_r.shape[1]
    cw = conv_w.shape[0]
    ts = _tile(seq, 256)
    s_tiles = seq // ts
    row = lambda b, s: (b * s_tiles + s, 0)
    vec = pl.BlockSpec((1, w), lambda b, s: (0, 0))
    return pl.pallas_call(
        functools.partial(_lru_scan_kernel, ts=ts, n_grp=n_grp, gw=gw, conv_w=cw),
        grid=(bsz, s_tiles),
        in_specs=[pl.BlockSpec((ts, w), row),
                  pl.BlockSpec((ts, w), lambda b, s: (b * s_tiles + s, 1)),
                  pl.BlockSpec((cw, w), lambda b, s: (0, 0)),
                  vec,
                  pl.BlockSpec((n_grp, gw, gw), lambda b, s: (0, 0, 0)),
                  vec,
                  pl.BlockSpec((n_grp, gw, gw), lambda b, s: (0, 0, 0)),
                  vec,
                  vec],
        out_specs=pl.BlockSpec((ts, w), row),
        out_shape=jax.ShapeDtypeStruct((t, w), BF16),
        scratch_shapes=[pltpu.VMEM((ts + SUBLANES, w), F32),
                        pltpu.VMEM((ts, w), F32),
                        pltpu.VMEM((ts, w), F32),
                        pltpu.VMEM((SUBLANES, w), F32)],
        compiler_params=_cparams(("parallel", "arbitrary")),
        name="lru_scan",
    )(u, u, conv_w, conv_b.reshape(1, w), w_r, b_r.reshape(1, w), w_i, b_i.reshape(1, w),
      lam.reshape(1, w))


def _attn_kernel(q_ref, k_ref, v_ref, o_ref, kaug_ref, *, nb, blk, dh, topk):
    seq = nb * blk
    first = (pl.program_id(0) == 0) & (pl.program_id(1) == 0)

    @pl.when(first)
    def _():
        row = lax.broadcasted_iota(jnp.int32, (seq, LANES), 0)
        lane = lax.broadcasted_iota(jnp.int32, (seq, LANES), 1)
        kaug_ref[:, dh:dh + LANES] = (lane == row // blk).astype(BF16)

    kaug_ref[:, 0:dh] = k_ref[...]
    kmean = jnp.sum(k_ref[...].astype(F32).reshape(nb, blk, dh), axis=1) * (1.0 / blk)
    kmean = jnp.concatenate([kmean, jnp.zeros((LANES - nb, dh), F32)], axis=0).astype(BF16)

    scale = dh ** -0.5
    nt = (((1,), (1,)), ((), ()))
    lane = lax.broadcasted_iota(jnp.int32, (blk, LANES), 1)
    row_k = lax.broadcasted_iota(jnp.int32, (blk, blk), 0)
    col_k = lax.broadcasted_iota(jnp.int32, (blk, blk), 1)

    def qtile(i, carry):
        r0 = pl.multiple_of(i * blk, blk)
        q = q_ref[pl.ds(r0, blk), :]
        gate = lax.dot_general(q, kmean, nt, preferred_element_type=F32)
        g = jnp.where(lane < i, gate, NEG_INF)
        sel = lane == i
        for _ in range(topk):
            m = jnp.max(g, axis=1, keepdims=True)
            first_max = jnp.min(jnp.where(g == m, lane, LANES), axis=1, keepdims=True)
            hit = lane == first_max
            sel = sel | (hit & (lane < i))
            g = jnp.where(hit, -jnp.inf, g)
        bias = jnp.where(sel, 0.0, NEG_INF).astype(BF16)
        q_aug = jnp.concatenate([q, bias], axis=1)

        s = lax.dot_general(q_aug, kaug_ref[pl.ds(r0, blk), :], nt, preferred_element_type=F32) * scale
        s = jnp.where(col_k <= row_k, s, NEG_INF)
        m = jnp.max(s, axis=1, keepdims=True)
        p = jnp.exp(s - m)
        l = jnp.sum(p, axis=1, keepdims=True)
        acc = jnp.dot(p.astype(BF16), v_ref[pl.ds(r0, blk), :], preferred_element_type=F32)

        def kv(j, mla):
            m, l, acc = mla
            c0 = pl.multiple_of(j * blk, blk)
            s = lax.dot_general(q_aug, kaug_ref[pl.ds(c0, blk), :], nt, preferred_element_type=F32) * scale
            m_new = jnp.maximum(m, jnp.max(s, axis=1, keepdims=True))
            alpha = jnp.exp(m - m_new)
            p = jnp.exp(s - m_new)
            l = alpha * l + jnp.sum(p, axis=1, keepdims=True)
            acc = alpha * acc + jnp.dot(p.astype(BF16), v_ref[pl.ds(c0, blk), :], preferred_element_type=F32)
            return m_new, l, acc

        m, l, acc = lax.fori_loop(0, i, kv, (m, l, acc))
        o_ref[pl.ds(r0, blk), :] = (acc / l).astype(o_ref.dtype)
        return carry

    lax.fori_loop(0, nb, qtile, 0)


def _moba_attention(qkv, bsz, seq, d):
    t = qkv.shape[0]
    dh = d // N_HEADS
    blk = MOBA_BLOCK
    nb = seq // blk
    assert seq % blk == 0 and nb <= LANES and dh == LANES
    return pl.pallas_call(
        functools.partial(_attn_kernel, nb=nb, blk=blk, dh=dh, topk=min(MOBA_TOPK, nb)),
        grid=(bsz, N_HEADS),
        in_specs=[pl.BlockSpec((seq, dh), lambda b, h: (b, h)),
                  pl.BlockSpec((seq, dh), lambda b, h: (b, N_HEADS + h)),
                  pl.BlockSpec((seq, dh), lambda b, h: (b, 2 * N_HEADS + h))],
        out_specs=pl.BlockSpec((seq, dh), lambda b, h: (b, h)),
        out_shape=jax.ShapeDtypeStruct((t, d), BF16),
        scratch_shapes=[pltpu.VMEM((seq, dh + LANES), BF16)],
        compiler_params=_cparams(("arbitrary", "arbitrary")),
        name="moba_attention",
    )(qkv, qkv, qkv)


def _router_kernel(h_ref, g_ref, w_ref, b_ref, xn_ref, ri_ref, rf_ref, *, n_grp, epg):
    xn = _rmsnorm_rows(h_ref[...], g_ref[...])
    xn_ref[...] = xn
    logits = jnp.dot(xn.astype(BF16), w_ref[...], preferred_element_type=F32) + b_ref[...]
    lane = lax.broadcasted_iota(jnp.int32, logits.shape, 1)

    lg = jnp.where(lane < n_grp, logits, -jnp.inf)
    mg = jnp.max(lg, axis=1, keepdims=True)
    gidx = jnp.min(jnp.where(lg == mg, lane, LANES), axis=1, keepdims=True)
    pg_sel = 1.0 / jnp.sum(jnp.exp(lg - mg), axis=1, keepdims=True)

    lo = n_grp + gidx * epg
    in_grp = (lane >= lo) & (lane < lo + epg)
    le = jnp.where(in_grp, logits, -jnp.inf)
    e = jnp.exp(le - jnp.max(le, axis=1, keepdims=True))
    pe = jnp.where(in_grp, e / jnp.sum(e, axis=1, keepdims=True), -1.0)
    v1 = jnp.max(pe, axis=1, keepdims=True)
    i1 = jnp.min(jnp.where(pe == v1, lane, LANES), axis=1, keepdims=True)
    pe2 = jnp.where(lane == i1, -1.0, pe)
    v2 = jnp.max(pe2, axis=1, keepdims=True)
    i2 = jnp.min(jnp.where(pe2 == v2, lane, LANES), axis=1, keepdims=True)
    den = v1 + v2
    ri_ref[...] = jnp.where(lane == 0, i1 - n_grp, jnp.where(lane == 1, i2 - n_grp, 0))
    rf_ref[...] = jnp.where(lane == 0, pg_sel * v1 / den, jnp.where(lane == 1, pg_sel * v2 / den, 0.0))


def _router(h, g, w_cat, b_cat, n_grp, epg):
    t, d = h.shape
    tm = _tile(t, 512)
    return pl.pallas_call(
        functools.partial(_router_kernel, n_grp=n_grp, epg=epg),
        grid=(t // tm,),
        in_specs=[pl.BlockSpec((tm, d), lambda i: (i, 0)),
                  pl.BlockSpec((1, d), lambda i: (0, 0)),
                  pl.BlockSpec((d, LANES), lambda i: (0, 0)),
                  pl.BlockSpec((1, LANES), lambda i: (0, 0))],
        out_specs=[pl.BlockSpec((tm, d), lambda i: (i, 0)),
                   pl.BlockSpec((tm, LANES), lambda i: (i, 0)),
                   pl.BlockSpec((tm, LANES), lambda i: (i, 0))],
        out_shape=[jax.ShapeDtypeStruct((t, d), F32),
                   jax.ShapeDtypeStruct((t, LANES), jnp.int32),
                   jax.ShapeDtypeStruct((t, LANES), F32)],
        compiler_params=_cparams(("parallel",)),
        name="moe_router",
    )(h, g.reshape(1, d), w_cat, b_cat)


def _expert_kernel(blk_e, blk_n, tok_ref, tok_next_ref, dst_ref, xn_hbm, wg_ref, wu_ref, wd_ref,
                   y_hbm, xbuf, obuf, gsem, ssem, *, n_rb, rb):
    del blk_e
    i = pl.program_id(0)
    slot = lax.rem(i, 2)
    n = blk_n[i]

    def gather_row(tref, sl, r):
        return pltpu.make_async_copy(xn_hbm.at[pl.ds(tref[0, 0, r], 1)], xbuf.at[sl, pl.ds(r, 1)],
                                     gsem.at[sl])

    def start_gather(tref, sl):
        def body(r, c):
            gather_row(tref, sl, r).start()
            return c
        lax.fori_loop(0, rb, body, 0, unroll=8)

    def scatter_row(sl, r, dst):
        return pltpu.make_async_copy(obuf.at[sl, pl.ds(r, 1)], y_hbm.at[pl.ds(dst, 1)], ssem.at[sl])

    def wait_scatter(sl):
        pltpu.make_async_copy(obuf.at[sl], y_hbm.at[pl.ds(0, rb)], ssem.at[sl]).wait()

    @pl.when(i == 0)
    def _():
        obuf[...] = jnp.zeros_like(obuf)
        n_real = y_hbm.shape[0] - 2 * rb
        for sl in range(2):
            init = pltpu.make_async_copy(obuf.at[sl], y_hbm.at[pl.ds(n_real + sl * rb, rb)], ssem.at[sl])
            init.start()
            init.wait()

    @pl.when((i == 0) & (n > 0))
    def _():
        start_gather(tok_ref, 0)

    n_next = blk_n[jnp.minimum(i + 1, n_rb - 1)]

    @pl.when((i + 1 < n_rb) & (n_next > 0))
    def _():
        start_gather(tok_next_ref, 1 - slot)

    @pl.when((i >= 2) & (blk_n[jnp.maximum(i - 2, 0)] > 0))
    def _():
        wait_scatter(slot)

    @pl.when(n > 0)
    def _():
        pltpu.make_async_copy(xn_hbm.at[pl.ds(0, rb)], xbuf.at[slot], gsem.at[slot]).wait()
        x = xbuf[slot].astype(BF16)
        g = jnp.dot(x, wg_ref[0], preferred_element_type=F32)
        u = jnp.dot(x, wu_ref[0], preferred_element_type=F32)
        hid = (g * jax.nn.sigmoid(g)) * u
        obuf[slot] = jnp.dot(hid.astype(BF16), wd_ref[0], preferred_element_type=F32)

        def body(r, c):
            scatter_row(slot, r, dst_ref[0, 0, r]).start()
            return c
        lax.fori_loop(0, rb, body, 0, unroll=8)

    @pl.when(i == n_rb - 1)
    def _():
        @pl.when((i >= 1) & (blk_n[jnp.maximum(i - 1, 0)] > 0))
        def _():
            wait_scatter(1 - slot)

        @pl.when(n > 0)
        def _():
            wait_scatter(slot)


def _experts(xn, row_tok, row_dst, blk_e, blk_n, w_gate, w_up, w_down, n_slots):
    t, d = xn.shape
    f = w_gate.shape[2]
    n_rb, rb = row_tok.shape
    tok3 = row_tok.reshape(n_rb, 1, rb)
    dst3 = row_dst.reshape(n_rb, 1, rb)
    smem_blk = lambda f_: pl.BlockSpec((1, 1, rb), f_, memory_space=pltpu.SMEM)
    grid_spec = pltpu.PrefetchScalarGridSpec(
        num_scalar_prefetch=2,
        grid=(n_rb,),
        in_specs=[smem_blk(lambda i, be, bn: (i, 0, 0)),
                  smem_blk(lambda i, be, bn: (jnp.minimum(i + 1, n_rb - 1), 0, 0)),
                  smem_blk(lambda i, be, bn: (i, 0, 0)),
                  pl.BlockSpec(memory_space=pl.ANY),
                  pl.BlockSpec((1, d, f), lambda i, be, bn: (be[i], 0, 0)),
                  pl.BlockSpec((1, d, f), lambda i, be, bn: (be[i], 0, 0)),
                  pl.BlockSpec((1, f, d), lambda i, be, bn: (be[i], 0, 0))],
        out_specs=pl.BlockSpec(memory_space=pl.ANY),
        scratch_shapes=[pltpu.VMEM((2, rb, d), F32),
                        pltpu.VMEM((2, rb, d), F32),
                        pltpu.SemaphoreType.DMA((2,)),
                        pltpu.SemaphoreType.DMA((2,))],
    )
    return pl.pallas_call(
        functools.partial(_expert_kernel, n_rb=n_rb, rb=rb),
        grid_spec=grid_spec,
        out_shape=jax.ShapeDtypeStruct((n_slots, d), F32),
        compiler_params=_cparams(("arbitrary",)),
        name="moe_experts",
    )(blk_e, blk_n, tok3, tok3, dst3, xn, w_gate, w_up, w_down)


def _combine_kernel(h_ref, y0_ref, y1_ref, rf_ref, *rest, final):
    if final:
        g_ref, o_ref = rest
    else:
        (o_ref,) = rest
    w = rf_ref[...]
    out = h_ref[...] + (w[:, 0:1] * y0_ref[...] + w[:, 1:2] * y1_ref[...])
    if final:
        out = _rmsnorm_rows(out, g_ref[...])
    o_ref[...] = out


def _combine(h, y, rf, final_g):
    t, d = h.shape
    tm = _tile(t, 512)
    final = final_g is not None
    in_specs = [pl.BlockSpec((tm, d), lambda i: (i, 0)),
                pl.BlockSpec((tm, d), lambda i: (i, 0)),
                pl.BlockSpec((tm, d), lambda i: (t // tm + i, 0)),
                pl.BlockSpec((tm, LANES), lambda i: (i, 0))]
    args = [h, y, y, rf]
    if final:
        in_specs.append(pl.BlockSpec((1, d), lambda i: (0, 0)))
        args.append(final_g.reshape(1, d))
    return pl.pallas_call(
        functools.partial(_combine_kernel, final=final),
        grid=(t // tm,),
        in_specs=in_specs,
        out_specs=pl.BlockSpec((tm, d), lambda i: (i, 0)),
        out_shape=jax.ShapeDtypeStruct((t, d), F32),
        compiler_params=_cparams(("parallel",)),
        name="moe_combine",
    )(*args)


def _dispatch_tables(ri, n_exp, rb):
    t = ri.shape[0]
    k = EXPERT_TOPK
    n_asg = t * k
    n_rb = -(-n_asg // rb) + n_exp
    flat_e = ri[:, :k].reshape(-1)
    order = jnp.argsort(flat_e).astype(jnp.int32)
    se = flat_e[order]
    counts = jnp.bincount(flat_e, length=n_exp).astype(jnp.int32)
    starts = jnp.cumsum(counts) - counts
    padded = (counts + rb - 1) // rb * rb
    pad_ends = jnp.cumsum(padded)
    pad_starts = pad_ends - padded
    dest = pad_starts[se] + jnp.arange(n_asg, dtype=jnp.int32) - starts[se]
    tok = order // k
    row_tok = jnp.zeros((n_rb * rb,), jnp.int32).at[dest].set(tok)
    rows = jnp.arange(n_rb * rb, dtype=jnp.int32)
    scratch_dst = n_asg + ((rows // rb) % 2) * rb + rows % rb
    row_dst = scratch_dst.at[dest].set((order % k) * t + tok)
    blk_start = jnp.arange(n_rb, dtype=jnp.int32) * rb
    blk_e = jnp.minimum(jnp.searchsorted(pad_ends, blk_start, side='right'), n_exp - 1).astype(jnp.int32)
    blk_n = jnp.clip(counts[blk_e] - (blk_start - pad_starts[blk_e]), 0, rb).astype(jnp.int32)
    return row_tok.reshape(n_rb, rb), row_dst.reshape(n_rb, rb), blk_e, blk_n


def _hier_moe(h, norm_g, w_grp, b_grp, w_exp, b_exp, w_gate, w_up, w_down, final_g):
    t, d = h.shape
    n_grp = w_grp.shape[1]
    n_exp = w_exp.shape[1]
    epg = n_exp // n_grp
    assert n_grp + n_exp <= LANES
    pad = LANES - n_grp - n_exp
    w_cat = jnp.concatenate([w_grp, w_exp, jnp.zeros((d, pad), F32)], axis=1).astype(BF16)
    b_cat = jnp.concatenate([b_grp, b_exp, jnp.zeros((pad,), F32)]).reshape(1, LANES)
    xn, ri, rf = _router(h, norm_g, w_cat, b_cat, n_grp, epg)
    row_tok, row_dst, blk_e, blk_n = _dispatch_tables(ri, n_exp, ROW_BLOCK)
    y = _experts(xn, row_tok, row_dst, blk_e, blk_n, w_gate.astype(BF16), w_up.astype(BF16),
                 w_down.astype(BF16), EXPERT_TOPK * t + 2 * ROW_BLOCK)
    return _combine(h, y, rf, final_g)


def _rope_tables(seq, dh):
    inv = 1.0 / (ROPE_THETA ** (jnp.arange(0, dh, 2, dtype=F32) / dh))
    ang = jnp.arange(seq, dtype=F32)[:, None] * inv[None, :]
    cos, sin = jnp.cos(ang), jnp.sin(ang)
    return jnp.concatenate([cos, cos], axis=1), jnp.concatenate([-sin, sin], axis=1)


def kernel(x, lru_norm, lru_w_in, lru_b_in, lru_conv_w, lru_conv_b, lru_w_r, lru_b_r, lru_w_i, lru_b_i, lru_lambda, lru_w_out, lru_b_out, att_norm, att_w_qkv, att_w_o, ffn_norm, moe_w_grp, moe_b_grp, moe_w_exp, moe_b_exp, moe_w_gate, moe_w_up, moe_w_down, final_norm):
    bsz, seq, d = x.shape
    depth = ffn_norm.shape[0]
    n_mixers = 2
    cc, ss = _rope_tables(seq, d // N_HEADS)
    h = x.reshape(bsz * seq, d)
    for layer in range(depth):
        j = layer // n_mixers
        if layer % n_mixers == 0:
            u = _lru_in(h, lru_norm[j], lru_w_in[j].astype(BF16), lru_b_in[j])
            hg = _lru_scan(u, lru_conv_w[j], lru_conv_b[j], lru_w_r[j].astype(BF16), lru_b_r[j],
                           lru_w_i[j].astype(BF16), lru_b_i[j], lru_lambda[j], bsz, seq)
            h = _matmul_residual(hg, lru_w_out[j].astype(BF16), lru_b_out[j], h)
        else:
            qkv = _qkv_rope(h, att_norm[j], att_w_qkv[j].astype(BF16), cc, ss, seq)
            o = _moba_attention(qkv, bsz, seq, d)
            h = _matmul_residual(o, att_w_o[j].astype(BF16), None, h)
        final_g = final_norm if layer == depth - 1 else None
        h = _hier_moe(h, ffn_norm[layer], moe_w_grp[layer], moe_b_grp[layer], moe_w_exp[layer],
                      moe_b_exp[layer], moe_w_gate[layer], moe_w_up[layer], moe_w_down[layer], final_g)
    return h.reshape(bsz, seq, d)
```

```python
import functools

import jax
import jax.numpy as jnp
from jax import lax
from jax.experimental import pallas as pl
from jax.experimental.pallas import tpu as pltpu

F32 = jnp.float32
BF16 = jnp.bfloat16

NORM_EPS = 1e-6
NEG_INF = -1e30
LRU_C = 8.0
N_HEADS = 16
MOBA_BLOCK = 256
MOBA_TOPK = 3
ROPE_THETA = 10000.0
EXPERT_TOPK = 2
ROW_BLOCK = 256
LANES = 128
SUBLANES = 8
BF16_SUBLANES = 16
VMEM_LIMIT_BYTES = 56 * 1024 * 1024


def _tile(n, pref):
    t = min(n, pref)
    while n % t:
        t //= 2
    return t


def _cparams(sem):
    return pltpu.CompilerParams(dimension_semantics=sem, vmem_limit_bytes=VMEM_LIMIT_BYTES)


def _rmsnorm_rows(x, g):
    ms = jnp.mean(x * x, axis=-1, keepdims=True)
    return x * lax.rsqrt(ms + NORM_EPS) * g


def _lru_in_kernel(x_ref, g_ref, w_ref, b_ref, o_ref, xn_ref, *, n_plain):
    j = pl.program_id(1)

    @pl.when(j == 0)
    def _():
        xn_ref[...] = _rmsnorm_rows(x_ref[...], g_ref[...]).astype(BF16)

    acc = jnp.dot(xn_ref[...], w_ref[...], preferred_element_type=F32) + b_ref[...]

    @pl.when(j < n_plain)
    def _():
        o_ref[...] = acc

    @pl.when(j >= n_plain)
    def _():
        o_ref[...] = jax.nn.gelu(acc, approximate=True)


def _lru_in(h, g, w, b):
    t, d = h.shape
    n = w.shape[1]
    tm, tn = _tile(t, 1024), _tile(n // 2, 1024)
    return pl.pallas_call(
        functools.partial(_lru_in_kernel, n_plain=(n // 2) // tn),
        grid=(t // tm, n // tn),
        in_specs=[pl.BlockSpec((tm, d), lambda i, j: (i, 0)),
                  pl.BlockSpec((1, d), lambda i, j: (0, 0)),
                  pl.BlockSpec((d, tn), lambda i, j: (0, j)),
                  pl.BlockSpec((1, tn), lambda i, j: (0, j))],
        out_specs=pl.BlockSpec((tm, tn), lambda i, j: (i, j)),
        out_shape=jax.ShapeDtypeStruct((t, n), F32),
        scratch_shapes=[pltpu.VMEM((tm, d), BF16)],
        compiler_params=_cparams(("parallel", "arbitrary")),
        name="lru_in",
    )(h, g.reshape(1, d), w, b.reshape(1, n))


def _qkv_kernel(x_ref, g_ref, w_ref, cc_ref, ss_ref, o_ref, xn_ref, *, n_rope, heads_per_tile, dh):
    j = pl.program_id(1)

    @pl.when(j == 0)
    def _():
        xn_ref[...] = _rmsnorm_rows(x_ref[...], g_ref[...]).astype(BF16)

    acc = jnp.dot(xn_ref[...], w_ref[...], preferred_element_type=F32)

    @pl.when(j < n_rope)
    def _():
        cc = cc_ref[...]
        ss = ss_ref[...]
        for hh in range(heads_per_tile):
            a = acc[:, hh * dh:(hh + 1) * dh]
            o_ref[:, hh * dh:(hh + 1) * dh] = (a * cc + pltpu.roll(a, dh // 2, 1) * ss).astype(o_ref.dtype)

    @pl.when(j >= n_rope)
    def _():
        o_ref[...] = acc.astype(o_ref.dtype)


def _qkv_rope(h, g, w, cc, ss, seq):
    t, d = h.shape
    n = w.shape[1]
    dh = d // N_HEADS
    tm, tn = _tile(seq, 1024), _tile(d, 1024)
    s_tiles = seq // tm
    return pl.pallas_call(
        functools.partial(_qkv_kernel, n_rope=(2 * d) // tn, heads_per_tile=tn // dh, dh=dh),
        grid=(t // tm, n // tn),
        in_specs=[pl.BlockSpec((tm, d), lambda i, j: (i, 0)),
                  pl.BlockSpec((1, d), lambda i, j: (0, 0)),
                  pl.BlockSpec((d, tn), lambda i, j: (0, j)),
                  pl.BlockSpec((tm, dh), lambda i, j: (i % s_tiles, 0)),
                  pl.BlockSpec((tm, dh), lambda i, j: (i % s_tiles, 0))],
        out_specs=pl.BlockSpec((tm, tn), lambda i, j: (i, j)),
        out_shape=jax.ShapeDtypeStruct((t, n), BF16),
        scratch_shapes=[pltpu.VMEM((tm, d), BF16)],
        compiler_params=_cparams(("parallel", "arbitrary")),
        name="qkv_rope",
    )(h, g.reshape(1, d), w, cc, ss)


def _mm_res_kernel(a_ref, w_ref, *rest, has_bias):
    if has_bias:
        b_ref, r_ref, o_ref = rest
    else:
        r_ref, o_ref = rest
    y = jnp.dot(a_ref[...], w_ref[...], preferred_element_type=F32)
    if has_bias:
        y = y + b_ref[...]
    o_ref[...] = r_ref[...] + y


def _matmul_residual(a, w, b, res):
    t, k = a.shape
    n = w.shape[1]
    tm, tn = _tile(t, 1024), _tile(n, 1024)
    has_bias = b is not None
    in_specs = [pl.BlockSpec((tm, k), lambda i, j: (i, 0)),
                pl.BlockSpec((k, tn), lambda i, j: (0, j))]
    args = [a, w]
    if has_bias:
        in_specs.append(pl.BlockSpec((1, tn), lambda i, j: (0, j)))
        args.append(b.reshape(1, n))
    in_specs.append(pl.BlockSpec((tm, tn), lambda i, j: (i, j)))
    args.append(res)
    return pl.pallas_call(
        functools.partial(_mm_res_kernel, has_bias=has_bias),
        grid=(t // tm, n // tn),
        in_specs=in_specs,
        out_specs=pl.BlockSpec((tm, tn), lambda i, j: (i, j)),
        out_shape=jax.ShapeDtypeStruct((t, n), F32),
        compiler_params=_cparams(("parallel", "parallel")),
        name="matmul_residual",
    )(*args)


def _lru_scan_kernel(xb_ref, gate_ref, cw_ref, cb_ref, wr_ref, br_ref, wi_ref, bi_ref, lam_ref,
                     o_ref, ext_ref, a_ref, b_ref, h_ref, *, ts, n_grp, gw, conv_w):
    s = pl.program_id(1)
    w = xb_ref.shape[1]

    @pl.when(s == 0)
    def _():
        ext_ref[0:SUBLANES, :] = jnp.zeros((SUBLANES, w), F32)
        h_ref[...] = jnp.zeros_like(h_ref)

    ext_ref[SUBLANES:SUBLANES + ts, :] = xb_ref[...]
    xc = cb_ref[...] + cw_ref[conv_w - 1:conv_w, :] * xb_ref[...]
    for k in range(1, conv_w):
        xc = xc + cw_ref[conv_w - 1 - k:conv_w - k, :] * ext_ref[SUBLANES - k:SUBLANES - k + ts, :]
    ext_ref[0:SUBLANES, :] = ext_ref[ts:ts + SUBLANES, :]

    z = -lam_ref[...]
    softplus = jnp.maximum(z, 0.0) + jnp.log1p(jnp.exp(-jnp.abs(z)))
    c = -LRU_C * softplus
    for g in range(n_grp):
        sl = slice(g * gw, (g + 1) * gw)
        xg = xc[:, sl]
        xg16 = xg.astype(BF16)
        r = jax.nn.sigmoid(jnp.dot(xg16, wr_ref[g], preferred_element_type=F32) + br_ref[:, sl])
        ig = jax.nn.sigmoid(jnp.dot(xg16, wi_ref[g], preferred_element_type=F32) + bi_ref[:, sl])
        log_a = c[:, sl] * r
        a_ref[:, sl] = jnp.exp(log_a)
        th = jnp.tanh(log_a)
        b_ref[:, sl] = jnp.sqrt(-2.0 * th / (1.0 - th)) * ig * xg

    rows = lax.broadcasted_iota(jnp.int32, (SUBLANES, w), 0)

    def chunk(ci, h):
        r0 = pl.multiple_of(ci * SUBLANES, SUBLANES)
        a = a_ref[pl.ds(r0, SUBLANES), :]
        b = b_ref[pl.ds(r0, SUBLANES), :]
        for sh in (1, 2, 4):
            keep = rows >= sh
            a_sh = jnp.where(keep, pltpu.roll(a, sh, 0), 1.0)
            b_sh = jnp.where(keep, pltpu.roll(b, sh, 0), 0.0)
            b = a * b_sh + b
            a = a * a_sh
        hc = a * h + b
        b_ref[pl.ds(r0, SUBLANES), :] = hc * gate_ref[pl.ds(r0, SUBLANES), :]
        return jnp.broadcast_to(hc[SUBLANES - 1:SUBLANES, :], (SUBLANES, w))

    h_ref[...] = lax.fori_loop(0, ts // SUBLANES, chunk, h_ref[...])
    o_ref[...] = b_ref[...].astype(o_ref.dtype)


def _lru_scan(u, conv_w, conv_b, w_r, b_r, w_i, b_i, lam, bsz, seq):
    t = u.shape[0]
    w = u.shape[1] // 2
    n_grp, gw = w_r.shape[0], w_r.shape[1]
    cw = conv_w.shape[0]
    ts = _tile(seq, 256)
    s_tiles = seq // ts
    row = lambda b, s: (b * s_tiles + s, 0)
    vec = pl.BlockSpec((1, w), lambda b, s: (0, 0))
    return pl.pallas_call(
        functools.partial(_lru_scan_kernel, ts=ts, n_grp=n_grp, gw=gw, conv_w=cw),
        grid=(bsz, s_tiles),
        in_specs=[pl.BlockSpec((ts, w), row),
                  pl.BlockSpec((ts, w), lambda b, s: (b * s_tiles + s, 1)),
                  pl.BlockSpec((cw, w), lambda b, s: (0, 0)),
                  vec,
                  pl.BlockSpec((n_grp, gw, gw), lambda b, s: (0, 0, 0)),
                  vec,
                  pl.BlockSpec((n_grp, gw, gw), lambda b, s: (0, 0, 0)),
                  vec,
                  vec],
        out_specs=pl.BlockSpec((ts, w), row),
        out_shape=jax.ShapeDtypeStruct((t, w), BF16),
        scratch_shapes=[pltpu.VMEM((ts + SUBLANES, w), F32),
                        pltpu.VMEM((ts, w), F32),
                        pltpu.VMEM((ts, w), F32),
                        pltpu.VMEM((SUBLANES, w), F32)],
        compiler_params=_cparams(("parallel", "arbitrary")),
        name="lru_scan",
    )(u, u, conv_w, conv_b.reshape(1, w), w_r, b_r.reshape(1, w), w_i, b_i.reshape(1, w),
      lam.reshape(1, w))


def _attn_kernel(q_ref, k_ref, v_ref, o_ref, kaug_ref, vaug_t_ref, *, nb, blk, dh, topk):
    seq = nb * blk
    nbp = vaug_t_ref.shape[0] - dh
    first = (pl.program_id(0) == 0) & (pl.program_id(1) == 0)

    @pl.when(first)
    def _():
        row = lax.broadcasted_iota(jnp.int32, (seq, LANES), 0)
        lane = lax.broadcasted_iota(jnp.int32, (seq, LANES), 1)
        kaug_ref[:, dh:dh + LANES] = (lane == row // blk).astype(BF16)
        vaug_t_ref[dh:dh + nbp, :] = (lax.broadcasted_iota(jnp.int32, (nbp, seq), 0) == 0).astype(BF16)

    kaug_ref[:, 0:dh] = k_ref[...]
    for j in range(nb):
        vaug_t_ref[0:dh, j * blk:(j + 1) * blk] = v_ref[j * blk:(j + 1) * blk, :].astype(F32).T.astype(BF16)
    kmean = jnp.sum(k_ref[...].astype(F32).reshape(nb, blk, dh), axis=1) * (1.0 / blk)
    if nbp > nb:
        kmean = jnp.concatenate([kmean, jnp.zeros((nbp - nb, dh), F32)], axis=0)
    kmean = kmean.astype(BF16)

    c = (dh ** -0.5) * 1.4426950408889634
    blk_id = lax.broadcasted_iota(jnp.int32, (nbp, blk), 0)
    key_pos = lax.broadcasted_iota(jnp.int32, (blk, blk), 0)
    q_pos = lax.broadcasted_iota(jnp.int32, (blk, blk), 1)
    zeros_pad = jnp.zeros((LANES - nbp, blk), BF16)

    def scores(i):
        q_t = q_ref[i * blk:(i + 1) * blk, :].astype(F32).T.astype(BF16)
        gate_t = jnp.dot(kmean, q_t, preferred_element_type=F32)
        g = jnp.where(blk_id < i, gate_t, NEG_INF)
        sel = blk_id == i
        for _ in range(topk):
            m = jnp.max(g, axis=0, keepdims=True)
            first_max = jnp.min(jnp.where(g == m, blk_id, nbp), axis=0, keepdims=True)
            hit = blk_id == first_max
            sel = sel | (hit & (blk_id < i))
            g = jnp.where(hit, -jnp.inf, g)
        bias_t = jnp.where(sel, 0.0, NEG_INF).astype(BF16)
        q_aug_t = jnp.concatenate([q_t, bias_t, zeros_pad], axis=0)
        return jnp.dot(kaug_ref[0:(i + 1) * blk, :], q_aug_t, preferred_element_type=F32)

    def softmax_pv(i, s):
        s_own = jnp.where(key_pos <= q_pos, s[i * blk:, :], NEG_INF)
        m = jnp.max(s_own, axis=0, keepdims=True)
        if i > 0:
            s_past = s[:i * blk, :]
            m = jnp.maximum(m, jnp.max(s_past, axis=0, keepdims=True))
            p = jnp.concatenate([jnp.exp2((s_past - m) * c), jnp.exp2((s_own - m) * c)], axis=0)
        else:
            p = jnp.exp2((s_own - m) * c)
        out_t = jnp.dot(vaug_t_ref[:, 0:(i + 1) * blk], p.astype(BF16), preferred_element_type=F32)
        o_t = out_t[0:dh, :] / out_t[dh:dh + 1, :]
        o_ref[i * blk:(i + 1) * blk, :] = o_t.T.astype(o_ref.dtype)

    s_prev = scores(0)
    for i in range(1, nb):
        s_next = scores(i)
        softmax_pv(i - 1, s_prev)
        s_prev = s_next
    softmax_pv(nb - 1, s_prev)


def _moba_attention(qkv, bsz, seq, d):
    t = qkv.shape[0]
    dh = d // N_HEADS
    blk = MOBA_BLOCK
    nb = seq // blk
    assert seq % blk == 0 and nb <= BF16_SUBLANES and dh == LANES
    return pl.pallas_call(
        functools.partial(_attn_kernel, nb=nb, blk=blk, dh=dh, topk=min(MOBA_TOPK, nb)),
        grid=(bsz, N_HEADS),
        in_specs=[pl.BlockSpec((seq, dh), lambda b, h: (b, h)),
                  pl.BlockSpec((seq, dh), lambda b, h: (b, N_HEADS + h)),
                  pl.BlockSpec((seq, dh), lambda b, h: (b, 2 * N_HEADS + h))],
        out_specs=pl.BlockSpec((seq, dh), lambda b, h: (b, h)),
        out_shape=jax.ShapeDtypeStruct((t, d), BF16),
        scratch_shapes=[pltpu.VMEM((seq, dh + LANES), BF16),
                        pltpu.VMEM((dh + BF16_SUBLANES, seq), BF16)],
        compiler_params=_cparams(("arbitrary", "arbitrary")),
        name="moba_attention",
    )(qkv, qkv, qkv)


def _router_kernel(h_ref, g_ref, w_ref, b_ref, xn_ref, ri_ref, rf_ref, *, n_grp, epg):
    xn = _rmsnorm_rows(h_ref[...], g_ref[...])
    xn_ref[...] = xn
    logits = jnp.dot(xn.astype(BF16), w_ref[...], preferred_element_type=F32) + b_ref[...]
    lane = lax.broadcasted_iota(jnp.int32, logits.shape, 1)

    lg = jnp.where(lane < n_grp, logits, -jnp.inf)
    mg = jnp.max(lg, axis=1, keepdims=True)
    gidx = jnp.min(jnp.where(lg == mg, lane, LANES), axis=1, keepdims=True)
    pg_sel = 1.0 / jnp.sum(jnp.exp(lg - mg), axis=1, keepdims=True)

    lo = n_grp + gidx * epg
    in_grp = (lane >= lo) & (lane < lo + epg)
    le = jnp.where(in_grp, logits, -jnp.inf)
    e = jnp.exp(le - jnp.max(le, axis=1, keepdims=True))
    pe = jnp.where(in_grp, e / jnp.sum(e, axis=1, keepdims=True), -1.0)
    v1 = jnp.max(pe, axis=1, keepdims=True)
    i1 = jnp.min(jnp.where(pe == v1, lane, LANES), axis=1, keepdims=True)
    pe2 = jnp.where(lane == i1, -1.0, pe)
    v2 = jnp.max(pe2, axis=1, keepdims=True)
    i2 = jnp.min(jnp.where(pe2 == v2, lane, LANES), axis=1, keepdims=True)
    den = v1 + v2
    ri_ref[...] = jnp.where(lane == 0, i1 - n_grp, jnp.where(lane == 1, i2 - n_grp, 0))
    rf_ref[...] = jnp.where(lane == 0, pg_sel * v1 / den, jnp.where(lane == 1, pg_sel * v2 / den, 0.0))


def _router(h, g, w_cat, b_cat, n_grp, epg):
    t, d = h.shape
    tm = _tile(t, 512)
    return pl.pallas_call(
        functools.partial(_router_kernel, n_grp=n_grp, epg=epg),
        grid=(t // tm,),
        in_specs=[pl.BlockSpec((tm, d), lambda i: (i, 0)),
                  pl.BlockSpec((1, d), lambda i: (0, 0)),
                  pl.BlockSpec((d, LANES), lambda i: (0, 0)),
                  pl.BlockSpec((1, LANES), lambda i: (0, 0))],
        out_specs=[pl.BlockSpec((tm, d), lambda i: (i, 0)),
                   pl.BlockSpec((tm, LANES), lambda i: (i, 0)),
                   pl.BlockSpec((tm, LANES), lambda i: (i, 0))],
        out_shape=[jax.ShapeDtypeStruct((t, d), F32),
                   jax.ShapeDtypeStruct((t, LANES), jnp.int32),
                   jax.ShapeDtypeStruct((t, LANES), F32)],
        compiler_params=_cparams(("parallel",)),
        name="moe_router",
    )(h, g.reshape(1, d), w_cat, b_cat)


def _expert_kernel(blk_e, blk_n, tok_ref, tok_next_ref, dst_ref, xn_hbm, wg_ref, wu_ref, wd_ref,
                   y_hbm, xbuf, obuf, gsem, ssem, *, n_rb, rb):
    del blk_e
    i = pl.program_id(0)
    slot = lax.rem(i, 2)
    n = blk_n[i]

    def gather_row(tref, sl, r):
        return pltpu.make_async_copy(xn_hbm.at[pl.ds(tref[0, 0, r], 1)], xbuf.at[sl, pl.ds(r, 1)],
                                     gsem.at[sl])

    def start_gather(tref, sl):
        def body(r, c):
            gather_row(tref, sl, r).start()
            return c
        lax.fori_loop(0, rb, body, 0, unroll=8)

    def scatter_row(sl, r, dst):
        return pltpu.make_async_copy(obuf.at[sl, pl.ds(r, 1)], y_hbm.at[pl.ds(dst, 1)], ssem.at[sl])

    def wait_scatter(sl):
        pltpu.make_async_copy(obuf.at[sl], y_hbm.at[pl.ds(0, rb)], ssem.at[sl]).wait()

    @pl.when(i == 0)
    def _():
        obuf[...] = jnp.zeros_like(obuf)
        n_real = y_hbm.shape[0] - 2 * rb
        for sl in range(2):
            init = pltpu.make_async_copy(obuf.at[sl], y_hbm.at[pl.ds(n_real + sl * rb, rb)], ssem.at[sl])
            init.start()
            init.wait()

    @pl.when((i == 0) & (n > 0))
    def _():
        start_gather(tok_ref, 0)

    n_next = blk_n[jnp.minimum(i + 1, n_rb - 1)]

    @pl.when((i + 1 < n_rb) & (n_next > 0))
    def _():
        start_gather(tok_next_ref, 1 - slot)

    @pl.when((i >= 2) & (blk_n[jnp.maximum(i - 2, 0)] > 0))
    def _():
        wait_scatter(slot)

    @pl.when(n > 0)
    def _():
        pltpu.make_async_copy(xn_hbm.at[pl.ds(0, rb)], xbuf.at[slot], gsem.at[slot]).wait()
        x = xbuf[slot].astype(BF16)
        g = jnp.dot(x, wg_ref[0], preferred_element_type=F32)
        u = jnp.dot(x, wu_ref[0], preferred_element_type=F32)
        hid = (g * jax.nn.sigmoid(g)) * u
        obuf[slot] = jnp.dot(hid.astype(BF16), wd_ref[0], preferred_element_type=F32)

        def body(r, c):
            scatter_row(slot, r, dst_ref[0, 0, r]).start()
            return c
        lax.fori_loop(0, rb, body, 0, unroll=8)

    @pl.when(i == n_rb - 1)
    def _():
        @pl.when((i >= 1) & (blk_n[jnp.maximum(i - 1, 0)] > 0))
        def _():
            wait_scatter(1 - slot)

        @pl.when(n > 0)
        def _():
            wait_scatter(slot)


def _experts(xn, row_tok, row_dst, blk_e, blk_n, w_gate, w_up, w_down, n_slots):
    t, d = xn.shape
    f = w_gate.shape[2]
    n_rb, rb = row_tok.shape
    tok3 = row_tok.reshape(n_rb, 1, rb)
    dst3 = row_dst.reshape(n_rb, 1, rb)
    smem_blk = lambda f_: pl.BlockSpec((1, 1, rb), f_, memory_space=pltpu.SMEM)
    grid_spec = pltpu.PrefetchScalarGridSpec(
        num_scalar_prefetch=2,
        grid=(n_rb,),
        in_specs=[smem_blk(lambda i, be, bn: (i, 0, 0)),
                  smem_blk(lambda i, be, bn: (jnp.minimum(i + 1, n_rb - 1), 0, 0)),
                  smem_blk(lambda i, be, bn: (i, 0, 0)),
                  pl.BlockSpec(memory_space=pl.ANY),
                  pl.BlockSpec((1, d, f), lambda i, be, bn: (be[i], 0, 0)),
                  pl.BlockSpec((1, d, f), lambda i, be, bn: (be[i], 0, 0)),
                  pl.BlockSpec((1, f, d), lambda i, be, bn: (be[i], 0, 0))],
        out_specs=pl.BlockSpec(memory_space=pl.ANY),
        scratch_shapes=[pltpu.VMEM((2, rb, d), F32),
                        pltpu.VMEM((2, rb, d), F32),
                        pltpu.SemaphoreType.DMA((2,)),
                        pltpu.SemaphoreType.DMA((2,))],
    )
    return pl.pallas_call(
        functools.partial(_expert_kernel, n_rb=n_rb, rb=rb),
        grid_spec=grid_spec,
        out_shape=jax.ShapeDtypeStruct((n_slots, d), F32),
        compiler_params=_cparams(("arbitrary",)),
        name="moe_experts",
    )(blk_e, blk_n, tok3, tok3, dst3, xn, w_gate, w_up, w_down)


def _combine_kernel(h_ref, y0_ref, y1_ref, rf_ref, *rest, final):
    if final:
        g_ref, o_ref = rest
    else:
        (o_ref,) = rest
    w = rf_ref[...]
    out = h_ref[...] + (w[:, 0:1] * y0_ref[...] + w[:, 1:2] * y1_ref[...])
    if final:
        out = _rmsnorm_rows(out, g_ref[...])
    o_ref[...] = out


def _combine(h, y, rf, final_g):
    t, d = h.shape
    tm = _tile(t, 512)
    final = final_g is not None
    in_specs = [pl.BlockSpec((tm, d), lambda i: (i, 0)),
                pl.BlockSpec((tm, d), lambda i: (i, 0)),
                pl.BlockSpec((tm, d), lambda i: (t // tm + i, 0)),
                pl.BlockSpec((tm, LANES), lambda i: (i, 0))]
    args = [h, y, y, rf]
    if final:
        in_specs.append(pl.BlockSpec((1, d), lambda i: (0, 0)))
        args.append(final_g.reshape(1, d))
    return pl.pallas_call(
        functools.partial(_combine_kernel, final=final),
        grid=(t // tm,),
        in_specs=in_specs,
        out_specs=pl.BlockSpec((tm, d), lambda i: (i, 0)),
        out_shape=jax.ShapeDtypeStruct((t, d), F32),
        compiler_params=_cparams(("parallel",)),
        name="moe_combine",
    )(*args)


def _dispatch_tables(ri, n_exp, rb):
    t = ri.shape[0]
    k = EXPERT_TOPK
    n_asg = t * k
    n_rb = -(-n_asg // rb) + n_exp
    flat_e = ri[:, :k].reshape(-1)
    order = jnp.argsort(flat_e).astype(jnp.int32)
    se = flat_e[order]
    counts = jnp.bincount(flat_e, length=n_exp).astype(jnp.int32)
    starts = jnp.cumsum(counts) - counts
    padded = (counts + rb - 1) // rb * rb
    pad_ends = jnp.cumsum(padded)
    pad_starts = pad_ends - padded
    dest = pad_starts[se] + jnp.arange(n_asg, dtype=jnp.int32) - starts[se]
    tok = order // k
    row_tok = jnp.zeros((n_rb * rb,), jnp.int32).at[dest].set(tok)
    rows = jnp.arange(n_rb * rb, dtype=jnp.int32)
    scratch_dst = n_asg + ((rows // rb) % 2) * rb + rows % rb
    row_dst = scratch_dst.at[dest].set((order % k) * t + tok)
    blk_start = jnp.arange(n_rb, dtype=jnp.int32) * rb
    blk_e = jnp.minimum(jnp.searchsorted(pad_ends, blk_start, side='right'), n_exp - 1).astype(jnp.int32)
    blk_n = jnp.clip(counts[blk_e] - (blk_start - pad_starts[blk_e]), 0, rb).astype(jnp.int32)
    return row_tok.reshape(n_rb, rb), row_dst.reshape(n_rb, rb), blk_e, blk_n


def _hier_moe(h, norm_g, w_grp, b_grp, w_exp, b_exp, w_gate, w_up, w_down, final_g):
    t, d = h.shape
    n_grp = w_grp.shape[1]
    n_exp = w_exp.shape[1]
    epg = n_exp // n_grp
    assert n_grp + n_exp <= LANES
    pad = LANES - n_grp - n_exp
    w_cat = jnp.concatenate([w_grp, w_exp, jnp.zeros((d, pad), F32)], axis=1).astype(BF16)
    b_cat = jnp.concatenate([b_grp, b_exp, jnp.zeros((pad,), F32)]).reshape(1, LANES)
    xn, ri, rf = _router(h, norm_g, w_cat, b_cat, n_grp, epg)
    row_tok, row_dst, blk_e, blk_n = _dispatch_tables(ri, n_exp, ROW_BLOCK)
    y = _experts(xn, row_tok, row_dst, blk_e, blk_n, w_gate.astype(BF16), w_up.astype(BF16),
                 w_down.astype(BF16), EXPERT_TOPK * t + 2 * ROW_BLOCK)
    return _combine(h, y, rf, final_g)


def _rope_tables(seq, dh):
    inv = 1.0 / (ROPE_THETA ** (jnp.arange(0, dh, 2, dtype=F32) / dh))
    ang = jnp.arange(seq, dtype=F32)[:, None] * inv[None, :]
    cos, sin = jnp.cos(ang), jnp.sin(ang)
    return jnp.concatenate([cos, cos], axis=1), jnp.concatenate([-sin, sin], axis=1)


def kernel(x, lru_norm, lru_w_in, lru_b_in, lru_conv_w, lru_conv_b, lru_w_r, lru_b_r, lru_w_i, lru_b_i, lru_lambda, lru_w_out, lru_b_out, att_norm, att_w_qkv, att_w_o, ffn_norm, moe_w_grp, moe_b_grp, moe_w_exp, moe_b_exp, moe_w_gate, moe_w_up, moe_w_down, final_norm):
    bsz, seq, d = x.shape
    depth = ffn_norm.shape[0]
    n_mixers = 2
    cc, ss = _rope_tables(seq, d // N_HEADS)
    h = x.reshape(bsz * seq, d)
    for layer in range(depth):
        j = layer // n_mixers
        if layer % n_mixers == 0:
            u = _lru_in(h, lru_norm[j], lru_w_in[j].astype(BF16), lru_b_in[j])
            hg = _lru_scan(u, lru_conv_w[j], lru_conv_b[j], lru_w_r[j].astype(BF16), lru_b_r[j],
                           lru_w_i[j].astype(BF16), lru_b_i[j], lru_lambda[j], bsz, seq)
            h = _matmul_residual(hg, lru_w_out[j].astype(BF16), lru_b_out[j], h)
        else:
            qkv = _qkv_rope(h, att_norm[j], att_w_qkv[j].astype(BF16), cc, ss, seq)
            o = _moba_attention(qkv, bsz, seq, d)
            h = _matmul_residual(o, att_w_o[j].astype(BF16), None, h)
        final_g = final_norm if layer == depth - 1 else None
        h = _hier_moe(h, ffn_norm[layer], moe_w_grp[layer], moe_b_grp[layer], moe_w_exp[layer],
                      moe_b_exp[layer], moe_w_gate[layer], moe_w_up[layer], moe_w_down[layer], final_g)
    return h.reshape(bsz, seq, d)
```

```python
import functools

import jax
import jax.numpy as jnp
from jax import lax
from jax.experimental import pallas as pl
from jax.experimental.pallas import tpu as pltpu

F32 = jnp.float32
BF16 = jnp.bfloat16

NORM_EPS = 1e-6
NEG_INF = -1e30
LRU_C = 8.0
N_HEADS = 16
MOBA_BLOCK = 256
MOBA_TOPK = 3
ROPE_THETA = 10000.0
EXPERT_TOPK = 2
ROW_BLOCK = 256
LANES = 128
LANE_SHIFT = 7
SUBLANES = 8
BF16_SUBLANES = 16
MXU_COLS = 256
VMEM_LIMIT_BYTES = 56 * 1024 * 1024


def _tile(n, pref):
    t = min(n, pref)
    while n % t:
        t //= 2
    return t


def _cparams(sem):
    return pltpu.CompilerParams(dimension_semantics=sem, vmem_limit_bytes=VMEM_LIMIT_BYTES)


def _rmsnorm_rows(x, g):
    ms = jnp.mean(x * x, axis=-1, keepdims=True)
    return x * lax.rsqrt(ms + NORM_EPS) * g


def _lru_in_kernel(x_ref, g_ref, w_ref, b_ref, o_ref, xn_ref, *, n_plain):
    j = pl.program_id(1)

    @pl.when(j == 0)
    def _():
        xn_ref[...] = _rmsnorm_rows(x_ref[...], g_ref[...]).astype(BF16)

    acc = jnp.dot(xn_ref[...], w_ref[...], preferred_element_type=F32) + b_ref[...]

    @pl.when(j < n_plain)
    def _():
        o_ref[...] = acc

    @pl.when(j >= n_plain)
    def _():
        o_ref[...] = jax.nn.gelu(acc, approximate=True)


def _lru_in(h, g, w, b):
    t, d = h.shape
    n = w.shape[1]
    tm, tn = _tile(t, 1024), _tile(n // 2, 1024)
    return pl.pallas_call(
        functools.partial(_lru_in_kernel, n_plain=(n // 2) // tn),
        grid=(t // tm, n // tn),
        in_specs=[pl.BlockSpec((tm, d), lambda i, j: (i, 0)),
                  pl.BlockSpec((1, d), lambda i, j: (0, 0)),
                  pl.BlockSpec((d, tn), lambda i, j: (0, j)),
                  pl.BlockSpec((1, tn), lambda i, j: (0, j))],
        out_specs=pl.BlockSpec((tm, tn), lambda i, j: (i, j)),
        out_shape=jax.ShapeDtypeStruct((t, n), F32),
        scratch_shapes=[pltpu.VMEM((tm, d), BF16)],
        compiler_params=_cparams(("parallel", "arbitrary")),
        name="lru_in",
    )(h, g.reshape(1, d), w, b.reshape(1, n))


def _qkv_kernel(x_ref, g_ref, w_ref, cc_ref, ss_ref, o_ref, xn_ref, *, n_rope, heads_per_tile, dh):
    j = pl.program_id(1)

    @pl.when(j == 0)
    def _():
        xn_ref[...] = _rmsnorm_rows(x_ref[...], g_ref[...]).astype(BF16)

    acc = jnp.dot(xn_ref[...], w_ref[...], preferred_element_type=F32)

    @pl.when(j < n_rope)
    def _():
        cc = cc_ref[...]
        ss = ss_ref[...]
        for hh in range(heads_per_tile):
            a = acc[:, hh * dh:(hh + 1) * dh]
            o_ref[:, hh * dh:(hh + 1) * dh] = (a * cc + pltpu.roll(a, dh // 2, 1) * ss).astype(o_ref.dtype)

    @pl.when(j >= n_rope)
    def _():
        o_ref[...] = acc.astype(o_ref.dtype)


def _qkv_rope(h, g, w, cc, ss, seq):
    t, d = h.shape
    n = w.shape[1]
    dh = d // N_HEADS
    tm, tn = _tile(seq, 1024), _tile(d, 1024)
    s_tiles = seq // tm
    return pl.pallas_call(
        functools.partial(_qkv_kernel, n_rope=(2 * d) // tn, heads_per_tile=tn // dh, dh=dh),
        grid=(t // tm, n // tn),
        in_specs=[pl.BlockSpec((tm, d), lambda i, j: (i, 0)),
                  pl.BlockSpec((1, d), lambda i, j: (0, 0)),
                  pl.BlockSpec((d, tn), lambda i, j: (0, j)),
                  pl.BlockSpec((tm, dh), lambda i, j: (i % s_tiles, 0)),
                  pl.BlockSpec((tm, dh), lambda i, j: (i % s_tiles, 0))],
        out_specs=pl.BlockSpec((tm, tn), lambda i, j: (i, j)),
        out_shape=jax.ShapeDtypeStruct((t, n), BF16),
        scratch_shapes=[pltpu.VMEM((tm, d), BF16)],
        compiler_params=_cparams(("parallel", "arbitrary")),
        name="qkv_rope",
    )(h, g.reshape(1, d), w, cc, ss)


def _mm_res_kernel(a_ref, w_ref, *rest, has_bias):
    if has_bias:
        b_ref, r_ref, o_ref = rest
    else:
        r_ref, o_ref = rest
    y = jnp.dot(a_ref[...], w_ref[...], preferred_element_type=F32)
    if has_bias:
        y = y + b_ref[...]
    o_ref[...] = r_ref[...] + y


def _matmul_residual(a, w, b, res):
    t, k = a.shape
    n = w.shape[1]
    tm, tn = _tile(t, 1024), _tile(n, 1024)
    has_bias = b is not None
    in_specs = [pl.BlockSpec((tm, k), lambda i, j: (i, 0)),
                pl.BlockSpec((k, tn), lambda i, j: (0, j))]
    args = [a, w]
    if has_bias:
        in_specs.append(pl.BlockSpec((1, tn), lambda i, j: (0, j)))
        args.append(b.reshape(1, n))
    in_specs.append(pl.BlockSpec((tm, tn), lambda i, j: (i, j)))
    args.append(res)
    return pl.pallas_call(
        functools.partial(_mm_res_kernel, has_bias=has_bias),
        grid=(t // tm, n // tn),
        in_specs=in_specs,
        out_specs=pl.BlockSpec((tm, tn), lambda i, j: (i, j)),
        out_shape=jax.ShapeDtypeStruct((t, n), F32),
        compiler_params=_cparams(("parallel", "parallel")),
        name="matmul_residual",
    )(*args)


def _lru_scan_kernel(xb_ref, gate_ref, cw_ref, cb_ref, wr_ref, br_ref, wi_ref, bi_ref, lam_ref,
                     o_ref, ext_ref, a_ref, b_ref, h_ref, *, ts, n_grp, gw, conv_w):
    s = pl.program_id(1)
    w = xb_ref.shape[1]

    @pl.when(s == 0)
    def _():
        ext_ref[0:SUBLANES, :] = jnp.zeros((SUBLANES, w), F32)
        h_ref[...] = jnp.zeros_like(h_ref)

    ext_ref[SUBLANES:SUBLANES + ts, :] = xb_ref[...]
    xc = cb_ref[...] + cw_ref[conv_w - 1:conv_w, :] * xb_ref[...]
    for k in range(1, conv_w):
        xc = xc + cw_ref[conv_w - 1 - k:conv_w - k, :] * ext_ref[SUBLANES - k:SUBLANES - k + ts, :]
    ext_ref[0:SUBLANES, :] = ext_ref[ts:ts + SUBLANES, :]

    z = -lam_ref[...]
    softplus = jnp.maximum(z, 0.0) + jnp.log1p(jnp.exp(-jnp.abs(z)))
    c = -LRU_C * softplus
    for g in range(n_grp):
        sl = slice(g * gw, (g + 1) * gw)
        xg = xc[:, sl]
        xg16 = xg.astype(BF16)
        r = jax.nn.sigmoid(jnp.dot(xg16, wr_ref[g], preferred_element_type=F32) + br_ref[:, sl])
        ig = jax.nn.sigmoid(jnp.dot(xg16, wi_ref[g], preferred_element_type=F32) + bi_ref[:, sl])
        log_a = c[:, sl] * r
        a_ref[:, sl] = jnp.exp(log_a)
        th = jnp.tanh(log_a)
        b_ref[:, sl] = jnp.sqrt(-2.0 * th / (1.0 - th)) * ig * xg

    rows = lax.broadcasted_iota(jnp.int32, (SUBLANES, w), 0)

    def chunk(ci, h):
        r0 = pl.multiple_of(ci * SUBLANES, SUBLANES)
        a = a_ref[pl.ds(r0, SUBLANES), :]
        b = b_ref[pl.ds(r0, SUBLANES), :]
        for sh in (1, 2, 4):
            keep = rows >= sh
            a_sh = jnp.where(keep, pltpu.roll(a, sh, 0), 1.0)
            b_sh = jnp.where(keep, pltpu.roll(b, sh, 0), 0.0)
            b = a * b_sh + b
            a = a * a_sh
        hc = a * h + b
        b_ref[pl.ds(r0, SUBLANES), :] = hc * gate_ref[pl.ds(r0, SUBLANES), :]
        return jnp.broadcast_to(hc[SUBLANES - 1:SUBLANES, :], (SUBLANES, w))

    h_ref[...] = lax.fori_loop(0, ts // SUBLANES, chunk, h_ref[...])
    o_ref[...] = b_ref[...].astype(o_ref.dtype)


def _lru_scan(u, conv_w, conv_b, w_r, b_r, w_i, b_i, lam, bsz, seq):
    t = u.shape[0]
    w = u.shape[1] // 2
    n_grp, gw = w_r.shape[0], w_r.shape[1]
    cw = conv_w.shape[0]
    ts = _tile(seq, 256)
    s_tiles = seq // ts
    row = lambda b, s: (b * s_tiles + s, 0)
    vec = pl.BlockSpec((1, w), lambda b, s: (0, 0))
    return pl.pallas_call(
        functools.partial(_lru_scan_kernel, ts=ts, n_grp=n_grp, gw=gw, conv_w=cw),
        grid=(bsz, s_tiles),
        in_specs=[pl.BlockSpec((ts, w), row),
                  pl.BlockSpec((ts, w), lambda b, s: (b * s_tiles + s, 1)),
                  pl.BlockSpec((cw, w), lambda b, s: (0, 0)),
                  vec,
                  pl.BlockSpec((n_grp, gw, gw), lambda b, s: (0, 0, 0)),
                  vec,
                  pl.BlockSpec((n_grp, gw, gw), lambda b, s: (0, 0, 0)),
                  vec,
                  vec],
        out_specs=pl.BlockSpec((ts, w), row),
        out_shape=jax.ShapeDtypeStruct((t, w), BF16),
        scratch_shapes=[pltpu.VMEM((ts + SUBLANES, w), F32),
                        pltpu.VMEM((ts, w), F32),
                        pltpu.VMEM((ts, w), F32),
                        pltpu.VMEM((SUBLANES, w), F32)],
        compiler_params=_cparams(("parallel", "arbitrary")),
        name="lru_scan",
    )(u, u, conv_w, conv_b.reshape(1, w), w_r, b_r.reshape(1, w), w_i, b_i.reshape(1, w),
      lam.reshape(1, w))


def _attn_kernel(q_ref, k_ref, v_ref, o_ref, kaug_ref, vaug_t_ref, *, nb, blk, dh, topk):
    seq = nb * blk
    nbp = vaug_t_ref.shape[0] - dh
    first = (pl.program_id(0) == 0) & (pl.program_id(1) == 0)

    @pl.when(first)
    def _():
        row = lax.broadcasted_iota(jnp.int32, (seq, LANES), 0)
        lane = lax.broadcasted_iota(jnp.int32, (seq, LANES), 1)
        kaug_ref[:, dh:dh + LANES] = (lane == row // blk).astype(BF16)
        vaug_t_ref[dh:dh + nbp, :] = (lax.broadcasted_iota(jnp.int32, (nbp, seq), 0) == 0).astype(BF16)

    kaug_ref[:, 0:dh] = k_ref[...]
    for j in range(nb):
        vaug_t_ref[0:dh, j * blk:(j + 1) * blk] = v_ref[j * blk:(j + 1) * blk, :].astype(F32).T.astype(BF16)
    kmean = jnp.sum(k_ref[...].astype(F32).reshape(nb, blk, dh), axis=1) * (1.0 / blk)
    if nbp > nb:
        kmean = jnp.concatenate([kmean, jnp.zeros((nbp - nb, dh), F32)], axis=0)
    kmean = kmean.astype(BF16)

    c = (dh ** -0.5) * 1.4426950408889634
    blk_id = lax.broadcasted_iota(jnp.int32, (nbp, blk), 0)
    key_pos = lax.broadcasted_iota(jnp.int32, (blk, blk), 0)
    q_pos = lax.broadcasted_iota(jnp.int32, (blk, blk), 1)
    zeros_pad = jnp.zeros((LANES - nbp, blk), BF16)

    def scores(i):
        q_t = q_ref[i * blk:(i + 1) * blk, :].astype(F32).T.astype(BF16)
        gate_t = jnp.dot(kmean, q_t, preferred_element_type=F32)
        g = jnp.where(blk_id < i, gate_t, NEG_INF)
        sel = blk_id == i
        for _ in range(topk):
            m = jnp.max(g, axis=0, keepdims=True)
            first_max = jnp.min(jnp.where(g == m, blk_id, nbp), axis=0, keepdims=True)
            hit = blk_id == first_max
            sel = sel | (hit & (blk_id < i))
            g = jnp.where(hit, -jnp.inf, g)
        bias_t = jnp.where(sel, 0.0, NEG_INF).astype(BF16)
        q_aug_t = jnp.concatenate([q_t, bias_t, zeros_pad], axis=0)
        return jnp.dot(kaug_ref[0:(i + 1) * blk, :], q_aug_t, preferred_element_type=F32)

    def softmax_pv(i, s):
        s_own = jnp.where(key_pos <= q_pos, s[i * blk:, :], NEG_INF)
        m = jnp.max(s_own, axis=0, keepdims=True)
        if i > 0:
            s_past = s[:i * blk, :]
            m = jnp.maximum(m, jnp.max(s_past, axis=0, keepdims=True))
            p = jnp.concatenate([jnp.exp2((s_past - m) * c), jnp.exp2((s_own - m) * c)], axis=0)
        else:
            p = jnp.exp2((s_own - m) * c)
        out_t = jnp.dot(vaug_t_ref[:, 0:(i + 1) * blk], p.astype(BF16), preferred_element_type=F32)
        o_t = out_t[0:dh, :] / out_t[dh:dh + 1, :]
        o_ref[i * blk:(i + 1) * blk, :] = o_t.T.astype(o_ref.dtype)

    s_prev = scores(0)
    for i in range(1, nb):
        s_next = scores(i)
        softmax_pv(i - 1, s_prev)
        s_prev = s_next
    softmax_pv(nb - 1, s_prev)


def _moba_attention(qkv, bsz, seq, d):
    t = qkv.shape[0]
    dh = d // N_HEADS
    blk = MOBA_BLOCK
    nb = seq // blk
    assert seq % blk == 0 and nb <= BF16_SUBLANES and dh == LANES
    return pl.pallas_call(
        functools.partial(_attn_kernel, nb=nb, blk=blk, dh=dh, topk=min(MOBA_TOPK, nb)),
        grid=(bsz, N_HEADS),
        in_specs=[pl.BlockSpec((seq, dh), lambda b, h: (b, h)),
                  pl.BlockSpec((seq, dh), lambda b, h: (b, N_HEADS + h)),
                  pl.BlockSpec((seq, dh), lambda b, h: (b, 2 * N_HEADS + h))],
        out_specs=pl.BlockSpec((seq, dh), lambda b, h: (b, h)),
        out_shape=jax.ShapeDtypeStruct((t, d), BF16),
        scratch_shapes=[pltpu.VMEM((seq, dh + LANES), BF16),
                        pltpu.VMEM((dh + BF16_SUBLANES, seq), BF16)],
        compiler_params=_cparams(("arbitrary", "arbitrary")),
        name="moba_attention",
    )(qkv, qkv, qkv)


def _router_kernel(h_ref, g_ref, w_ref, b_ref, xn_ref, et_ref, rf_ref, *, n_grp, epg):
    xn = _rmsnorm_rows(h_ref[...], g_ref[...])
    xn_ref[...] = xn
    logits = jnp.dot(xn.astype(BF16), w_ref[...], preferred_element_type=F32) + b_ref[...]
    lane = lax.broadcasted_iota(jnp.int32, logits.shape, 1)

    lg = jnp.where(lane < n_grp, logits, -jnp.inf)
    mg = jnp.max(lg, axis=1, keepdims=True)
    gidx = jnp.min(jnp.where(lg == mg, lane, LANES), axis=1, keepdims=True)
    pg_sel = 1.0 / jnp.sum(jnp.exp(lg - mg), axis=1, keepdims=True)

    lo = n_grp + gidx * epg
    in_grp = (lane >= lo) & (lane < lo + epg)
    le = jnp.where(in_grp, logits, -jnp.inf)
    e = jnp.exp(le - jnp.max(le, axis=1, keepdims=True))
    pe = jnp.where(in_grp, e / jnp.sum(e, axis=1, keepdims=True), -1.0)
    v1 = jnp.max(pe, axis=1, keepdims=True)
    i1 = jnp.min(jnp.where(pe == v1, lane, LANES), axis=1, keepdims=True)
    pe2 = jnp.where(lane == i1, -1.0, pe)
    v2 = jnp.max(pe2, axis=1, keepdims=True)
    i2 = jnp.min(jnp.where(pe2 == v2, lane, LANES), axis=1, keepdims=True)
    den = v1 + v2
    ids = jnp.where(lane == 0, i1 - n_grp, jnp.where(lane == 1, i2 - n_grp, 0)).astype(F32)
    et_ref[...] = ids.T[0:SUBLANES, :].astype(jnp.int32)
    rf_ref[...] = jnp.where(lane == 0, pg_sel * v1 / den, jnp.where(lane == 1, pg_sel * v2 / den, 0.0))


def _router(h, g, w_cat, b_cat, n_grp, epg):
    t, d = h.shape
    tm = _tile(t, 512)
    return pl.pallas_call(
        functools.partial(_router_kernel, n_grp=n_grp, epg=epg),
        grid=(t // tm,),
        in_specs=[pl.BlockSpec((tm, d), lambda i: (i, 0)),
                  pl.BlockSpec((1, d), lambda i: (0, 0)),
                  pl.BlockSpec((d, LANES), lambda i: (0, 0)),
                  pl.BlockSpec((1, LANES), lambda i: (0, 0))],
        out_specs=[pl.BlockSpec((tm, d), lambda i: (i, 0)),
                   pl.BlockSpec((SUBLANES, tm), lambda i: (0, i)),
                   pl.BlockSpec((tm, LANES), lambda i: (i, 0))],
        out_shape=[jax.ShapeDtypeStruct((t, d), F32),
                   jax.ShapeDtypeStruct((SUBLANES, t), jnp.int32),
                   jax.ShapeDtypeStruct((t, LANES), F32)],
        compiler_params=_cparams(("parallel",)),
        name="moe_router",
    )(h, g.reshape(1, d), w_cat, b_cat)


DISPATCH_CHUNK = 512
DISPATCH_SMEM_CHUNK = 4096


def _dispatch_kernel(et_ref, tok_ref, dst_ref, blk_ref, rank_ref, dest_ref, dest_sm, rows_sm, rows_vm,
                     sem, *, n_exp, topk, rb, n_rb):
    t = et_ref.shape[1]
    c = DISPATCH_CHUNK
    e_iota = lax.broadcasted_iota(jnp.int32, (n_exp, c), 0)
    earlier = (lax.broadcasted_iota(jnp.int32, (c, c), 0) < lax.broadcasted_iota(jnp.int32, (c, c), 1))
    earlier = jnp.where(earlier, 1.0, 0.0).astype(BF16)

    run = jnp.zeros((n_exp, 1), F32)
    for k in range(topk):
        def rank_chunk(j, run, k=k):
            off = pl.multiple_of(j * c, c)
            oh = e_iota == et_ref[k:k + 1, pl.ds(off, c)]
            ohf = jnp.where(oh, 1.0, 0.0)
            pre = jnp.dot(ohf.astype(BF16), earlier, preferred_element_type=F32)
            rank_ref[k:k + 1, pl.ds(off, c)] = jnp.sum(jnp.where(oh, pre + run, 0.0), axis=0, keepdims=True)
            return run + jnp.sum(ohf, axis=1, keepdims=True)
        run = lax.fori_loop(0, t // c, rank_chunk, run)

    cnt = run
    nblk = jnp.floor((cnt + (rb - 1)) * (1.0 / rb))
    before = (lax.broadcasted_iota(jnp.int32, (n_exp, n_exp), 1) < lax.broadcasted_iota(jnp.int32, (n_exp, n_exp), 0))
    before = jnp.where(before, 1.0, 0.0).astype(BF16)
    bstart = jnp.dot(before, jnp.broadcast_to(nblk, (n_exp, LANES)).astype(BF16),
                     preferred_element_type=F32)[:, 0:1]

    for k in range(topk):
        def dest_chunk(j, carry, k=k):
            off = pl.multiple_of(j * c, c)
            oh = e_iota == et_ref[k:k + 1, pl.ds(off, c)]
            base = jnp.sum(jnp.where(oh, bstart, 0.0), axis=0, keepdims=True) * rb
            dest_ref[k:k + 1, pl.ds(off, c)] = (base + rank_ref[k:k + 1, pl.ds(off, c)]).astype(jnp.int32)
            return carry
        lax.fori_loop(0, t // c, dest_chunk, 0)

    rows_vm[...] = jnp.full(rows_vm.shape, -1, jnp.int32)
    fill = pltpu.make_async_copy(rows_vm, rows_sm, sem)
    fill.start()
    fill.wait()
    sc = DISPATCH_SMEM_CHUNK
    for k in range(topk):
        def scatter_chunk(j, carry, k=k):
            off = pl.multiple_of(j * sc, sc)
            cp = pltpu.make_async_copy(dest_ref.at[pl.ds(k, 1), pl.ds(off, sc)], dest_sm, sem)
            cp.start()
            cp.wait()

            def one(a, carry2):
                d = dest_sm[0, a]
                rows_sm[lax.shift_right_logical(d, LANE_SHIFT), d & (LANES - 1)] = k * t + off + a
                return carry2
            return lax.fori_loop(0, sc, one, carry, unroll=8)
        lax.fori_loop(0, t // sc, scatter_chunk, 0)
    back = pltpu.make_async_copy(rows_sm, rows_vm, sem)
    back.start()
    back.wait()

    a = rows_vm[...]
    valid = a >= 0
    tok = a
    for k in range(1, topk):
        tok = jnp.where(a >= k * t, a - k * t, tok)
    flat = (lax.broadcasted_iota(jnp.int32, a.shape, 0) * LANES + lax.broadcasted_iota(jnp.int32, a.shape, 1))
    scratch_dst = topk * t + ((flat // rb) % 2) * rb + flat % rb
    tok_ref[...] = jnp.where(valid, tok, 0)
    dst_ref[...] = jnp.where(valid, a, scratch_dst)

    nbp = blk_ref.shape[1]
    b_row = lax.broadcasted_iota(jnp.int32, (1, nbp), 1).astype(F32)
    bend = bstart + nblk
    blk_e = jnp.minimum(jnp.sum(jnp.where(bend <= b_row, 1.0, 0.0), axis=0, keepdims=True), n_exp - 1.0)
    mine = lax.broadcasted_iota(jnp.int32, (n_exp, nbp), 0).astype(F32) == blk_e
    cnt_b = jnp.sum(jnp.where(mine, cnt, 0.0), axis=0, keepdims=True)
    bstart_b = jnp.sum(jnp.where(mine, bstart, 0.0), axis=0, keepdims=True)
    blk_n = jnp.clip(cnt_b - (b_row - bstart_b) * rb, 0.0, float(rb))
    row = lax.broadcasted_iota(jnp.int32, blk_ref.shape, 0)
    blk_ref[...] = jnp.where(row == 0, blk_e, jnp.where(row == 1, blk_n, 0.0)).astype(jnp.int32)


def _dispatch(et, n_exp, rb):
    t = et.shape[1]
    topk = EXPERT_TOPK
    n_asg = topk * t
    assert n_asg % rb == 0 and t % DISPATCH_SMEM_CHUNK == 0 and (rb & (rb - 1)) == 0 and rb % LANES == 0
    n_rb = n_asg // rb + n_exp
    n_rows = n_rb * rb
    nbp = -(-n_rb // LANES) * LANES
    tok, dst, blk = pl.pallas_call(
        functools.partial(_dispatch_kernel, n_exp=n_exp, topk=topk, rb=rb, n_rb=n_rb),
        in_specs=[pl.BlockSpec(memory_space=pltpu.VMEM)],
        out_specs=[pl.BlockSpec(memory_space=pltpu.VMEM)] * 3,
        out_shape=[jax.ShapeDtypeStruct((n_rows // LANES, LANES), jnp.int32),
                   jax.ShapeDtypeStruct((n_rows // LANES, LANES), jnp.int32),
                   jax.ShapeDtypeStruct((SUBLANES, nbp), jnp.int32)],
        scratch_shapes=[pltpu.VMEM((SUBLANES, t), F32),
                        pltpu.VMEM((SUBLANES, t), jnp.int32),
                        pltpu.SMEM((1, DISPATCH_SMEM_CHUNK), jnp.int32),
                        pltpu.SMEM((n_rows // LANES, LANES), jnp.int32),
                        pltpu.VMEM((n_rows // LANES, LANES), jnp.int32),
                        pltpu.SemaphoreType.DMA(())],
        compiler_params=pltpu.CompilerParams(vmem_limit_bytes=VMEM_LIMIT_BYTES),
        name="moe_dispatch",
    )(et)
    return tok.reshape(n_rb, rb), dst.reshape(n_rb, rb), blk[0, :n_rb], blk[1, :n_rb]


def _expert_kernel(blk_e, blk_n, tok_ref, tok_next_ref, dst_prev_ref, xn_hbm, wg_ref, wu_ref, wd_ref,
                   y_hbm, xbuf, obuf, gsem, ssem, *, n_rb, rb):
    del blk_e
    i = pl.program_id(0)
    slot = lax.rem(i, 2)
    n = blk_n[i]

    def gather_row(tref, sl, r):
        return pltpu.make_async_copy(xn_hbm.at[pl.ds(tref[0, 0, r], 1)], xbuf.at[sl, pl.ds(r, 1)],
                                     gsem.at[sl])

    def start_gather(tref, sl):
        def body(r, c):
            gather_row(tref, sl, r).start()
            return c
        lax.fori_loop(0, rb, body, 0, unroll=8)

    def scatter_row(sl, r, dst):
        return pltpu.make_async_copy(obuf.at[sl, pl.ds(r, 1)], y_hbm.at[pl.ds(dst, 1)], ssem.at[sl])

    def wait_scatter(sl):
        pltpu.make_async_copy(obuf.at[sl], y_hbm.at[pl.ds(0, rb)], ssem.at[sl]).wait()

    @pl.when(i == 0)
    def _():
        obuf[...] = jnp.zeros_like(obuf)
        n_real = y_hbm.shape[0] - 2 * rb
        for sl in range(2):
            init = pltpu.make_async_copy(obuf.at[sl], y_hbm.at[pl.ds(n_real + sl * rb, rb)], ssem.at[sl])
            init.start()
            init.wait()

    n_prev = blk_n[jnp.maximum(i - 1, 0)]
    n_next = blk_n[jnp.minimum(i + 1, n_rb - 1)]
    do_gather = (i + 1 < n_rb) & (n_next > 0)
    do_scatter = (i >= 1) & (n_prev > 0)

    @pl.when((i == 0) & (n > 0))
    def _():
        start_gather(tok_ref, 0)

    @pl.when((i >= 2) & (blk_n[jnp.maximum(i - 2, 0)] > 0))
    def _():
        wait_scatter(slot)

    @pl.when(n > 0)
    def _():
        pltpu.make_async_copy(xn_hbm.at[pl.ds(0, rb)], xbuf.at[slot], gsem.at[slot]).wait()

    d = xbuf.shape[2]
    f = wg_ref.shape[2]
    fc, dc = min(f, MXU_COLS), min(d, MXU_COLS)
    n_pieces = f // fc + d // dc
    rows_per_piece = -(-rb // n_pieces)

    def compute(interleave_dma):
        piece = [0]

        def issue_rows():
            if interleave_dma:
                lo = piece[0] * rows_per_piece
                for r in range(lo, min(rb, lo + rows_per_piece)):
                    gather_row(tok_next_ref, 1 - slot, r).start()
                    scatter_row(1 - slot, r, dst_prev_ref[0, 0, r]).start()
            piece[0] += 1

        x = xbuf[slot].astype(BF16)
        hid = []
        for c in range(f // fc):
            cols = slice(c * fc, (c + 1) * fc)
            g = jnp.dot(x, wg_ref[0, :, cols], preferred_element_type=F32)
            u = jnp.dot(x, wu_ref[0, :, cols], preferred_element_type=F32)
            hid.append(((g * jax.nn.sigmoid(g)) * u).astype(BF16))
            issue_rows()
        hid = jnp.concatenate(hid, axis=1)
        for c in range(d // dc):
            cols = slice(c * dc, (c + 1) * dc)
            obuf[slot, :, cols] = jnp.dot(hid, wd_ref[0, :, cols], preferred_element_type=F32)
            issue_rows()

    fast = (n > 0) & do_gather & do_scatter

    @pl.when(fast)
    def _():
        compute(True)

    @pl.when(jnp.logical_not(fast))
    def _():
        @pl.when(do_gather)
        def _():
            start_gather(tok_next_ref, 1 - slot)

        @pl.when(do_scatter)
        def _():
            def body(r, c):
                scatter_row(1 - slot, r, dst_prev_ref[0, 0, r]).start()
                return c
            lax.fori_loop(0, rb, body, 0, unroll=8)

        @pl.when(n > 0)
        def _():
            compute(False)

    @pl.when((i == n_rb - 1) & do_scatter)
    def _():
        wait_scatter(1 - slot)


def _experts(xn, row_tok, row_dst, blk_e, blk_n, w_gate, w_up, w_down, n_slots):
    t, d = xn.shape
    f = w_gate.shape[2]
    n_rb, rb = row_tok.shape
    assert (n_slots - 2 * rb) % rb == 0
    tok3 = row_tok.reshape(n_rb, 1, rb)
    dst3 = row_dst.reshape(n_rb, 1, rb)
    smem_blk = lambda f_: pl.BlockSpec((1, 1, rb), f_, memory_space=pltpu.SMEM)
    grid_spec = pltpu.PrefetchScalarGridSpec(
        num_scalar_prefetch=2,
        grid=(n_rb,),
        in_specs=[smem_blk(lambda i, be, bn: (i, 0, 0)),
                  smem_blk(lambda i, be, bn: (jnp.minimum(i + 1, n_rb - 1), 0, 0)),
                  smem_blk(lambda i, be, bn: (jnp.maximum(i - 1, 0), 0, 0)),
                  pl.BlockSpec(memory_space=pl.ANY),
                  pl.BlockSpec((1, d, f), lambda i, be, bn: (be[i], 0, 0)),
                  pl.BlockSpec((1, d, f), lambda i, be, bn: (be[i], 0, 0)),
                  pl.BlockSpec((1, f, d), lambda i, be, bn: (be[i], 0, 0))],
        out_specs=pl.BlockSpec(memory_space=pl.ANY),
        scratch_shapes=[pltpu.VMEM((2, rb, d), F32),
                        pltpu.VMEM((2, rb, d), F32),
                        pltpu.SemaphoreType.DMA((2,)),
                        pltpu.SemaphoreType.DMA((2,))],
    )
    return pl.pallas_call(
        functools.partial(_expert_kernel, n_rb=n_rb, rb=rb),
        grid_spec=grid_spec,
        out_shape=jax.ShapeDtypeStruct((n_slots, d), F32),
        compiler_params=_cparams(("arbitrary",)),
        name="moe_experts",
    )(blk_e, blk_n, tok3, tok3, dst3, xn, w_gate, w_up, w_down)


def _combine_kernel(h_ref, y0_ref, y1_ref, rf_ref, *rest, final):
    if final:
        g_ref, o_ref = rest
    else:
        (o_ref,) = rest
    w = rf_ref[...]
    out = h_ref[...] + (w[:, 0:1] * y0_ref[...] + w[:, 1:2] * y1_ref[...])
    if final:
        out = _rmsnorm_rows(out, g_ref[...])
    o_ref[...] = out


def _combine(h, y, rf, final_g):
    t, d = h.shape
    tm = _tile(t, 512)
    final = final_g is not None
    in_specs = [pl.BlockSpec((tm, d), lambda i: (i, 0)),
                pl.BlockSpec((tm, d), lambda i: (i, 0)),
                pl.BlockSpec((tm, d), lambda i: (t // tm + i, 0)),
                pl.BlockSpec((tm, LANES), lambda i: (i, 0))]
    args = [h, y, y, rf]
    if final:
        in_specs.append(pl.BlockSpec((1, d), lambda i: (0, 0)))
        args.append(final_g.reshape(1, d))
    return pl.pallas_call(
        functools.partial(_combine_kernel, final=final),
        grid=(t // tm,),
        in_specs=in_specs,
        out_specs=pl.BlockSpec((tm, d), lambda i: (i, 0)),
        out_shape=jax.ShapeDtypeStruct((t, d), F32),
        compiler_params=_cparams(("parallel",)),
        name="moe_combine",
    )(*args)


def _hier_moe(h, norm_g, w_grp, b_grp, w_exp, b_exp, w_gate, w_up, w_down, final_g):
    t, d = h.shape
    n_grp = w_grp.shape[1]
    n_exp = w_exp.shape[1]
    epg = n_exp // n_grp
    assert n_grp + n_exp <= LANES
    pad = LANES - n_grp - n_exp
    w_cat = jnp.concatenate([w_grp, w_exp, jnp.zeros((d, pad), F32)], axis=1).astype(BF16)
    b_cat = jnp.concatenate([b_grp, b_exp, jnp.zeros((pad,), F32)]).reshape(1, LANES)
    xn, et, rf = _router(h, norm_g, w_cat, b_cat, n_grp, epg)
    row_tok, row_dst, blk_e, blk_n = _dispatch(et, n_exp, ROW_BLOCK)
    y = _experts(xn, row_tok, row_dst, blk_e, blk_n, w_gate.astype(BF16), w_up.astype(BF16),
                 w_down.astype(BF16), EXPERT_TOPK * t + 2 * ROW_BLOCK)
    return _combine(h, y, rf, final_g)


def _rope_tables(seq, dh):
    inv = 1.0 / (ROPE_THETA ** (jnp.arange(0, dh, 2, dtype=F32) / dh))
    ang = jnp.arange(seq, dtype=F32)[:, None] * inv[None, :]
    cos, sin = jnp.cos(ang), jnp.sin(ang)
    return jnp.concatenate([cos, cos], axis=1), jnp.concatenate([-sin, sin], axis=1)


def kernel(x, lru_norm, lru_w_in, lru_b_in, lru_conv_w, lru_conv_b, lru_w_r, lru_b_r, lru_w_i, lru_b_i, lru_lambda, lru_w_out, lru_b_out, att_norm, att_w_qkv, att_w_o, ffn_norm, moe_w_grp, moe_b_grp, moe_w_exp, moe_b_exp, moe_w_gate, moe_w_up, moe_w_down, final_norm):
    bsz, seq, d = x.shape
    depth = ffn_norm.shape[0]
    n_mixers = 2
    cc, ss = _rope_tables(seq, d // N_HEADS)
    h = x.reshape(bsz * seq, d)
    for layer in range(depth):
        j = layer // n_mixers
        if layer % n_mixers == 0:
            u = _lru_in(h, lru_norm[j], lru_w_in[j].astype(BF16), lru_b_in[j])
            hg = _lru_scan(u, lru_conv_w[j], lru_conv_b[j], lru_w_r[j].astype(BF16), lru_b_r[j],
                           lru_w_i[j].astype(BF16), lru_b_i[j], lru_lambda[j], bsz, seq)
            h = _matmul_residual(hg, lru_w_out[j].astype(BF16), lru_b_out[j], h)
        else:
            qkv = _qkv_rope(h, att_norm[j], att_w_qkv[j].astype(BF16), cc, ss, seq)
            o = _moba_attention(qkv, bsz, seq, d)
            h = _matmul_residual(o, att_w_o[j].astype(BF16), None, h)
        final_g = final_norm if layer == depth - 1 else None
        h = _hier_moe(h, ffn_norm[layer], moe_w_grp[layer], moe_b_grp[layer], moe_w_exp[layer],
                      moe_b_exp[layer], moe_w_gate[layer], moe_w_up[layer], moe_w_down[layer], final_g)
    return h.reshape(bsz, seq, d)
```

```python
import functools

import jax
import jax.numpy as jnp
from jax import lax
from jax.experimental import pallas as pl
from jax.experimental.pallas import tpu as pltpu

F32 = jnp.float32
BF16 = jnp.bfloat16

NORM_EPS = 1e-6
NEG_INF = -1e30
LRU_C = 8.0
N_HEADS = 16
MOBA_BLOCK = 256
MOBA_TOPK = 3
ROPE_THETA = 10000.0
EXPERT_TOPK = 2
ROW_BLOCK = 256
LANES = 128
LANE_SHIFT = 7
SUBLANES = 8
BF16_SUBLANES = 16
MXU_COLS = 256
VMEM_LIMIT_BYTES = 56 * 1024 * 1024


def _tile(n, pref):
    t = min(n, pref)
    while n % t:
        t //= 2
    return t


def _cparams(sem):
    return pltpu.CompilerParams(dimension_semantics=sem, vmem_limit_bytes=VMEM_LIMIT_BYTES)


def _rmsnorm_rows(x, g):
    ms = jnp.mean(x * x, axis=-1, keepdims=True)
    return x * lax.rsqrt(ms + NORM_EPS) * g


def _lru_in_kernel(x_ref, g_ref, w_ref, b_ref, o_ref, xn_ref, *, n_plain):
    j = pl.program_id(1)

    @pl.when(j == 0)
    def _():
        xn_ref[...] = _rmsnorm_rows(x_ref[...], g_ref[...]).astype(BF16)

    acc = jnp.dot(xn_ref[...], w_ref[...], preferred_element_type=F32) + b_ref[...]

    @pl.when(j < n_plain)
    def _():
        o_ref[...] = acc

    @pl.when(j >= n_plain)
    def _():
        o_ref[...] = jax.nn.gelu(acc, approximate=True)


def _lru_in(h, g, w, b):
    t, d = h.shape
    n = w.shape[1]
    tm, tn = _tile(t, 1024), _tile(n // 2, 1024)
    return pl.pallas_call(
        functools.partial(_lru_in_kernel, n_plain=(n // 2) // tn),
        grid=(t // tm, n // tn),
        in_specs=[pl.BlockSpec((tm, d), lambda i, j: (i, 0)),
                  pl.BlockSpec((1, d), lambda i, j: (0, 0)),
                  pl.BlockSpec((d, tn), lambda i, j: (0, j)),
                  pl.BlockSpec((1, tn), lambda i, j: (0, j))],
        out_specs=pl.BlockSpec((tm, tn), lambda i, j: (i, j)),
        out_shape=jax.ShapeDtypeStruct((t, n), F32),
        scratch_shapes=[pltpu.VMEM((tm, d), BF16)],
        compiler_params=_cparams(("parallel", "arbitrary")),
        name="lru_in",
    )(h, g.reshape(1, d), w, b.reshape(1, n))


def _qkv_kernel(x_ref, g_ref, w_ref, cc_ref, ss_ref, o_ref, xn_ref, *, n_rope, heads_per_tile, dh):
    j = pl.program_id(1)

    @pl.when(j == 0)
    def _():
        xn_ref[...] = _rmsnorm_rows(x_ref[...], g_ref[...]).astype(BF16)

    acc = jnp.dot(xn_ref[...], w_ref[...], preferred_element_type=F32)

    @pl.when(j < n_rope)
    def _():
        cc = cc_ref[...]
        ss = ss_ref[...]
        for hh in range(heads_per_tile):
            a = acc[:, hh * dh:(hh + 1) * dh]
            o_ref[:, hh * dh:(hh + 1) * dh] = (a * cc + pltpu.roll(a, dh // 2, 1) * ss).astype(o_ref.dtype)

    @pl.when(j >= n_rope)
    def _():
        o_ref[...] = acc.astype(o_ref.dtype)


def _qkv_rope(h, g, w, cc, ss, seq):
    t, d = h.shape
    n = w.shape[1]
    dh = d // N_HEADS
    tm, tn = _tile(seq, 1024), _tile(d, 1024)
    s_tiles = seq // tm
    return pl.pallas_call(
        functools.partial(_qkv_kernel, n_rope=(2 * d) // tn, heads_per_tile=tn // dh, dh=dh),
        grid=(t // tm, n // tn),
        in_specs=[pl.BlockSpec((tm, d), lambda i, j: (i, 0)),
                  pl.BlockSpec((1, d), lambda i, j: (0, 0)),
                  pl.BlockSpec((d, tn), lambda i, j: (0, j)),
                  pl.BlockSpec((tm, dh), lambda i, j: (i % s_tiles, 0)),
                  pl.BlockSpec((tm, dh), lambda i, j: (i % s_tiles, 0))],
        out_specs=pl.BlockSpec((tm, tn), lambda i, j: (i, j)),
        out_shape=jax.ShapeDtypeStruct((t, n), BF16),
        scratch_shapes=[pltpu.VMEM((tm, d), BF16)],
        compiler_params=_cparams(("parallel", "arbitrary")),
        name="qkv_rope",
    )(h, g.reshape(1, d), w, cc, ss)


def _mm_res_kernel(a_ref, w_ref, *rest, has_bias):
    if has_bias:
        b_ref, r_ref, o_ref = rest
    else:
        r_ref, o_ref = rest
    y = jnp.dot(a_ref[...], w_ref[...], preferred_element_type=F32)
    if has_bias:
        y = y + b_ref[...]
    o_ref[...] = r_ref[...] + y


def _matmul_residual(a, w, b, res):
    t, k = a.shape
    n = w.shape[1]
    tm, tn = _tile(t, 1024), _tile(n, 1024)
    has_bias = b is not None
    in_specs = [pl.BlockSpec((tm, k), lambda i, j: (i, 0)),
                pl.BlockSpec((k, tn), lambda i, j: (0, j))]
    args = [a, w]
    if has_bias:
        in_specs.append(pl.BlockSpec((1, tn), lambda i, j: (0, j)))
        args.append(b.reshape(1, n))
    in_specs.append(pl.BlockSpec((tm, tn), lambda i, j: (i, j)))
    args.append(res)
    return pl.pallas_call(
        functools.partial(_mm_res_kernel, has_bias=has_bias),
        grid=(t // tm, n // tn),
        in_specs=in_specs,
        out_specs=pl.BlockSpec((tm, tn), lambda i, j: (i, j)),
        out_shape=jax.ShapeDtypeStruct((t, n), F32),
        compiler_params=_cparams(("parallel", "parallel")),
        name="matmul_residual",
    )(*args)


def _lru_scan_kernel(xb_ref, gate_ref, cw_ref, cb_ref, wr_ref, br_ref, wi_ref, bi_ref, lam_ref,
                     o_ref, ext_ref, a_ref, b_ref, h_ref, *, ts, n_grp, gw, conv_w):
    s = pl.program_id(1)
    w = xb_ref.shape[1]

    @pl.when(s == 0)
    def _():
        ext_ref[0:SUBLANES, :] = jnp.zeros((SUBLANES, w), F32)
        h_ref[...] = jnp.zeros_like(h_ref)

    ext_ref[SUBLANES:SUBLANES + ts, :] = xb_ref[...]
    xc = cb_ref[...] + cw_ref[conv_w - 1:conv_w, :] * xb_ref[...]
    for k in range(1, conv_w):
        xc = xc + cw_ref[conv_w - 1 - k:conv_w - k, :] * ext_ref[SUBLANES - k:SUBLANES - k + ts, :]
    ext_ref[0:SUBLANES, :] = ext_ref[ts:ts + SUBLANES, :]

    z = -lam_ref[...]
    softplus = jnp.maximum(z, 0.0) + jnp.log1p(jnp.exp(-jnp.abs(z)))
    c = -LRU_C * softplus
    for g in range(n_grp):
        sl = slice(g * gw, (g + 1) * gw)
        xg = xc[:, sl]
        xg16 = xg.astype(BF16)
        r = jax.nn.sigmoid(jnp.dot(xg16, wr_ref[g], preferred_element_type=F32) + br_ref[:, sl])
        ig = jax.nn.sigmoid(jnp.dot(xg16, wi_ref[g], preferred_element_type=F32) + bi_ref[:, sl])
        log_a = c[:, sl] * r
        a_ref[:, sl] = jnp.exp(log_a)
        th = jnp.tanh(log_a)
        b_ref[:, sl] = jnp.sqrt(-2.0 * th / (1.0 - th)) * ig * xg

    rows = lax.broadcasted_iota(jnp.int32, (SUBLANES, w), 0)

    def chunk(ci, h):
        r0 = pl.multiple_of(ci * SUBLANES, SUBLANES)
        a = a_ref[pl.ds(r0, SUBLANES), :]
        b = b_ref[pl.ds(r0, SUBLANES), :]
        for sh in (1, 2, 4):
            keep = rows >= sh
            a_sh = jnp.where(keep, pltpu.roll(a, sh, 0), 1.0)
            b_sh = jnp.where(keep, pltpu.roll(b, sh, 0), 0.0)
            b = a * b_sh + b
            a = a * a_sh
        hc = a * h + b
        b_ref[pl.ds(r0, SUBLANES), :] = hc * gate_ref[pl.ds(r0, SUBLANES), :]
        return jnp.broadcast_to(hc[SUBLANES - 1:SUBLANES, :], (SUBLANES, w))

    h_ref[...] = lax.fori_loop(0, ts // SUBLANES, chunk, h_ref[...])
    o_ref[...] = b_ref[...].astype(o_ref.dtype)


def _lru_scan(u, conv_w, conv_b, w_r, b_r, w_i, b_i, lam, bsz, seq):
    t = u.shape[0]
    w = u.shape[1] // 2
    n_grp, gw = w_r.shape[0], w_r.shape[1]
    cw = conv_w.shape[0]
    ts = _tile(seq, 256)
    s_tiles = seq // ts
    row = lambda b, s: (b * s_tiles + s, 0)
    vec = pl.BlockSpec((1, w), lambda b, s: (0, 0))
    return pl.pallas_call(
        functools.partial(_lru_scan_kernel, ts=ts, n_grp=n_grp, gw=gw, conv_w=cw),
        grid=(bsz, s_tiles),
        in_specs=[pl.BlockSpec((ts, w), row),
                  pl.BlockSpec((ts, w), lambda b, s: (b * s_tiles + s, 1)),
                  pl.BlockSpec((cw, w), lambda b, s: (0, 0)),
                  vec,
                  pl.BlockSpec((n_grp, gw, gw), lambda b, s: (0, 0, 0)),
                  vec,
                  pl.BlockSpec((n_grp, gw, gw), lambda b, s: (0, 0, 0)),
                  vec,
                  vec],
        out_specs=pl.BlockSpec((ts, w), row),
        out_shape=jax.ShapeDtypeStruct((t, w), BF16),
        scratch_shapes=[pltpu.VMEM((ts + SUBLANES, w), F32),
                        pltpu.VMEM((ts, w), F32),
                        pltpu.VMEM((ts, w), F32),
                        pltpu.VMEM((SUBLANES, w), F32)],
        compiler_params=_cparams(("parallel", "arbitrary")),
        name="lru_scan",
    )(u, u, conv_w, conv_b.reshape(1, w), w_r, b_r.reshape(1, w), w_i, b_i.reshape(1, w),
      lam.reshape(1, w))


def _attn_kernel(q_ref, k_ref, v_ref, o_ref, kaug_ref, vaug_t_ref, *, nb, blk, dh, topk):
    seq = nb * blk
    nbp = vaug_t_ref.shape[0] - dh
    first = (pl.program_id(0) == 0) & (pl.program_id(1) == 0)

    @pl.when(first)
    def _():
        row = lax.broadcasted_iota(jnp.int32, (seq, LANES), 0)
        lane = lax.broadcasted_iota(jnp.int32, (seq, LANES), 1)
        kaug_ref[:, dh:dh + LANES] = (lane == row // blk).astype(BF16)
        vaug_t_ref[dh:dh + nbp, :] = (lax.broadcasted_iota(jnp.int32, (nbp, seq), 0) == 0).astype(BF16)

    kaug_ref[:, 0:dh] = k_ref[...]
    for j in range(nb):
        vaug_t_ref[0:dh, j * blk:(j + 1) * blk] = v_ref[j * blk:(j + 1) * blk, :].astype(F32).T.astype(BF16)
    kmean = jnp.sum(k_ref[...].astype(F32).reshape(nb, blk, dh), axis=1) * (1.0 / blk)
    if nbp > nb:
        kmean = jnp.concatenate([kmean, jnp.zeros((nbp - nb, dh), F32)], axis=0)
    kmean = kmean.astype(BF16)

    c = (dh ** -0.5) * 1.4426950408889634
    blk_id = lax.broadcasted_iota(jnp.int32, (nbp, blk), 0)
    key_pos = lax.broadcasted_iota(jnp.int32, (blk, blk), 0)
    q_pos = lax.broadcasted_iota(jnp.int32, (blk, blk), 1)
    zeros_pad = jnp.zeros((LANES - nbp, blk), BF16)

    def scores(i):
        q_t = q_ref[i * blk:(i + 1) * blk, :].astype(F32).T.astype(BF16)
        gate_t = jnp.dot(kmean, q_t, preferred_element_type=F32)
        g = jnp.where(blk_id < i, gate_t, NEG_INF)
        sel = blk_id == i
        for _ in range(topk):
            m = jnp.max(g, axis=0, keepdims=True)
            first_max = jnp.min(jnp.where(g == m, blk_id, nbp), axis=0, keepdims=True)
            hit = blk_id == first_max
            sel = sel | (hit & (blk_id < i))
            g = jnp.where(hit, -jnp.inf, g)
        bias_t = jnp.where(sel, 0.0, NEG_INF).astype(BF16)
        q_aug_t = jnp.concatenate([q_t, bias_t, zeros_pad], axis=0)
        return jnp.dot(kaug_ref[0:(i + 1) * blk, :], q_aug_t, preferred_element_type=F32)

    def softmax_pv(i, s):
        s_own = jnp.where(key_pos <= q_pos, s[i * blk:, :], NEG_INF)
        m = jnp.max(s_own, axis=0, keepdims=True)
        if i > 0:
            s_past = s[:i * blk, :]
            m = jnp.maximum(m, jnp.max(s_past, axis=0, keepdims=True))
            p = jnp.concatenate([jnp.exp2((s_past - m) * c), jnp.exp2((s_own - m) * c)], axis=0)
        else:
            p = jnp.exp2((s_own - m) * c)
        out_t = jnp.dot(vaug_t_ref[:, 0:(i + 1) * blk], p.astype(BF16), preferred_element_type=F32)
        o_t = out_t[0:dh, :] / out_t[dh:dh + 1, :]
        o_ref[i * blk:(i + 1) * blk, :] = o_t.T.astype(o_ref.dtype)

    s_prev = scores(0)
    for i in range(1, nb):
        s_next = scores(i)
        softmax_pv(i - 1, s_prev)
        s_prev = s_next
    softmax_pv(nb - 1, s_prev)


def _moba_attention(qkv, bsz, seq, d):
    t = qkv.shape[0]
    dh = d // N_HEADS
    blk = MOBA_BLOCK
    nb = seq // blk
    assert seq % blk == 0 and nb <= BF16_SUBLANES and dh == LANES
    return pl.pallas_call(
        functools.partial(_attn_kernel, nb=nb, blk=blk, dh=dh, topk=min(MOBA_TOPK, nb)),
        grid=(bsz, N_HEADS),
        in_specs=[pl.BlockSpec((seq, dh), lambda b, h: (b, h)),
                  pl.BlockSpec((seq, dh), lambda b, h: (b, N_HEADS + h)),
                  pl.BlockSpec((seq, dh), lambda b, h: (b, 2 * N_HEADS + h))],
        out_specs=pl.BlockSpec((seq, dh), lambda b, h: (b, h)),
        out_shape=jax.ShapeDtypeStruct((t, d), BF16),
        scratch_shapes=[pltpu.VMEM((seq, dh + LANES), BF16),
                        pltpu.VMEM((dh + BF16_SUBLANES, seq), BF16)],
        compiler_params=_cparams(("arbitrary", "arbitrary")),
        name="moba_attention",
    )(qkv, qkv, qkv)


def _pack_bf16_pairs(lo, hi):
    lo_bits = lax.bitcast_convert_type(lo.astype(BF16).astype(F32), jnp.uint32)
    hi_bits = lax.bitcast_convert_type(hi.astype(BF16).astype(F32), jnp.uint32)
    return (hi_bits & jnp.uint32(0xFFFF0000)) | (lo_bits >> 16)


def _unpack_bf16_pairs(u):
    lo = lax.bitcast_convert_type(u << 16, F32)
    hi = lax.bitcast_convert_type(u & jnp.uint32(0xFFFF0000), F32)
    return lo, hi


def _router_kernel(h_ref, g_ref, w_ref, b_ref, xn_ref, et_ref, rf_ref, *, n_grp, epg):
    xn = _rmsnorm_rows(h_ref[...], g_ref[...])
    hd = xn.shape[1] // 2
    xn_ref[...] = _pack_bf16_pairs(xn[:, :hd], xn[:, hd:])
    logits = jnp.dot(xn.astype(BF16), w_ref[...], preferred_element_type=F32) + b_ref[...]
    lane = lax.broadcasted_iota(jnp.int32, logits.shape, 1)

    lg = jnp.where(lane < n_grp, logits, -jnp.inf)
    mg = jnp.max(lg, axis=1, keepdims=True)
    gidx = jnp.min(jnp.where(lg == mg, lane, LANES), axis=1, keepdims=True)
    pg_sel = 1.0 / jnp.sum(jnp.exp(lg - mg), axis=1, keepdims=True)

    lo = n_grp + gidx * epg
    in_grp = (lane >= lo) & (lane < lo + epg)
    le = jnp.where(in_grp, logits, -jnp.inf)
    e = jnp.exp(le - jnp.max(le, axis=1, keepdims=True))
    pe = jnp.where(in_grp, e / jnp.sum(e, axis=1, keepdims=True), -1.0)
    v1 = jnp.max(pe, axis=1, keepdims=True)
    i1 = jnp.min(jnp.where(pe == v1, lane, LANES), axis=1, keepdims=True)
    pe2 = jnp.where(lane == i1, -1.0, pe)
    v2 = jnp.max(pe2, axis=1, keepdims=True)
    i2 = jnp.min(jnp.where(pe2 == v2, lane, LANES), axis=1, keepdims=True)
    den = v1 + v2
    ids = jnp.where(lane == 0, i1 - n_grp, jnp.where(lane == 1, i2 - n_grp, 0)).astype(F32)
    et_ref[...] = ids.T[0:SUBLANES, :].astype(jnp.int32)
    rf_ref[...] = jnp.where(lane == 0, pg_sel * v1 / den, jnp.where(lane == 1, pg_sel * v2 / den, 0.0))


def _router(h, g, w_cat, b_cat, n_grp, epg):
    t, d = h.shape
    tm = _tile(t, 512)
    return pl.pallas_call(
        functools.partial(_router_kernel, n_grp=n_grp, epg=epg),
        grid=(t // tm,),
        in_specs=[pl.BlockSpec((tm, d), lambda i: (i, 0)),
                  pl.BlockSpec((1, d), lambda i: (0, 0)),
                  pl.BlockSpec((d, LANES), lambda i: (0, 0)),
                  pl.BlockSpec((1, LANES), lambda i: (0, 0))],
        out_specs=[pl.BlockSpec((tm, d // 2), lambda i: (i, 0)),
                   pl.BlockSpec((SUBLANES, tm), lambda i: (0, i)),
                   pl.BlockSpec((tm, LANES), lambda i: (i, 0))],
        out_shape=[jax.ShapeDtypeStruct((t, d // 2), jnp.uint32),
                   jax.ShapeDtypeStruct((SUBLANES, t), jnp.int32),
                   jax.ShapeDtypeStruct((t, LANES), F32)],
        compiler_params=_cparams(("parallel",)),
        name="moe_router",
    )(h, g.reshape(1, d), w_cat, b_cat)


DISPATCH_CHUNK = 512
DISPATCH_SMEM_CHUNK = 4096


def _dispatch_kernel(et_ref, tok_ref, dst_ref, blk_ref, rank_ref, dest_ref, dest_sm, rows_sm, rows_vm,
                     sem, *, n_exp, topk, rb, n_rb):
    t = et_ref.shape[1]
    c = DISPATCH_CHUNK
    e_iota = lax.broadcasted_iota(jnp.int32, (n_exp, c), 0)
    earlier = (lax.broadcasted_iota(jnp.int32, (c, c), 0) < lax.broadcasted_iota(jnp.int32, (c, c), 1))
    earlier = jnp.where(earlier, 1.0, 0.0).astype(BF16)

    run = jnp.zeros((n_exp, 1), F32)
    for k in range(topk):
        def rank_chunk(j, run, k=k):
            off = pl.multiple_of(j * c, c)
            oh = e_iota == et_ref[k:k + 1, pl.ds(off, c)]
            ohf = jnp.where(oh, 1.0, 0.0)
            pre = jnp.dot(ohf.astype(BF16), earlier, preferred_element_type=F32)
            rank_ref[k:k + 1, pl.ds(off, c)] = jnp.sum(jnp.where(oh, pre + run, 0.0), axis=0, keepdims=True)
            return run + jnp.sum(ohf, axis=1, keepdims=True)
        run = lax.fori_loop(0, t // c, rank_chunk, run)

    cnt = run
    nblk = jnp.floor((cnt + (rb - 1)) * (1.0 / rb))
    before = (lax.broadcasted_iota(jnp.int32, (n_exp, n_exp), 1) < lax.broadcasted_iota(jnp.int32, (n_exp, n_exp), 0))
    before = jnp.where(before, 1.0, 0.0).astype(BF16)
    bstart = jnp.dot(before, jnp.broadcast_to(nblk, (n_exp, LANES)).astype(BF16),
                     preferred_element_type=F32)[:, 0:1]

    for k in range(topk):
        def dest_chunk(j, carry, k=k):
            off = pl.multiple_of(j * c, c)
            oh = e_iota == et_ref[k:k + 1, pl.ds(off, c)]
            base = jnp.sum(jnp.where(oh, bstart, 0.0), axis=0, keepdims=True) * rb
            dest_ref[k:k + 1, pl.ds(off, c)] = (base + rank_ref[k:k + 1, pl.ds(off, c)]).astype(jnp.int32)
            return carry
        lax.fori_loop(0, t // c, dest_chunk, 0)

    rows_vm[...] = jnp.full(rows_vm.shape, -1, jnp.int32)
    fill = pltpu.make_async_copy(rows_vm, rows_sm, sem)
    fill.start()
    fill.wait()
    sc = DISPATCH_SMEM_CHUNK
    for k in range(topk):
        def scatter_chunk(j, carry, k=k):
            off = pl.multiple_of(j * sc, sc)
            cp = pltpu.make_async_copy(dest_ref.at[pl.ds(k, 1), pl.ds(off, sc)], dest_sm, sem)
            cp.start()
            cp.wait()

            def one(a, carry2):
                d = dest_sm[0, a]
                rows_sm[lax.shift_right_logical(d, LANE_SHIFT), d & (LANES - 1)] = k * t + off + a
                return carry2
            return lax.fori_loop(0, sc, one, carry, unroll=8)
        lax.fori_loop(0, t // sc, scatter_chunk, 0)
    back = pltpu.make_async_copy(rows_sm, rows_vm, sem)
    back.start()
    back.wait()

    a = rows_vm[...]
    valid = a >= 0
    tok = a
    for k in range(1, topk):
        tok = jnp.where(a >= k * t, a - k * t, tok)
    flat = (lax.broadcasted_iota(jnp.int32, a.shape, 0) * LANES + lax.broadcasted_iota(jnp.int32, a.shape, 1))
    scratch_dst = topk * t + ((flat // rb) % 2) * rb + flat % rb
    tok_ref[...] = jnp.where(valid, tok, 0)
    dst_ref[...] = jnp.where(valid, a, scratch_dst)

    nbp = blk_ref.shape[1]
    b_row = lax.broadcasted_iota(jnp.int32, (1, nbp), 1).astype(F32)
    bend = bstart + nblk
    blk_e = jnp.minimum(jnp.sum(jnp.where(bend <= b_row, 1.0, 0.0), axis=0, keepdims=True), n_exp - 1.0)
    mine = lax.broadcasted_iota(jnp.int32, (n_exp, nbp), 0).astype(F32) == blk_e
    cnt_b = jnp.sum(jnp.where(mine, cnt, 0.0), axis=0, keepdims=True)
    bstart_b = jnp.sum(jnp.where(mine, bstart, 0.0), axis=0, keepdims=True)
    blk_n = jnp.clip(cnt_b - (b_row - bstart_b) * rb, 0.0, float(rb))
    row = lax.broadcasted_iota(jnp.int32, blk_ref.shape, 0)
    blk_ref[...] = jnp.where(row == 0, blk_e, jnp.where(row == 1, blk_n, 0.0)).astype(jnp.int32)


def _dispatch(et, n_exp, rb):
    t = et.shape[1]
    topk = EXPERT_TOPK
    n_asg = topk * t
    assert n_asg % rb == 0 and t % DISPATCH_SMEM_CHUNK == 0 and (rb & (rb - 1)) == 0 and rb % LANES == 0
    n_rb = n_asg // rb + n_exp
    n_rows = n_rb * rb
    nbp = -(-n_rb // LANES) * LANES
    tok, dst, blk = pl.pallas_call(
        functools.partial(_dispatch_kernel, n_exp=n_exp, topk=topk, rb=rb, n_rb=n_rb),
        in_specs=[pl.BlockSpec(memory_space=pltpu.VMEM)],
        out_specs=[pl.BlockSpec(memory_space=pltpu.VMEM)] * 3,
        out_shape=[jax.ShapeDtypeStruct((n_rows // LANES, LANES), jnp.int32),
                   jax.ShapeDtypeStruct((n_rows // LANES, LANES), jnp.int32),
                   jax.ShapeDtypeStruct((SUBLANES, nbp), jnp.int32)],
        scratch_shapes=[pltpu.VMEM((SUBLANES, t), F32),
                        pltpu.VMEM((SUBLANES, t), jnp.int32),
                        pltpu.SMEM((1, DISPATCH_SMEM_CHUNK), jnp.int32),
                        pltpu.SMEM((n_rows // LANES, LANES), jnp.int32),
                        pltpu.VMEM((n_rows // LANES, LANES), jnp.int32),
                        pltpu.SemaphoreType.DMA(())],
        compiler_params=pltpu.CompilerParams(vmem_limit_bytes=VMEM_LIMIT_BYTES),
        name="moe_dispatch",
    )(et)
    return tok.reshape(n_rb, rb), dst.reshape(n_rb, rb), blk[0, :n_rb], blk[1, :n_rb]


def _expert_kernel(blk_e, blk_n, tok_ref, tok_next_ref, dst_prev_ref, xn_hbm, wg_ref, wu_ref, wd_ref,
                   y_hbm, xbuf, obuf, gsem, ssem, *, n_rb, rb):
    del blk_e
    i = pl.program_id(0)
    slot = lax.rem(i, 2)
    n = blk_n[i]

    def gather_row(tref, sl, r):
        return pltpu.make_async_copy(xn_hbm.at[pl.ds(tref[0, 0, r], 1)], xbuf.at[sl, pl.ds(r, 1)],
                                     gsem.at[sl])

    def start_gather(tref, sl):
        def body(r, c):
            gather_row(tref, sl, r).start()
            return c
        lax.fori_loop(0, rb, body, 0, unroll=8)

    def scatter_row(sl, r, dst):
        return pltpu.make_async_copy(obuf.at[sl, pl.ds(r, 1)], y_hbm.at[pl.ds(dst, 1)], ssem.at[sl])

    def wait_scatter(sl):
        pltpu.make_async_copy(obuf.at[sl], y_hbm.at[pl.ds(0, rb)], ssem.at[sl]).wait()

    @pl.when(i == 0)
    def _():
        obuf[...] = jnp.zeros_like(obuf)
        n_real = y_hbm.shape[0] - 2 * rb
        for sl in range(2):
            init = pltpu.make_async_copy(obuf.at[sl], y_hbm.at[pl.ds(n_real + sl * rb, rb)], ssem.at[sl])
            init.start()
            init.wait()

    n_prev = blk_n[jnp.maximum(i - 1, 0)]
    n_next = blk_n[jnp.minimum(i + 1, n_rb - 1)]
    do_gather = (i + 1 < n_rb) & (n_next > 0)
    do_scatter = (i >= 1) & (n_prev > 0)

    @pl.when((i == 0) & (n > 0))
    def _():
        start_gather(tok_ref, 0)

    @pl.when((i >= 2) & (blk_n[jnp.maximum(i - 2, 0)] > 0))
    def _():
        wait_scatter(slot)

    @pl.when(n > 0)
    def _():
        pltpu.make_async_copy(xn_hbm.at[pl.ds(0, rb)], xbuf.at[slot], gsem.at[slot]).wait()

    hd = xbuf.shape[2]
    f = wg_ref.shape[2]
    fc, dc = min(f, MXU_COLS), min(hd, MXU_COLS)
    n_pieces = f // fc + hd // dc
    rows_per_piece = -(-rb // n_pieces)

    def compute(interleave_dma):
        piece = [0]

        def issue_rows():
            if interleave_dma:
                lo = piece[0] * rows_per_piece
                for r in range(lo, min(rb, lo + rows_per_piece)):
                    gather_row(tok_next_ref, 1 - slot, r).start()
                    scatter_row(1 - slot, r, dst_prev_ref[0, 0, r]).start()
            piece[0] += 1

        x_lo, x_hi = _unpack_bf16_pairs(xbuf[slot])
        x = jnp.concatenate([x_lo.astype(BF16), x_hi.astype(BF16)], axis=1)
        hid = []
        for c in range(f // fc):
            cols = slice(c * fc, (c + 1) * fc)
            g = jnp.dot(x, wg_ref[0, :, cols], preferred_element_type=F32)
            u = jnp.dot(x, wu_ref[0, :, cols], preferred_element_type=F32)
            hid.append(((g * jax.nn.sigmoid(g)) * u).astype(BF16))
            issue_rows()
        hid = jnp.concatenate(hid, axis=1)
        for c in range(hd // dc):
            cols = slice(c * dc, (c + 1) * dc)
            cols_hi = slice(hd + c * dc, hd + (c + 1) * dc)
            y_lo = jnp.dot(hid, wd_ref[0, :, cols], preferred_element_type=F32)
            y_hi = jnp.dot(hid, wd_ref[0, :, cols_hi], preferred_element_type=F32)
            obuf[slot, :, cols] = _pack_bf16_pairs(y_lo, y_hi)
            issue_rows()

    fast = (n > 0) & do_gather & do_scatter

    @pl.when(fast)
    def _():
        compute(True)

    @pl.when(jnp.logical_not(fast))
    def _():
        @pl.when(do_gather)
        def _():
            start_gather(tok_next_ref, 1 - slot)

        @pl.when(do_scatter)
        def _():
            def body(r, c):
                scatter_row(1 - slot, r, dst_prev_ref[0, 0, r]).start()
                return c
            lax.fori_loop(0, rb, body, 0, unroll=8)

        @pl.when(n > 0)
        def _():
            compute(False)

    @pl.when((i == n_rb - 1) & do_scatter)
    def _():
        wait_scatter(1 - slot)


def _experts(xn, row_tok, row_dst, blk_e, blk_n, w_gate, w_up, w_down, n_slots):
    d, f = w_gate.shape[1:]
    assert xn.shape[1] * 2 == d
    n_rb, rb = row_tok.shape
    assert (n_slots - 2 * rb) % rb == 0
    tok3 = row_tok.reshape(n_rb, 1, rb)
    dst3 = row_dst.reshape(n_rb, 1, rb)
    smem_blk = lambda f_: pl.BlockSpec((1, 1, rb), f_, memory_space=pltpu.SMEM)
    grid_spec = pltpu.PrefetchScalarGridSpec(
        num_scalar_prefetch=2,
        grid=(n_rb,),
        in_specs=[smem_blk(lambda i, be, bn: (i, 0, 0)),
                  smem_blk(lambda i, be, bn: (jnp.minimum(i + 1, n_rb - 1), 0, 0)),
                  smem_blk(lambda i, be, bn: (jnp.maximum(i - 1, 0), 0, 0)),
                  pl.BlockSpec(memory_space=pl.ANY),
                  pl.BlockSpec((1, d, f), lambda i, be, bn: (be[i], 0, 0)),
                  pl.BlockSpec((1, d, f), lambda i, be, bn: (be[i], 0, 0)),
                  pl.BlockSpec((1, f, d), lambda i, be, bn: (be[i], 0, 0))],
        out_specs=pl.BlockSpec(memory_space=pl.ANY),
        scratch_shapes=[pltpu.VMEM((2, rb, d // 2), jnp.uint32),
                        pltpu.VMEM((2, rb, d // 2), jnp.uint32),
                        pltpu.SemaphoreType.DMA((2,)),
                        pltpu.SemaphoreType.DMA((2,))],
    )
    return pl.pallas_call(
        functools.partial(_expert_kernel, n_rb=n_rb, rb=rb),
        grid_spec=grid_spec,
        out_shape=jax.ShapeDtypeStruct((n_slots, d // 2), jnp.uint32),
        compiler_params=_cparams(("arbitrary",)),
        name="moe_experts",
    )(blk_e, blk_n, tok3, tok3, dst3, xn, w_gate, w_up, w_down)


def _combine_kernel(h_ref, y0_ref, y1_ref, rf_ref, *rest, final):
    if final:
        g_ref, o_ref = rest
    else:
        (o_ref,) = rest
    w = rf_ref[...]
    y0_lo, y0_hi = _unpack_bf16_pairs(y0_ref[...])
    y1_lo, y1_hi = _unpack_bf16_pairs(y1_ref[...])
    y = jnp.concatenate([w[:, 0:1] * y0_lo + w[:, 1:2] * y1_lo, w[:, 0:1] * y0_hi + w[:, 1:2] * y1_hi], axis=1)
    out = h_ref[...] + y
    if final:
        out = _rmsnorm_rows(out, g_ref[...])
    o_ref[...] = out


def _combine(h, y, rf, final_g):
    t, d = h.shape
    tm = _tile(t, 512)
    final = final_g is not None
    in_specs = [pl.BlockSpec((tm, d), lambda i: (i, 0)),
                pl.BlockSpec((tm, d // 2), lambda i: (i, 0)),
                pl.BlockSpec((tm, d // 2), lambda i: (t // tm + i, 0)),
                pl.BlockSpec((tm, LANES), lambda i: (i, 0))]
    args = [h, y, y, rf]
    if final:
        in_specs.append(pl.BlockSpec((1, d), lambda i: (0, 0)))
        args.append(final_g.reshape(1, d))
    return pl.pallas_call(
        functools.partial(_combine_kernel, final=final),
        grid=(t // tm,),
        in_specs=in_specs,
        out_specs=pl.BlockSpec((tm, d), lambda i: (i, 0)),
        out_shape=jax.ShapeDtypeStruct((t, d), F32),
        compiler_params=_cparams(("parallel",)),
        name="moe_combine",
    )(*args)


def _hier_moe(h, norm_g, w_grp, b_grp, w_exp, b_exp, w_gate, w_up, w_down, final_g):
    t, d = h.shape
    n_grp = w_grp.shape[1]
    n_exp = w_exp.shape[1]
    epg = n_exp // n_grp
    assert n_grp + n_exp <= LANES
    pad = LANES - n_grp - n_exp
    w_cat = jnp.concatenate([w_grp, w_exp, jnp.zeros((d, pad), F32)], axis=1).astype(BF16)
    b_cat = jnp.concatenate([b_grp, b_exp, jnp.zeros((pad,), F32)]).reshape(1, LANES)
    xn, et, rf = _router(h, norm_g, w_cat, b_cat, n_grp, epg)
    row_tok, row_dst, blk_e, blk_n = _dispatch(et, n_exp, ROW_BLOCK)
    y = _experts(xn, row_tok, row_dst, blk_e, blk_n, w_gate.astype(BF16), w_up.astype(BF16),
                 w_down.astype(BF16), EXPERT_TOPK * t + 2 * ROW_BLOCK)
    return _combine(h, y, rf, final_g)


def _rope_tables(seq, dh):
    inv = 1.0 / (ROPE_THETA ** (jnp.arange(0, dh, 2, dtype=F32) / dh))
    ang = jnp.arange(seq, dtype=F32)[:, None] * inv[None, :]
    cos, sin = jnp.cos(ang), jnp.sin(ang)
    return jnp.concatenate([cos, cos], axis=1), jnp.concatenate([-sin, sin], axis=1)


def kernel(x, lru_norm, lru_w_in, lru_b_in, lru_conv_w, lru_conv_b, lru_w_r, lru_b_r, lru_w_i, lru_b_i, lru_lambda, lru_w_out, lru_b_out, att_norm, att_w_qkv, att_w_o, ffn_norm, moe_w_grp, moe_b_grp, moe_w_exp, moe_b_exp, moe_w_gate, moe_w_up, moe_w_down, final_norm):
    bsz, seq, d = x.shape
    depth = ffn_norm.shape[0]
    n_mixers = 2
    cc, ss = _rope_tables(seq, d // N_HEADS)
    h = x.reshape(bsz * seq, d)
    for layer in range(depth):
        j = layer // n_mixers
        if layer % n_mixers == 0:
            u = _lru_in(h, lru_norm[j], lru_w_in[j].astype(BF16), lru_b_in[j])
            hg = _lru_scan(u, lru_conv_w[j], lru_conv_b[j], lru_w_r[j].astype(BF16), lru_b_r[j],
                           lru_w_i[j].astype(BF16), lru_b_i[j], lru_lambda[j], bsz, seq)
            h = _matmul_residual(hg, lru_w_out[j].astype(BF16), lru_b_out[j], h)
        else:
            qkv = _qkv_rope(h, att_norm[j], att_w_qkv[j].astype(BF16), cc, ss, seq)
            o = _moba_attention(qkv, bsz, seq, d)
            h = _matmul_residual(o, att_w_o[j].astype(BF16), None, h)
        final_g = final_norm if layer == depth - 1 else None
        h = _hier_moe(h, ffn_norm[layer], moe_w_grp[layer], moe_b_grp[layer], moe_w_exp[layer],
                      moe_b_exp[layer], moe_w_gate[layer], moe_w_up[layer], moe_w_down[layer], final_g)
    return h.reshape(bsz, seq, d)
```

```python
import functools

import jax
import jax.numpy as jnp
from jax import lax
from jax.experimental import pallas as pl
from jax.experimental.pallas import tpu as pltpu

F32 = jnp.float32
BF16 = jnp.bfloat16

NORM_EPS = 1e-6
NEG_INF = -1e30
LRU_C = 8.0
N_HEADS = 16
MOBA_BLOCK = 256
MOBA_TOPK = 3
ROPE_THETA = 10000.0
EXPERT_TOPK = 2
ROW_BLOCK = 256
LANES = 128
LANE_SHIFT = 7
SUBLANES = 8
BF16_SUBLANES = 16
MXU_COLS = 256
VMEM_LIMIT_BYTES = 56 * 1024 * 1024


def _tile(n, pref):
    t = min(n, pref)
    while n % t:
        t //= 2
    return t


def _cparams(sem):
    return pltpu.CompilerParams(dimension_semantics=sem, vmem_limit_bytes=VMEM_LIMIT_BYTES)


def _rmsnorm_rows(x, g):
    ms = jnp.mean(x * x, axis=-1, keepdims=True)
    return x * lax.rsqrt(ms + NORM_EPS) * g


def _lru_in_kernel(x_ref, g_ref, w_ref, b_ref, o_ref, xn_ref, *, n_plain):
    j = pl.program_id(1)

    @pl.when(j == 0)
    def _():
        xn_ref[...] = _rmsnorm_rows(x_ref[...], g_ref[...]).astype(BF16)

    acc = jnp.dot(xn_ref[...], w_ref[...], preferred_element_type=F32) + b_ref[...]
    o_ref[...] = jnp.where(j >= n_plain, jax.nn.gelu(acc, approximate=True), acc)


def _lru_in(h, g, w, b):
    t, d = h.shape
    n = w.shape[1]
    tm, tn = _tile(t, 1024), _tile(n // 2, 1024)
    return pl.pallas_call(
        functools.partial(_lru_in_kernel, n_plain=(n // 2) // tn),
        grid=(t // tm, n // tn),
        in_specs=[pl.BlockSpec((tm, d), lambda i, j: (i, 0)),
                  pl.BlockSpec((1, d), lambda i, j: (0, 0)),
                  pl.BlockSpec((d, tn), lambda i, j: (0, j)),
                  pl.BlockSpec((1, tn), lambda i, j: (0, j))],
        out_specs=pl.BlockSpec((tm, tn), lambda i, j: (i, j)),
        out_shape=jax.ShapeDtypeStruct((t, n), F32),
        scratch_shapes=[pltpu.VMEM((tm, d), BF16)],
        compiler_params=_cparams(("parallel", "arbitrary")),
        name="lru_in",
    )(h, g.reshape(1, d), w, b.reshape(1, n))


def _qkv_kernel(x_ref, g_ref, w_ref, cc_ref, ss_ref, o_ref, xn_ref, *, n_rope, heads_per_tile, dh):
    j = pl.program_id(1)

    @pl.when(j == 0)
    def _():
        xn_ref[...] = _rmsnorm_rows(x_ref[...], g_ref[...]).astype(BF16)

    acc = jnp.dot(xn_ref[...], w_ref[...], preferred_element_type=F32)

    is_rope = j < n_rope
    cc = cc_ref[...]
    ss = ss_ref[...]
    for hh in range(heads_per_tile):
        a = acc[:, hh * dh:(hh + 1) * dh]
        rot = a * cc + pltpu.roll(a, dh // 2, 1) * ss
        o_ref[:, hh * dh:(hh + 1) * dh] = jnp.where(is_rope, rot, a).astype(o_ref.dtype)


def _qkv_rope(h, g, w, cc, ss, seq):
    t, d = h.shape
    n = w.shape[1]
    dh = d // N_HEADS
    tm, tn = _tile(seq, 1024), _tile(d, 1024)
    s_tiles = seq // tm
    return pl.pallas_call(
        functools.partial(_qkv_kernel, n_rope=(2 * d) // tn, heads_per_tile=tn // dh, dh=dh),
        grid=(t // tm, n // tn),
        in_specs=[pl.BlockSpec((tm, d), lambda i, j: (i, 0)),
                  pl.BlockSpec((1, d), lambda i, j: (0, 0)),
                  pl.BlockSpec((d, tn), lambda i, j: (0, j)),
                  pl.BlockSpec((tm, dh), lambda i, j: (i % s_tiles, 0)),
                  pl.BlockSpec((tm, dh), lambda i, j: (i % s_tiles, 0))],
        out_specs=pl.BlockSpec((tm, tn), lambda i, j: (i, j)),
        out_shape=jax.ShapeDtypeStruct((t, n), BF16),
        scratch_shapes=[pltpu.VMEM((tm, d), BF16)],
        compiler_params=_cparams(("parallel", "arbitrary")),
        name="qkv_rope",
    )(h, g.reshape(1, d), w, cc, ss)


def _mm_res_kernel(a_ref, w_ref, *rest, has_bias):
    if has_bias:
        b_ref, r_ref, o_ref = rest
    else:
        r_ref, o_ref = rest
    y = jnp.dot(a_ref[...], w_ref[...], preferred_element_type=F32)
    if has_bias:
        y = y + b_ref[...]
    o_ref[...] = r_ref[...] + y


def _matmul_residual(a, w, b, res):
    t, k = a.shape
    n = w.shape[1]
    tm, tn = _tile(t, 1024), _tile(n, 1024)
    has_bias = b is not None
    in_specs = [pl.BlockSpec((tm, k), lambda i, j: (i, 0)),
                pl.BlockSpec((k, tn), lambda i, j: (0, j))]
    args = [a, w]
    if has_bias:
        in_specs.append(pl.BlockSpec((1, tn), lambda i, j: (0, j)))
        args.append(b.reshape(1, n))
    in_specs.append(pl.BlockSpec((tm, tn), lambda i, j: (i, j)))
    args.append(res)
    return pl.pallas_call(
        functools.partial(_mm_res_kernel, has_bias=has_bias),
        grid=(t // tm, n // tn),
        in_specs=in_specs,
        out_specs=pl.BlockSpec((tm, tn), lambda i, j: (i, j)),
        out_shape=jax.ShapeDtypeStruct((t, n), F32),
        compiler_params=_cparams(("parallel", "parallel")),
        name="matmul_residual",
    )(*args)


def _lru_scan_kernel(xb_ref, gate_ref, cw_ref, cb_ref, wr_ref, br_ref, wi_ref, bi_ref, lam_ref,
                     o_ref, ext_ref, a_ref, b_ref, h_ref, *, ts, n_grp, gw, conv_w):
    s = pl.program_id(1)
    w = xb_ref.shape[1]

    @pl.when(s == 0)
    def _():
        ext_ref[0:SUBLANES, :] = jnp.zeros((SUBLANES, w), F32)
        h_ref[...] = jnp.zeros_like(h_ref)

    ext_ref[SUBLANES:SUBLANES + ts, :] = xb_ref[...]
    xc = cb_ref[...] + cw_ref[conv_w - 1:conv_w, :] * xb_ref[...]
    for k in range(1, conv_w):
        xc = xc + cw_ref[conv_w - 1 - k:conv_w - k, :] * ext_ref[SUBLANES - k:SUBLANES - k + ts, :]
    ext_ref[0:SUBLANES, :] = ext_ref[ts:ts + SUBLANES, :]

    z = -lam_ref[...]
    softplus = jnp.maximum(z, 0.0) + jnp.log1p(jnp.exp(-jnp.abs(z)))
    c = -LRU_C * softplus
    for g in range(n_grp):
        sl = slice(g * gw, (g + 1) * gw)
        xg = xc[:, sl]
        xg16 = xg.astype(BF16)
        r = jax.nn.sigmoid(jnp.dot(xg16, wr_ref[g], preferred_element_type=F32) + br_ref[:, sl])
        ig = jax.nn.sigmoid(jnp.dot(xg16, wi_ref[g], preferred_element_type=F32) + bi_ref[:, sl])
        log_a = c[:, sl] * r
        a_ref[:, sl] = jnp.exp(log_a)
        th = jnp.tanh(log_a)
        b_ref[:, sl] = jnp.sqrt(-2.0 * th / (1.0 - th)) * ig * xg

    rows = lax.broadcasted_iota(jnp.int32, (SUBLANES, w), 0)

    def chunk(ci, h):
        r0 = pl.multiple_of(ci * SUBLANES, SUBLANES)
        a = a_ref[pl.ds(r0, SUBLANES), :]
        b = b_ref[pl.ds(r0, SUBLANES), :]
        for sh in (1, 2, 4):
            keep = rows >= sh
            a_sh = jnp.where(keep, pltpu.roll(a, sh, 0), 1.0)
            b_sh = jnp.where(keep, pltpu.roll(b, sh, 0), 0.0)
            b = a * b_sh + b
            a = a * a_sh
        hc = a * h + b
        b_ref[pl.ds(r0, SUBLANES), :] = hc * gate_ref[pl.ds(r0, SUBLANES), :]
        return jnp.broadcast_to(hc[SUBLANES - 1:SUBLANES, :], (SUBLANES, w))

    h_ref[...] = lax.fori_loop(0, ts // SUBLANES, chunk, h_ref[...])
    o_ref[...] = b_ref[...].astype(o_ref.dtype)


def _lru_scan(u, conv_w, conv_b, w_r, b_r, w_i, b_i, lam, bsz, seq):
    t = u.shape[0]
    w = u.shape[1] // 2
    n_grp, gw = w_r.shape[0], w_r.shape[1]
    cw = conv_w.shape[0]
    ts = _tile(seq, 256)
    s_tiles = seq // ts
    row = lambda b, s: (b * s_tiles + s, 0)
    vec = pl.BlockSpec((1, w), lambda b, s: (0, 0))
    return pl.pallas_call(
        functools.partial(_lru_scan_kernel, ts=ts, n_grp=n_grp, gw=gw, conv_w=cw),
        grid=(bsz, s_tiles),
        in_specs=[pl.BlockSpec((ts, w), row),
                  pl.BlockSpec((ts, w), lambda b, s: (b * s_tiles + s, 1)),
                  pl.BlockSpec((cw, w), lambda b, s: (0, 0)),
                  vec,
                  pl.BlockSpec((n_grp, gw, gw), lambda b, s: (0, 0, 0)),
                  vec,
                  pl.BlockSpec((n_grp, gw, gw), lambda b, s: (0, 0, 0)),
                  vec,
                  vec],
        out_specs=pl.BlockSpec((ts, w), row),
        out_shape=jax.ShapeDtypeStruct((t, w), BF16),
        scratch_shapes=[pltpu.VMEM((ts + SUBLANES, w), F32),
                        pltpu.VMEM((ts, w), F32),
                        pltpu.VMEM((ts, w), F32),
                        pltpu.VMEM((SUBLANES, w), F32)],
        compiler_params=_cparams(("parallel", "arbitrary")),
        name="lru_scan",
    )(u, u, conv_w, conv_b.reshape(1, w), w_r, b_r.reshape(1, w), w_i, b_i.reshape(1, w),
      lam.reshape(1, w))


def _attn_kernel(q_ref, k_ref, v_ref, o_ref, kaug_ref, vaug_t_ref, *, nb, blk, dh, topk):
    seq = nb * blk
    nbp = vaug_t_ref.shape[0] - dh
    first = (pl.program_id(0) == 0) & (pl.program_id(1) == 0)

    @pl.when(first)
    def _():
        row = lax.broadcasted_iota(jnp.int32, (seq, LANES), 0)
        lane = lax.broadcasted_iota(jnp.int32, (seq, LANES), 1)
        kaug_ref[:, dh:dh + LANES] = (lane == row // blk).astype(BF16)
        vaug_t_ref[dh:dh + nbp, :] = (lax.broadcasted_iota(jnp.int32, (nbp, seq), 0) == 0).astype(BF16)

    kaug_ref[:, 0:dh] = k_ref[...]
    for j in range(nb):
        vaug_t_ref[0:dh, j * blk:(j + 1) * blk] = v_ref[j * blk:(j + 1) * blk, :].astype(F32).T.astype(BF16)
    kmean = jnp.sum(k_ref[...].astype(F32).reshape(nb, blk, dh), axis=1) * (1.0 / blk)
    if nbp > nb:
        kmean = jnp.concatenate([kmean, jnp.zeros((nbp - nb, dh), F32)], axis=0)
    kmean = kmean.astype(BF16)

    c = (dh ** -0.5) * 1.4426950408889634
    blk_id = lax.broadcasted_iota(jnp.int32, (nbp, blk), 0)
    key_pos = lax.broadcasted_iota(jnp.int32, (blk, blk), 0)
    q_pos = lax.broadcasted_iota(jnp.int32, (blk, blk), 1)
    zeros_pad = jnp.zeros((LANES - nbp, blk), BF16)

    def scores(i):
        q_t = q_ref[i * blk:(i + 1) * blk, :].astype(F32).T.astype(BF16)
        gate_t = jnp.dot(kmean, q_t, preferred_element_type=F32)
        g = jnp.where(blk_id < i, gate_t, NEG_INF)
        sel = blk_id == i
        for _ in range(topk):
            m = jnp.max(g, axis=0, keepdims=True)
            first_max = jnp.min(jnp.where(g == m, blk_id, nbp), axis=0, keepdims=True)
            hit = blk_id == first_max
            sel = sel | (hit & (blk_id < i))
            g = jnp.where(hit, -jnp.inf, g)
        bias_t = jnp.where(sel, 0.0, NEG_INF).astype(BF16)
        q_aug_t = jnp.concatenate([q_t, bias_t, zeros_pad], axis=0)
        return jnp.dot(kaug_ref[0:(i + 1) * blk, :], q_aug_t, preferred_element_type=F32)

    def softmax_pv(i, s):
        s_own = jnp.where(key_pos <= q_pos, s[i * blk:, :], NEG_INF)
        m = jnp.max(s_own, axis=0, keepdims=True)
        if i > 0:
            s_past = s[:i * blk, :]
            m = jnp.maximum(m, jnp.max(s_past, axis=0, keepdims=True))
            p = jnp.concatenate([jnp.exp2((s_past - m) * c), jnp.exp2((s_own - m) * c)], axis=0)
        else:
            p = jnp.exp2((s_own - m) * c)
        out_t = jnp.dot(vaug_t_ref[:, 0:(i + 1) * blk], p.astype(BF16), preferred_element_type=F32)
        o_t = out_t[0:dh, :] / out_t[dh:dh + 1, :]
        o_ref[i * blk:(i + 1) * blk, :] = o_t.T.astype(o_ref.dtype)

    s_prev = scores(0)
    for i in range(1, nb):
        s_next = scores(i)
        softmax_pv(i - 1, s_prev)
        s_prev = s_next
    softmax_pv(nb - 1, s_prev)


def _moba_attention(qkv, bsz, seq, d):
    t = qkv.shape[0]
    dh = d // N_HEADS
    blk = MOBA_BLOCK
    nb = seq // blk
    assert seq % blk == 0 and nb <= BF16_SUBLANES and dh == LANES
    return pl.pallas_call(
        functools.partial(_attn_kernel, nb=nb, blk=blk, dh=dh, topk=min(MOBA_TOPK, nb)),
        grid=(bsz, N_HEADS),
        in_specs=[pl.BlockSpec((seq, dh), lambda b, h: (b, h)),
                  pl.BlockSpec((seq, dh), lambda b, h: (b, N_HEADS + h)),
                  pl.BlockSpec((seq, dh), lambda b, h: (b, 2 * N_HEADS + h))],
        out_specs=pl.BlockSpec((seq, dh), lambda b, h: (b, h)),
        out_shape=jax.ShapeDtypeStruct((t, d), BF16),
        scratch_shapes=[pltpu.VMEM((seq, dh + LANES), BF16),
                        pltpu.VMEM((dh + BF16_SUBLANES, seq), BF16)],
        compiler_params=_cparams(("arbitrary", "arbitrary")),
        name="moba_attention",
    )(qkv, qkv, qkv)


def _pack_bf16_pairs(lo, hi):
    lo_bits = lax.bitcast_convert_type(lo.astype(BF16).astype(F32), jnp.uint32)
    hi_bits = lax.bitcast_convert_type(hi.astype(BF16).astype(F32), jnp.uint32)
    return (hi_bits & jnp.uint32(0xFFFF0000)) | (lo_bits >> 16)


def _unpack_bf16_pairs(u):
    lo = lax.bitcast_convert_type(u << 16, F32)
    hi = lax.bitcast_convert_type(u & jnp.uint32(0xFFFF0000), F32)
    return lo, hi


def _router_kernel(h_ref, g_ref, w_ref, b_ref, xn_ref, et_ref, rf_ref, *, n_grp, epg):
    xn = _rmsnorm_rows(h_ref[...], g_ref[...])
    hd = xn.shape[1] // 2
    xn_ref[...] = _pack_bf16_pairs(xn[:, :hd], xn[:, hd:])
    logits = jnp.dot(xn.astype(BF16), w_ref[...], preferred_element_type=F32) + b_ref[...]
    lane = lax.broadcasted_iota(jnp.int32, logits.shape, 1)

    lg = jnp.where(lane < n_grp, logits, -jnp.inf)
    mg = jnp.max(lg, axis=1, keepdims=True)
    gidx = jnp.min(jnp.where(lg == mg, lane, LANES), axis=1, keepdims=True)
    pg_sel = 1.0 / jnp.sum(jnp.exp(lg - mg), axis=1, keepdims=True)

    lo = n_grp + gidx * epg
    in_grp = (lane >= lo) & (lane < lo + epg)
    le = jnp.where(in_grp, logits, -jnp.inf)
    e = jnp.exp(le - jnp.max(le, axis=1, keepdims=True))
    pe = jnp.where(in_grp, e / jnp.sum(e, axis=1, keepdims=True), -1.0)
    v1 = jnp.max(pe, axis=1, keepdims=True)
    i1 = jnp.min(jnp.where(pe == v1, lane, LANES), axis=1, keepdims=True)
    pe2 = jnp.where(lane == i1, -1.0, pe)
    v2 = jnp.max(pe2, axis=1, keepdims=True)
    i2 = jnp.min(jnp.where(pe2 == v2, lane, LANES), axis=1, keepdims=True)
    den = v1 + v2
    ids = jnp.where(lane == 0, i1 - n_grp, jnp.where(lane == 1, i2 - n_grp, 0)).astype(F32)
    et_ref[...] = ids.T[0:SUBLANES, :].astype(jnp.int32)
    rf_ref[...] = jnp.where(lane == 0, pg_sel * v1 / den, jnp.where(lane == 1, pg_sel * v2 / den, 0.0))


def _router(h, g, w_cat, b_cat, n_grp, epg):
    t, d = h.shape
    tm = _tile(t, 512)
    return pl.pallas_call(
        functools.partial(_router_kernel, n_grp=n_grp, epg=epg),
        grid=(t // tm,),
        in_specs=[pl.BlockSpec((tm, d), lambda i: (i, 0)),
                  pl.BlockSpec((1, d), lambda i: (0, 0)),
                  pl.BlockSpec((d, LANES), lambda i: (0, 0)),
                  pl.BlockSpec((1, LANES), lambda i: (0, 0))],
        out_specs=[pl.BlockSpec((tm, d // 2), lambda i: (i, 0)),
                   pl.BlockSpec((SUBLANES, tm), lambda i: (0, i)),
                   pl.BlockSpec((tm, LANES), lambda i: (i, 0))],
        out_shape=[jax.ShapeDtypeStruct((t, d // 2), jnp.uint32),
                   jax.ShapeDtypeStruct((SUBLANES, t), jnp.int32),
                   jax.ShapeDtypeStruct((t, LANES), F32)],
        compiler_params=_cparams(("parallel",)),
        name="moe_router",
    )(h, g.reshape(1, d), w_cat, b_cat)


DISPATCH_CHUNK = 512
DISPATCH_SMEM_CHUNK = 4096


def _dispatch_kernel(et_ref, tok_ref, dst_ref, blk_ref, rank_ref, dest_ref, dest_sm, rows_sm, rows_vm,
                     sem, *, n_exp, topk, rb, n_rb):
    t = et_ref.shape[1]
    c = DISPATCH_CHUNK
    e_iota = lax.broadcasted_iota(jnp.int32, (n_exp, c), 0)
    earlier = (lax.broadcasted_iota(jnp.int32, (c, c), 0) < lax.broadcasted_iota(jnp.int32, (c, c), 1))
    earlier = jnp.where(earlier, 1.0, 0.0).astype(BF16)

    run = jnp.zeros((n_exp, 1), F32)
    for k in range(topk):
        def rank_chunk(j, run, k=k):
            off = pl.multiple_of(j * c, c)
            oh = e_iota == et_ref[k:k + 1, pl.ds(off, c)]
            ohf = jnp.where(oh, 1.0, 0.0)
            pre = jnp.dot(ohf.astype(BF16), earlier, preferred_element_type=F32)
            rank_ref[k:k + 1, pl.ds(off, c)] = jnp.sum(jnp.where(oh, pre + run, 0.0), axis=0, keepdims=True)
            return run + jnp.sum(ohf, axis=1, keepdims=True)
        run = lax.fori_loop(0, t // c, rank_chunk, run)

    cnt = run
    nblk = jnp.floor((cnt + (rb - 1)) * (1.0 / rb))
    before = (lax.broadcasted_iota(jnp.int32, (n_exp, n_exp), 1) < lax.broadcasted_iota(jnp.int32, (n_exp, n_exp), 0))
    before = jnp.where(before, 1.0, 0.0).astype(BF16)
    bstart = jnp.dot(before, jnp.broadcast_to(nblk, (n_exp, LANES)).astype(BF16),
                     preferred_element_type=F32)[:, 0:1]

    for k in range(topk):
        def dest_chunk(j, carry, k=k):
            off = pl.multiple_of(j * c, c)
            oh = e_iota == et_ref[k:k + 1, pl.ds(off, c)]
            base = jnp.sum(jnp.where(oh, bstart, 0.0), axis=0, keepdims=True) * rb
            dest_ref[k:k + 1, pl.ds(off, c)] = (base + rank_ref[k:k + 1, pl.ds(off, c)]).astype(jnp.int32)
            return carry
        lax.fori_loop(0, t // c, dest_chunk, 0)

    rows_vm[...] = jnp.full(rows_vm.shape, -1, jnp.int32)
    fill = pltpu.make_async_copy(rows_vm, rows_sm, sem)
    fill.start()
    fill.wait()
    sc = DISPATCH_SMEM_CHUNK
    for k in range(topk):
        def scatter_chunk(j, carry, k=k):
            off = pl.multiple_of(j * sc, sc)
            cp = pltpu.make_async_copy(dest_ref.at[pl.ds(k, 1), pl.ds(off, sc)], dest_sm, sem)
            cp.start()
            cp.wait()

            def one(a, carry2):
                d = dest_sm[0, a]
                rows_sm[lax.shift_right_logical(d, LANE_SHIFT), d & (LANES - 1)] = k * t + off + a
                return carry2
            return lax.fori_loop(0, sc, one, carry, unroll=8)
        lax.fori_loop(0, t // sc, scatter_chunk, 0)
    back = pltpu.make_async_copy(rows_sm, rows_vm, sem)
    back.start()
    back.wait()

    a = rows_vm[...]
    valid = a >= 0
    tok = a
    for k in range(1, topk):
        tok = jnp.where(a >= k * t, a - k * t, tok)
    flat = (lax.broadcasted_iota(jnp.int32, a.shape, 0) * LANES + lax.broadcasted_iota(jnp.int32, a.shape, 1))
    scratch_dst = topk * t + ((flat // rb) % 2) * rb + flat % rb
    tok_ref[...] = jnp.where(valid, tok, 0)
    dst_ref[...] = jnp.where(valid, a, scratch_dst)

    nbp = blk_ref.shape[1]
    b_row = lax.broadcasted_iota(jnp.int32, (1, nbp), 1).astype(F32)
    bend = bstart + nblk
    blk_e = jnp.minimum(jnp.sum(jnp.where(bend <= b_row, 1.0, 0.0), axis=0, keepdims=True), n_exp - 1.0)
    mine = lax.broadcasted_iota(jnp.int32, (n_exp, nbp), 0).astype(F32) == blk_e
    cnt_b = jnp.sum(jnp.where(mine, cnt, 0.0), axis=0, keepdims=True)
    bstart_b = jnp.sum(jnp.where(mine, bstart, 0.0), axis=0, keepdims=True)
    blk_n = jnp.clip(cnt_b - (b_row - bstart_b) * rb, 0.0, float(rb))
    row = lax.broadcasted_iota(jnp.int32, blk_ref.shape, 0)
    blk_ref[...] = jnp.where(row == 0, blk_e, jnp.where(row == 1, blk_n, 0.0)).astype(jnp.int32)


def _dispatch(et, n_exp, rb):
    t = et.shape[1]
    topk = EXPERT_TOPK
    n_asg = topk * t
    assert n_asg % rb == 0 and t % DISPATCH_SMEM_CHUNK == 0 and (rb & (rb - 1)) == 0 and rb % LANES == 0
    n_rb = n_asg // rb + n_exp
    n_rows = n_rb * rb
    nbp = -(-n_rb // LANES) * LANES
    tok, dst, blk = pl.pallas_call(
        functools.partial(_dispatch_kernel, n_exp=n_exp, topk=topk, rb=rb, n_rb=n_rb),
        in_specs=[pl.BlockSpec(memory_space=pltpu.VMEM)],
        out_specs=[pl.BlockSpec(memory_space=pltpu.VMEM)] * 3,
        out_shape=[jax.ShapeDtypeStruct((n_rows // LANES, LANES), jnp.int32),
                   jax.ShapeDtypeStruct((n_rows // LANES, LANES), jnp.int32),
                   jax.ShapeDtypeStruct((SUBLANES, nbp), jnp.int32)],
        scratch_shapes=[pltpu.VMEM((SUBLANES, t), F32),
                        pltpu.VMEM((SUBLANES, t), jnp.int32),
                        pltpu.SMEM((1, DISPATCH_SMEM_CHUNK), jnp.int32),
                        pltpu.SMEM((n_rows // LANES, LANES), jnp.int32),
                        pltpu.VMEM((n_rows // LANES, LANES), jnp.int32),
                        pltpu.SemaphoreType.DMA(())],
        compiler_params=pltpu.CompilerParams(vmem_limit_bytes=VMEM_LIMIT_BYTES),
        name="moe_dispatch",
    )(et)
    return tok.reshape(n_rb, rb), dst.reshape(n_rb, rb), blk[0, :n_rb], blk[1, :n_rb]


def _expert_kernel(blk_e, blk_n, tok_ref, tok_next_ref, dst_prev_ref, xn_hbm, wg_ref, wu_ref, wd_ref,
                   y_hbm, xbuf, obuf, gsem, ssem, *, n_rb, rb):
    del blk_e
    i = pl.program_id(0)
    slot = lax.rem(i, 2)
    n = blk_n[i]

    def gather_row(tref, sl, r):
        return pltpu.make_async_copy(xn_hbm.at[pl.ds(tref[0, 0, r], 1)], xbuf.at[sl, pl.ds(r, 1)],
                                     gsem.at[sl])

    def start_gather(tref, sl):
        def body(r, c):
            gather_row(tref, sl, r).start()
            return c
        lax.fori_loop(0, rb, body, 0, unroll=8)

    def scatter_row(sl, r, dst):
        return pltpu.make_async_copy(obuf.at[sl, pl.ds(r, 1)], y_hbm.at[pl.ds(dst, 1)], ssem.at[sl])

    def wait_scatter(sl):
        pltpu.make_async_copy(obuf.at[sl], y_hbm.at[pl.ds(0, rb)], ssem.at[sl]).wait()

    @pl.when(i == 0)
    def _():
        obuf[...] = jnp.zeros_like(obuf)
        n_real = y_hbm.shape[0] - 2 * rb
        for sl in range(2):
            init = pltpu.make_async_copy(obuf.at[sl], y_hbm.at[pl.ds(n_real + sl * rb, rb)], ssem.at[sl])
            init.start()
            init.wait()

    n_prev = blk_n[jnp.maximum(i - 1, 0)]
    n_next = blk_n[jnp.minimum(i + 1, n_rb - 1)]
    do_gather = (i + 1 < n_rb) & (n_next > 0)
    do_scatter = (i >= 1) & (n_prev > 0)

    @pl.when((i == 0) & (n > 0))
    def _():
        start_gather(tok_ref, 0)

    @pl.when(n > 0)
    def _():
        pltpu.make_async_copy(xn_hbm.at[pl.ds(0, rb)], xbuf.at[slot], gsem.at[slot]).wait()

    hd = xbuf.shape[2]
    f = wg_ref.shape[2]
    fc, dc = min(f, MXU_COLS), min(hd, MXU_COLS)

    def free_obuf():
        @pl.when((i >= 2) & (blk_n[jnp.maximum(i - 2, 0)] > 0))
        def _():
            wait_scatter(slot)

    def compute(interleave_dma):
        def issue(piece, n_piece, start_row):
            if interleave_dma:
                per = -(-rb // n_piece)
                for r in range(piece * per, min(rb, (piece + 1) * per)):
                    start_row(r)

        x_lo, x_hi = _unpack_bf16_pairs(xbuf[slot])
        x = jnp.concatenate([x_lo.astype(BF16), x_hi.astype(BF16)], axis=1)
        hid = []
        for c in range(f // fc):
            cols = slice(c * fc, (c + 1) * fc)
            g = jnp.dot(x, wg_ref[0, :, cols], preferred_element_type=F32)
            u = jnp.dot(x, wu_ref[0, :, cols], preferred_element_type=F32)
            hid.append(((g * jax.nn.sigmoid(g)) * u).astype(BF16))
            issue(c, f // fc, lambda r: gather_row(tok_next_ref, 1 - slot, r).start())
        hid = jnp.concatenate(hid, axis=1)
        free_obuf()
        for c in range(hd // dc):
            cols = slice(c * dc, (c + 1) * dc)
            cols_hi = slice(hd + c * dc, hd + (c + 1) * dc)
            y_lo = jnp.dot(hid, wd_ref[0, :, cols], preferred_element_type=F32)
            y_hi = jnp.dot(hid, wd_ref[0, :, cols_hi], preferred_element_type=F32)
            obuf[slot, :, cols] = _pack_bf16_pairs(y_lo, y_hi)
            issue(c, hd // dc, lambda r: scatter_row(1 - slot, r, dst_prev_ref[0, 0, r]).start())

    fast = (n > 0) & do_gather & do_scatter

    @pl.when(fast)
    def _():
        compute(True)

    @pl.when(jnp.logical_not(fast))
    def _():
        @pl.when(do_gather)
        def _():
            start_gather(tok_next_ref, 1 - slot)

        @pl.when(do_scatter)
        def _():
            def body(r, c):
                scatter_row(1 - slot, r, dst_prev_ref[0, 0, r]).start()
                return c
            lax.fori_loop(0, rb, body, 0, unroll=8)

        @pl.when(n > 0)
        def _():
            compute(False)

        @pl.when(n == 0)
        def _():
            free_obuf()

    @pl.when((i == n_rb - 1) & do_scatter)
    def _():
        wait_scatter(1 - slot)


def _experts(xn, row_tok, row_dst, blk_e, blk_n, w_gate, w_up, w_down, n_slots):
    d, f = w_gate.shape[1:]
    assert xn.shape[1] * 2 == d
    n_rb, rb = row_tok.shape
    assert (n_slots - 2 * rb) % rb == 0
    tok3 = row_tok.reshape(n_rb, 1, rb)
    dst3 = row_dst.reshape(n_rb, 1, rb)
    smem_blk = lambda f_: pl.BlockSpec((1, 1, rb), f_, memory_space=pltpu.SMEM)
    grid_spec = pltpu.PrefetchScalarGridSpec(
        num_scalar_prefetch=2,
        grid=(n_rb,),
        in_specs=[smem_blk(lambda i, be, bn: (i, 0, 0)),
                  smem_blk(lambda i, be, bn: (jnp.minimum(i + 1, n_rb - 1), 0, 0)),
                  smem_blk(lambda i, be, bn: (jnp.maximum(i - 1, 0), 0, 0)),
                  pl.BlockSpec(memory_space=pl.ANY),
                  pl.BlockSpec((1, d, f), lambda i, be, bn: (be[i], 0, 0)),
                  pl.BlockSpec((1, d, f), lambda i, be, bn: (be[i], 0, 0)),
                  pl.BlockSpec((1, f, d), lambda i, be, bn: (be[i], 0, 0))],
        out_specs=pl.BlockSpec(memory_space=pl.ANY),
        scratch_shapes=[pltpu.VMEM((2, rb, d // 2), jnp.uint32),
                        pltpu.VMEM((2, rb, d // 2), jnp.uint32),
                        pltpu.SemaphoreType.DMA((2,)),
                        pltpu.SemaphoreType.DMA((2,))],
    )
    return pl.pallas_call(
        functools.partial(_expert_kernel, n_rb=n_rb, rb=rb),
        grid_spec=grid_spec,
        out_shape=jax.ShapeDtypeStruct((n_slots, d // 2), jnp.uint32),
        compiler_params=_cparams(("arbitrary",)),
        name="moe_experts",
    )(blk_e, blk_n, tok3, tok3, dst3, xn, w_gate, w_up, w_down)


def _combine_kernel(h_ref, y0_ref, y1_ref, rf_ref, *rest, final):
    if final:
        g_ref, o_ref = rest
    else:
        (o_ref,) = rest
    w = rf_ref[...]
    y0_lo, y0_hi = _unpack_bf16_pairs(y0_ref[...])
    y1_lo, y1_hi = _unpack_bf16_pairs(y1_ref[...])
    y = jnp.concatenate([w[:, 0:1] * y0_lo + w[:, 1:2] * y1_lo, w[:, 0:1] * y0_hi + w[:, 1:2] * y1_hi], axis=1)
    out = h_ref[...] + y
    if final:
        out = _rmsnorm_rows(out, g_ref[...])
    o_ref[...] = out


def _combine(h, y, rf, final_g):
    t, d = h.shape
    tm = _tile(t, 512)
    final = final_g is not None
    in_specs = [pl.BlockSpec((tm, d), lambda i: (i, 0)),
                pl.BlockSpec((tm, d // 2), lambda i: (i, 0)),
                pl.BlockSpec((tm, d // 2), lambda i: (t // tm + i, 0)),
                pl.BlockSpec((tm, LANES), lambda i: (i, 0))]
    args = [h, y, y, rf]
    if final:
        in_specs.append(pl.BlockSpec((1, d), lambda i: (0, 0)))
        args.append(final_g.reshape(1, d))
    return pl.pallas_call(
        functools.partial(_combine_kernel, final=final),
        grid=(t // tm,),
        in_specs=in_specs,
        out_specs=pl.BlockSpec((tm, d), lambda i: (i, 0)),
        out_shape=jax.ShapeDtypeStruct((t, d), F32),
        compiler_params=_cparams(("parallel",)),
        name="moe_combine",
    )(*args)


def _hier_moe(h, norm_g, w_grp, b_grp, w_exp, b_exp, w_gate, w_up, w_down, final_g):
    t, d = h.shape
    n_grp = w_grp.shape[1]
    n_exp = w_exp.shape[1]
    epg = n_exp // n_grp
    assert n_grp + n_exp <= LANES
    pad = LANES - n_grp - n_exp
    w_cat = jnp.concatenate([w_grp, w_exp, jnp.zeros((d, pad), F32)], axis=1).astype(BF16)
    b_cat = jnp.concatenate([b_grp, b_exp, jnp.zeros((pad,), F32)]).reshape(1, LANES)
    xn, et, rf = _router(h, norm_g, w_cat, b_cat, n_grp, epg)
    row_tok, row_dst, blk_e, blk_n = _dispatch(et, n_exp, ROW_BLOCK)
    y = _experts(xn, row_tok, row_dst, blk_e, blk_n, w_gate.astype(BF16), w_up.astype(BF16),
                 w_down.astype(BF16), EXPERT_TOPK * t + 2 * ROW_BLOCK)
    return _combine(h, y, rf, final_g)


def _rope_tables(seq, dh):
    inv = 1.0 / (ROPE_THETA ** (jnp.arange(0, dh, 2, dtype=F32) / dh))
    ang = jnp.arange(seq, dtype=F32)[:, None] * inv[None, :]
    cos, sin = jnp.cos(ang), jnp.sin(ang)
    return jnp.concatenate([cos, cos], axis=1), jnp.concatenate([-sin, sin], axis=1)


def kernel(x, lru_norm, lru_w_in, lru_b_in, lru_conv_w, lru_conv_b, lru_w_r, lru_b_r, lru_w_i, lru_b_i, lru_lambda, lru_w_out, lru_b_out, att_norm, att_w_qkv, att_w_o, ffn_norm, moe_w_grp, moe_b_grp, moe_w_exp, moe_b_exp, moe_w_gate, moe_w_up, moe_w_down, final_norm):
    bsz, seq, d = x.shape
    depth = ffn_norm.shape[0]
    n_mixers = 2
    cc, ss = _rope_tables(seq, d // N_HEADS)
    h = x.reshape(bsz * seq, d)
    for layer in range(depth):
        j = layer // n_mixers
        if layer % n_mixers == 0:
            u = _lru_in(h, lru_norm[j], lru_w_in[j].astype(BF16), lru_b_in[j])
            hg = _lru_scan(u, lru_conv_w[j], lru_conv_b[j], lru_w_r[j].astype(BF16), lru_b_r[j],
                           lru_w_i[j].astype(BF16), lru_b_i[j], lru_lambda[j], bsz, seq)
            h = _matmul_residual(hg, lru_w_out[j].astype(BF16), lru_b_out[j], h)
        else:
            qkv = _qkv_rope(h, att_norm[j], att_w_qkv[j].astype(BF16), cc, ss, seq)
            o = _moba_attention(qkv, bsz, seq, d)
            h = _matmul_residual(o, att_w_o[j].astype(BF16), None, h)
        final_g = final_norm if layer == depth - 1 else None
        h = _hier_moe(h, ffn_norm[layer], moe_w_grp[layer], moe_b_grp[layer], moe_w_exp[layer],
                      moe_b_exp[layer], moe_w_gate[layer], moe_w_up[layer], moe_w_down[layer], final_g)
    return h.reshape(bsz, seq, d)
```

```python
import functools

import jax
import jax.numpy as jnp
from jax import lax
from jax.experimental import pallas as pl
from jax.experimental.pallas import tpu as pltpu

F32 = jnp.float32
BF16 = jnp.bfloat16

NORM_EPS = 1e-6
NEG_INF = -1e30
LRU_C = 8.0
N_HEADS = 16
MOBA_BLOCK = 256
MOBA_TOPK = 3
ROPE_THETA = 10000.0
EXPERT_TOPK = 2
ROW_BLOCK = 256
LANES = 128
LANE_SHIFT = 7
SUBLANES = 8
BF16_SUBLANES = 16
MXU_COLS = 256
VMEM_LIMIT_BYTES = 56 * 1024 * 1024


def _tile(n, pref):
    t = min(n, pref)
    while n % t:
        t //= 2
    return t


def _cparams(sem):
    return pltpu.CompilerParams(dimension_semantics=sem, vmem_limit_bytes=VMEM_LIMIT_BYTES)


def _rmsnorm_rows(x, g):
    ms = jnp.mean(x * x, axis=-1, keepdims=True)
    return x * lax.rsqrt(ms + NORM_EPS) * g


def _lru_in_kernel(x_ref, g_ref, w_ref, b_ref, o_ref, xn_ref, *, n_plain):
    j = pl.program_id(1)

    @pl.when(j == 0)
    def _():
        xn_ref[...] = _rmsnorm_rows(x_ref[...], g_ref[...]).astype(BF16)

    acc = jnp.dot(xn_ref[...], w_ref[...], preferred_element_type=F32) + b_ref[...]

    @pl.when(j < n_plain)
    def _():
        o_ref[...] = acc

    @pl.when(j >= n_plain)
    def _():
        o_ref[...] = jax.nn.gelu(acc, approximate=True)


def _lru_in(h, g, w, b):
    t, d = h.shape
    n = w.shape[1]
    tm, tn = _tile(t, 1024), _tile(n // 2, 1024)
    return pl.pallas_call(
        functools.partial(_lru_in_kernel, n_plain=(n // 2) // tn),
        grid=(t // tm, n // tn),
        in_specs=[pl.BlockSpec((tm, d), lambda i, j: (i, 0)),
                  pl.BlockSpec((1, d), lambda i, j: (0, 0)),
                  pl.BlockSpec((d, tn), lambda i, j: (0, j)),
                  pl.BlockSpec((1, tn), lambda i, j: (0, j))],
        out_specs=pl.BlockSpec((tm, tn), lambda i, j: (i, j)),
        out_shape=jax.ShapeDtypeStruct((t, n), F32),
        scratch_shapes=[pltpu.VMEM((tm, d), BF16)],
        compiler_params=_cparams(("parallel", "arbitrary")),
        name="lru_in",
    )(h, g.reshape(1, d), w, b.reshape(1, n))


def _qkv_kernel(x_ref, g_ref, w_ref, cc_ref, ss_ref, o_ref, xn_ref, *, n_rope, heads_per_tile, dh):
    j = pl.program_id(1)

    @pl.when(j == 0)
    def _():
        xn_ref[...] = _rmsnorm_rows(x_ref[...], g_ref[...]).astype(BF16)

    acc = jnp.dot(xn_ref[...], w_ref[...], preferred_element_type=F32)

    is_rope = j < n_rope
    cc = cc_ref[...]
    ss = ss_ref[...]
    for hh in range(heads_per_tile):
        a = acc[:, hh * dh:(hh + 1) * dh]
        rot = a * cc + pltpu.roll(a, dh // 2, 1) * ss
        o_ref[:, hh * dh:(hh + 1) * dh] = jnp.where(is_rope, rot, a).astype(o_ref.dtype)


def _qkv_rope(h, g, w, cc, ss, seq):
    t, d = h.shape
    n = w.shape[1]
    dh = d // N_HEADS
    tm, tn = _tile(seq, 1024), _tile(d, 1024)
    s_tiles = seq // tm
    return pl.pallas_call(
        functools.partial(_qkv_kernel, n_rope=(2 * d) // tn, heads_per_tile=tn // dh, dh=dh),
        grid=(t // tm, n // tn),
        in_specs=[pl.BlockSpec((tm, d), lambda i, j: (i, 0)),
                  pl.BlockSpec((1, d), lambda i, j: (0, 0)),
                  pl.BlockSpec((d, tn), lambda i, j: (0, j)),
                  pl.BlockSpec((tm, dh), lambda i, j: (i % s_tiles, 0)),
                  pl.BlockSpec((tm, dh), lambda i, j: (i % s_tiles, 0))],
        out_specs=pl.BlockSpec((tm, tn), lambda i, j: (i, j)),
        out_shape=jax.ShapeDtypeStruct((t, n), BF16),
        scratch_shapes=[pltpu.VMEM((tm, d), BF16)],
        compiler_params=_cparams(("parallel", "arbitrary")),
        name="qkv_rope",
    )(h, g.reshape(1, d), w, cc, ss)


def _mm_res_kernel(a_ref, w_ref, *rest, has_bias):
    if has_bias:
        b_ref, r_ref, o_ref = rest
    else:
        r_ref, o_ref = rest
    y = jnp.dot(a_ref[...], w_ref[...], preferred_element_type=F32)
    if has_bias:
        y = y + b_ref[...]
    o_ref[...] = r_ref[...] + y


def _matmul_residual(a, w, b, res):
    t, k = a.shape
    n = w.shape[1]
    tm, tn = _tile(t, 1024), _tile(n, 1024)
    has_bias = b is not None
    in_specs = [pl.BlockSpec((tm, k), lambda i, j: (i, 0)),
                pl.BlockSpec((k, tn), lambda i, j: (0, j))]
    args = [a, w]
    if has_bias:
        in_specs.append(pl.BlockSpec((1, tn), lambda i, j: (0, j)))
        args.append(b.reshape(1, n))
    in_specs.append(pl.BlockSpec((tm, tn), lambda i, j: (i, j)))
    args.append(res)
    return pl.pallas_call(
        functools.partial(_mm_res_kernel, has_bias=has_bias),
        grid=(t // tm, n // tn),
        in_specs=in_specs,
        out_specs=pl.BlockSpec((tm, tn), lambda i, j: (i, j)),
        out_shape=jax.ShapeDtypeStruct((t, n), F32),
        compiler_params=_cparams(("parallel", "parallel")),
        name="matmul_residual",
    )(*args)


def _lru_scan_kernel(xb_ref, gate_ref, cw_ref, cb_ref, wr_ref, br_ref, wi_ref, bi_ref, lam_ref,
                     o_ref, ext_ref, a_ref, b_ref, h_ref, *, ts, n_grp, gw, conv_w):
    s = pl.program_id(1)
    w = xb_ref.shape[1]

    @pl.when(s == 0)
    def _():
        ext_ref[0:SUBLANES, :] = jnp.zeros((SUBLANES, w), F32)
        h_ref[...] = jnp.zeros_like(h_ref)

    ext_ref[SUBLANES:SUBLANES + ts, :] = xb_ref[...]
    xc = cb_ref[...] + cw_ref[conv_w - 1:conv_w, :] * xb_ref[...]
    for k in range(1, conv_w):
        xc = xc + cw_ref[conv_w - 1 - k:conv_w - k, :] * ext_ref[SUBLANES - k:SUBLANES - k + ts, :]
    ext_ref[0:SUBLANES, :] = ext_ref[ts:ts + SUBLANES, :]

    z = -lam_ref[...]
    softplus = jnp.maximum(z, 0.0) + jnp.log1p(jnp.exp(-jnp.abs(z)))
    c = -LRU_C * softplus
    for g in range(n_grp):
        sl = slice(g * gw, (g + 1) * gw)
        xg = xc[:, sl]
        xg16 = xg.astype(BF16)
        r = jax.nn.sigmoid(jnp.dot(xg16, wr_ref[g], preferred_element_type=F32) + br_ref[:, sl])
        ig = jax.nn.sigmoid(jnp.dot(xg16, wi_ref[g], preferred_element_type=F32) + bi_ref[:, sl])
        log_a = c[:, sl] * r
        a_ref[:, sl] = jnp.exp(log_a)
        th = jnp.tanh(log_a)
        b_ref[:, sl] = jnp.sqrt(-2.0 * th / (1.0 - th)) * ig * xg

    rows = lax.broadcasted_iota(jnp.int32, (SUBLANES, w), 0)

    def chunk(ci, h):
        r0 = pl.multiple_of(ci * SUBLANES, SUBLANES)
        a = a_ref[pl.ds(r0, SUBLANES), :]
        b = b_ref[pl.ds(r0, SUBLANES), :]
        for sh in (1, 2, 4):
            keep = rows >= sh
            a_sh = jnp.where(keep, pltpu.roll(a, sh, 0), 1.0)
            b_sh = jnp.where(keep, pltpu.roll(b, sh, 0), 0.0)
            b = a * b_sh + b
            a = a * a_sh
        hc = a * h + b
        b_ref[pl.ds(r0, SUBLANES), :] = hc * gate_ref[pl.ds(r0, SUBLANES), :]
        return jnp.broadcast_to(hc[SUBLANES - 1:SUBLANES, :], (SUBLANES, w))

    h_ref[...] = lax.fori_loop(0, ts // SUBLANES, chunk, h_ref[...])
    o_ref[...] = b_ref[...].astype(o_ref.dtype)


def _lru_scan(u, conv_w, conv_b, w_r, b_r, w_i, b_i, lam, bsz, seq):
    t = u.shape[0]
    w = u.shape[1] // 2
    n_grp, gw = w_r.shape[0], w_r.shape[1]
    cw = conv_w.shape[0]
    ts = _tile(seq, 256)
    s_tiles = seq // ts
    row = lambda b, s: (b * s_tiles + s, 0)
    vec = pl.BlockSpec((1, w), lambda b, s: (0, 0))
    return pl.pallas_call(
        functools.partial(_lru_scan_kernel, ts=ts, n_grp=n_grp, gw=gw, conv_w=cw),
        grid=(bsz, s_tiles),
        in_specs=[pl.BlockSpec((ts, w), row),
                  pl.BlockSpec((ts, w), lambda b, s: (b * s_tiles + s, 1)),
                  pl.BlockSpec((cw, w), lambda b, s: (0, 0)),
                  vec,
                  pl.BlockSpec((n_grp, gw, gw), lambda b, s: (0, 0, 0)),
                  vec,
                  pl.BlockSpec((n_grp, gw, gw), lambda b, s: (0, 0, 0)),
                  vec,
                  vec],
        out_specs=pl.BlockSpec((ts, w), row),
        out_shape=jax.ShapeDtypeStruct((t, w), BF16),
        scratch_shapes=[pltpu.VMEM((ts + SUBLANES, w), F32),
                        pltpu.VMEM((ts, w), F32),
                        pltpu.VMEM((ts, w), F32),
                        pltpu.VMEM((SUBLANES, w), F32)],
        compiler_params=_cparams(("parallel", "arbitrary")),
        name="lru_scan",
    )(u, u, conv_w, conv_b.reshape(1, w), w_r, b_r.reshape(1, w), w_i, b_i.reshape(1, w),
      lam.reshape(1, w))


def _attn_kernel(q_ref, k_ref, v_ref, o_ref, kaug_ref, vaug_t_ref, *, nb, blk, dh, topk):
    seq = nb * blk
    nbp = vaug_t_ref.shape[0] - dh
    first = (pl.program_id(0) == 0) & (pl.program_id(1) == 0)

    @pl.when(first)
    def _():
        row = lax.broadcasted_iota(jnp.int32, (seq, LANES), 0)
        lane = lax.broadcasted_iota(jnp.int32, (seq, LANES), 1)
        kaug_ref[:, dh:dh + LANES] = (lane == row // blk).astype(BF16)
        vaug_t_ref[dh:dh + nbp, :] = (lax.broadcasted_iota(jnp.int32, (nbp, seq), 0) == 0).astype(BF16)

    kaug_ref[:, 0:dh] = k_ref[...]
    for j in range(nb):
        vaug_t_ref[0:dh, j * blk:(j + 1) * blk] = v_ref[j * blk:(j + 1) * blk, :].astype(F32).T.astype(BF16)
    kmean = jnp.sum(k_ref[...].astype(F32).reshape(nb, blk, dh), axis=1) * (1.0 / blk)
    if nbp > nb:
        kmean = jnp.concatenate([kmean, jnp.zeros((nbp - nb, dh), F32)], axis=0)
    kmean = kmean.astype(BF16)

    c = (dh ** -0.5) * 1.4426950408889634
    blk_id = lax.broadcasted_iota(jnp.int32, (nbp, blk), 0)
    key_pos = lax.broadcasted_iota(jnp.int32, (blk, blk), 0)
    q_pos = lax.broadcasted_iota(jnp.int32, (blk, blk), 1)
    zeros_pad = jnp.zeros((LANES - nbp, blk), BF16)

    def scores(i):
        q_t = q_ref[i * blk:(i + 1) * blk, :].astype(F32).T.astype(BF16)
        gate_t = jnp.dot(kmean, q_t, preferred_element_type=F32)
        g = jnp.where(blk_id < i, gate_t, NEG_INF)
        sel = blk_id == i
        for _ in range(topk):
            m = jnp.max(g, axis=0, keepdims=True)
            first_max = jnp.min(jnp.where(g == m, blk_id, nbp), axis=0, keepdims=True)
            hit = blk_id == first_max
            sel = sel | (hit & (blk_id < i))
            g = jnp.where(hit, -jnp.inf, g)
        bias_t = jnp.where(sel, 0.0, NEG_INF).astype(BF16)
        q_aug_t = jnp.concatenate([q_t, bias_t, zeros_pad], axis=0)
        return jnp.dot(kaug_ref[0:(i + 1) * blk, :], q_aug_t, preferred_element_type=F32)

    def softmax_pv(i, s):
        s_own = jnp.where(key_pos <= q_pos, s[i * blk:, :], NEG_INF)
        m = jnp.max(s_own, axis=0, keepdims=True)
        if i > 0:
            s_past = s[:i * blk, :]
            m = jnp.maximum(m, jnp.max(s_past, axis=0, keepdims=True))
            p = jnp.concatenate([jnp.exp2((s_past - m) * c), jnp.exp2((s_own - m) * c)], axis=0)
        else:
            p = jnp.exp2((s_own - m) * c)
        out_t = jnp.dot(vaug_t_ref[:, 0:(i + 1) * blk], p.astype(BF16), preferred_element_type=F32)
        o_t = out_t[0:dh, :] / out_t[dh:dh + 1, :]
        o_ref[i * blk:(i + 1) * blk, :] = o_t.T.astype(o_ref.dtype)

    s_prev = scores(0)
    for i in range(1, nb):
        s_next = scores(i)
        softmax_pv(i - 1, s_prev)
        s_prev = s_next
    softmax_pv(nb - 1, s_prev)


def _moba_attention(qkv, bsz, seq, d):
    t = qkv.shape[0]
    dh = d // N_HEADS
    blk = MOBA_BLOCK
    nb = seq // blk
    assert seq % blk == 0 and nb <= BF16_SUBLANES and dh == LANES
    return pl.pallas_call(
        functools.partial(_attn_kernel, nb=nb, blk=blk, dh=dh, topk=min(MOBA_TOPK, nb)),
        grid=(bsz, N_HEADS),
        in_specs=[pl.BlockSpec((seq, dh), lambda b, h: (b, h)),
                  pl.BlockSpec((seq, dh), lambda b, h: (b, N_HEADS + h)),
                  pl.BlockSpec((seq, dh), lambda b, h: (b, 2 * N_HEADS + h))],
        out_specs=pl.BlockSpec((seq, dh), lambda b, h: (b, h)),
        out_shape=jax.ShapeDtypeStruct((t, d), BF16),
        scratch_shapes=[pltpu.VMEM((seq, dh + LANES), BF16),
                        pltpu.VMEM((dh + BF16_SUBLANES, seq), BF16)],
        compiler_params=_cparams(("arbitrary", "arbitrary")),
        name="moba_attention",
    )(qkv, qkv, qkv)


def _pack_bf16_pairs(lo, hi):
    lo_bits = lax.bitcast_convert_type(lo.astype(BF16).astype(F32), jnp.uint32)
    hi_bits = lax.bitcast_convert_type(hi.astype(BF16).astype(F32), jnp.uint32)
    return (hi_bits & jnp.uint32(0xFFFF0000)) | (lo_bits >> 16)


def _unpack_bf16_pairs(u):
    lo = lax.bitcast_convert_type(u << 16, F32)
    hi = lax.bitcast_convert_type(u & jnp.uint32(0xFFFF0000), F32)
    return lo, hi


def _router_kernel(h_ref, g_ref, w_ref, b_ref, xn_ref, et_ref, rf_ref, *, n_grp, epg):
    xn = _rmsnorm_rows(h_ref[...], g_ref[...])
    hd = xn.shape[1] // 2
    xn_ref[...] = _pack_bf16_pairs(xn[:, :hd], xn[:, hd:])
    logits = jnp.dot(xn.astype(BF16), w_ref[...], preferred_element_type=F32) + b_ref[...]
    lane = lax.broadcasted_iota(jnp.int32, logits.shape, 1)

    lg = jnp.where(lane < n_grp, logits, -jnp.inf)
    mg = jnp.max(lg, axis=1, keepdims=True)
    gidx = jnp.min(jnp.where(lg == mg, lane, LANES), axis=1, keepdims=True)
    pg_sel = 1.0 / jnp.sum(jnp.exp(lg - mg), axis=1, keepdims=True)

    lo = n_grp + gidx * epg
    in_grp = (lane >= lo) & (lane < lo + epg)
    le = jnp.where(in_grp, logits, -jnp.inf)
    e = jnp.exp(le - jnp.max(le, axis=1, keepdims=True))
    pe = jnp.where(in_grp, e / jnp.sum(e, axis=1, keepdims=True), -1.0)
    v1 = jnp.max(pe, axis=1, keepdims=True)
    i1 = jnp.min(jnp.where(pe == v1, lane, LANES), axis=1, keepdims=True)
    pe2 = jnp.where(lane == i1, -1.0, pe)
    v2 = jnp.max(pe2, axis=1, keepdims=True)
    i2 = jnp.min(jnp.where(pe2 == v2, lane, LANES), axis=1, keepdims=True)
    den = v1 + v2
    ids = jnp.where(lane == 0, i1 - n_grp, jnp.where(lane == 1, i2 - n_grp, 0)).astype(F32)
    et_ref[...] = ids.T[0:SUBLANES, :].astype(jnp.int32)
    rf_ref[...] = jnp.where(lane == 0, pg_sel * v1 / den, jnp.where(lane == 1, pg_sel * v2 / den, 0.0))


def _router(h, g, w_cat, b_cat, n_grp, epg):
    t, d = h.shape
    tm = _tile(t, 512)
    return pl.pallas_call(
        functools.partial(_router_kernel, n_grp=n_grp, epg=epg),
        grid=(t // tm,),
        in_specs=[pl.BlockSpec((tm, d), lambda i: (i, 0)),
                  pl.BlockSpec((1, d), lambda i: (0, 0)),
                  pl.BlockSpec((d, LANES), lambda i: (0, 0)),
                  pl.BlockSpec((1, LANES), lambda i: (0, 0))],
        out_specs=[pl.BlockSpec((tm, d // 2), lambda i: (i, 0)),
                   pl.BlockSpec((SUBLANES, tm), lambda i: (0, i)),
                   pl.BlockSpec((tm, LANES), lambda i: (i, 0))],
        out_shape=[jax.ShapeDtypeStruct((t, d // 2), jnp.uint32),
                   jax.ShapeDtypeStruct((SUBLANES, t), jnp.int32),
                   jax.ShapeDtypeStruct((t, LANES), F32)],
        compiler_params=_cparams(("parallel",)),
        name="moe_router",
    )(h, g.reshape(1, d), w_cat, b_cat)


DISPATCH_CHUNK = 512
DISPATCH_SMEM_CHUNK = 4096


def _dispatch_kernel(et_ref, tok_ref, dst_ref, blk_ref, rank_ref, dest_ref, dest_sm, rows_sm, rows_vm,
                     sem, *, n_exp, topk, rb, n_rb):
    t = et_ref.shape[1]
    c = DISPATCH_CHUNK
    e_iota = lax.broadcasted_iota(jnp.int32, (n_exp, c), 0)
    earlier = (lax.broadcasted_iota(jnp.int32, (c, c), 0) < lax.broadcasted_iota(jnp.int32, (c, c), 1))
    earlier = jnp.where(earlier, 1.0, 0.0).astype(BF16)

    run = jnp.zeros((n_exp, 1), F32)
    for k in range(topk):
        def rank_chunk(j, run, k=k):
            off = pl.multiple_of(j * c, c)
            oh = e_iota == et_ref[k:k + 1, pl.ds(off, c)]
            ohf = jnp.where(oh, 1.0, 0.0)
            pre = jnp.dot(ohf.astype(BF16), earlier, preferred_element_type=F32)
            rank_ref[k:k + 1, pl.ds(off, c)] = jnp.sum(jnp.where(oh, pre + run, 0.0), axis=0, keepdims=True)
            return run + jnp.sum(ohf, axis=1, keepdims=True)
        run = lax.fori_loop(0, t // c, rank_chunk, run)

    cnt = run
    nblk = jnp.floor((cnt + (rb - 1)) * (1.0 / rb))
    before = (lax.broadcasted_iota(jnp.int32, (n_exp, n_exp), 1) < lax.broadcasted_iota(jnp.int32, (n_exp, n_exp), 0))
    before = jnp.where(before, 1.0, 0.0).astype(BF16)
    bstart = jnp.dot(before, jnp.broadcast_to(nblk, (n_exp, LANES)).astype(BF16),
                     preferred_element_type=F32)[:, 0:1]

    for k in range(topk):
        def dest_chunk(j, carry, k=k):
            off = pl.multiple_of(j * c, c)
            oh = e_iota == et_ref[k:k + 1, pl.ds(off, c)]
            base = jnp.sum(jnp.where(oh, bstart, 0.0), axis=0, keepdims=True) * rb
            dest_ref[k:k + 1, pl.ds(off, c)] = (base + rank_ref[k:k + 1, pl.ds(off, c)]).astype(jnp.int32)
            return carry
        lax.fori_loop(0, t // c, dest_chunk, 0)

    rows_vm[...] = jnp.full(rows_vm.shape, -1, jnp.int32)
    fill = pltpu.make_async_copy(rows_vm, rows_sm, sem)
    fill.start()
    fill.wait()
    sc = DISPATCH_SMEM_CHUNK
    for k in range(topk):
        def scatter_chunk(j, carry, k=k):
            off = pl.multiple_of(j * sc, sc)
            cp = pltpu.make_async_copy(dest_ref.at[pl.ds(k, 1), pl.ds(off, sc)], dest_sm, sem)
            cp.start()
            cp.wait()

            def one(a, carry2):
                d = dest_sm[0, a]
                rows_sm[lax.shift_right_logical(d, LANE_SHIFT), d & (LANES - 1)] = k * t + off + a
                return carry2
            return lax.fori_loop(0, sc, one, carry, unroll=8)
        lax.fori_loop(0, t // sc, scatter_chunk, 0)
    back = pltpu.make_async_copy(rows_sm, rows_vm, sem)
    back.start()
    back.wait()

    a = rows_vm[...]
    valid = a >= 0
    tok = a
    for k in range(1, topk):
        tok = jnp.where(a >= k * t, a - k * t, tok)
    flat = (lax.broadcasted_iota(jnp.int32, a.shape, 0) * LANES + lax.broadcasted_iota(jnp.int32, a.shape, 1))
    scratch_dst = topk * t + ((flat // rb) % 2) * rb + flat % rb
    tok_ref[...] = jnp.where(valid, tok, 0)
    dst_ref[...] = jnp.where(valid, a, scratch_dst)

    nbp = blk_ref.shape[1]
    b_row = lax.broadcasted_iota(jnp.int32, (1, nbp), 1).astype(F32)
    bend = bstart + nblk
    blk_e = jnp.minimum(jnp.sum(jnp.where(bend <= b_row, 1.0, 0.0), axis=0, keepdims=True), n_exp - 1.0)
    mine = lax.broadcasted_iota(jnp.int32, (n_exp, nbp), 0).astype(F32) == blk_e
    cnt_b = jnp.sum(jnp.where(mine, cnt, 0.0), axis=0, keepdims=True)
    bstart_b = jnp.sum(jnp.where(mine, bstart, 0.0), axis=0, keepdims=True)
    blk_n = jnp.clip(cnt_b - (b_row - bstart_b) * rb, 0.0, float(rb))
    row = lax.broadcasted_iota(jnp.int32, blk_ref.shape, 0)
    blk_ref[...] = jnp.where(row == 0, blk_e, jnp.where(row == 1, blk_n, 0.0)).astype(jnp.int32)


def _dispatch(et, n_exp, rb):
    t = et.shape[1]
    topk = EXPERT_TOPK
    n_asg = topk * t
    assert n_asg % rb == 0 and t % DISPATCH_SMEM_CHUNK == 0 and (rb & (rb - 1)) == 0 and rb % LANES == 0
    n_rb = n_asg // rb + n_exp
    n_rows = n_rb * rb
    nbp = -(-n_rb // LANES) * LANES
    tok, dst, blk = pl.pallas_call(
        functools.partial(_dispatch_kernel, n_exp=n_exp, topk=topk, rb=rb, n_rb=n_rb),
        in_specs=[pl.BlockSpec(memory_space=pltpu.VMEM)],
        out_specs=[pl.BlockSpec(memory_space=pltpu.VMEM)] * 3,
        out_shape=[jax.ShapeDtypeStruct((n_rows // LANES, LANES), jnp.int32),
                   jax.ShapeDtypeStruct((n_rows // LANES, LANES), jnp.int32),
                   jax.ShapeDtypeStruct((SUBLANES, nbp), jnp.int32)],
        scratch_shapes=[pltpu.VMEM((SUBLANES, t), F32),
                        pltpu.VMEM((SUBLANES, t), jnp.int32),
                        pltpu.SMEM((1, DISPATCH_SMEM_CHUNK), jnp.int32),
                        pltpu.SMEM((n_rows // LANES, LANES), jnp.int32),
                        pltpu.VMEM((n_rows // LANES, LANES), jnp.int32),
                        pltpu.SemaphoreType.DMA(())],
        compiler_params=pltpu.CompilerParams(vmem_limit_bytes=VMEM_LIMIT_BYTES),
        name="moe_dispatch",
    )(et)
    return tok.reshape(n_rb, rb), dst.reshape(n_rb, rb), blk[0, :n_rb], blk[1, :n_rb]


WEIGHT_CHUNK_ROWS = 512
WEIGHT_RING = 4


def _expert_kernel(blk_e, blk_n, tok_ref, tok_next_ref, dst_prev_ref, xn_hbm, wg_hbm, wu_hbm, wd_hbm,
                   y_hbm, xbuf, obuf, wbuf_g, wbuf_u, wbuf_d, stg, run_end, next_e, run_start, wslot_of,
                   succ_e, st, gsem, ssem, wsem, *, n_rb, rb, layer):
    i = pl.program_id(0)
    slot = lax.rem(i, 2)
    n = blk_n[i]
    d, f = wbuf_g.shape[1:]
    cr = stg.shape[1]
    n_gu = d // cr
    n_dc = d // f
    n_chunks = 2 * n_gu + (f // cr) * n_dc
    ring = stg.shape[0]
    G_DONE, P_EXP, P_CHUNK = 0, 1, 2

    chunks = ([(wg_hbm, wbuf_g, r * cr, 0) for r in range(n_gu)]
              + [(wu_hbm, wbuf_u, r * cr, 0) for r in range(n_gu)]
              + [(wd_hbm, wbuf_d, r * cr, h * f) for r in range(f // cr) for h in range(n_dc)])
    assert len(chunks) == n_chunks

    def chunk_copy(e, c, k):
        for cid, (w_hbm, _, r0, c0) in enumerate(chunks):
            @pl.when(c == cid)
            def _(w_hbm=w_hbm, r0=r0, c0=c0):
                pltpu.make_async_copy(w_hbm.at[layer, e, pl.ds(r0, cr), pl.ds(c0, f)], stg.at[k],
                                      wsem.at[k]).start()

    def start_next_chunk(k):
        e = st[P_EXP]
        c = st[P_CHUNK]

        @pl.when(e >= 0)
        def _():
            chunk_copy(e, c, k)
            last = c + 1 == n_chunks
            st[P_CHUNK] = jnp.where(last, 0, c + 1)
            st[P_EXP] = jnp.where(last, succ_e[e], e)

    def convert_chunk(c, ws):
        g = st[G_DONE]
        k = lax.rem(g, ring)
        pltpu.make_async_copy(wg_hbm.at[layer, 0, pl.ds(0, cr), :], stg.at[k], wsem.at[k]).wait()
        start_next_chunk(lax.rem(g + ring - 1, ring))
        for cid, (_, wbuf, r0, c0) in enumerate(chunks):
            @pl.when(c == cid)
            def _(wbuf=wbuf, r0=r0, c0=c0):
                wbuf[ws, r0:r0 + cr, c0:c0 + f] = stg[k].astype(BF16)
        st[G_DONE] = g + 1

    @pl.when(i == 0)
    def _():
        def back(ii, carry):
            end_nb, next_nb = carry
            b = n_rb - 1 - ii
            nb = jnp.minimum(b + 1, n_rb - 1)
            nb_active = (b + 1 < n_rb) & (blk_n[nb] > 0)
            same = nb_active & (blk_e[nb] == blk_e[b])
            end_b = jnp.where(same, end_nb, b + 1)
            next_b = jnp.where(same, next_nb, jnp.where(nb_active, blk_e[nb], -1))
            run_end[b] = end_b
            next_e[b] = next_b

            @pl.when((blk_n[b] > 0) & jnp.logical_not(same))
            def _():
                succ_e[blk_e[b]] = next_b
            return end_b, next_b
        lax.fori_loop(0, n_rb, back, (jnp.int32(n_rb), jnp.int32(-1)))

        def fwd(b, carry):
            start_pb, wslot_pb = carry
            pb = jnp.maximum(b - 1, 0)
            same = (b > 0) & (blk_e[pb] == blk_e[b])
            start_b = jnp.where(same, start_pb, b)
            wslot_b = jnp.where(b == 0, 0, jnp.where(same, wslot_pb, 1 - wslot_pb))
            run_start[b] = start_b
            wslot_of[b] = wslot_b
            return start_b, wslot_b
        lax.fori_loop(0, n_rb, fwd, (jnp.int32(0), jnp.int32(0)))

        st[G_DONE] = 0
        st[P_CHUNK] = 0
        st[P_EXP] = jnp.where(n > 0, blk_e[0], -1)

        @pl.when(n > 0)
        def _():
            for k in range(ring - 1):
                start_next_chunk(k)

            def first(c, carry):
                convert_chunk(c, 0)
                return carry
            lax.fori_loop(0, n_chunks, first, 0)

    ws = wslot_of[i]

    @pl.when((n > 0) & (next_e[i] >= 0))
    def _():
        run_len = run_end[i] - run_start[i]
        per = lax.div(n_chunks + run_len - 1, run_len)
        lo = (i - run_start[i]) * per
        hi = jnp.minimum(lo + per, n_chunks)

        def nxt(c, carry):
            convert_chunk(c, 1 - ws)
            return carry
        lax.fori_loop(lo, hi, nxt, 0)

    def gather_row(tref, sl, r):
        return pltpu.make_async_copy(xn_hbm.at[pl.ds(tref[0, 0, r], 1)], xbuf.at[sl, pl.ds(r, 1)],
                                     gsem.at[sl])

    def start_gather(tref, sl):
        def body(r, c):
            gather_row(tref, sl, r).start()
            return c
        lax.fori_loop(0, rb, body, 0, unroll=8)

    def scatter_row(sl, r, dst):
        return pltpu.make_async_copy(obuf.at[sl, pl.ds(r, 1)], y_hbm.at[pl.ds(dst, 1)], ssem.at[sl])

    def wait_scatter(sl):
        pltpu.make_async_copy(obuf.at[sl], y_hbm.at[pl.ds(0, rb)], ssem.at[sl]).wait()

    @pl.when(i == 0)
    def _():
        obuf[...] = jnp.zeros_like(obuf)
        n_real = y_hbm.shape[0] - 2 * rb
        for sl in range(2):
            init = pltpu.make_async_copy(obuf.at[sl], y_hbm.at[pl.ds(n_real + sl * rb, rb)], ssem.at[sl])
            init.start()
            init.wait()

    n_prev = blk_n[jnp.maximum(i - 1, 0)]
    n_next = blk_n[jnp.minimum(i + 1, n_rb - 1)]
    do_gather = (i + 1 < n_rb) & (n_next > 0)
    do_scatter = (i >= 1) & (n_prev > 0)

    @pl.when((i == 0) & (n > 0))
    def _():
        start_gather(tok_ref, 0)

    @pl.when(n > 0)
    def _():
        pltpu.make_async_copy(xn_hbm.at[pl.ds(0, rb)], xbuf.at[slot], gsem.at[slot]).wait()

    hd = xbuf.shape[2]
    fc, dc = min(f, MXU_COLS), min(hd, MXU_COLS)

    def free_obuf():
        @pl.when((i >= 2) & (blk_n[jnp.maximum(i - 2, 0)] > 0))
        def _():
            wait_scatter(slot)

    def compute(interleave_dma):
        def issue(piece, n_piece, start_row):
            if interleave_dma:
                per = -(-rb // n_piece)
                for r in range(piece * per, min(rb, (piece + 1) * per)):
                    start_row(r)

        x_lo, x_hi = _unpack_bf16_pairs(xbuf[slot])
        x = jnp.concatenate([x_lo.astype(BF16), x_hi.astype(BF16)], axis=1)
        hid = []
        for c in range(f // fc):
            cols = slice(c * fc, (c + 1) * fc)
            g = jnp.dot(x, wbuf_g[ws, :, cols], preferred_element_type=F32)
            u = jnp.dot(x, wbuf_u[ws, :, cols], preferred_element_type=F32)
            hid.append(((g * jax.nn.sigmoid(g)) * u).astype(BF16))
            issue(c, f // fc, lambda r: gather_row(tok_next_ref, 1 - slot, r).start())
        hid = jnp.concatenate(hid, axis=1)
        free_obuf()
        for c in range(hd // dc):
            cols = slice(c * dc, (c + 1) * dc)
            cols_hi = slice(hd + c * dc, hd + (c + 1) * dc)
            y_lo = jnp.dot(hid, wbuf_d[ws, :, cols], preferred_element_type=F32)
            y_hi = jnp.dot(hid, wbuf_d[ws, :, cols_hi], preferred_element_type=F32)
            obuf[slot, :, cols] = _pack_bf16_pairs(y_lo, y_hi)
            issue(c, hd // dc, lambda r: scatter_row(1 - slot, r, dst_prev_ref[0, 0, r]).start())

    fast = (n > 0) & do_gather & do_scatter

    @pl.when(fast)
    def _():
        compute(True)

    @pl.when(jnp.logical_not(fast))
    def _():
        @pl.when(do_gather)
        def _():
            start_gather(tok_next_ref, 1 - slot)

        @pl.when(do_scatter)
        def _():
            def body(r, c):
                scatter_row(1 - slot, r, dst_prev_ref[0, 0, r]).start()
                return c
            lax.fori_loop(0, rb, body, 0, unroll=8)

        @pl.when(n > 0)
        def _():
            compute(False)

        @pl.when(n == 0)
        def _():
            free_obuf()

    @pl.when((i == n_rb - 1) & do_scatter)
    def _():
        wait_scatter(1 - slot)


def _experts(xn, row_tok, row_dst, blk_e, blk_n, w_gate, w_up, w_down, layer, n_slots):
    n_exp, d, f = w_gate.shape[1:]
    assert xn.shape[1] * 2 == d and w_down.shape[1:] == (n_exp, f, d)
    n_rb, rb = row_tok.shape
    assert (n_slots - 2 * rb) % rb == 0
    cr = min(WEIGHT_CHUNK_ROWS, f)
    assert d % cr == 0 and f % cr == 0 and d % f == 0
    tok3 = row_tok.reshape(n_rb, 1, rb)
    dst3 = row_dst.reshape(n_rb, 1, rb)
    smem_blk = lambda f_: pl.BlockSpec((1, 1, rb), f_, memory_space=pltpu.SMEM)
    hbm = pl.BlockSpec(memory_space=pl.ANY)
    grid_spec = pltpu.PrefetchScalarGridSpec(
        num_scalar_prefetch=2,
        grid=(n_rb,),
        in_specs=[smem_blk(lambda i, be, bn: (i, 0, 0)),
                  smem_blk(lambda i, be, bn: (jnp.minimum(i + 1, n_rb - 1), 0, 0)),
                  smem_blk(lambda i, be, bn: (jnp.maximum(i - 1, 0), 0, 0)),
                  hbm, hbm, hbm, hbm],
        out_specs=hbm,
        scratch_shapes=[pltpu.VMEM((2, rb, d // 2), jnp.uint32),
                        pltpu.VMEM((2, rb, d // 2), jnp.uint32),
                        pltpu.VMEM((2, d, f), BF16),
                        pltpu.VMEM((2, d, f), BF16),
                        pltpu.VMEM((2, f, d), BF16),
                        pltpu.VMEM((WEIGHT_RING, cr, f), F32),
                        pltpu.SMEM((n_rb,), jnp.int32),
                        pltpu.SMEM((n_rb,), jnp.int32),
                        pltpu.SMEM((n_rb,), jnp.int32),
                        pltpu.SMEM((n_rb,), jnp.int32),
                        pltpu.SMEM((n_exp,), jnp.int32),
                        pltpu.SMEM((4,), jnp.int32),
                        pltpu.SemaphoreType.DMA((2,)),
                        pltpu.SemaphoreType.DMA((2,)),
                        pltpu.SemaphoreType.DMA((WEIGHT_RING,))],
    )
    return pl.pallas_call(
        functools.partial(_expert_kernel, n_rb=n_rb, rb=rb, layer=layer),
        grid_spec=grid_spec,
        out_shape=jax.ShapeDtypeStruct((n_slots, d // 2), jnp.uint32),
        compiler_params=_cparams(("arbitrary",)),
        name="moe_experts",
    )(blk_e, blk_n, tok3, tok3, dst3, xn, w_gate, w_up, w_down)


def _combine_kernel(h_ref, y0_ref, y1_ref, rf_ref, *rest, final):
    if final:
        g_ref, o_ref = rest
    else:
        (o_ref,) = rest
    w = rf_ref[...]
    y0_lo, y0_hi = _unpack_bf16_pairs(y0_ref[...])
    y1_lo, y1_hi = _unpack_bf16_pairs(y1_ref[...])
    y = jnp.concatenate([w[:, 0:1] * y0_lo + w[:, 1:2] * y1_lo, w[:, 0:1] * y0_hi + w[:, 1:2] * y1_hi], axis=1)
    out = h_ref[...] + y
    if final:
        out = _rmsnorm_rows(out, g_ref[...])
    o_ref[...] = out


def _combine(h, y, rf, final_g):
    t, d = h.shape
    tm = _tile(t, 512)
    final = final_g is not None
    in_specs = [pl.BlockSpec((tm, d), lambda i: (i, 0)),
                pl.BlockSpec((tm, d // 2), lambda i: (i, 0)),
                pl.BlockSpec((tm, d // 2), lambda i: (t // tm + i, 0)),
                pl.BlockSpec((tm, LANES), lambda i: (i, 0))]
    args = [h, y, y, rf]
    if final:
        in_specs.append(pl.BlockSpec((1, d), lambda i: (0, 0)))
        args.append(final_g.reshape(1, d))
    return pl.pallas_call(
        functools.partial(_combine_kernel, final=final),
        grid=(t // tm,),
        in_specs=in_specs,
        out_specs=pl.BlockSpec((tm, d), lambda i: (i, 0)),
        out_shape=jax.ShapeDtypeStruct((t, d), F32),
        compiler_params=_cparams(("parallel",)),
        name="moe_combine",
    )(*args)


def _hier_moe(h, norm_g, w_grp, b_grp, w_exp, b_exp, w_gate, w_up, w_down, layer, final_g):
    t, d = h.shape
    n_grp = w_grp.shape[1]
    n_exp = w_exp.shape[1]
    epg = n_exp // n_grp
    assert n_grp + n_exp <= LANES
    pad = LANES - n_grp - n_exp
    w_cat = jnp.concatenate([w_grp, w_exp, jnp.zeros((d, pad), F32)], axis=1).astype(BF16)
    b_cat = jnp.concatenate([b_grp, b_exp, jnp.zeros((pad,), F32)]).reshape(1, LANES)
    xn, et, rf = _router(h, norm_g, w_cat, b_cat, n_grp, epg)
    row_tok, row_dst, blk_e, blk_n = _dispatch(et, n_exp, ROW_BLOCK)
    y = _experts(xn, row_tok, row_dst, blk_e, blk_n, w_gate, w_up, w_down, layer,
                 EXPERT_TOPK * t + 2 * ROW_BLOCK)
    return _combine(h, y, rf, final_g)


def _rope_tables(seq, dh):
    inv = 1.0 / (ROPE_THETA ** (jnp.arange(0, dh, 2, dtype=F32) / dh))
    ang = jnp.arange(seq, dtype=F32)[:, None] * inv[None, :]
    cos, sin = jnp.cos(ang), jnp.sin(ang)
    return jnp.concatenate([cos, cos], axis=1), jnp.concatenate([-sin, sin], axis=1)


def kernel(x, lru_norm, lru_w_in, lru_b_in, lru_conv_w, lru_conv_b, lru_w_r, lru_b_r, lru_w_i, lru_b_i, lru_lambda, lru_w_out, lru_b_out, att_norm, att_w_qkv, att_w_o, ffn_norm, moe_w_grp, moe_b_grp, moe_w_exp, moe_b_exp, moe_w_gate, moe_w_up, moe_w_down, final_norm):
    bsz, seq, d = x.shape
    depth = ffn_norm.shape[0]
    n_mixers = 2
    cc, ss = _rope_tables(seq, d // N_HEADS)
    h = x.reshape(bsz * seq, d)
    for layer in range(depth):
        j = layer // n_mixers
        if layer % n_mixers == 0:
            u = _lru_in(h, lru_norm[j], lru_w_in[j].astype(BF16), lru_b_in[j])
            hg = _lru_scan(u, lru_conv_w[j], lru_conv_b[j], lru_w_r[j].astype(BF16), lru_b_r[j],
                           lru_w_i[j].astype(BF16), lru_b_i[j], lru_lambda[j], bsz, seq)
            h = _matmul_residual(hg, lru_w_out[j].astype(BF16), lru_b_out[j], h)
        else:
            qkv = _qkv_rope(h, att_norm[j], att_w_qkv[j].astype(BF16), cc, ss, seq)
            o = _moba_attention(qkv, bsz, seq, d)
            h = _matmul_residual(o, att_w_o[j].astype(BF16), None, h)
        final_g = final_norm if layer == depth - 1 else None
        h = _hier_moe(h, ffn_norm[layer], moe_w_grp[layer], moe_b_grp[layer], moe_w_exp[layer],
                      moe_b_exp[layer], moe_w_gate, moe_w_up, moe_w_down, layer, final_g)
    return h.reshape(bsz, seq, d)
```

```python
import functools

import jax
import jax.numpy as jnp
from jax import lax
from jax.experimental import pallas as pl
from jax.experimental.pallas import tpu as pltpu

F32 = jnp.float32
BF16 = jnp.bfloat16

NORM_EPS = 1e-6
NEG_INF = -1e30
LRU_C = 8.0
N_HEADS = 16
MOBA_BLOCK = 256
MOBA_TOPK = 3
ROPE_THETA = 10000.0
EXPERT_TOPK = 2
ROW_BLOCK = 256
LANES = 128
LANE_SHIFT = 7
SUBLANES = 8
BF16_SUBLANES = 16
MXU_COLS = 256
VMEM_LIMIT_BYTES = 56 * 1024 * 1024


def _tile(n, pref):
    t = min(n, pref)
    while n % t:
        t //= 2
    return t


def _cparams(sem):
    return pltpu.CompilerParams(dimension_semantics=sem, vmem_limit_bytes=VMEM_LIMIT_BYTES)


def _rmsnorm_rows(x, g):
    ms = jnp.mean(x * x, axis=-1, keepdims=True)
    return x * lax.rsqrt(ms + NORM_EPS) * g


def _lru_in_kernel(x_ref, g_ref, w_ref, b_ref, o_ref, xn_ref, *, n_plain):
    j = pl.program_id(1)

    @pl.when(j == 0)
    def _():
        xn_ref[...] = _rmsnorm_rows(x_ref[...], g_ref[...]).astype(BF16)

    acc = jnp.dot(xn_ref[...], w_ref[...], preferred_element_type=F32) + b_ref[...]

    @pl.when(j < n_plain)
    def _():
        o_ref[...] = acc

    @pl.when(j >= n_plain)
    def _():
        o_ref[...] = jax.nn.gelu(acc, approximate=True)


def _lru_in(h, g, w, b):
    t, d = h.shape
    n = w.shape[1]
    tm, tn = _tile(t, 1024), _tile(n // 2, 1024)
    return pl.pallas_call(
        functools.partial(_lru_in_kernel, n_plain=(n // 2) // tn),
        grid=(t // tm, n // tn),
        in_specs=[pl.BlockSpec((tm, d), lambda i, j: (i, 0)),
                  pl.BlockSpec((1, d), lambda i, j: (0, 0)),
                  pl.BlockSpec((d, tn), lambda i, j: (0, j)),
                  pl.BlockSpec((1, tn), lambda i, j: (0, j))],
        out_specs=pl.BlockSpec((tm, tn), lambda i, j: (i, j)),
        out_shape=jax.ShapeDtypeStruct((t, n), F32),
        scratch_shapes=[pltpu.VMEM((tm, d), BF16)],
        compiler_params=_cparams(("parallel", "arbitrary")),
        name="lru_in",
    )(h, g.reshape(1, d), w, b.reshape(1, n))


def _qkv_kernel(x_ref, g_ref, w_ref, cc_ref, ss_ref, o_ref, xn_ref, *, n_rope, heads_per_tile, dh):
    j = pl.program_id(1)

    @pl.when(j == 0)
    def _():
        xn_ref[...] = _rmsnorm_rows(x_ref[...], g_ref[...]).astype(BF16)

    acc = jnp.dot(xn_ref[...], w_ref[...], preferred_element_type=F32)

    is_rope = j < n_rope
    cc = cc_ref[...]
    ss = ss_ref[...]
    for hh in range(heads_per_tile):
        a = acc[:, hh * dh:(hh + 1) * dh]
        rot = a * cc + pltpu.roll(a, dh // 2, 1) * ss
        o_ref[:, hh * dh:(hh + 1) * dh] = jnp.where(is_rope, rot, a).astype(o_ref.dtype)


def _qkv_rope(h, g, w, cc, ss, seq):
    t, d = h.shape
    n = w.shape[1]
    dh = d // N_HEADS
    tm, tn = _tile(seq, 1024), _tile(d, 1024)
    s_tiles = seq // tm
    return pl.pallas_call(
        functools.partial(_qkv_kernel, n_rope=(2 * d) // tn, heads_per_tile=tn // dh, dh=dh),
        grid=(t // tm, n // tn),
        in_specs=[pl.BlockSpec((tm, d), lambda i, j: (i, 0)),
                  pl.BlockSpec((1, d), lambda i, j: (0, 0)),
                  pl.BlockSpec((d, tn), lambda i, j: (0, j)),
                  pl.BlockSpec((tm, dh), lambda i, j: (i % s_tiles, 0)),
                  pl.BlockSpec((tm, dh), lambda i, j: (i % s_tiles, 0))],
        out_specs=pl.BlockSpec((tm, tn), lambda i, j: (i, j)),
        out_shape=jax.ShapeDtypeStruct((t, n), BF16),
        scratch_shapes=[pltpu.VMEM((tm, d), BF16)],
        compiler_params=_cparams(("parallel", "arbitrary")),
        name="qkv_rope",
    )(h, g.reshape(1, d), w, cc, ss)


def _lru_scan_kernel(xb_ref, gate_ref, cw_ref, cb_ref, wr_ref, br_ref, wi_ref, bi_ref, lam_ref,
                     o_ref, ext_ref, a_ref, b_ref, h_ref, *, ts, n_grp, gw, conv_w):
    s = pl.program_id(1)
    w = xb_ref.shape[1]

    @pl.when(s == 0)
    def _():
        ext_ref[0:SUBLANES, :] = jnp.zeros((SUBLANES, w), F32)
        h_ref[...] = jnp.zeros_like(h_ref)

    ext_ref[SUBLANES:SUBLANES + ts, :] = xb_ref[...]
    xc = cb_ref[...] + cw_ref[conv_w - 1:conv_w, :] * xb_ref[...]
    for k in range(1, conv_w):
        xc = xc + cw_ref[conv_w - 1 - k:conv_w - k, :] * ext_ref[SUBLANES - k:SUBLANES - k + ts, :]
    ext_ref[0:SUBLANES, :] = ext_ref[ts:ts + SUBLANES, :]

    z = -lam_ref[...]
    softplus = jnp.maximum(z, 0.0) + jnp.log1p(jnp.exp(-jnp.abs(z)))
    c = -LRU_C * softplus
    for g in range(n_grp):
        sl = slice(g * gw, (g + 1) * gw)
        xg = xc[:, sl]
        xg16 = xg.astype(BF16)
        r = jax.nn.sigmoid(jnp.dot(xg16, wr_ref[g], preferred_element_type=F32) + br_ref[:, sl])
        ig = jax.nn.sigmoid(jnp.dot(xg16, wi_ref[g], preferred_element_type=F32) + bi_ref[:, sl])
        log_a = c[:, sl] * r
        a_ref[:, sl] = jnp.exp(log_a)
        th = jnp.tanh(log_a)
        b_ref[:, sl] = jnp.sqrt(-2.0 * th / (1.0 - th)) * ig * xg

    rows = lax.broadcasted_iota(jnp.int32, (SUBLANES, w), 0)

    def chunk(ci, h):
        r0 = pl.multiple_of(ci * SUBLANES, SUBLANES)
        a = a_ref[pl.ds(r0, SUBLANES), :]
        b = b_ref[pl.ds(r0, SUBLANES), :]
        for sh in (1, 2, 4):
            keep = rows >= sh
            a_sh = jnp.where(keep, pltpu.roll(a, sh, 0), 1.0)
            b_sh = jnp.where(keep, pltpu.roll(b, sh, 0), 0.0)
            b = a * b_sh + b
            a = a * a_sh
        hc = a * h + b
        b_ref[pl.ds(r0, SUBLANES), :] = hc * gate_ref[pl.ds(r0, SUBLANES), :]
        return jnp.broadcast_to(hc[SUBLANES - 1:SUBLANES, :], (SUBLANES, w))

    h_ref[...] = lax.fori_loop(0, ts // SUBLANES, chunk, h_ref[...])
    o_ref[...] = b_ref[...].astype(o_ref.dtype)


def _lru_scan(u, conv_w, conv_b, w_r, b_r, w_i, b_i, lam, bsz, seq):
    t = u.shape[0]
    w = u.shape[1] // 2
    n_grp, gw = w_r.shape[0], w_r.shape[1]
    cw = conv_w.shape[0]
    ts = _tile(seq, 256)
    s_tiles = seq // ts
    row = lambda b, s: (b * s_tiles + s, 0)
    vec = pl.BlockSpec((1, w), lambda b, s: (0, 0))
    return pl.pallas_call(
        functools.partial(_lru_scan_kernel, ts=ts, n_grp=n_grp, gw=gw, conv_w=cw),
        grid=(bsz, s_tiles),
        in_specs=[pl.BlockSpec((ts, w), row),
                  pl.BlockSpec((ts, w), lambda b, s: (b * s_tiles + s, 1)),
                  pl.BlockSpec((cw, w), lambda b, s: (0, 0)),
                  vec,
                  pl.BlockSpec((n_grp, gw, gw), lambda b, s: (0, 0, 0)),
                  vec,
                  pl.BlockSpec((n_grp, gw, gw), lambda b, s: (0, 0, 0)),
                  vec,
                  vec],
        out_specs=pl.BlockSpec((ts, w), row),
        out_shape=jax.ShapeDtypeStruct((t, w), BF16),
        scratch_shapes=[pltpu.VMEM((ts + SUBLANES, w), F32),
                        pltpu.VMEM((ts, w), F32),
                        pltpu.VMEM((ts, w), F32),
                        pltpu.VMEM((SUBLANES, w), F32)],
        compiler_params=_cparams(("parallel", "arbitrary")),
        name="lru_scan",
    )(u, u, conv_w, conv_b.reshape(1, w), w_r, b_r.reshape(1, w), w_i, b_i.reshape(1, w),
      lam.reshape(1, w))


def _attn_kernel(q_ref, k_ref, v_ref, o_ref, kaug_ref, vaug_t_ref, *, nb, blk, dh, topk):
    seq = nb * blk
    nbp = vaug_t_ref.shape[0] - dh
    first = (pl.program_id(0) == 0) & (pl.program_id(1) == 0)

    @pl.when(first)
    def _():
        row = lax.broadcasted_iota(jnp.int32, (seq, LANES), 0)
        lane = lax.broadcasted_iota(jnp.int32, (seq, LANES), 1)
        kaug_ref[:, dh:dh + LANES] = (lane == row // blk).astype(BF16)
        vaug_t_ref[dh:dh + nbp, :] = (lax.broadcasted_iota(jnp.int32, (nbp, seq), 0) == 0).astype(BF16)

    kaug_ref[:, 0:dh] = k_ref[...]
    for j in range(nb):
        vaug_t_ref[0:dh, j * blk:(j + 1) * blk] = v_ref[j * blk:(j + 1) * blk, :].astype(F32).T.astype(BF16)
    kmean = jnp.sum(k_ref[...].astype(F32).reshape(nb, blk, dh), axis=1) * (1.0 / blk)
    if nbp > nb:
        kmean = jnp.concatenate([kmean, jnp.zeros((nbp - nb, dh), F32)], axis=0)
    kmean = kmean.astype(BF16)

    c = (dh ** -0.5) * 1.4426950408889634
    blk_id = lax.broadcasted_iota(jnp.int32, (nbp, blk), 0)
    key_pos = lax.broadcasted_iota(jnp.int32, (blk, blk), 0)
    q_pos = lax.broadcasted_iota(jnp.int32, (blk, blk), 1)
    zeros_pad = jnp.zeros((LANES - nbp, blk), BF16)

    def scores(i):
        q_t = q_ref[i * blk:(i + 1) * blk, :].astype(F32).T.astype(BF16)
        gate_t = jnp.dot(kmean, q_t, preferred_element_type=F32)
        g = jnp.where(blk_id < i, gate_t, NEG_INF)
        sel = blk_id == i
        for _ in range(topk):
            m = jnp.max(g, axis=0, keepdims=True)
            first_max = jnp.min(jnp.where(g == m, blk_id, nbp), axis=0, keepdims=True)
            hit = blk_id == first_max
            sel = sel | (hit & (blk_id < i))
            g = jnp.where(hit, -jnp.inf, g)
        bias_t = jnp.where(sel, 0.0, NEG_INF).astype(BF16)
        q_aug_t = jnp.concatenate([q_t, bias_t, zeros_pad], axis=0)
        return jnp.dot(kaug_ref[0:(i + 1) * blk, :], q_aug_t, preferred_element_type=F32)

    def probs(i, s):
        s_own = jnp.where(key_pos <= q_pos, s[i * blk:, :], NEG_INF)
        m = jnp.max(s_own, axis=0, keepdims=True)
        if i > 0:
            s_past = s[:i * blk, :]
            m = jnp.maximum(m, jnp.max(s_past, axis=0, keepdims=True))
            p = jnp.concatenate([jnp.exp2((s_past - m) * c), jnp.exp2((s_own - m) * c)], axis=0)
        else:
            p = jnp.exp2((s_own - m) * c)
        return p.astype(BF16)

    def pv(i, p):
        out_t = jnp.dot(vaug_t_ref[:, 0:(i + 1) * blk], p, preferred_element_type=F32)
        o_t = out_t[0:dh, :] / out_t[dh:dh + 1, :]
        o_ref[i * blk:(i + 1) * blk, :] = o_t.T.astype(o_ref.dtype)

    order = list(range(nb - 1, -1, -1))
    s_of, p_of = {}, {}
    for step in range(nb + 2):
        if step < nb:
            s_of[order[step]] = scores(order[step])
        if 1 <= step <= nb:
            t1 = order[step - 1]
            p_of[t1] = probs(t1, s_of.pop(t1))
        if step >= 2:
            t2 = order[step - 2]
            pv(t2, p_of.pop(t2))


def _moba_attention(qkv, bsz, seq, d):
    t = qkv.shape[0]
    dh = d // N_HEADS
    blk = MOBA_BLOCK
    nb = seq // blk
    assert seq % blk == 0 and nb <= BF16_SUBLANES and dh == LANES
    return pl.pallas_call(
        functools.partial(_attn_kernel, nb=nb, blk=blk, dh=dh, topk=min(MOBA_TOPK, nb)),
        grid=(bsz, N_HEADS),
        in_specs=[pl.BlockSpec((seq, dh), lambda b, h: (b, h)),
                  pl.BlockSpec((seq, dh), lambda b, h: (b, N_HEADS + h)),
                  pl.BlockSpec((seq, dh), lambda b, h: (b, 2 * N_HEADS + h))],
        out_specs=pl.BlockSpec((seq, dh), lambda b, h: (b, h)),
        out_shape=jax.ShapeDtypeStruct((t, d), BF16),
        scratch_shapes=[pltpu.VMEM((seq, dh + LANES), BF16),
                        pltpu.VMEM((dh + BF16_SUBLANES, seq), BF16)],
        compiler_params=_cparams(("arbitrary", "arbitrary")),
        name="moba_attention",
    )(qkv, qkv, qkv)


def _pack_bf16_pairs(lo, hi):
    lo_bits = lax.bitcast_convert_type(lo.astype(BF16).astype(F32), jnp.uint32)
    hi_bits = lax.bitcast_convert_type(hi.astype(BF16).astype(F32), jnp.uint32)
    return (hi_bits & jnp.uint32(0xFFFF0000)) | (lo_bits >> 16)


def _unpack_bf16_pairs(u):
    lo = lax.bitcast_convert_type(u << 16, F32)
    hi = lax.bitcast_convert_type(u & jnp.uint32(0xFFFF0000), F32)
    return lo, hi


def _route_rows(h, g, w_cat, b_cat, n_grp, epg):
    xn = _rmsnorm_rows(h, g)
    hd = xn.shape[1] // 2
    xn_packed = _pack_bf16_pairs(xn[:, :hd], xn[:, hd:])
    logits = jnp.dot(xn.astype(BF16), w_cat, preferred_element_type=F32) + b_cat
    lane = lax.broadcasted_iota(jnp.int32, logits.shape, 1)

    lg = jnp.where(lane < n_grp, logits, -jnp.inf)
    mg = jnp.max(lg, axis=1, keepdims=True)
    gidx = jnp.min(jnp.where(lg == mg, lane, LANES), axis=1, keepdims=True)
    pg_sel = 1.0 / jnp.sum(jnp.exp(lg - mg), axis=1, keepdims=True)

    lo = n_grp + gidx * epg
    in_grp = (lane >= lo) & (lane < lo + epg)
    le = jnp.where(in_grp, logits, -jnp.inf)
    e = jnp.exp(le - jnp.max(le, axis=1, keepdims=True))
    pe = jnp.where(in_grp, e / jnp.sum(e, axis=1, keepdims=True), -1.0)
    v1 = jnp.max(pe, axis=1, keepdims=True)
    i1 = jnp.min(jnp.where(pe == v1, lane, LANES), axis=1, keepdims=True)
    pe2 = jnp.where(lane == i1, -1.0, pe)
    v2 = jnp.max(pe2, axis=1, keepdims=True)
    i2 = jnp.min(jnp.where(pe2 == v2, lane, LANES), axis=1, keepdims=True)
    den = v1 + v2
    ids = jnp.where(lane == 0, i1 - n_grp, jnp.where(lane == 1, i2 - n_grp, 0)).astype(F32)
    et = ids.T[0:SUBLANES, :].astype(jnp.int32)
    rf = jnp.where(lane == 0, pg_sel * v1 / den, jnp.where(lane == 1, pg_sel * v2 / den, 0.0))
    return xn_packed, et, rf


def _mix_out_router_kernel(a_ref, w_ref, *rest, has_bias, n_grp, epg):
    if has_bias:
        b_ref, rest = rest[0], rest[1:]
    r_ref, g_ref, wc_ref, bc_ref, h_ref, xn_ref, et_ref, rf_ref, h_prev = rest

    @pl.when(pl.program_id(0) == 0)
    def _():
        h_prev[...] = jnp.zeros_like(h_prev)

    xn_packed, et, rf = _route_rows(h_prev[...], g_ref[...], wc_ref[...], bc_ref[...], n_grp, epg)
    xn_ref[...] = xn_packed
    et_ref[...] = et
    rf_ref[...] = rf

    y = jnp.dot(a_ref[...], w_ref[...], preferred_element_type=F32)
    if has_bias:
        y = y + b_ref[...]
    h = r_ref[...] + y
    h_ref[...] = h
    h_prev[...] = h


def _mix_out_router(a, w, b, res, g, w_cat, b_cat, n_grp, epg):
    t, k = a.shape
    d = w.shape[1]
    tm = _tile(t, 256)
    n = t // tm
    cur = lambda i: (jnp.minimum(i, n - 1), 0)
    prev = lambda i: (jnp.maximum(i - 1, 0), 0)
    has_bias = b is not None
    in_specs = [pl.BlockSpec((tm, k), cur),
                pl.BlockSpec((k, d), lambda i: (0, 0))]
    args = [a, w]
    if has_bias:
        in_specs.append(pl.BlockSpec((1, d), lambda i: (0, 0)))
        args.append(b.reshape(1, d))
    in_specs += [pl.BlockSpec((tm, d), cur),
                 pl.BlockSpec((1, d), lambda i: (0, 0)),
                 pl.BlockSpec((d, LANES), lambda i: (0, 0)),
                 pl.BlockSpec((1, LANES), lambda i: (0, 0))]
    args += [res, g.reshape(1, d), w_cat, b_cat]
    return pl.pallas_call(
        functools.partial(_mix_out_router_kernel, has_bias=has_bias, n_grp=n_grp, epg=epg),
        grid=(n + 1,),
        in_specs=in_specs,
        out_specs=[pl.BlockSpec((tm, d), cur),
                   pl.BlockSpec((tm, d // 2), prev),
                   pl.BlockSpec((SUBLANES, tm), lambda i: (0, jnp.maximum(i - 1, 0))),
                   pl.BlockSpec((tm, LANES), prev)],
        out_shape=[jax.ShapeDtypeStruct((t, d), F32),
                   jax.ShapeDtypeStruct((t, d // 2), jnp.uint32),
                   jax.ShapeDtypeStruct((SUBLANES, t), jnp.int32),
                   jax.ShapeDtypeStruct((t, LANES), F32)],
        scratch_shapes=[pltpu.VMEM((tm, d), F32)],
        compiler_params=_cparams(("arbitrary",)),
        name="mix_out_router",
    )(*args)


DISPATCH_CHUNK = 512
DISPATCH_SMEM_CHUNK = 4096


def _dispatch_kernel(et_ref, tok_ref, dst_ref, blk_ref, rank_ref, dest_ref, dest_sm, rows_sm, rows_vm,
                     sem, *, n_exp, topk, rb, n_rb):
    t = et_ref.shape[1]
    c = DISPATCH_CHUNK
    e_iota = lax.broadcasted_iota(jnp.int32, (n_exp, c), 0)
    earlier = (lax.broadcasted_iota(jnp.int32, (c, c), 0) < lax.broadcasted_iota(jnp.int32, (c, c), 1))
    earlier = jnp.where(earlier, 1.0, 0.0).astype(BF16)

    run = jnp.zeros((n_exp, 1), F32)
    for k in range(topk):
        def rank_chunk(j, run, k=k):
            off = pl.multiple_of(j * c, c)
            oh = e_iota == et_ref[k:k + 1, pl.ds(off, c)]
            ohf = jnp.where(oh, 1.0, 0.0)
            pre = jnp.dot(ohf.astype(BF16), earlier, preferred_element_type=F32)
            rank_ref[k:k + 1, pl.ds(off, c)] = jnp.sum(jnp.where(oh, pre + run, 0.0), axis=0, keepdims=True)
            return run + jnp.sum(ohf, axis=1, keepdims=True)
        run = lax.fori_loop(0, t // c, rank_chunk, run)

    cnt = run
    nblk = jnp.floor((cnt + (rb - 1)) * (1.0 / rb))
    before = (lax.broadcasted_iota(jnp.int32, (n_exp, n_exp), 1) < lax.broadcasted_iota(jnp.int32, (n_exp, n_exp), 0))
    before = jnp.where(before, 1.0, 0.0).astype(BF16)
    bstart = jnp.dot(before, jnp.broadcast_to(nblk, (n_exp, LANES)).astype(BF16),
                     preferred_element_type=F32)[:, 0:1]

    for k in range(topk):
        def dest_chunk(j, carry, k=k):
            off = pl.multiple_of(j * c, c)
            oh = e_iota == et_ref[k:k + 1, pl.ds(off, c)]
            base = jnp.sum(jnp.where(oh, bstart, 0.0), axis=0, keepdims=True) * rb
            dest_ref[k:k + 1, pl.ds(off, c)] = (base + rank_ref[k:k + 1, pl.ds(off, c)]).astype(jnp.int32)
            return carry
        lax.fori_loop(0, t // c, dest_chunk, 0)

    rows_vm[...] = jnp.full(rows_vm.shape, -1, jnp.int32)
    fill = pltpu.make_async_copy(rows_vm, rows_sm, sem)
    fill.start()
    fill.wait()
    sc = DISPATCH_SMEM_CHUNK
    for k in range(topk):
        def scatter_chunk(j, carry, k=k):
            off = pl.multiple_of(j * sc, sc)
            cp = pltpu.make_async_copy(dest_ref.at[pl.ds(k, 1), pl.ds(off, sc)], dest_sm, sem)
            cp.start()
            cp.wait()

            def one(a, carry2):
                d = dest_sm[0, a]
                rows_sm[lax.shift_right_logical(d, LANE_SHIFT), d & (LANES - 1)] = k * t + off + a
                return carry2
            return lax.fori_loop(0, sc, one, carry, unroll=8)
        lax.fori_loop(0, t // sc, scatter_chunk, 0)
    back = pltpu.make_async_copy(rows_sm, rows_vm, sem)
    back.start()
    back.wait()

    a = rows_vm[...]
    valid = a >= 0
    tok = a
    for k in range(1, topk):
        tok = jnp.where(a >= k * t, a - k * t, tok)
    flat = (lax.broadcasted_iota(jnp.int32, a.shape, 0) * LANES + lax.broadcasted_iota(jnp.int32, a.shape, 1))
    scratch_dst = topk * t + ((flat // rb) % 2) * rb + flat % rb
    tok_ref[...] = jnp.where(valid, tok, 0)
    dst_ref[...] = jnp.where(valid, a, scratch_dst)

    nbp = blk_ref.shape[1]
    b_row = lax.broadcasted_iota(jnp.int32, (1, nbp), 1).astype(F32)
    bend = bstart + nblk
    blk_e = jnp.minimum(jnp.sum(jnp.where(bend <= b_row, 1.0, 0.0), axis=0, keepdims=True), n_exp - 1.0)
    mine = lax.broadcasted_iota(jnp.int32, (n_exp, nbp), 0).astype(F32) == blk_e
    cnt_b = jnp.sum(jnp.where(mine, cnt, 0.0), axis=0, keepdims=True)
    bstart_b = jnp.sum(jnp.where(mine, bstart, 0.0), axis=0, keepdims=True)
    blk_n = jnp.clip(cnt_b - (b_row - bstart_b) * rb, 0.0, float(rb))
    row = lax.broadcasted_iota(jnp.int32, blk_ref.shape, 0)
    blk_ref[...] = jnp.where(row == 0, blk_e, jnp.where(row == 1, blk_n, 0.0)).astype(jnp.int32)


def _dispatch(et, n_exp, rb):
    t = et.shape[1]
    topk = EXPERT_TOPK
    n_asg = topk * t
    assert n_asg % rb == 0 and t % DISPATCH_SMEM_CHUNK == 0 and (rb & (rb - 1)) == 0 and rb % LANES == 0
    n_rb = n_asg // rb + n_exp
    n_rows = n_rb * rb
    nbp = -(-n_rb // LANES) * LANES
    tok, dst, blk = pl.pallas_call(
        functools.partial(_dispatch_kernel, n_exp=n_exp, topk=topk, rb=rb, n_rb=n_rb),
        in_specs=[pl.BlockSpec(memory_space=pltpu.VMEM)],
        out_specs=[pl.BlockSpec(memory_space=pltpu.VMEM)] * 3,
        out_shape=[jax.ShapeDtypeStruct((n_rows // LANES, LANES), jnp.int32),
                   jax.ShapeDtypeStruct((n_rows // LANES, LANES), jnp.int32),
                   jax.ShapeDtypeStruct((SUBLANES, nbp), jnp.int32)],
        scratch_shapes=[pltpu.VMEM((SUBLANES, t), F32),
                        pltpu.VMEM((SUBLANES, t), jnp.int32),
                        pltpu.SMEM((1, DISPATCH_SMEM_CHUNK), jnp.int32),
                        pltpu.SMEM((n_rows // LANES, LANES), jnp.int32),
                        pltpu.VMEM((n_rows // LANES, LANES), jnp.int32),
                        pltpu.SemaphoreType.DMA(())],
        compiler_params=pltpu.CompilerParams(vmem_limit_bytes=VMEM_LIMIT_BYTES),
        name="moe_dispatch",
    )(et)
    return tok.reshape(n_rb, rb), dst.reshape(n_rb, rb), blk[0, :n_rb], blk[1, :n_rb]


WEIGHT_CHUNK_ROWS = 512
WEIGHT_RING = 4


def _expert_kernel(blk_e, blk_n, tok_ref, tok_next_ref, dst_prev_ref, xn_hbm, wg_hbm, wu_hbm, wd_hbm,
                   y_hbm, xbuf, obuf, wbuf_g, wbuf_u, wbuf_d, stg, run_end, next_e, run_start, wslot_of,
                   succ_e, st, gsem, ssem, wsem, *, n_rb, rb, layer):
    i = pl.program_id(0)
    slot = lax.rem(i, 2)
    n = blk_n[i]
    d, f = wbuf_g.shape[1:]
    cr = stg.shape[1]
    n_gu = d // cr
    n_dc = d // f
    n_chunks = 2 * n_gu + (f // cr) * n_dc
    ring = stg.shape[0]
    G_DONE, P_EXP, P_CHUNK = 0, 1, 2

    chunks = ([(wg_hbm, wbuf_g, r * cr, 0) for r in range(n_gu)]
              + [(wu_hbm, wbuf_u, r * cr, 0) for r in range(n_gu)]
              + [(wd_hbm, wbuf_d, r * cr, h * f) for r in range(f // cr) for h in range(n_dc)])
    assert len(chunks) == n_chunks

    def chunk_copy(e, c, k):
        for cid, (w_hbm, _, r0, c0) in enumerate(chunks):
            @pl.when(c == cid)
            def _(w_hbm=w_hbm, r0=r0, c0=c0):
                pltpu.make_async_copy(w_hbm.at[layer, e, pl.ds(r0, cr), pl.ds(c0, f)], stg.at[k],
                                      wsem.at[k]).start()

    def start_next_chunk(k):
        e = st[P_EXP]
        c = st[P_CHUNK]

        @pl.when(e >= 0)
        def _():
            chunk_copy(e, c, k)
            last = c + 1 == n_chunks
            st[P_CHUNK] = jnp.where(last, 0, c + 1)
            st[P_EXP] = jnp.where(last, succ_e[e], e)

    def convert_chunk(c, ws):
        g = st[G_DONE]
        k = lax.rem(g, ring)
        pltpu.make_async_copy(wg_hbm.at[layer, 0, pl.ds(0, cr), :], stg.at[k], wsem.at[k]).wait()
        start_next_chunk(lax.rem(g + ring - 1, ring))
        for cid, (_, wbuf, r0, c0) in enumerate(chunks):
            @pl.when(c == cid)
            def _(wbuf=wbuf, r0=r0, c0=c0):
                wbuf[ws, r0:r0 + cr, c0:c0 + f] = stg[k].astype(BF16)
        st[G_DONE] = g + 1

    @pl.when(i == 0)
    def _():
        def back(ii, carry):
            end_nb, next_nb = carry
            b = n_rb - 1 - ii
            nb = jnp.minimum(b + 1, n_rb - 1)
            nb_active = (b + 1 < n_rb) & (blk_n[nb] > 0)
            same = nb_active & (blk_e[nb] == blk_e[b])
            end_b = jnp.where(same, end_nb, b + 1)
            next_b = jnp.where(same, next_nb, jnp.where(nb_active, blk_e[nb], -1))
            run_end[b] = end_b
            next_e[b] = next_b

            @pl.when((blk_n[b] > 0) & jnp.logical_not(same))
            def _():
                succ_e[blk_e[b]] = next_b
            return end_b, next_b
        lax.fori_loop(0, n_rb, back, (jnp.int32(n_rb), jnp.int32(-1)))

        def fwd(b, carry):
            start_pb, wslot_pb = carry
            pb = jnp.maximum(b - 1, 0)
            same = (b > 0) & (blk_e[pb] == blk_e[b])
            start_b = jnp.where(same, start_pb, b)
            wslot_b = jnp.where(b == 0, 0, jnp.where(same, wslot_pb, 1 - wslot_pb))
            run_start[b] = start_b
            wslot_of[b] = wslot_b
            return start_b, wslot_b
        lax.fori_loop(0, n_rb, fwd, (jnp.int32(0), jnp.int32(0)))

        st[G_DONE] = 0
        st[P_CHUNK] = 0
        st[P_EXP] = jnp.where(n > 0, blk_e[0], -1)

        @pl.when(n > 0)
        def _():
            for k in range(ring - 1):
                start_next_chunk(k)

            def first(c, carry):
                convert_chunk(c, 0)
                return carry
            lax.fori_loop(0, n_chunks, first, 0)

    ws = wslot_of[i]

    @pl.when((n > 0) & (next_e[i] >= 0))
    def _():
        run_len = run_end[i] - run_start[i]
        per = lax.div(n_chunks + run_len - 1, run_len)
        lo = (i - run_start[i]) * per
        hi = jnp.minimum(lo + per, n_chunks)

        def nxt(c, carry):
            convert_chunk(c, 1 - ws)
            return carry
        lax.fori_loop(lo, hi, nxt, 0)

    def gather_row(tref, sl, r):
        return pltpu.make_async_copy(xn_hbm.at[pl.ds(tref[0, 0, r], 1)], xbuf.at[sl, pl.ds(r, 1)],
                                     gsem.at[sl])

    def start_gather(tref, sl):
        def body(r, c):
            gather_row(tref, sl, r).start()
            return c
        lax.fori_loop(0, rb, body, 0, unroll=8)

    def scatter_row(sl, r, dst):
        return pltpu.make_async_copy(obuf.at[sl, pl.ds(r, 1)], y_hbm.at[pl.ds(dst, 1)], ssem.at[sl])

    def wait_scatter(sl):
        pltpu.make_async_copy(obuf.at[sl], y_hbm.at[pl.ds(0, rb)], ssem.at[sl]).wait()

    @pl.when(i == 0)
    def _():
        obuf[...] = jnp.zeros_like(obuf)
        n_real = y_hbm.shape[0] - 2 * rb
        for sl in range(2):
            init = pltpu.make_async_copy(obuf.at[sl], y_hbm.at[pl.ds(n_real + sl * rb, rb)], ssem.at[sl])
            init.start()
            init.wait()

    n_prev = blk_n[jnp.maximum(i - 1, 0)]
    n_next = blk_n[jnp.minimum(i + 1, n_rb - 1)]
    do_gather = (i + 1 < n_rb) & (n_next > 0)
    do_scatter = (i >= 1) & (n_prev > 0)

    @pl.when((i == 0) & (n > 0))
    def _():
        start_gather(tok_ref, 0)

    @pl.when(n > 0)
    def _():
        pltpu.make_async_copy(xn_hbm.at[pl.ds(0, rb)], xbuf.at[slot], gsem.at[slot]).wait()

    hd = xbuf.shape[2]
    fc, dc = min(f, MXU_COLS), min(hd, MXU_COLS)

    def free_obuf():
        @pl.when((i >= 2) & (blk_n[jnp.maximum(i - 2, 0)] > 0))
        def _():
            wait_scatter(slot)

    def compute(interleave_dma):
        def issue(piece, n_piece, start_row):
            if interleave_dma:
                per = -(-rb // n_piece)
                for r in range(piece * per, min(rb, (piece + 1) * per)):
                    start_row(r)

        x_lo, x_hi = _unpack_bf16_pairs(xbuf[slot])
        x = jnp.concatenate([x_lo.astype(BF16), x_hi.astype(BF16)], axis=1)
        hid = []
        for c in range(f // fc):
            cols = slice(c * fc, (c + 1) * fc)
            g = jnp.dot(x, wbuf_g[ws, :, cols], preferred_element_type=F32)
            u = jnp.dot(x, wbuf_u[ws, :, cols], preferred_element_type=F32)
            hid.append(((g * jax.nn.sigmoid(g)) * u).astype(BF16))
            issue(c, f // fc, lambda r: gather_row(tok_next_ref, 1 - slot, r).start(priority=r % 2))
        hid = jnp.concatenate(hid, axis=1)
        free_obuf()
        for c in range(hd // dc):
            cols = slice(c * dc, (c + 1) * dc)
            cols_hi = slice(hd + c * dc, hd + (c + 1) * dc)
            y_lo = jnp.dot(hid, wbuf_d[ws, :, cols], preferred_element_type=F32)
            y_hi = jnp.dot(hid, wbuf_d[ws, :, cols_hi], preferred_element_type=F32)
            obuf[slot, :, cols] = _pack_bf16_pairs(y_lo, y_hi)
            issue(c, hd // dc,
                  lambda r: scatter_row(1 - slot, r, dst_prev_ref[0, 0, r]).start(priority=r % 2))

    fast = (n > 0) & do_gather & do_scatter

    @pl.when(fast)
    def _():
        compute(True)

    @pl.when(jnp.logical_not(fast))
    def _():
        @pl.when(do_gather)
        def _():
            start_gather(tok_next_ref, 1 - slot)

        @pl.when(do_scatter)
        def _():
            def body(r, c):
                scatter_row(1 - slot, r, dst_prev_ref[0, 0, r]).start()
                return c
            lax.fori_loop(0, rb, body, 0, unroll=8)

        @pl.when(n > 0)
        def _():
            compute(False)

        @pl.when(n == 0)
        def _():
            free_obuf()

    @pl.when((i == n_rb - 1) & do_scatter)
    def _():
        wait_scatter(1 - slot)


def _experts(xn, row_tok, row_dst, blk_e, blk_n, w_gate, w_up, w_down, layer, n_slots):
    n_exp, d, f = w_gate.shape[1:]
    assert xn.shape[1] * 2 == d and w_down.shape[1:] == (n_exp, f, d)
    n_rb, rb = row_tok.shape
    assert (n_slots - 2 * rb) % rb == 0
    cr = min(WEIGHT_CHUNK_ROWS, f)
    assert d % cr == 0 and f % cr == 0 and d % f == 0
    tok3 = row_tok.reshape(n_rb, 1, rb)
    dst3 = row_dst.reshape(n_rb, 1, rb)
    smem_blk = lambda f_: pl.BlockSpec((1, 1, rb), f_, memory_space=pltpu.SMEM)
    hbm = pl.BlockSpec(memory_space=pl.ANY)
    grid_spec = pltpu.PrefetchScalarGridSpec(
        num_scalar_prefetch=2,
        grid=(n_rb,),
        in_specs=[smem_blk(lambda i, be, bn: (i, 0, 0)),
                  smem_blk(lambda i, be, bn: (jnp.minimum(i + 1, n_rb - 1), 0, 0)),
                  smem_blk(lambda i, be, bn: (jnp.maximum(i - 1, 0), 0, 0)),
                  hbm, hbm, hbm, hbm],
        out_specs=hbm,
        scratch_shapes=[pltpu.VMEM((2, rb, d // 2), jnp.uint32),
                        pltpu.VMEM((2, rb, d // 2), jnp.uint32),
                        pltpu.VMEM((2, d, f), BF16),
                        pltpu.VMEM((2, d, f), BF16),
                        pltpu.VMEM((2, f, d), BF16),
                        pltpu.VMEM((WEIGHT_RING, cr, f), F32),
                        pltpu.SMEM((n_rb,), jnp.int32),
                        pltpu.SMEM((n_rb,), jnp.int32),
                        pltpu.SMEM((n_rb,), jnp.int32),
                        pltpu.SMEM((n_rb,), jnp.int32),
                        pltpu.SMEM((n_exp,), jnp.int32),
                        pltpu.SMEM((4,), jnp.int32),
                        pltpu.SemaphoreType.DMA((2,)),
                        pltpu.SemaphoreType.DMA((2,)),
                        pltpu.SemaphoreType.DMA((WEIGHT_RING,))],
    )
    return pl.pallas_call(
        functools.partial(_expert_kernel, n_rb=n_rb, rb=rb, layer=layer),
        grid_spec=grid_spec,
        out_shape=jax.ShapeDtypeStruct((n_slots, d // 2), jnp.uint32),
        compiler_params=_cparams(("arbitrary",)),
        name="moe_experts",
    )(blk_e, blk_n, tok3, tok3, dst3, xn, w_gate, w_up, w_down)


def _combine_kernel(h_ref, y0_ref, y1_ref, rf_ref, *rest, final):
    if final:
        g_ref, o_ref = rest
    else:
        (o_ref,) = rest
    w = rf_ref[...]
    y0_lo, y0_hi = _unpack_bf16_pairs(y0_ref[...])
    y1_lo, y1_hi = _unpack_bf16_pairs(y1_ref[...])
    y = jnp.concatenate([w[:, 0:1] * y0_lo + w[:, 1:2] * y1_lo, w[:, 0:1] * y0_hi + w[:, 1:2] * y1_hi], axis=1)
    out = h_ref[...] + y
    if final:
        out = _rmsnorm_rows(out, g_ref[...])
    o_ref[...] = out


def _combine(h, y, rf, final_g):
    t, d = h.shape
    tm = _tile(t, 512)
    final = final_g is not None
    in_specs = [pl.BlockSpec((tm, d), lambda i: (i, 0)),
                pl.BlockSpec((tm, d // 2), lambda i: (i, 0)),
                pl.BlockSpec((tm, d // 2), lambda i: (t // tm + i, 0)),
                pl.BlockSpec((tm, LANES), lambda i: (i, 0))]
    args = [h, y, y, rf]
    if final:
        in_specs.append(pl.BlockSpec((1, d), lambda i: (0, 0)))
        args.append(final_g.reshape(1, d))
    return pl.pallas_call(
        functools.partial(_combine_kernel, final=final),
        grid=(t // tm,),
        in_specs=in_specs,
        out_specs=pl.BlockSpec((tm, d), lambda i: (i, 0)),
        out_shape=jax.ShapeDtypeStruct((t, d), F32),
        compiler_params=_cparams(("parallel",)),
        name="moe_combine",
    )(*args)


def _mix_out_moe(act, w_out, b_out, h, norm_g, w_grp, b_grp, w_exp, b_exp, w_gate, w_up, w_down, layer, final_g):
    t, d = h.shape
    n_grp = w_grp.shape[1]
    n_exp = w_exp.shape[1]
    epg = n_exp // n_grp
    assert n_grp + n_exp <= LANES
    pad = LANES - n_grp - n_exp
    w_cat = jnp.concatenate([w_grp, w_exp, jnp.zeros((d, pad), F32)], axis=1).astype(BF16)
    b_cat = jnp.concatenate([b_grp, b_exp, jnp.zeros((pad,), F32)]).reshape(1, LANES)
    h, xn, et, rf = _mix_out_router(act, w_out, b_out, h, norm_g, w_cat, b_cat, n_grp, epg)
    row_tok, row_dst, blk_e, blk_n = _dispatch(et, n_exp, ROW_BLOCK)
    y = _experts(xn, row_tok, row_dst, blk_e, blk_n, w_gate, w_up, w_down, layer,
                 EXPERT_TOPK * t + 2 * ROW_BLOCK)
    return _combine(h, y, rf, final_g)


def _rope_tables(seq, dh):
    inv = 1.0 / (ROPE_THETA ** (jnp.arange(0, dh, 2, dtype=F32) / dh))
    ang = jnp.arange(seq, dtype=F32)[:, None] * inv[None, :]
    cos, sin = jnp.cos(ang), jnp.sin(ang)
    return jnp.concatenate([cos, cos], axis=1), jnp.concatenate([-sin, sin], axis=1)


def kernel(x, lru_norm, lru_w_in, lru_b_in, lru_conv_w, lru_conv_b, lru_w_r, lru_b_r, lru_w_i, lru_b_i, lru_lambda, lru_w_out, lru_b_out, att_norm, att_w_qkv, att_w_o, ffn_norm, moe_w_grp, moe_b_grp, moe_w_exp, moe_b_exp, moe_w_gate, moe_w_up, moe_w_down, final_norm):
    bsz, seq, d = x.shape
    depth = ffn_norm.shape[0]
    n_mixers = 2
    cc, ss = _rope_tables(seq, d // N_HEADS)
    h = x.reshape(bsz * seq, d)
    for layer in range(depth):
        j = layer // n_mixers
        if layer % n_mixers == 0:
            u = _lru_in(h, lru_norm[j], lru_w_in[j].astype(BF16), lru_b_in[j])
            act = _lru_scan(u, lru_conv_w[j], lru_conv_b[j], lru_w_r[j].astype(BF16), lru_b_r[j],
                            lru_w_i[j].astype(BF16), lru_b_i[j], lru_lambda[j], bsz, seq)
            w_out, b_out = lru_w_out[j].astype(BF16), lru_b_out[j]
        else:
            qkv = _qkv_rope(h, att_norm[j], att_w_qkv[j].astype(BF16), cc, ss, seq)
            act = _moba_attention(qkv, bsz, seq, d)
            w_out, b_out = att_w_o[j].astype(BF16), None
        final_g = final_norm if layer == depth - 1 else None
        h = _mix_out_moe(act, w_out, b_out, h, ffn_norm[layer], moe_w_grp[layer], moe_b_grp[layer],
                         moe_w_exp[layer], moe_b_exp[layer], moe_w_gate, moe_w_up, moe_w_down, layer, final_g)
    return h.reshape(bsz, seq, d)
```

```python
import functools

import jax
import jax.numpy as jnp
from jax import lax
from jax.experimental import pallas as pl
from jax.experimental.pallas import tpu as pltpu

F32 = jnp.float32
BF16 = jnp.bfloat16

NORM_EPS = 1e-6
NEG_INF = -1e30
LRU_C = 8.0
N_HEADS = 16
MOBA_BLOCK = 256
MOBA_TOPK = 3
ROPE_THETA = 10000.0
EXPERT_TOPK = 2
ROW_BLOCK = 256
LANES = 128
LANE_SHIFT = 7
SUBLANES = 8
BF16_SUBLANES = 16
MXU_COLS = 256
VMEM_LIMIT_BYTES = 56 * 1024 * 1024


def _tile(n, pref):
    t = min(n, pref)
    while n % t:
        t //= 2
    return t


def _cparams(sem):
    return pltpu.CompilerParams(dimension_semantics=sem, vmem_limit_bytes=VMEM_LIMIT_BYTES)


def _rmsnorm_rows(x, g):
    ms = jnp.mean(x * x, axis=-1, keepdims=True)
    return x * lax.rsqrt(ms + NORM_EPS) * g


def _lru_in_kernel(x_ref, g_ref, w_ref, b_ref, o_ref, xn_ref, *, n_plain):
    j = pl.program_id(1)

    @pl.when(j == 0)
    def _():
        xn_ref[...] = _rmsnorm_rows(x_ref[...], g_ref[...]).astype(BF16)

    acc = jnp.dot(xn_ref[...], w_ref[...], preferred_element_type=F32) + b_ref[...]

    @pl.when(j < n_plain)
    def _():
        o_ref[...] = acc

    @pl.when(j >= n_plain)
    def _():
        o_ref[...] = jax.nn.gelu(acc, approximate=True)


def _lru_in(h, g, w, b):
    t, d = h.shape
    n = w.shape[1]
    tm, tn = _tile(t, 1024), _tile(n // 2, 1024)
    return pl.pallas_call(
        functools.partial(_lru_in_kernel, n_plain=(n // 2) // tn),
        grid=(t // tm, n // tn),
        in_specs=[pl.BlockSpec((tm, d), lambda i, j: (i, 0)),
                  pl.BlockSpec((1, d), lambda i, j: (0, 0)),
                  pl.BlockSpec((d, tn), lambda i, j: (0, j)),
                  pl.BlockSpec((1, tn), lambda i, j: (0, j))],
        out_specs=pl.BlockSpec((tm, tn), lambda i, j: (i, j)),
        out_shape=jax.ShapeDtypeStruct((t, n), F32),
        scratch_shapes=[pltpu.VMEM((tm, d), BF16)],
        compiler_params=_cparams(("parallel", "arbitrary")),
        name="lru_in",
    )(h, g.reshape(1, d), w, b.reshape(1, n))


def _qkv_kernel(x_ref, g_ref, w_ref, cc_ref, ss_ref, o_ref, xn_ref, *, n_rope, heads_per_tile, dh):
    j = pl.program_id(1)

    @pl.when(j == 0)
    def _():
        xn_ref[...] = _rmsnorm_rows(x_ref[...], g_ref[...]).astype(BF16)

    acc = jnp.dot(xn_ref[...], w_ref[...], preferred_element_type=F32)

    is_rope = j < n_rope
    cc = cc_ref[...]
    ss = ss_ref[...]
    for hh in range(heads_per_tile):
        a = acc[:, hh * dh:(hh + 1) * dh]
        rot = a * cc + pltpu.roll(a, dh // 2, 1) * ss
        o_ref[:, hh * dh:(hh + 1) * dh] = jnp.where(is_rope, rot, a).astype(o_ref.dtype)


def _qkv_rope(h, g, w, cc, ss, seq):
    t, d = h.shape
    n = w.shape[1]
    dh = d // N_HEADS
    tm, tn = _tile(seq, 1024), _tile(d, 1024)
    s_tiles = seq // tm
    return pl.pallas_call(
        functools.partial(_qkv_kernel, n_rope=(2 * d) // tn, heads_per_tile=tn // dh, dh=dh),
        grid=(t // tm, n // tn),
        in_specs=[pl.BlockSpec((tm, d), lambda i, j: (i, 0)),
                  pl.BlockSpec((1, d), lambda i, j: (0, 0)),
                  pl.BlockSpec((d, tn), lambda i, j: (0, j)),
                  pl.BlockSpec((tm, dh), lambda i, j: (i % s_tiles, 0)),
                  pl.BlockSpec((tm, dh), lambda i, j: (i % s_tiles, 0))],
        out_specs=pl.BlockSpec((tm, tn), lambda i, j: (i, j)),
        out_shape=jax.ShapeDtypeStruct((t, n), BF16),
        scratch_shapes=[pltpu.VMEM((tm, d), BF16)],
        compiler_params=_cparams(("parallel", "arbitrary")),
        name="qkv_rope",
    )(h, g.reshape(1, d), w, cc, ss)


def _lru_scan_kernel(xb_ref, gate_ref, cw_ref, cb_ref, wr_ref, br_ref, wi_ref, bi_ref, lam_ref,
                     o_ref, ext_ref, a_ref, b_ref, h_ref, *, ts, n_grp, gw, conv_w):
    s = pl.program_id(1)
    w = xb_ref.shape[1]

    @pl.when(s == 0)
    def _():
        ext_ref[0:SUBLANES, :] = jnp.zeros((SUBLANES, w), F32)
        h_ref[...] = jnp.zeros_like(h_ref)

    ext_ref[SUBLANES:SUBLANES + ts, :] = xb_ref[...]
    xc = cb_ref[...] + cw_ref[conv_w - 1:conv_w, :] * xb_ref[...]
    for k in range(1, conv_w):
        xc = xc + cw_ref[conv_w - 1 - k:conv_w - k, :] * ext_ref[SUBLANES - k:SUBLANES - k + ts, :]
    ext_ref[0:SUBLANES, :] = ext_ref[ts:ts + SUBLANES, :]

    z = -lam_ref[...]
    softplus = jnp.maximum(z, 0.0) + jnp.log1p(jnp.exp(-jnp.abs(z)))
    c = -LRU_C * softplus
    for g in range(n_grp):
        sl = slice(g * gw, (g + 1) * gw)
        xg = xc[:, sl]
        xg16 = xg.astype(BF16)
        r = jax.nn.sigmoid(jnp.dot(xg16, wr_ref[g], preferred_element_type=F32) + br_ref[:, sl])
        ig = jax.nn.sigmoid(jnp.dot(xg16, wi_ref[g], preferred_element_type=F32) + bi_ref[:, sl])
        log_a = c[:, sl] * r
        a_ref[:, sl] = jnp.exp(log_a)
        th = jnp.tanh(log_a)
        b_ref[:, sl] = jnp.sqrt(-2.0 * th / (1.0 - th)) * ig * xg

    rows = lax.broadcasted_iota(jnp.int32, (SUBLANES, w), 0)

    def chunk(ci, h):
        r0 = pl.multiple_of(ci * SUBLANES, SUBLANES)
        a = a_ref[pl.ds(r0, SUBLANES), :]
        b = b_ref[pl.ds(r0, SUBLANES), :]
        for sh in (1, 2, 4):
            keep = rows >= sh
            a_sh = jnp.where(keep, pltpu.roll(a, sh, 0), 1.0)
            b_sh = jnp.where(keep, pltpu.roll(b, sh, 0), 0.0)
            b = a * b_sh + b
            a = a * a_sh
        hc = a * h + b
        b_ref[pl.ds(r0, SUBLANES), :] = hc * gate_ref[pl.ds(r0, SUBLANES), :]
        return jnp.broadcast_to(hc[SUBLANES - 1:SUBLANES, :], (SUBLANES, w))

    h_ref[...] = lax.fori_loop(0, ts // SUBLANES, chunk, h_ref[...])
    o_ref[...] = b_ref[...].astype(o_ref.dtype)


def _lru_scan(u, conv_w, conv_b, w_r, b_r, w_i, b_i, lam, bsz, seq):
    t = u.shape[0]
    w = u.shape[1] // 2
    n_grp, gw = w_r.shape[0], w_r.shape[1]
    cw = conv_w.shape[0]
    ts = _tile(seq, 256)
    s_tiles = seq // ts
    row = lambda b, s: (b * s_tiles + s, 0)
    vec = pl.BlockSpec((1, w), lambda b, s: (0, 0))
    return pl.pallas_call(
        functools.partial(_lru_scan_kernel, ts=ts, n_grp=n_grp, gw=gw, conv_w=cw),
        grid=(bsz, s_tiles),
        in_specs=[pl.BlockSpec((ts, w), row),
                  pl.BlockSpec((ts, w), lambda b, s: (b * s_tiles + s, 1)),
                  pl.BlockSpec((cw, w), lambda b, s: (0, 0)),
                  vec,
                  pl.BlockSpec((n_grp, gw, gw), lambda b, s: (0, 0, 0)),
                  vec,
                  pl.BlockSpec((n_grp, gw, gw), lambda b, s: (0, 0, 0)),
                  vec,
                  vec],
        out_specs=pl.BlockSpec((ts, w), row),
        out_shape=jax.ShapeDtypeStruct((t, w), BF16),
        scratch_shapes=[pltpu.VMEM((ts + SUBLANES, w), F32),
                        pltpu.VMEM((ts, w), F32),
                        pltpu.VMEM((ts, w), F32),
                        pltpu.VMEM((SUBLANES, w), F32)],
        compiler_params=_cparams(("parallel", "arbitrary")),
        name="lru_scan",
    )(u, u, conv_w, conv_b.reshape(1, w), w_r, b_r.reshape(1, w), w_i, b_i.reshape(1, w),
      lam.reshape(1, w))


def _attn_kernel(q_ref, k_ref, v_ref, o_ref, kaug_ref, vaug_t_ref, *, nb, blk, dh, topk):
    seq = nb * blk
    nbp = vaug_t_ref.shape[0] - dh
    first = (pl.program_id(0) == 0) & (pl.program_id(1) == 0)

    @pl.when(first)
    def _():
        row = lax.broadcasted_iota(jnp.int32, (seq, LANES), 0)
        lane = lax.broadcasted_iota(jnp.int32, (seq, LANES), 1)
        kaug_ref[:, dh:dh + LANES] = (lane == row // blk).astype(BF16)
        vaug_t_ref[dh:dh + nbp, :] = (lax.broadcasted_iota(jnp.int32, (nbp, seq), 0) == 0).astype(BF16)

    kaug_ref[:, 0:dh] = k_ref[...]
    for j in range(nb):
        vaug_t_ref[0:dh, j * blk:(j + 1) * blk] = v_ref[j * blk:(j + 1) * blk, :].astype(F32).T.astype(BF16)
    kmean = jnp.sum(k_ref[...].astype(F32).reshape(nb, blk, dh), axis=1) * (1.0 / blk)
    if nbp > nb:
        kmean = jnp.concatenate([kmean, jnp.zeros((nbp - nb, dh), F32)], axis=0)
    kmean = kmean.astype(BF16)

    c = (dh ** -0.5) * 1.4426950408889634
    blk_id = lax.broadcasted_iota(jnp.int32, (nbp, blk), 0)
    key_pos = lax.broadcasted_iota(jnp.int32, (blk, blk), 0)
    q_pos = lax.broadcasted_iota(jnp.int32, (blk, blk), 1)
    zeros_pad = jnp.zeros((LANES - nbp, blk), BF16)

    def scores(i):
        q_t = q_ref[i * blk:(i + 1) * blk, :].astype(F32).T.astype(BF16)
        gate_t = jnp.dot(kmean, q_t, preferred_element_type=F32)
        g = jnp.where(blk_id < i, gate_t, NEG_INF)
        sel = blk_id == i
        for _ in range(topk):
            m = jnp.max(g, axis=0, keepdims=True)
            first_max = jnp.min(jnp.where(g == m, blk_id, nbp), axis=0, keepdims=True)
            hit = blk_id == first_max
            sel = sel | (hit & (blk_id < i))
            g = jnp.where(hit, -jnp.inf, g)
        bias_t = jnp.where(sel, 0.0, NEG_INF).astype(BF16)
        q_aug_t = jnp.concatenate([q_t, bias_t, zeros_pad], axis=0)
        return jnp.dot(kaug_ref[0:(i + 1) * blk, :], q_aug_t, preferred_element_type=F32)

    def probs(i, s):
        s_own = jnp.where(key_pos <= q_pos, s[i * blk:, :], NEG_INF)
        m = jnp.max(s_own, axis=0, keepdims=True)
        if i > 0:
            s_past = s[:i * blk, :]
            m = jnp.maximum(m, jnp.max(s_past, axis=0, keepdims=True))
            p = jnp.concatenate([jnp.exp2((s_past - m) * c), jnp.exp2((s_own - m) * c)], axis=0)
        else:
            p = jnp.exp2((s_own - m) * c)
        return p.astype(BF16)

    def pv(i, p):
        out_t = jnp.dot(vaug_t_ref[:, 0:(i + 1) * blk], p, preferred_element_type=F32)
        o_t = out_t[0:dh, :] / out_t[dh:dh + 1, :]
        o_ref[i * blk:(i + 1) * blk, :] = o_t.T.astype(o_ref.dtype)

    order = list(range(nb - 1, -1, -1))
    s_of, p_of = {}, {}
    for step in range(nb + 2):
        if step < nb:
            s_of[order[step]] = scores(order[step])
        if 1 <= step <= nb:
            t1 = order[step - 1]
            p_of[t1] = probs(t1, s_of.pop(t1))
        if step >= 2:
            t2 = order[step - 2]
            pv(t2, p_of.pop(t2))


def _moba_attention(qkv, bsz, seq, d):
    t = qkv.shape[0]
    dh = d // N_HEADS
    blk = MOBA_BLOCK
    nb = seq // blk
    assert seq % blk == 0 and nb <= BF16_SUBLANES and dh == LANES
    return pl.pallas_call(
        functools.partial(_attn_kernel, nb=nb, blk=blk, dh=dh, topk=min(MOBA_TOPK, nb)),
        grid=(bsz, N_HEADS),
        in_specs=[pl.BlockSpec((seq, dh), lambda b, h: (b, h)),
                  pl.BlockSpec((seq, dh), lambda b, h: (b, N_HEADS + h)),
                  pl.BlockSpec((seq, dh), lambda b, h: (b, 2 * N_HEADS + h))],
        out_specs=pl.BlockSpec((seq, dh), lambda b, h: (b, h)),
        out_shape=jax.ShapeDtypeStruct((t, d), BF16),
        scratch_shapes=[pltpu.VMEM((seq, dh + LANES), BF16),
                        pltpu.VMEM((dh + BF16_SUBLANES, seq), BF16)],
        compiler_params=_cparams(("arbitrary", "arbitrary")),
        name="moba_attention",
    )(qkv, qkv, qkv)


def _pack_bf16_pairs(lo, hi):
    lo_bits = lax.bitcast_convert_type(lo.astype(BF16).astype(F32), jnp.uint32)
    hi_bits = lax.bitcast_convert_type(hi.astype(BF16).astype(F32), jnp.uint32)
    return (hi_bits & jnp.uint32(0xFFFF0000)) | (lo_bits >> 16)


def _unpack_bf16_pairs(u):
    lo = lax.bitcast_convert_type(u << 16, F32)
    hi = lax.bitcast_convert_type(u & jnp.uint32(0xFFFF0000), F32)
    return lo, hi


def _store_token_slabs(ref, packed, slab_rows, first_tile=0):
    m, width = packed.shape
    for j in range(width // LANES):
        ref[pl.ds(first_tile + j, m, stride=slab_rows), :] = packed[:, j * LANES:(j + 1) * LANES]


def _load_token_slabs(ref, m, slab_rows):
    return jnp.concatenate([ref[pl.ds(j, m, stride=slab_rows), :] for j in range(slab_rows)], axis=1)


def _route_rows(h, g, w_cat, b_cat, n_grp, epg):
    xn = _rmsnorm_rows(h, g)
    hd = xn.shape[1] // 2
    xn_packed = _pack_bf16_pairs(xn[:, :hd], xn[:, hd:])
    logits = jnp.dot(xn.astype(BF16), w_cat, preferred_element_type=F32) + b_cat
    lane = lax.broadcasted_iota(jnp.int32, logits.shape, 1)

    lg = jnp.where(lane < n_grp, logits, -jnp.inf)
    mg = jnp.max(lg, axis=1, keepdims=True)
    gidx = jnp.min(jnp.where(lg == mg, lane, LANES), axis=1, keepdims=True)
    pg_sel = 1.0 / jnp.sum(jnp.exp(lg - mg), axis=1, keepdims=True)

    lo = n_grp + gidx * epg
    in_grp = (lane >= lo) & (lane < lo + epg)
    le = jnp.where(in_grp, logits, -jnp.inf)
    e = jnp.exp(le - jnp.max(le, axis=1, keepdims=True))
    pe = jnp.where(in_grp, e / jnp.sum(e, axis=1, keepdims=True), -1.0)
    v1 = jnp.max(pe, axis=1, keepdims=True)
    i1 = jnp.min(jnp.where(pe == v1, lane, LANES), axis=1, keepdims=True)
    pe2 = jnp.where(lane == i1, -1.0, pe)
    v2 = jnp.max(pe2, axis=1, keepdims=True)
    i2 = jnp.min(jnp.where(pe2 == v2, lane, LANES), axis=1, keepdims=True)
    den = v1 + v2
    ids = jnp.where(lane == 0, i1 - n_grp, jnp.where(lane == 1, i2 - n_grp, 0)).astype(F32)
    et = ids.T[0:SUBLANES, :].astype(jnp.int32)
    rf = jnp.where(lane == 0, pg_sel * v1 / den, jnp.where(lane == 1, pg_sel * v2 / den, 0.0))
    return xn_packed, et, rf


def _mix_out_router_kernel(a_ref, w_ref, *rest, has_bias, n_grp, epg):
    if has_bias:
        b_ref, rest = rest[0], rest[1:]
    r_ref, g_ref, wc_ref, bc_ref, h_ref, xn_ref, et_ref, rf_ref, h_prev = rest

    @pl.when(pl.program_id(0) == 0)
    def _():
        h_prev[...] = jnp.zeros_like(h_prev)

    xn_packed, et, rf = _route_rows(h_prev[...], g_ref[...], wc_ref[...], bc_ref[...], n_grp, epg)
    _store_token_slabs(xn_ref, xn_packed, xn_packed.shape[1] // LANES)
    et_ref[...] = et
    rf_ref[...] = rf

    y = jnp.dot(a_ref[...], w_ref[...], preferred_element_type=F32)
    if has_bias:
        y = y + b_ref[...]
    h = r_ref[...] + y
    h_ref[...] = h
    h_prev[...] = h


def _mix_out_router(a, w, b, res, g, w_cat, b_cat, n_grp, epg):
    t, k = a.shape
    d = w.shape[1]
    tm = _tile(t, 256)
    n = t // tm
    cur = lambda i: (jnp.minimum(i, n - 1), 0)
    prev = lambda i: (jnp.maximum(i - 1, 0), 0)
    has_bias = b is not None
    in_specs = [pl.BlockSpec((tm, k), cur),
                pl.BlockSpec((k, d), lambda i: (0, 0))]
    args = [a, w]
    if has_bias:
        in_specs.append(pl.BlockSpec((1, d), lambda i: (0, 0)))
        args.append(b.reshape(1, d))
    in_specs += [pl.BlockSpec((tm, d), cur),
                 pl.BlockSpec((1, d), lambda i: (0, 0)),
                 pl.BlockSpec((d, LANES), lambda i: (0, 0)),
                 pl.BlockSpec((1, LANES), lambda i: (0, 0))]
    args += [res, g.reshape(1, d), w_cat, b_cat]
    return pl.pallas_call(
        functools.partial(_mix_out_router_kernel, has_bias=has_bias, n_grp=n_grp, epg=epg),
        grid=(n + 1,),
        in_specs=in_specs,
        out_specs=[pl.BlockSpec((tm, d), cur),
                   pl.BlockSpec((tm * (d // 2 // LANES), LANES), prev),
                   pl.BlockSpec((SUBLANES, tm), lambda i: (0, jnp.maximum(i - 1, 0))),
                   pl.BlockSpec((tm, LANES), prev)],
        out_shape=[jax.ShapeDtypeStruct((t, d), F32),
                   jax.ShapeDtypeStruct((t * (d // 2 // LANES), LANES), jnp.uint32),
                   jax.ShapeDtypeStruct((SUBLANES, t), jnp.int32),
                   jax.ShapeDtypeStruct((t, LANES), F32)],
        scratch_shapes=[pltpu.VMEM((tm, d), F32)],
        compiler_params=_cparams(("arbitrary",)),
        name="mix_out_router",
    )(*args)


DISPATCH_CHUNK = 512
DISPATCH_SMEM_CHUNK = 4096


def _dispatch_kernel(et_ref, tok_ref, dst_ref, blk_ref, rank_ref, dest_ref, dest_sm, rows_sm, rows_vm,
                     sem, *, n_exp, topk, rb, slab_rows):
    t = et_ref.shape[1]
    c = DISPATCH_CHUNK
    e_iota = lax.broadcasted_iota(jnp.int32, (n_exp, c), 0)
    earlier = (lax.broadcasted_iota(jnp.int32, (c, c), 0) < lax.broadcasted_iota(jnp.int32, (c, c), 1))
    earlier = jnp.where(earlier, 1.0, 0.0).astype(BF16)

    run = jnp.zeros((n_exp, 1), F32)
    for k in range(topk):
        def rank_chunk(j, run, k=k):
            off = pl.multiple_of(j * c, c)
            oh = e_iota == et_ref[k:k + 1, pl.ds(off, c)]
            ohf = jnp.where(oh, 1.0, 0.0)
            pre = jnp.dot(ohf.astype(BF16), earlier, preferred_element_type=F32)
            rank_ref[k:k + 1, pl.ds(off, c)] = jnp.sum(jnp.where(oh, pre + run, 0.0), axis=0, keepdims=True)
            return run + jnp.sum(ohf, axis=1, keepdims=True)
        run = lax.fori_loop(0, t // c, rank_chunk, run)

    cnt = run
    nblk = jnp.floor((cnt + (rb - 1)) * (1.0 / rb))
    before = (lax.broadcasted_iota(jnp.int32, (n_exp, n_exp), 1) < lax.broadcasted_iota(jnp.int32, (n_exp, n_exp), 0))
    before = jnp.where(before, 1.0, 0.0).astype(BF16)
    bstart = jnp.dot(before, jnp.broadcast_to(nblk, (n_exp, LANES)).astype(BF16),
                     preferred_element_type=F32)[:, 0:1]

    for k in range(topk):
        def dest_chunk(j, carry, k=k):
            off = pl.multiple_of(j * c, c)
            oh = e_iota == et_ref[k:k + 1, pl.ds(off, c)]
            base = jnp.sum(jnp.where(oh, bstart, 0.0), axis=0, keepdims=True) * rb
            dest_ref[k:k + 1, pl.ds(off, c)] = (base + rank_ref[k:k + 1, pl.ds(off, c)]).astype(jnp.int32)
            return carry
        lax.fori_loop(0, t // c, dest_chunk, 0)

    rows_vm[...] = jnp.full(rows_vm.shape, -1, jnp.int32)
    fill = pltpu.make_async_copy(rows_vm, rows_sm, sem)
    fill.start()
    fill.wait()
    sc = DISPATCH_SMEM_CHUNK
    for k in range(topk):
        def scatter_chunk(j, carry, k=k):
            off = pl.multiple_of(j * sc, sc)
            cp = pltpu.make_async_copy(dest_ref.at[pl.ds(k, 1), pl.ds(off, sc)], dest_sm, sem)
            cp.start()
            cp.wait()

            def one(a, carry2):
                d = dest_sm[0, a]
                rows_sm[lax.shift_right_logical(d, LANE_SHIFT), d & (LANES - 1)] = k * t + off + a
                return carry2
            return lax.fori_loop(0, sc, one, carry, unroll=8)
        lax.fori_loop(0, t // sc, scatter_chunk, 0)
    back = pltpu.make_async_copy(rows_sm, rows_vm, sem)
    back.start()
    back.wait()

    a = rows_vm[...]
    valid = a >= 0
    tok = a
    for k in range(1, topk):
        tok = jnp.where(a >= k * t, a - k * t, tok)
    flat = (lax.broadcasted_iota(jnp.int32, a.shape, 0) * LANES + lax.broadcasted_iota(jnp.int32, a.shape, 1))
    scratch_dst = topk * t + ((flat // rb) % 2) * rb + flat % rb
    tok_ref[...] = jnp.where(valid, tok, 0) * slab_rows
    dst_ref[...] = jnp.where(valid, a, scratch_dst) * slab_rows

    nbp = blk_ref.shape[1]
    b_row = lax.broadcasted_iota(jnp.int32, (1, nbp), 1).astype(F32)
    bend = bstart + nblk
    blk_e = jnp.minimum(jnp.sum(jnp.where(bend <= b_row, 1.0, 0.0), axis=0, keepdims=True), n_exp - 1.0)
    mine = lax.broadcasted_iota(jnp.int32, (n_exp, nbp), 0).astype(F32) == blk_e
    cnt_b = jnp.sum(jnp.where(mine, cnt, 0.0), axis=0, keepdims=True)
    bstart_b = jnp.sum(jnp.where(mine, bstart, 0.0), axis=0, keepdims=True)
    blk_n = jnp.clip(cnt_b - (b_row - bstart_b) * rb, 0.0, float(rb))
    row = lax.broadcasted_iota(jnp.int32, blk_ref.shape, 0)
    blk_ref[...] = jnp.where(row == 0, blk_e, jnp.where(row == 1, blk_n, 0.0)).astype(jnp.int32)


def _dispatch(et, n_exp, rb, slab_rows):
    t = et.shape[1]
    topk = EXPERT_TOPK
    n_asg = topk * t
    assert n_asg % rb == 0 and t % DISPATCH_SMEM_CHUNK == 0 and (rb & (rb - 1)) == 0 and rb % LANES == 0
    n_rb = n_asg // rb + n_exp
    n_rows = n_rb * rb
    nbp = -(-n_rb // LANES) * LANES
    tok, dst, blk = pl.pallas_call(
        functools.partial(_dispatch_kernel, n_exp=n_exp, topk=topk, rb=rb, slab_rows=slab_rows),
        in_specs=[pl.BlockSpec(memory_space=pltpu.VMEM)],
        out_specs=[pl.BlockSpec(memory_space=pltpu.VMEM)] * 3,
        out_shape=[jax.ShapeDtypeStruct((n_rows // LANES, LANES), jnp.int32),
                   jax.ShapeDtypeStruct((n_rows // LANES, LANES), jnp.int32),
                   jax.ShapeDtypeStruct((SUBLANES, nbp), jnp.int32)],
        scratch_shapes=[pltpu.VMEM((SUBLANES, t), F32),
                        pltpu.VMEM((SUBLANES, t), jnp.int32),
                        pltpu.SMEM((1, DISPATCH_SMEM_CHUNK), jnp.int32),
                        pltpu.SMEM((n_rows // LANES, LANES), jnp.int32),
                        pltpu.VMEM((n_rows // LANES, LANES), jnp.int32),
                        pltpu.SemaphoreType.DMA(())],
        compiler_params=pltpu.CompilerParams(vmem_limit_bytes=VMEM_LIMIT_BYTES),
        name="moe_dispatch",
    )(et)
    return tok.reshape(n_rb, rb), dst.reshape(n_rb, rb), blk[0, :n_rb], blk[1, :n_rb]


WEIGHT_CHUNK_ROWS = 512
WEIGHT_RING = 4


def _expert_kernel(blk_e, blk_n, tok_ref, tok_next_ref, dst_prev_ref, xn_hbm, wg_hbm, wu_hbm, wd_hbm,
                   y_hbm, xbuf, obuf, wbuf_g, wbuf_u, wbuf_d, stg, run_end, next_e, run_start, wslot_of,
                   succ_e, st, gsem, ssem, wsem, *, n_rb, rb, layer):
    i = pl.program_id(0)
    slot = lax.rem(i, 2)
    n = blk_n[i]
    d, f = wbuf_g.shape[1:]
    cr = stg.shape[1]
    n_gu = d // cr
    n_dc = d // f
    n_chunks = 2 * n_gu + (f // cr) * n_dc
    ring = stg.shape[0]
    G_DONE, P_EXP, P_CHUNK = 0, 1, 2

    chunks = ([(wg_hbm, wbuf_g, r * cr, 0) for r in range(n_gu)]
              + [(wu_hbm, wbuf_u, r * cr, 0) for r in range(n_gu)]
              + [(wd_hbm, wbuf_d, r * cr, h * f) for r in range(f // cr) for h in range(n_dc)])
    assert len(chunks) == n_chunks

    def chunk_copy(e, c, k):
        for cid, (w_hbm, _, r0, c0) in enumerate(chunks):
            @pl.when(c == cid)
            def _(w_hbm=w_hbm, r0=r0, c0=c0):
                pltpu.make_async_copy(w_hbm.at[layer, e, pl.ds(r0, cr), pl.ds(c0, f)], stg.at[k],
                                      wsem.at[k]).start()

    def start_next_chunk(k):
        e = st[P_EXP]
        c = st[P_CHUNK]

        @pl.when(e >= 0)
        def _():
            chunk_copy(e, c, k)
            last = c + 1 == n_chunks
            st[P_CHUNK] = jnp.where(last, 0, c + 1)
            st[P_EXP] = jnp.where(last, succ_e[e], e)

    def convert_chunk(c, ws):
        g = st[G_DONE]
        k = lax.rem(g, ring)
        pltpu.make_async_copy(wg_hbm.at[layer, 0, pl.ds(0, cr), :], stg.at[k], wsem.at[k]).wait()
        start_next_chunk(lax.rem(g + ring - 1, ring))
        for cid, (_, wbuf, r0, c0) in enumerate(chunks):
            @pl.when(c == cid)
            def _(wbuf=wbuf, r0=r0, c0=c0):
                wbuf[ws, r0:r0 + cr, c0:c0 + f] = stg[k].astype(BF16)
        st[G_DONE] = g + 1

    @pl.when(i == 0)
    def _():
        def back(ii, carry):
            end_nb, next_nb = carry
            b = n_rb - 1 - ii
            nb = jnp.minimum(b + 1, n_rb - 1)
            nb_active = (b + 1 < n_rb) & (blk_n[nb] > 0)
            same = nb_active & (blk_e[nb] == blk_e[b])
            end_b = jnp.where(same, end_nb, b + 1)
            next_b = jnp.where(same, next_nb, jnp.where(nb_active, blk_e[nb], -1))
            run_end[b] = end_b
            next_e[b] = next_b

            @pl.when((blk_n[b] > 0) & jnp.logical_not(same))
            def _():
                succ_e[blk_e[b]] = next_b
            return end_b, next_b
        lax.fori_loop(0, n_rb, back, (jnp.int32(n_rb), jnp.int32(-1)))

        def fwd(b, carry):
            start_pb, wslot_pb = carry
            pb = jnp.maximum(b - 1, 0)
            same = (b > 0) & (blk_e[pb] == blk_e[b])
            start_b = jnp.where(same, start_pb, b)
            wslot_b = jnp.where(b == 0, 0, jnp.where(same, wslot_pb, 1 - wslot_pb))
            run_start[b] = start_b
            wslot_of[b] = wslot_b
            return start_b, wslot_b
        lax.fori_loop(0, n_rb, fwd, (jnp.int32(0), jnp.int32(0)))

        st[G_DONE] = 0
        st[P_CHUNK] = 0
        st[P_EXP] = jnp.where(n > 0, blk_e[0], -1)

        @pl.when(n > 0)
        def _():
            for k in range(ring - 1):
                start_next_chunk(k)

            def first(c, carry):
                convert_chunk(c, 0)
                return carry
            lax.fori_loop(0, n_chunks, first, 0)

    ws = wslot_of[i]

    @pl.when((n > 0) & (next_e[i] >= 0))
    def _():
        run_len = run_end[i] - run_start[i]
        per = lax.div(n_chunks + run_len - 1, run_len)
        lo = (i - run_start[i]) * per
        hi = jnp.minimum(lo + per, n_chunks)

        def nxt(c, carry):
            convert_chunk(c, 1 - ws)
            return carry
        lax.fori_loop(lo, hi, nxt, 0)

    ns = xbuf.shape[1] // rb

    def gather_row(tref, sl, r):
        return pltpu.make_async_copy(xn_hbm.at[pl.ds(tref[0, 0, r], ns)], xbuf.at[sl, pl.ds(r * ns, ns)],
                                     gsem.at[sl])

    def start_gather(tref, sl):
        def body(r, c):
            gather_row(tref, sl, r).start()
            return c
        lax.fori_loop(0, rb, body, 0, unroll=8)

    def scatter_row(sl, r, dst):
        return pltpu.make_async_copy(obuf.at[sl, pl.ds(r * ns, ns)], y_hbm.at[pl.ds(dst, ns)], ssem.at[sl])

    def wait_scatter(sl):
        pltpu.make_async_copy(obuf.at[sl], y_hbm.at[pl.ds(0, rb * ns)], ssem.at[sl]).wait()

    @pl.when(i == 0)
    def _():
        obuf[...] = jnp.zeros_like(obuf)
        n_real = y_hbm.shape[0] - 2 * rb * ns
        for sl in range(2):
            init = pltpu.make_async_copy(obuf.at[sl], y_hbm.at[pl.ds(n_real + sl * rb * ns, rb * ns)],
                                         ssem.at[sl])
            init.start()
            init.wait()

    n_prev = blk_n[jnp.maximum(i - 1, 0)]
    n_next = blk_n[jnp.minimum(i + 1, n_rb - 1)]
    do_gather = (i + 1 < n_rb) & (n_next > 0)
    do_scatter = (i >= 1) & (n_prev > 0)

    @pl.when((i == 0) & (n > 0))
    def _():
        start_gather(tok_ref, 0)

    @pl.when(n > 0)
    def _():
        pltpu.make_async_copy(xn_hbm.at[pl.ds(0, rb * ns)], xbuf.at[slot], gsem.at[slot]).wait()

    hd = ns * LANES
    fc, dc = min(f, MXU_COLS), min(hd, MXU_COLS)

    def free_obuf():
        @pl.when((i >= 2) & (blk_n[jnp.maximum(i - 2, 0)] > 0))
        def _():
            wait_scatter(slot)

    def compute(interleave_dma):
        def issue(piece, n_piece, start_row):
            if interleave_dma:
                per = -(-rb // n_piece)
                for r in range(piece * per, min(rb, (piece + 1) * per)):
                    start_row(r)

        x_lo, x_hi = _unpack_bf16_pairs(_load_token_slabs(xbuf.at[slot], rb, ns))
        x = jnp.concatenate([x_lo.astype(BF16), x_hi.astype(BF16)], axis=1)
        hid = []
        for c in range(f // fc):
            cols = slice(c * fc, (c + 1) * fc)
            g = jnp.dot(x, wbuf_g[ws, :, cols], preferred_element_type=F32)
            u = jnp.dot(x, wbuf_u[ws, :, cols], preferred_element_type=F32)
            hid.append(((g * jax.nn.sigmoid(g)) * u).astype(BF16))
            issue(c, f // fc, lambda r: gather_row(tok_next_ref, 1 - slot, r).start(priority=r % 2))
        hid = jnp.concatenate(hid, axis=1)
        free_obuf()
        for c in range(hd // dc):
            cols = slice(c * dc, (c + 1) * dc)
            cols_hi = slice(hd + c * dc, hd + (c + 1) * dc)
            y_lo = jnp.dot(hid, wbuf_d[ws, :, cols], preferred_element_type=F32)
            y_hi = jnp.dot(hid, wbuf_d[ws, :, cols_hi], preferred_element_type=F32)
            _store_token_slabs(obuf.at[slot], _pack_bf16_pairs(y_lo, y_hi), ns, first_tile=c * dc // LANES)
            issue(c, hd // dc,
                  lambda r: scatter_row(1 - slot, r, dst_prev_ref[0, 0, r]).start(priority=r % 2))

    fast = (n > 0) & do_gather & do_scatter

    @pl.when(fast)
    def _():
        compute(True)

    @pl.when(jnp.logical_not(fast))
    def _():
        @pl.when(do_gather)
        def _():
            start_gather(tok_next_ref, 1 - slot)

        @pl.when(do_scatter)
        def _():
            def body(r, c):
                scatter_row(1 - slot, r, dst_prev_ref[0, 0, r]).start()
                return c
            lax.fori_loop(0, rb, body, 0, unroll=8)

        @pl.when(n > 0)
        def _():
            compute(False)

        @pl.when(n == 0)
        def _():
            free_obuf()

    @pl.when((i == n_rb - 1) & do_scatter)
    def _():
        wait_scatter(1 - slot)


def _experts(xn, row_tok, row_dst, blk_e, blk_n, w_gate, w_up, w_down, layer, n_slots):
    n_exp, d, f = w_gate.shape[1:]
    ns = d // 2 // LANES
    assert xn.shape[1] == LANES and d % (2 * LANES) == 0 and w_down.shape[1:] == (n_exp, f, d)
    n_rb, rb = row_tok.shape
    assert (n_slots - 2 * rb) % rb == 0
    cr = min(WEIGHT_CHUNK_ROWS, f)
    assert d % cr == 0 and f % cr == 0 and d % f == 0
    tok3 = row_tok.reshape(n_rb, 1, rb)
    dst3 = row_dst.reshape(n_rb, 1, rb)
    smem_blk = lambda f_: pl.BlockSpec((1, 1, rb), f_, memory_space=pltpu.SMEM)
    hbm = pl.BlockSpec(memory_space=pl.ANY)
    grid_spec = pltpu.PrefetchScalarGridSpec(
        num_scalar_prefetch=2,
        grid=(n_rb,),
        in_specs=[smem_blk(lambda i, be, bn: (i, 0, 0)),
                  smem_blk(lambda i, be, bn: (jnp.minimum(i + 1, n_rb - 1), 0, 0)),
                  smem_blk(lambda i, be, bn: (jnp.maximum(i - 1, 0), 0, 0)),
                  hbm, hbm, hbm, hbm],
        out_specs=hbm,
        scratch_shapes=[pltpu.VMEM((2, rb * ns, LANES), jnp.uint32),
                        pltpu.VMEM((2, rb * ns, LANES), jnp.uint32),
                        pltpu.VMEM((2, d, f), BF16),
                        pltpu.VMEM((2, d, f), BF16),
                        pltpu.VMEM((2, f, d), BF16),
                        pltpu.VMEM((WEIGHT_RING, cr, f), F32),
                        pltpu.SMEM((n_rb,), jnp.int32),
                        pltpu.SMEM((n_rb,), jnp.int32),
                        pltpu.SMEM((n_rb,), jnp.int32),
                        pltpu.SMEM((n_rb,), jnp.int32),
                        pltpu.SMEM((n_exp,), jnp.int32),
                        pltpu.SMEM((4,), jnp.int32),
                        pltpu.SemaphoreType.DMA((2,)),
                        pltpu.SemaphoreType.DMA((2,)),
                        pltpu.SemaphoreType.DMA((WEIGHT_RING,))],
    )
    return pl.pallas_call(
        functools.partial(_expert_kernel, n_rb=n_rb, rb=rb, layer=layer),
        grid_spec=grid_spec,
        out_shape=jax.ShapeDtypeStruct((n_slots * ns, LANES), jnp.uint32),
        compiler_params=_cparams(("arbitrary",)),
        name="moe_experts",
    )(blk_e, blk_n, tok3, tok3, dst3, xn, w_gate, w_up, w_down)


def _combine_kernel(h_ref, y0_ref, y1_ref, rf_ref, *rest, final):
    if final:
        g_ref, o_ref = rest
    else:
        (o_ref,) = rest
    w = rf_ref[...]
    m = h_ref.shape[0]
    ns = y0_ref.shape[0] // m
    y0_lo, y0_hi = _unpack_bf16_pairs(_load_token_slabs(y0_ref, m, ns))
    y1_lo, y1_hi = _unpack_bf16_pairs(_load_token_slabs(y1_ref, m, ns))
    y = jnp.concatenate([w[:, 0:1] * y0_lo + w[:, 1:2] * y1_lo, w[:, 0:1] * y0_hi + w[:, 1:2] * y1_hi], axis=1)
    out = h_ref[...] + y
    if final:
        out = _rmsnorm_rows(out, g_ref[...])
    o_ref[...] = out


def _combine(h, y, rf, final_g):
    t, d = h.shape
    tm = _tile(t, 512)
    final = final_g is not None
    in_specs = [pl.BlockSpec((tm, d), lambda i: (i, 0)),
                pl.BlockSpec((tm * (d // 2 // LANES), LANES), lambda i: (i, 0)),
                pl.BlockSpec((tm * (d // 2 // LANES), LANES), lambda i: (t // tm + i, 0)),
                pl.BlockSpec((tm, LANES), lambda i: (i, 0))]
    args = [h, y, y, rf]
    if final:
        in_specs.append(pl.BlockSpec((1, d), lambda i: (0, 0)))
        args.append(final_g.reshape(1, d))
    return pl.pallas_call(
        functools.partial(_combine_kernel, final=final),
        grid=(t // tm,),
        in_specs=in_specs,
        out_specs=pl.BlockSpec((tm, d), lambda i: (i, 0)),
        out_shape=jax.ShapeDtypeStruct((t, d), F32),
        compiler_params=_cparams(("parallel",)),
        name="moe_combine",
    )(*args)


def _mix_out_moe(act, w_out, b_out, h, norm_g, w_grp, b_grp, w_exp, b_exp, w_gate, w_up, w_down, layer, final_g):
    t, d = h.shape
    n_grp = w_grp.shape[1]
    n_exp = w_exp.shape[1]
    epg = n_exp // n_grp
    assert n_grp + n_exp <= LANES
    pad = LANES - n_grp - n_exp
    w_cat = jnp.concatenate([w_grp, w_exp, jnp.zeros((d, pad), F32)], axis=1).astype(BF16)
    b_cat = jnp.concatenate([b_grp, b_exp, jnp.zeros((pad,), F32)]).reshape(1, LANES)
    h, xn, et, rf = _mix_out_router(act, w_out, b_out, h, norm_g, w_cat, b_cat, n_grp, epg)
    row_tok, row_dst, blk_e, blk_n = _dispatch(et, n_exp, ROW_BLOCK, d // 2 // LANES)
    y = _experts(xn, row_tok, row_dst, blk_e, blk_n, w_gate, w_up, w_down, layer,
                 EXPERT_TOPK * t + 2 * ROW_BLOCK)
    return _combine(h, y, rf, final_g)


def _rope_tables(seq, dh):
    inv = 1.0 / (ROPE_THETA ** (jnp.arange(0, dh, 2, dtype=F32) / dh))
    ang = jnp.arange(seq, dtype=F32)[:, None] * inv[None, :]
    cos, sin = jnp.cos(ang), jnp.sin(ang)
    return jnp.concatenate([cos, cos], axis=1), jnp.concatenate([-sin, sin], axis=1)


def kernel(x, lru_norm, lru_w_in, lru_b_in, lru_conv_w, lru_conv_b, lru_w_r, lru_b_r, lru_w_i, lru_b_i, lru_lambda, lru_w_out, lru_b_out, att_norm, att_w_qkv, att_w_o, ffn_norm, moe_w_grp, moe_b_grp, moe_w_exp, moe_b_exp, moe_w_gate, moe_w_up, moe_w_down, final_norm):
    bsz, seq, d = x.shape
    depth = ffn_norm.shape[0]
    n_mixers = 2
    cc, ss = _rope_tables(seq, d // N_HEADS)
    h = x.reshape(bsz * seq, d)
    for layer in range(depth):
        j = layer // n_mixers
        if layer % n_mixers == 0:
            u = _lru_in(h, lru_norm[j], lru_w_in[j].astype(BF16), lru_b_in[j])
            act = _lru_scan(u, lru_conv_w[j], lru_conv_b[j], lru_w_r[j].astype(BF16), lru_b_r[j],
                            lru_w_i[j].astype(BF16), lru_b_i[j], lru_lambda[j], bsz, seq)
            w_out, b_out = lru_w_out[j].astype(BF16), lru_b_out[j]
        else:
            qkv = _qkv_rope(h, att_norm[j], att_w_qkv[j].astype(BF16), cc, ss, seq)
            act = _moba_attention(qkv, bsz, seq, d)
            w_out, b_out = att_w_o[j].astype(BF16), None
        final_g = final_norm if layer == depth - 1 else None
        h = _mix_out_moe(act, w_out, b_out, h, ffn_norm[layer], moe_w_grp[layer], moe_b_grp[layer],
                         moe_w_exp[layer], moe_b_exp[layer], moe_w_gate, moe_w_up, moe_w_down, layer, final_g)
    return h.reshape(bsz, seq, d)
```

```python
import functools

import jax
import jax.numpy as jnp
from jax import lax
from jax.experimental import pallas as pl
from jax.experimental.pallas import tpu as pltpu

F32 = jnp.float32
BF16 = jnp.bfloat16

NORM_EPS = 1e-6
NEG_INF = -1e30
LRU_C = 8.0
N_HEADS = 16
MOBA_BLOCK = 256
MOBA_TOPK = 3
ROPE_THETA = 10000.0
EXPERT_TOPK = 2
ROW_BLOCK = 256
LANES = 128
LANE_SHIFT = 7
SUBLANES = 8
BF16_SUBLANES = 16
MXU_COLS = 256
VMEM_LIMIT_BYTES = 56 * 1024 * 1024


def _tile(n, pref):
    t = min(n, pref)
    while n % t:
        t //= 2
    return t


def _cparams(sem):
    return pltpu.CompilerParams(dimension_semantics=sem, vmem_limit_bytes=VMEM_LIMIT_BYTES)


def _rmsnorm_rows(x, g):
    ms = jnp.mean(x * x, axis=-1, keepdims=True)
    return x * lax.rsqrt(ms + NORM_EPS) * g


def _lru_in_kernel(x_ref, g_ref, w_ref, b_ref, o_ref, xn_ref, *, n_plain):
    j = pl.program_id(1)

    @pl.when(j == 0)
    def _():
        xn_ref[...] = _rmsnorm_rows(x_ref[...], g_ref[...]).astype(BF16)

    acc = jnp.dot(xn_ref[...], w_ref[...], preferred_element_type=F32) + b_ref[...]

    @pl.when(j < n_plain)
    def _():
        o_ref[...] = acc

    @pl.when(j >= n_plain)
    def _():
        o_ref[...] = jax.nn.gelu(acc, approximate=True)


def _lru_in(h, g, w, b):
    t, d = h.shape
    n = w.shape[1]
    tm, tn = _tile(t, 1024), _tile(n // 2, 1024)
    return pl.pallas_call(
        functools.partial(_lru_in_kernel, n_plain=(n // 2) // tn),
        grid=(t // tm, n // tn),
        in_specs=[pl.BlockSpec((tm, d), lambda i, j: (i, 0)),
                  pl.BlockSpec((1, d), lambda i, j: (0, 0)),
                  pl.BlockSpec((d, tn), lambda i, j: (0, j)),
                  pl.BlockSpec((1, tn), lambda i, j: (0, j))],
        out_specs=pl.BlockSpec((tm, tn), lambda i, j: (i, j)),
        out_shape=jax.ShapeDtypeStruct((t, n), F32),
        scratch_shapes=[pltpu.VMEM((tm, d), BF16)],
        compiler_params=_cparams(("parallel", "arbitrary")),
        name="lru_in",
    )(h, g.reshape(1, d), w, b.reshape(1, n))


def _qkv_kernel(x_ref, g_ref, w_ref, cc_ref, ss_ref, o_ref, xn_ref, *, n_rope, heads_per_tile, dh):
    j = pl.program_id(1)

    @pl.when(j == 0)
    def _():
        xn_ref[...] = _rmsnorm_rows(x_ref[...], g_ref[...]).astype(BF16)

    acc = jnp.dot(xn_ref[...], w_ref[...], preferred_element_type=F32)

    is_rope = j < n_rope
    cc = cc_ref[...]
    ss = ss_ref[...]
    for hh in range(heads_per_tile):
        a = acc[:, hh * dh:(hh + 1) * dh]
        rot = a * cc + pltpu.roll(a, dh // 2, 1) * ss
        o_ref[:, hh * dh:(hh + 1) * dh] = jnp.where(is_rope, rot, a).astype(o_ref.dtype)


def _qkv_rope(h, g, w, cc, ss, seq):
    t, d = h.shape
    n = w.shape[1]
    dh = d // N_HEADS
    tm, tn = _tile(seq, 1024), _tile(d, 1024)
    s_tiles = seq // tm
    return pl.pallas_call(
        functools.partial(_qkv_kernel, n_rope=(2 * d) // tn, heads_per_tile=tn // dh, dh=dh),
        grid=(t // tm, n // tn),
        in_specs=[pl.BlockSpec((tm, d), lambda i, j: (i, 0)),
                  pl.BlockSpec((1, d), lambda i, j: (0, 0)),
                  pl.BlockSpec((d, tn), lambda i, j: (0, j)),
                  pl.BlockSpec((tm, dh), lambda i, j: (i % s_tiles, 0)),
                  pl.BlockSpec((tm, dh), lambda i, j: (i % s_tiles, 0))],
        out_specs=pl.BlockSpec((tm, tn), lambda i, j: (i, j)),
        out_shape=jax.ShapeDtypeStruct((t, n), BF16),
        scratch_shapes=[pltpu.VMEM((tm, d), BF16)],
        compiler_params=_cparams(("parallel", "arbitrary")),
        name="qkv_rope",
    )(h, g.reshape(1, d), w, cc, ss)


def _lru_scan_kernel(xb_ref, gate_ref, cw_ref, cb_ref, wr_ref, br_ref, wi_ref, bi_ref, lam_ref,
                     o_ref, ext_ref, a_ref, b_ref, h_ref, *, ts, n_grp, gw, conv_w):
    s = pl.program_id(1)
    w = xb_ref.shape[1]

    @pl.when(s == 0)
    def _():
        ext_ref[0:SUBLANES, :] = jnp.zeros((SUBLANES, w), F32)
        h_ref[...] = jnp.zeros_like(h_ref)

    ext_ref[SUBLANES:SUBLANES + ts, :] = xb_ref[...]
    xc = cb_ref[...] + cw_ref[conv_w - 1:conv_w, :] * xb_ref[...]
    for k in range(1, conv_w):
        xc = xc + cw_ref[conv_w - 1 - k:conv_w - k, :] * ext_ref[SUBLANES - k:SUBLANES - k + ts, :]
    ext_ref[0:SUBLANES, :] = ext_ref[ts:ts + SUBLANES, :]

    z = -lam_ref[...]
    softplus = jnp.maximum(z, 0.0) + jnp.log1p(jnp.exp(-jnp.abs(z)))
    c = -LRU_C * softplus
    for g in range(n_grp):
        sl = slice(g * gw, (g + 1) * gw)
        xg = xc[:, sl]
        xg16 = xg.astype(BF16)
        r = jax.nn.sigmoid(jnp.dot(xg16, wr_ref[g], preferred_element_type=F32) + br_ref[:, sl])
        ig = jax.nn.sigmoid(jnp.dot(xg16, wi_ref[g], preferred_element_type=F32) + bi_ref[:, sl])
        log_a = c[:, sl] * r
        a_ref[:, sl] = jnp.exp(log_a)
        th = jnp.tanh(log_a)
        b_ref[:, sl] = jnp.sqrt(-2.0 * th / (1.0 - th)) * ig * xg

    rows = lax.broadcasted_iota(jnp.int32, (SUBLANES, w), 0)

    def chunk(ci, h):
        r0 = pl.multiple_of(ci * SUBLANES, SUBLANES)
        a = a_ref[pl.ds(r0, SUBLANES), :]
        b = b_ref[pl.ds(r0, SUBLANES), :]
        for sh in (1, 2, 4):
            keep = rows >= sh
            a_sh = jnp.where(keep, pltpu.roll(a, sh, 0), 1.0)
            b_sh = jnp.where(keep, pltpu.roll(b, sh, 0), 0.0)
            b = a * b_sh + b
            a = a * a_sh
        hc = a * h + b
        b_ref[pl.ds(r0, SUBLANES), :] = hc * gate_ref[pl.ds(r0, SUBLANES), :]
        return jnp.broadcast_to(hc[SUBLANES - 1:SUBLANES, :], (SUBLANES, w))

    h_ref[...] = lax.fori_loop(0, ts // SUBLANES, chunk, h_ref[...])
    o_ref[...] = b_ref[...].astype(o_ref.dtype)


def _lru_scan(u, conv_w, conv_b, w_r, b_r, w_i, b_i, lam, bsz, seq):
    t = u.shape[0]
    w = u.shape[1] // 2
    n_grp, gw = w_r.shape[0], w_r.shape[1]
    cw = conv_w.shape[0]
    ts = _tile(seq, 256)
    s_tiles = seq // ts
    row = lambda b, s: (b * s_tiles + s, 0)
    vec = pl.BlockSpec((1, w), lambda b, s: (0, 0))
    return pl.pallas_call(
        functools.partial(_lru_scan_kernel, ts=ts, n_grp=n_grp, gw=gw, conv_w=cw),
        grid=(bsz, s_tiles),
        in_specs=[pl.BlockSpec((ts, w), row),
                  pl.BlockSpec((ts, w), lambda b, s: (b * s_tiles + s, 1)),
                  pl.BlockSpec((cw, w), lambda b, s: (0, 0)),
                  vec,
                  pl.BlockSpec((n_grp, gw, gw), lambda b, s: (0, 0, 0)),
                  vec,
                  pl.BlockSpec((n_grp, gw, gw), lambda b, s: (0, 0, 0)),
                  vec,
                  vec],
        out_specs=pl.BlockSpec((ts, w), row),
        out_shape=jax.ShapeDtypeStruct((t, w), BF16),
        scratch_shapes=[pltpu.VMEM((ts + SUBLANES, w), F32),
                        pltpu.VMEM((ts, w), F32),
                        pltpu.VMEM((ts, w), F32),
                        pltpu.VMEM((SUBLANES, w), F32)],
        compiler_params=_cparams(("parallel", "arbitrary")),
        name="lru_scan",
    )(u, u, conv_w, conv_b.reshape(1, w), w_r, b_r.reshape(1, w), w_i, b_i.reshape(1, w),
      lam.reshape(1, w))


def _attn_kernel(q_ref, k_ref, v_ref, o_ref, kaug_ref, vaug_t_ref, *, nb, blk, dh, topk):
    seq = nb * blk
    nbp = vaug_t_ref.shape[0] - dh
    first = (pl.program_id(0) == 0) & (pl.program_id(1) == 0)

    @pl.when(first)
    def _():
        row = lax.broadcasted_iota(jnp.int32, (seq, LANES), 0)
        lane = lax.broadcasted_iota(jnp.int32, (seq, LANES), 1)
        kaug_ref[:, dh:dh + LANES] = (lane == row // blk).astype(BF16)
        vaug_t_ref[dh:dh + nbp, :] = (lax.broadcasted_iota(jnp.int32, (nbp, seq), 0) == 0).astype(BF16)

    kaug_ref[:, 0:dh] = k_ref[...]
    for j in range(nb):
        vaug_t_ref[0:dh, j * blk:(j + 1) * blk] = v_ref[j * blk:(j + 1) * blk, :].astype(F32).T.astype(BF16)
    kmean = jnp.sum(k_ref[...].astype(F32).reshape(nb, blk, dh), axis=1) * (1.0 / blk)
    if nbp > nb:
        kmean = jnp.concatenate([kmean, jnp.zeros((nbp - nb, dh), F32)], axis=0)
    kmean = kmean.astype(BF16)

    c = (dh ** -0.5) * 1.4426950408889634
    blk_id = lax.broadcasted_iota(jnp.int32, (nbp, blk), 0)
    key_pos = lax.broadcasted_iota(jnp.int32, (blk, blk), 0)
    q_pos = lax.broadcasted_iota(jnp.int32, (blk, blk), 1)
    zeros_pad = jnp.zeros((LANES - nbp, blk), BF16)

    def scores(i):
        q_t = q_ref[i * blk:(i + 1) * blk, :].astype(F32).T.astype(BF16)
        gate_t = jnp.dot(kmean, q_t, preferred_element_type=F32)
        g = jnp.where(blk_id < i, gate_t, NEG_INF)
        sel = blk_id == i
        for _ in range(topk):
            m = jnp.max(g, axis=0, keepdims=True)
            first_max = jnp.min(jnp.where(g == m, blk_id, nbp), axis=0, keepdims=True)
            hit = blk_id == first_max
            sel = sel | (hit & (blk_id < i))
            g = jnp.where(hit, -jnp.inf, g)
        bias_t = jnp.where(sel, 0.0, NEG_INF).astype(BF16)
        q_aug_t = jnp.concatenate([q_t, bias_t, zeros_pad], axis=0)
        return jnp.dot(kaug_ref[0:(i + 1) * blk, :], q_aug_t, preferred_element_type=F32)

    def probs(i, s):
        s_own = jnp.where(key_pos <= q_pos, s[i * blk:, :], NEG_INF)
        m = jnp.max(s_own, axis=0, keepdims=True)
        if i > 0:
            s_past = s[:i * blk, :]
            m = jnp.maximum(m, jnp.max(s_past, axis=0, keepdims=True))
            p = jnp.concatenate([jnp.exp2((s_past - m) * c), jnp.exp2((s_own - m) * c)], axis=0)
        else:
            p = jnp.exp2((s_own - m) * c)
        return p.astype(BF16)

    def pv(i, p):
        out_t = jnp.dot(vaug_t_ref[:, 0:(i + 1) * blk], p, preferred_element_type=F32)
        o_t = out_t[0:dh, :] / out_t[dh:dh + 1, :]
        o_ref[i * blk:(i + 1) * blk, :] = o_t.T.astype(o_ref.dtype)

    order = list(range(nb - 1, -1, -1))
    s_of, p_of = {}, {}
    for step in range(nb + 2):
        if step < nb:
            s_of[order[step]] = scores(order[step])
        if 1 <= step <= nb:
            t1 = order[step - 1]
            p_of[t1] = probs(t1, s_of.pop(t1))
        if step >= 2:
            t2 = order[step - 2]
            pv(t2, p_of.pop(t2))


def _moba_attention(qkv, bsz, seq, d):
    t = qkv.shape[0]
    dh = d // N_HEADS
    blk = MOBA_BLOCK
    nb = seq // blk
    assert seq % blk == 0 and nb <= BF16_SUBLANES and dh == LANES
    return pl.pallas_call(
        functools.partial(_attn_kernel, nb=nb, blk=blk, dh=dh, topk=min(MOBA_TOPK, nb)),
        grid=(bsz, N_HEADS),
        in_specs=[pl.BlockSpec((seq, dh), lambda b, h: (b, h)),
                  pl.BlockSpec((seq, dh), lambda b, h: (b, N_HEADS + h)),
                  pl.BlockSpec((seq, dh), lambda b, h: (b, 2 * N_HEADS + h))],
        out_specs=pl.BlockSpec((seq, dh), lambda b, h: (b, h)),
        out_shape=jax.ShapeDtypeStruct((t, d), BF16),
        scratch_shapes=[pltpu.VMEM((seq, dh + LANES), BF16),
                        pltpu.VMEM((dh + BF16_SUBLANES, seq), BF16)],
        compiler_params=_cparams(("arbitrary", "arbitrary")),
        name="moba_attention",
    )(qkv, qkv, qkv)


def _pack_bf16_pairs(lo, hi):
    lo_bits = lax.bitcast_convert_type(lo.astype(BF16).astype(F32), jnp.uint32)
    hi_bits = lax.bitcast_convert_type(hi.astype(BF16).astype(F32), jnp.uint32)
    return (hi_bits & jnp.uint32(0xFFFF0000)) | (lo_bits >> 16)


def _unpack_bf16_pairs(u):
    lo = lax.bitcast_convert_type(u << 16, F32)
    hi = lax.bitcast_convert_type(u & jnp.uint32(0xFFFF0000), F32)
    return lo, hi


def _store_token_slabs(ref, packed, slab_rows, first_tile=0):
    m, width = packed.shape
    for j in range(width // LANES):
        ref[pl.ds(first_tile + j, m, stride=slab_rows), :] = packed[:, j * LANES:(j + 1) * LANES]


def _load_token_slabs(ref, m, slab_rows):
    return jnp.concatenate([ref[pl.ds(j, m, stride=slab_rows), :] for j in range(slab_rows)], axis=1)


def _route_rows(h, g, w_cat, b_cat, n_grp, epg):
    xn = _rmsnorm_rows(h, g)
    hd = xn.shape[1] // 2
    xn_packed = _pack_bf16_pairs(xn[:, :hd], xn[:, hd:])
    logits = jnp.dot(xn.astype(BF16), w_cat, preferred_element_type=F32) + b_cat
    lane = lax.broadcasted_iota(jnp.int32, logits.shape, 1)

    lg = jnp.where(lane < n_grp, logits, -jnp.inf)
    mg = jnp.max(lg, axis=1, keepdims=True)
    gidx = jnp.min(jnp.where(lg == mg, lane, LANES), axis=1, keepdims=True)
    pg_sel = 1.0 / jnp.sum(jnp.exp(lg - mg), axis=1, keepdims=True)

    lo = n_grp + gidx * epg
    in_grp = (lane >= lo) & (lane < lo + epg)
    le = jnp.where(in_grp, logits, -jnp.inf)
    e = jnp.exp(le - jnp.max(le, axis=1, keepdims=True))
    pe = jnp.where(in_grp, e / jnp.sum(e, axis=1, keepdims=True), -1.0)
    v1 = jnp.max(pe, axis=1, keepdims=True)
    i1 = jnp.min(jnp.where(pe == v1, lane, LANES), axis=1, keepdims=True)
    pe2 = jnp.where(lane == i1, -1.0, pe)
    v2 = jnp.max(pe2, axis=1, keepdims=True)
    i2 = jnp.min(jnp.where(pe2 == v2, lane, LANES), axis=1, keepdims=True)
    den = v1 + v2
    ids = jnp.where(lane == 0, i1 - n_grp, jnp.where(lane == 1, i2 - n_grp, 0)).astype(F32)
    et = ids.T[0:SUBLANES, :].astype(jnp.int32)
    rf = jnp.where(lane == 0, pg_sel * v1 / den, jnp.where(lane == 1, pg_sel * v2 / den, 0.0))
    return xn_packed, et, rf


def _mix_out_router_kernel(a_ref, w_ref, *rest, has_bias, n_grp, epg):
    if has_bias:
        b_ref, rest = rest[0], rest[1:]
    r_ref, g_ref, wc_ref, bc_ref, h_ref, xn_ref, et_ref, rf_ref, h_prev = rest

    @pl.when(pl.program_id(0) == 0)
    def _():
        h_prev[...] = jnp.zeros_like(h_prev)

    xn_packed, et, rf = _route_rows(h_prev[...], g_ref[...], wc_ref[...], bc_ref[...], n_grp, epg)
    _store_token_slabs(xn_ref, xn_packed, xn_packed.shape[1] // LANES)
    et_ref[...] = et
    rf_ref[...] = rf

    y = jnp.dot(a_ref[...], w_ref[...], preferred_element_type=F32)
    if has_bias:
        y = y + b_ref[...]
    h = r_ref[...] + y
    h_ref[...] = h
    h_prev[...] = h


def _mix_out_router(a, w, b, res, g, w_cat, b_cat, n_grp, epg):
    t, k = a.shape
    d = w.shape[1]
    tm = _tile(t, 256)
    n = t // tm
    cur = lambda i: (jnp.minimum(i, n - 1), 0)
    prev = lambda i: (jnp.maximum(i - 1, 0), 0)
    has_bias = b is not None
    in_specs = [pl.BlockSpec((tm, k), cur),
                pl.BlockSpec((k, d), lambda i: (0, 0))]
    args = [a, w]
    if has_bias:
        in_specs.append(pl.BlockSpec((1, d), lambda i: (0, 0)))
        args.append(b.reshape(1, d))
    in_specs += [pl.BlockSpec((tm, d), cur),
                 pl.BlockSpec((1, d), lambda i: (0, 0)),
                 pl.BlockSpec((d, LANES), lambda i: (0, 0)),
                 pl.BlockSpec((1, LANES), lambda i: (0, 0))]
    args += [res, g.reshape(1, d), w_cat, b_cat]
    return pl.pallas_call(
        functools.partial(_mix_out_router_kernel, has_bias=has_bias, n_grp=n_grp, epg=epg),
        grid=(n + 1,),
        in_specs=in_specs,
        out_specs=[pl.BlockSpec((tm, d), cur),
                   pl.BlockSpec((tm * (d // 2 // LANES), LANES), prev),
                   pl.BlockSpec((SUBLANES, tm), lambda i: (0, jnp.maximum(i - 1, 0))),
                   pl.BlockSpec((tm, LANES), prev)],
        out_shape=[jax.ShapeDtypeStruct((t, d), F32),
                   jax.ShapeDtypeStruct((t * (d // 2 // LANES), LANES), jnp.uint32),
                   jax.ShapeDtypeStruct((SUBLANES, t), jnp.int32),
                   jax.ShapeDtypeStruct((t, LANES), F32)],
        scratch_shapes=[pltpu.VMEM((tm, d), F32)],
        compiler_params=_cparams(("arbitrary",)),
        name="mix_out_router",
    )(*args)


DISPATCH_CHUNK = 512
DISPATCH_SMEM_CHUNK = 4096


def _dispatch_kernel(et_ref, tok_ref, dst_ref, blk_ref, rank_ref, dest_ref, dest_sm, rows_sm, rows_vm,
                     sem, *, n_exp, topk, rb, slab_rows):
    t = et_ref.shape[1]
    c = DISPATCH_CHUNK
    e_iota = lax.broadcasted_iota(jnp.int32, (n_exp, c), 0)
    earlier = (lax.broadcasted_iota(jnp.int32, (c, c), 0) < lax.broadcasted_iota(jnp.int32, (c, c), 1))
    earlier = jnp.where(earlier, 1.0, 0.0).astype(BF16)

    run = jnp.zeros((n_exp, 1), F32)
    for k in range(topk):
        def rank_chunk(j, run, k=k):
            off = pl.multiple_of(j * c, c)
            oh = e_iota == et_ref[k:k + 1, pl.ds(off, c)]
            ohf = jnp.where(oh, 1.0, 0.0)
            pre = jnp.dot(ohf.astype(BF16), earlier, preferred_element_type=F32)
            rank_ref[k:k + 1, pl.ds(off, c)] = jnp.sum(jnp.where(oh, pre + run, 0.0), axis=0, keepdims=True)
            return run + jnp.sum(ohf, axis=1, keepdims=True)
        run = lax.fori_loop(0, t // c, rank_chunk, run)

    cnt = run
    nblk = jnp.floor((cnt + (rb - 1)) * (1.0 / rb))
    before = (lax.broadcasted_iota(jnp.int32, (n_exp, n_exp), 1) < lax.broadcasted_iota(jnp.int32, (n_exp, n_exp), 0))
    before = jnp.where(before, 1.0, 0.0).astype(BF16)
    bstart = jnp.dot(before, jnp.broadcast_to(nblk, (n_exp, LANES)).astype(BF16),
                     preferred_element_type=F32)[:, 0:1]

    for k in range(topk):
        def dest_chunk(j, carry, k=k):
            off = pl.multiple_of(j * c, c)
            oh = e_iota == et_ref[k:k + 1, pl.ds(off, c)]
            base = jnp.sum(jnp.where(oh, bstart, 0.0), axis=0, keepdims=True) * rb
            dest_ref[k:k + 1, pl.ds(off, c)] = (base + rank_ref[k:k + 1, pl.ds(off, c)]).astype(jnp.int32)
            return carry
        lax.fori_loop(0, t // c, dest_chunk, 0)

    rows_vm[...] = jnp.full(rows_vm.shape, -1, jnp.int32)
    fill = pltpu.make_async_copy(rows_vm, rows_sm, sem)
    fill.start()
    fill.wait()
    sc = DISPATCH_SMEM_CHUNK
    for k in range(topk):
        def scatter_chunk(j, carry, k=k):
            off = pl.multiple_of(j * sc, sc)
            cp = pltpu.make_async_copy(dest_ref.at[pl.ds(k, 1), pl.ds(off, sc)], dest_sm, sem)
            cp.start()
            cp.wait()

            def one(a, carry2):
                d = dest_sm[0, a]
                rows_sm[lax.shift_right_logical(d, LANE_SHIFT), d & (LANES - 1)] = k * t + off + a
                return carry2
            return lax.fori_loop(0, sc, one, carry, unroll=8)
        lax.fori_loop(0, t // sc, scatter_chunk, 0)
    back = pltpu.make_async_copy(rows_sm, rows_vm, sem)
    back.start()
    back.wait()

    a = rows_vm[...]
    valid = a >= 0
    tok = a
    for k in range(1, topk):
        tok = jnp.where(a >= k * t, a - k * t, tok)
    flat = (lax.broadcasted_iota(jnp.int32, a.shape, 0) * LANES + lax.broadcasted_iota(jnp.int32, a.shape, 1))
    scratch_dst = topk * t + ((flat // rb) % 2) * rb + flat % rb
    tok_ref[...] = jnp.where(valid, tok, 0) * slab_rows
    dst_ref[...] = jnp.where(valid, a, scratch_dst) * slab_rows

    nbp = blk_ref.shape[1]
    b_row = lax.broadcasted_iota(jnp.int32, (1, nbp), 1).astype(F32)
    bend = bstart + nblk
    blk_e = jnp.minimum(jnp.sum(jnp.where(bend <= b_row, 1.0, 0.0), axis=0, keepdims=True), n_exp - 1.0)
    mine = lax.broadcasted_iota(jnp.int32, (n_exp, nbp), 0).astype(F32) == blk_e
    cnt_b = jnp.sum(jnp.where(mine, cnt, 0.0), axis=0, keepdims=True)
    bstart_b = jnp.sum(jnp.where(mine, bstart, 0.0), axis=0, keepdims=True)
    blk_n = jnp.clip(cnt_b - (b_row - bstart_b) * rb, 0.0, float(rb))
    row = lax.broadcasted_iota(jnp.int32, blk_ref.shape, 0)
    blk_ref[...] = jnp.where(row == 0, blk_e, jnp.where(row == 1, blk_n, 0.0)).astype(jnp.int32)


def _dispatch(et, n_exp, rb, slab_rows):
    t = et.shape[1]
    topk = EXPERT_TOPK
    n_asg = topk * t
    assert n_asg % rb == 0 and t % DISPATCH_SMEM_CHUNK == 0 and (rb & (rb - 1)) == 0 and rb % LANES == 0
    n_rb = n_asg // rb + n_exp
    n_rows = n_rb * rb
    nbp = -(-n_rb // LANES) * LANES
    tok, dst, blk = pl.pallas_call(
        functools.partial(_dispatch_kernel, n_exp=n_exp, topk=topk, rb=rb, slab_rows=slab_rows),
        in_specs=[pl.BlockSpec(memory_space=pltpu.VMEM)],
        out_specs=[pl.BlockSpec(memory_space=pltpu.VMEM)] * 3,
        out_shape=[jax.ShapeDtypeStruct((n_rows // LANES, LANES), jnp.int32),
                   jax.ShapeDtypeStruct((n_rows // LANES, LANES), jnp.int32),
                   jax.ShapeDtypeStruct((SUBLANES, nbp), jnp.int32)],
        scratch_shapes=[pltpu.VMEM((SUBLANES, t), F32),
                        pltpu.VMEM((SUBLANES, t), jnp.int32),
                        pltpu.SMEM((1, DISPATCH_SMEM_CHUNK), jnp.int32),
                        pltpu.SMEM((n_rows // LANES, LANES), jnp.int32),
                        pltpu.VMEM((n_rows // LANES, LANES), jnp.int32),
                        pltpu.SemaphoreType.DMA(())],
        compiler_params=pltpu.CompilerParams(vmem_limit_bytes=VMEM_LIMIT_BYTES),
        name="moe_dispatch",
    )(et)
    return tok.reshape(n_rb, rb), dst.reshape(n_rb, rb), blk[0, :n_rb], blk[1, :n_rb]


WEIGHT_CHUNK_ROWS = 512
WEIGHT_RING = 4


def _expert_kernel(blk_e, blk_n, tok_all, dst_all, xn_hbm, wg_hbm, wu_hbm, wd_hbm,
                   y_hbm, xbuf, obuf, wbuf_g, wbuf_u, wbuf_d, stg, run_end, next_e, run_start, wslot_of,
                   succ_e, st, gsem, ssem, wsem, *, n_rb, rb, layer):
    i = pl.program_id(0)
    slot = lax.rem(i, 2)
    n = blk_n[i]
    d, f = wbuf_g.shape[1:]
    cr = stg.shape[1]
    n_gu = d // cr
    n_dc = d // f
    n_chunks = 2 * n_gu + (f // cr) * n_dc
    ring = stg.shape[0]
    G_DONE, P_EXP, P_CHUNK = 0, 1, 2

    chunks = ([(wg_hbm, wbuf_g, r * cr, 0) for r in range(n_gu)]
              + [(wu_hbm, wbuf_u, r * cr, 0) for r in range(n_gu)]
              + [(wd_hbm, wbuf_d, r * cr, h * f) for r in range(f // cr) for h in range(n_dc)])
    assert len(chunks) == n_chunks

    def chunk_copy(e, c, k):
        for cid, (w_hbm, _, r0, c0) in enumerate(chunks):
            @pl.when(c == cid)
            def _(w_hbm=w_hbm, r0=r0, c0=c0):
                pltpu.make_async_copy(w_hbm.at[layer, e, pl.ds(r0, cr), pl.ds(c0, f)], stg.at[k],
                                      wsem.at[k]).start()

    def start_next_chunk(k):
        e = st[P_EXP]
        c = st[P_CHUNK]

        @pl.when(e >= 0)
        def _():
            chunk_copy(e, c, k)
            last = c + 1 == n_chunks
            st[P_CHUNK] = jnp.where(last, 0, c + 1)
            st[P_EXP] = jnp.where(last, succ_e[e], e)

    def convert_chunk(c, ws):
        g = st[G_DONE]
        k = lax.rem(g, ring)
        pltpu.make_async_copy(wg_hbm.at[layer, 0, pl.ds(0, cr), :], stg.at[k], wsem.at[k]).wait()
        start_next_chunk(lax.rem(g + ring - 1, ring))
        for cid, (_, wbuf, r0, c0) in enumerate(chunks):
            @pl.when(c == cid)
            def _(wbuf=wbuf, r0=r0, c0=c0):
                wbuf[ws, r0:r0 + cr, c0:c0 + f] = stg[k].astype(BF16)
        st[G_DONE] = g + 1

    @pl.when(i == 0)
    def _():
        def back(ii, carry):
            end_nb, next_nb = carry
            b = n_rb - 1 - ii
            nb = jnp.minimum(b + 1, n_rb - 1)
            nb_active = (b + 1 < n_rb) & (blk_n[nb] > 0)
            same = nb_active & (blk_e[nb] == blk_e[b])
            end_b = jnp.where(same, end_nb, b + 1)
            next_b = jnp.where(same, next_nb, jnp.where(nb_active, blk_e[nb], -1))
            run_end[b] = end_b
            next_e[b] = next_b

            @pl.when((blk_n[b] > 0) & jnp.logical_not(same))
            def _():
                succ_e[blk_e[b]] = next_b
            return end_b, next_b
        lax.fori_loop(0, n_rb, back, (jnp.int32(n_rb), jnp.int32(-1)))

        def fwd(b, carry):
            start_pb, wslot_pb = carry
            pb = jnp.maximum(b - 1, 0)
            same = (b > 0) & (blk_e[pb] == blk_e[b])
            start_b = jnp.where(same, start_pb, b)
            wslot_b = jnp.where(b == 0, 0, jnp.where(same, wslot_pb, 1 - wslot_pb))
            run_start[b] = start_b
            wslot_of[b] = wslot_b
            return start_b, wslot_b
        lax.fori_loop(0, n_rb, fwd, (jnp.int32(0), jnp.int32(0)))

        st[G_DONE] = 0
        st[P_CHUNK] = 0
        st[P_EXP] = jnp.where(n > 0, blk_e[0], -1)

        @pl.when(n > 0)
        def _():
            for k in range(ring - 1):
                start_next_chunk(k)

            def first(c, carry):
                convert_chunk(c, 0)
                return carry
            lax.fori_loop(0, n_chunks, first, 0)

    ws = wslot_of[i]

    @pl.when((n > 0) & (next_e[i] >= 0))
    def _():
        run_len = run_end[i] - run_start[i]
        per = lax.div(n_chunks + run_len - 1, run_len)
        lo = (i - run_start[i]) * per
        hi = jnp.minimum(lo + per, n_chunks)

        def nxt(c, carry):
            convert_chunk(c, 1 - ws)
            return carry
        lax.fori_loop(lo, hi, nxt, 0)

    ns = xbuf.shape[1] // rb

    def gather_row(blk, sl, r):
        return pltpu.make_async_copy(xn_hbm.at[pl.ds(tok_all[blk * rb + r], ns)], xbuf.at[sl, pl.ds(r * ns, ns)],
                                     gsem.at[sl])

    def start_gather(blk, sl):
        def body(r, c):
            gather_row(blk, sl, r).start()
            return c
        lax.fori_loop(0, rb, body, 0, unroll=8)

    def scatter_row(sl, r, dst):
        return pltpu.make_async_copy(obuf.at[sl, pl.ds(r * ns, ns)], y_hbm.at[pl.ds(dst, ns)], ssem.at[sl])

    def wait_scatter(sl):
        pltpu.make_async_copy(obuf.at[sl], y_hbm.at[pl.ds(0, rb * ns)], ssem.at[sl]).wait()

    @pl.when(i == 0)
    def _():
        obuf[...] = jnp.zeros_like(obuf)
        n_real = y_hbm.shape[0] - 2 * rb * ns
        for sl in range(2):
            init = pltpu.make_async_copy(obuf.at[sl], y_hbm.at[pl.ds(n_real + sl * rb * ns, rb * ns)],
                                         ssem.at[sl])
            init.start()
            init.wait()

    n_prev = blk_n[jnp.maximum(i - 1, 0)]
    n_next = blk_n[jnp.minimum(i + 1, n_rb - 1)]
    do_gather = (i + 1 < n_rb) & (n_next > 0)
    do_scatter = (i >= 1) & (n_prev > 0)

    @pl.when((i == 0) & (n > 0))
    def _():
        start_gather(0, 0)

    @pl.when(n > 0)
    def _():
        pltpu.make_async_copy(xn_hbm.at[pl.ds(0, rb * ns)], xbuf.at[slot], gsem.at[slot]).wait()

    hd = ns * LANES
    fc, dc = min(f, MXU_COLS), min(hd, MXU_COLS)

    def free_obuf():
        @pl.when((i >= 2) & (blk_n[jnp.maximum(i - 2, 0)] > 0))
        def _():
            wait_scatter(slot)

    def compute(interleave_dma):
        def issue(piece, n_piece, start_row):
            if interleave_dma:
                per = -(-rb // n_piece)
                for r in range(piece * per, min(rb, (piece + 1) * per)):
                    start_row(r)

        x_lo, x_hi = _unpack_bf16_pairs(_load_token_slabs(xbuf.at[slot], rb, ns))
        x = jnp.concatenate([x_lo.astype(BF16), x_hi.astype(BF16)], axis=1)
        hid = []
        for c in range(f // fc):
            cols = slice(c * fc, (c + 1) * fc)
            g = jnp.dot(x, wbuf_g[ws, :, cols], preferred_element_type=F32)
            u = jnp.dot(x, wbuf_u[ws, :, cols], preferred_element_type=F32)
            hid.append(((g * jax.nn.sigmoid(g)) * u).astype(BF16))
            issue(c, f // fc, lambda r: gather_row(i + 1, 1 - slot, r).start(priority=r % 2))
        hid = jnp.concatenate(hid, axis=1)
        free_obuf()
        for c in range(hd // dc):
            cols = slice(c * dc, (c + 1) * dc)
            cols_hi = slice(hd + c * dc, hd + (c + 1) * dc)
            y_lo = jnp.dot(hid, wbuf_d[ws, :, cols], preferred_element_type=F32)
            y_hi = jnp.dot(hid, wbuf_d[ws, :, cols_hi], preferred_element_type=F32)
            _store_token_slabs(obuf.at[slot], _pack_bf16_pairs(y_lo, y_hi), ns, first_tile=c * dc // LANES)
            issue(c, hd // dc,
                  lambda r: scatter_row(1 - slot, r, dst_all[(i - 1) * rb + r]).start(priority=r % 2))

    fast = (n > 0) & do_gather & do_scatter

    @pl.when(fast)
    def _():
        compute(True)

    @pl.when(jnp.logical_not(fast))
    def _():
        @pl.when(do_gather)
        def _():
            start_gather(i + 1, 1 - slot)

        @pl.when(do_scatter)
        def _():
            def body(r, c):
                scatter_row(1 - slot, r, dst_all[(i - 1) * rb + r]).start()
                return c
            lax.fori_loop(0, rb, body, 0, unroll=8)

        @pl.when(n > 0)
        def _():
            compute(False)

        @pl.when(n == 0)
        def _():
            free_obuf()

    @pl.when((i == n_rb - 1) & do_scatter)
    def _():
        wait_scatter(1 - slot)


def _experts(xn, row_tok, row_dst, blk_e, blk_n, w_gate, w_up, w_down, layer, n_slots):
    n_exp, d, f = w_gate.shape[1:]
    ns = d // 2 // LANES
    assert xn.shape[1] == LANES and d % (2 * LANES) == 0 and w_down.shape[1:] == (n_exp, f, d)
    n_rb, rb = row_tok.shape
    assert (n_slots - 2 * rb) % rb == 0
    cr = min(WEIGHT_CHUNK_ROWS, f)
    assert d % cr == 0 and f % cr == 0 and d % f == 0
    hbm = pl.BlockSpec(memory_space=pl.ANY)
    grid_spec = pltpu.PrefetchScalarGridSpec(
        num_scalar_prefetch=4,
        grid=(n_rb,),
        in_specs=[hbm, hbm, hbm, hbm],
        out_specs=hbm,
        scratch_shapes=[pltpu.VMEM((2, rb * ns, LANES), jnp.uint32),
                        pltpu.VMEM((2, rb * ns, LANES), jnp.uint32),
                        pltpu.VMEM((2, d, f), BF16),
                        pltpu.VMEM((2, d, f), BF16),
                        pltpu.VMEM((2, f, d), BF16),
                        pltpu.VMEM((WEIGHT_RING, cr, f), F32),
                        pltpu.SMEM((n_rb,), jnp.int32),
                        pltpu.SMEM((n_rb,), jnp.int32),
                        pltpu.SMEM((n_rb,), jnp.int32),
                        pltpu.SMEM((n_rb,), jnp.int32),
                        pltpu.SMEM((n_exp,), jnp.int32),
                        pltpu.SMEM((4,), jnp.int32),
                        pltpu.SemaphoreType.DMA((2,)),
                        pltpu.SemaphoreType.DMA((2,)),
                        pltpu.SemaphoreType.DMA((WEIGHT_RING,))],
    )
    return pl.pallas_call(
        functools.partial(_expert_kernel, n_rb=n_rb, rb=rb, layer=layer),
        grid_spec=grid_spec,
        out_shape=jax.ShapeDtypeStruct((n_slots * ns, LANES), jnp.uint32),
        compiler_params=_cparams(("arbitrary",)),
        name="moe_experts",
    )(blk_e, blk_n, row_tok.reshape(-1), row_dst.reshape(-1), xn, w_gate, w_up, w_down)


def _combine_kernel(h_ref, y0_ref, y1_ref, rf_ref, *rest, final):
    if final:
        g_ref, o_ref = rest
    else:
        (o_ref,) = rest
    w = rf_ref[...]
    m = h_ref.shape[0]
    ns = y0_ref.shape[0] // m
    y0_lo, y0_hi = _unpack_bf16_pairs(_load_token_slabs(y0_ref, m, ns))
    y1_lo, y1_hi = _unpack_bf16_pairs(_load_token_slabs(y1_ref, m, ns))
    y = jnp.concatenate([w[:, 0:1] * y0_lo + w[:, 1:2] * y1_lo, w[:, 0:1] * y0_hi + w[:, 1:2] * y1_hi], axis=1)
    out = h_ref[...] + y
    if final:
        out = _rmsnorm_rows(out, g_ref[...])
    o_ref[...] = out


def _combine(h, y, rf, final_g):
    t, d = h.shape
    tm = _tile(t, 512)
    final = final_g is not None
    in_specs = [pl.BlockSpec((tm, d), lambda i: (i, 0)),
                pl.BlockSpec((tm * (d // 2 // LANES), LANES), lambda i: (i, 0)),
                pl.BlockSpec((tm * (d // 2 // LANES), LANES), lambda i: (t // tm + i, 0)),
                pl.BlockSpec((tm, LANES), lambda i: (i, 0))]
    args = [h, y, y, rf]
    if final:
        in_specs.append(pl.BlockSpec((1, d), lambda i: (0, 0)))
        args.append(final_g.reshape(1, d))
    return pl.pallas_call(
        functools.partial(_combine_kernel, final=final),
        grid=(t // tm,),
        in_specs=in_specs,
        out_specs=pl.BlockSpec((tm, d), lambda i: (i, 0)),
        out_shape=jax.ShapeDtypeStruct((t, d), F32),
        compiler_params=_cparams(("parallel",)),
        name="moe_combine",
    )(*args)


def _mix_out_moe(act, w_out, b_out, h, norm_g, w_grp, b_grp, w_exp, b_exp, w_gate, w_up, w_down, layer, final_g):
    t, d = h.shape
    n_grp = w_grp.shape[1]
    n_exp = w_exp.shape[1]
    epg = n_exp // n_grp
    assert n_grp + n_exp <= LANES
    pad = LANES - n_grp - n_exp
    w_cat = jnp.concatenate([w_grp, w_exp, jnp.zeros((d, pad), F32)], axis=1).astype(BF16)
    b_cat = jnp.concatenate([b_grp, b_exp, jnp.zeros((pad,), F32)]).reshape(1, LANES)
    h, xn, et, rf = _mix_out_router(act, w_out, b_out, h, norm_g, w_cat, b_cat, n_grp, epg)
    row_tok, row_dst, blk_e, blk_n = _dispatch(et, n_exp, ROW_BLOCK, d // 2 // LANES)
    y = _experts(xn, row_tok, row_dst, blk_e, blk_n, w_gate, w_up, w_down, layer,
                 EXPERT_TOPK * t + 2 * ROW_BLOCK)
    return _combine(h, y, rf, final_g)


def _rope_tables(seq, dh):
    inv = 1.0 / (ROPE_THETA ** (jnp.arange(0, dh, 2, dtype=F32) / dh))
    ang = jnp.arange(seq, dtype=F32)[:, None] * inv[None, :]
    cos, sin = jnp.cos(ang), jnp.sin(ang)
    return jnp.concatenate([cos, cos], axis=1), jnp.concatenate([-sin, sin], axis=1)


def kernel(x, lru_norm, lru_w_in, lru_b_in, lru_conv_w, lru_conv_b, lru_w_r, lru_b_r, lru_w_i, lru_b_i, lru_lambda, lru_w_out, lru_b_out, att_norm, att_w_qkv, att_w_o, ffn_norm, moe_w_grp, moe_b_grp, moe_w_exp, moe_b_exp, moe_w_gate, moe_w_up, moe_w_down, final_norm):
    bsz, seq, d = x.shape
    depth = ffn_norm.shape[0]
    n_mixers = 2
    cc, ss = _rope_tables(seq, d // N_HEADS)
    h = x.reshape(bsz * seq, d)
    for layer in range(depth):
        j = layer // n_mixers
        if layer % n_mixers == 0:
            u = _lru_in(h, lru_norm[j], lru_w_in[j].astype(BF16), lru_b_in[j])
            act = _lru_scan(u, lru_conv_w[j], lru_conv_b[j], lru_w_r[j].astype(BF16), lru_b_r[j],
                            lru_w_i[j].astype(BF16), lru_b_i[j], lru_lambda[j], bsz, seq)
            w_out, b_out = lru_w_out[j].astype(BF16), lru_b_out[j]
        else:
            qkv = _qkv_rope(h, att_norm[j], att_w_qkv[j].astype(BF16), cc, ss, seq)
            act = _moba_attention(qkv, bsz, seq, d)
            w_out, b_out = att_w_o[j].astype(BF16), None
        final_g = final_norm if layer == depth - 1 else None
        h = _mix_out_moe(act, w_out, b_out, h, ffn_norm[layer], moe_w_grp[layer], moe_b_grp[layer],
                         moe_w_exp[layer], moe_b_exp[layer], moe_w_gate, moe_w_up, moe_w_down, layer, final_g)
    return h.reshape(bsz, seq, d)
```

```python
import functools

import jax
import jax.numpy as jnp
from jax import lax
from jax.experimental import pallas as pl
from jax.experimental.pallas import tpu as pltpu

F32 = jnp.float32
BF16 = jnp.bfloat16

NORM_EPS = 1e-6
NEG_INF = -1e30
LRU_C = 8.0
N_HEADS = 16
MOBA_BLOCK = 256
MOBA_TOPK = 3
ROPE_THETA = 10000.0
EXPERT_TOPK = 2
ROW_BLOCK = 256
LANES = 128
LANE_SHIFT = 7
SUBLANES = 8
BF16_SUBLANES = 16
MXU_COLS = 256
VMEM_LIMIT_BYTES = 56 * 1024 * 1024


def _tile(n, pref):
    t = min(n, pref)
    while n % t:
        t //= 2
    return t


def _cparams(sem):
    return pltpu.CompilerParams(dimension_semantics=sem, vmem_limit_bytes=VMEM_LIMIT_BYTES)


def _sigmoid(z):
    return 0.5 * jnp.tanh(0.5 * z) + 0.5


def _rmsnorm_rows(x, g):
    ms = jnp.mean(x * x, axis=-1, keepdims=True)
    return x * lax.rsqrt(ms + NORM_EPS) * g


def _lru_in_kernel(x_ref, g_ref, wx_ref, wg_ref, bx_ref, bg_ref, ox_ref, og_ref, xn_ref):
    @pl.when(pl.program_id(1) == 0)
    def _():
        xn_ref[...] = _rmsnorm_rows(x_ref[...], g_ref[...]).astype(BF16)

    xn = xn_ref[...]
    ox_ref[...] = jnp.dot(xn, wx_ref[...], preferred_element_type=F32) + bx_ref[...]
    og_ref[...] = jax.nn.gelu(jnp.dot(xn, wg_ref[...], preferred_element_type=F32) + bg_ref[...],
                              approximate=True)


def _lru_in(h, g, w, b):
    t, d = h.shape
    wdt = w.shape[1] // 2
    tm, tn = _tile(t, 1024), _tile(wdt, 512)
    nj = wdt // tn
    b2 = b.reshape(1, 2 * wdt)
    return pl.pallas_call(
        _lru_in_kernel,
        grid=(t // tm, nj),
        in_specs=[pl.BlockSpec((tm, d), lambda i, j: (i, 0)),
                  pl.BlockSpec((1, d), lambda i, j: (0, 0)),
                  pl.BlockSpec((d, tn), lambda i, j: (0, j)),
                  pl.BlockSpec((d, tn), lambda i, j: (0, nj + j)),
                  pl.BlockSpec((1, tn), lambda i, j: (0, j)),
                  pl.BlockSpec((1, tn), lambda i, j: (0, nj + j))],
        out_specs=[pl.BlockSpec((tm, tn), lambda i, j: (i, j)),
                   pl.BlockSpec((tm, tn), lambda i, j: (i, j))],
        out_shape=[jax.ShapeDtypeStruct((t, wdt), F32),
                   jax.ShapeDtypeStruct((t, wdt), F32)],
        scratch_shapes=[pltpu.VMEM((tm, d), BF16)],
        compiler_params=_cparams(("parallel", "arbitrary")),
        name="lru_in",
    )(h, g.reshape(1, d), w, w, b2, b2)


def _qkv_kernel(x_ref, g_ref, w_ref, cc_ref, ss_ref, o_ref, xn_ref, *, n_rope, heads_per_tile, dh):
    j = pl.program_id(1)

    @pl.when(j == 0)
    def _():
        xn_ref[...] = _rmsnorm_rows(x_ref[...], g_ref[...]).astype(BF16)

    acc = jnp.dot(xn_ref[...], w_ref[...], preferred_element_type=F32)

    is_rope = j < n_rope
    cc = cc_ref[...]
    ss = ss_ref[...]
    for hh in range(heads_per_tile):
        a = acc[:, hh * dh:(hh + 1) * dh]
        rot = a * cc + pltpu.roll(a, dh // 2, 1) * ss
        o_ref[:, hh * dh:(hh + 1) * dh] = jnp.where(is_rope, rot, a).astype(o_ref.dtype)


def _qkv_rope(h, g, w, cc, ss, seq):
    t, d = h.shape
    n = w.shape[1]
    dh = d // N_HEADS
    tm, tn = _tile(seq, 1024), _tile(d, 1024)
    s_tiles = seq // tm
    return pl.pallas_call(
        functools.partial(_qkv_kernel, n_rope=(2 * d) // tn, heads_per_tile=tn // dh, dh=dh),
        grid=(t // tm, n // tn),
        in_specs=[pl.BlockSpec((tm, d), lambda i, j: (i, 0)),
                  pl.BlockSpec((1, d), lambda i, j: (0, 0)),
                  pl.BlockSpec((d, tn), lambda i, j: (0, j)),
                  pl.BlockSpec((tm, dh), lambda i, j: (i % s_tiles, 0)),
                  pl.BlockSpec((tm, dh), lambda i, j: (i % s_tiles, 0))],
        out_specs=pl.BlockSpec((tm, tn), lambda i, j: (i, j)),
        out_shape=jax.ShapeDtypeStruct((t, n), BF16),
        scratch_shapes=[pltpu.VMEM((tm, d), BF16)],
        compiler_params=_cparams(("parallel", "arbitrary")),
        name="qkv_rope",
    )(h, g.reshape(1, d), w, cc, ss)


def _lru_scan_kernel(xb_ref, gate_ref, cw_ref, cb_ref, wr_ref, br_ref, wi_ref, bi_ref, lam_ref,
                     o_ref, ext_ref, a_ref, b_ref, h_ref, *, ts, n_grp, gw, conv_w):
    s = pl.program_id(1)
    w = xb_ref.shape[1]

    @pl.when(s == 0)
    def _():
        ext_ref[0:SUBLANES, :] = jnp.zeros((SUBLANES, w), F32)
        h_ref[...] = jnp.zeros_like(h_ref)

    ext_ref[SUBLANES:SUBLANES + ts, :] = xb_ref[...]
    xc = cb_ref[...] + cw_ref[conv_w - 1:conv_w, :] * xb_ref[...]
    for k in range(1, conv_w):
        xc = xc + cw_ref[conv_w - 1 - k:conv_w - k, :] * ext_ref[SUBLANES - k:SUBLANES - k + ts, :]
    ext_ref[0:SUBLANES, :] = ext_ref[ts:ts + SUBLANES, :]

    z = -lam_ref[...]
    softplus = jnp.maximum(z, 0.0) + jnp.log1p(jnp.exp(-jnp.abs(z)))
    c = -LRU_C * softplus
    for g in range(n_grp):
        sl = slice(g * gw, (g + 1) * gw)
        xg = xc[:, sl]
        xg16 = xg.astype(BF16)
        r = _sigmoid(jnp.dot(xg16, wr_ref[g], preferred_element_type=F32) + br_ref[:, sl])
        ig = _sigmoid(jnp.dot(xg16, wi_ref[g], preferred_element_type=F32) + bi_ref[:, sl])
        log_a = c[:, sl] * r
        a_ref[:, sl] = jnp.exp(log_a)
        th = jnp.tanh(log_a)
        b_ref[:, sl] = jnp.sqrt(-2.0 * th / (1.0 - th)) * ig * xg

    rows = lax.broadcasted_iota(jnp.int32, (SUBLANES, w), 0)

    def chunk(ci, h):
        r0 = pl.multiple_of(ci * SUBLANES, SUBLANES)
        a = a_ref[pl.ds(r0, SUBLANES), :]
        b = b_ref[pl.ds(r0, SUBLANES), :]
        for sh in (1, 2, 4):
            keep = rows >= sh
            a_sh = jnp.where(keep, pltpu.roll(a, sh, 0), 1.0)
            b_sh = jnp.where(keep, pltpu.roll(b, sh, 0), 0.0)
            b = a * b_sh + b
            a = a * a_sh
        hc = a * h + b
        b_ref[pl.ds(r0, SUBLANES), :] = hc * gate_ref[pl.ds(r0, SUBLANES), :]
        return jnp.broadcast_to(hc[SUBLANES - 1:SUBLANES, :], (SUBLANES, w))

    h_ref[...] = lax.fori_loop(0, ts // SUBLANES, chunk, h_ref[...])
    o_ref[...] = b_ref[...].astype(o_ref.dtype)


def _lru_scan(xb, gate, conv_w, conv_b, w_r, b_r, w_i, b_i, lam, bsz, seq):
    t, w = xb.shape
    n_grp, gw = w_r.shape[0], w_r.shape[1]
    cw = conv_w.shape[0]
    ts = _tile(seq, 256)
    s_tiles = seq // ts
    row = lambda b, s: (b * s_tiles + s, 0)
    vec = pl.BlockSpec((1, w), lambda b, s: (0, 0))
    return pl.pallas_call(
        functools.partial(_lru_scan_kernel, ts=ts, n_grp=n_grp, gw=gw, conv_w=cw),
        grid=(bsz, s_tiles),
        in_specs=[pl.BlockSpec((ts, w), row),
                  pl.BlockSpec((ts, w), row),
                  pl.BlockSpec((cw, w), lambda b, s: (0, 0)),
                  vec,
                  pl.BlockSpec((n_grp, gw, gw), lambda b, s: (0, 0, 0)),
                  vec,
                  pl.BlockSpec((n_grp, gw, gw), lambda b, s: (0, 0, 0)),
                  vec,
                  vec],
        out_specs=pl.BlockSpec((ts, w), row),
        out_shape=jax.ShapeDtypeStruct((t, w), BF16),
        scratch_shapes=[pltpu.VMEM((ts + SUBLANES, w), F32),
                        pltpu.VMEM((ts, w), F32),
                        pltpu.VMEM((ts, w), F32),
                        pltpu.VMEM((SUBLANES, w), F32)],
        compiler_params=_cparams(("parallel", "arbitrary")),
        name="lru_scan",
    )(xb, gate, conv_w, conv_b.reshape(1, w), w_r, b_r.reshape(1, w), w_i, b_i.reshape(1, w),
      lam.reshape(1, w))


ATTN_HEADS_PER_STEP = 1


def _attn_kernel(q_ref, k_ref, v_ref, o_ref, kaug_ref, vaug_t_ref, *, nb, blk, dh, topk):
    seq = nb * blk
    hp = kaug_ref.shape[0]
    nbp = vaug_t_ref.shape[1] - dh
    first = (pl.program_id(0) == 0) & (pl.program_id(1) == 0)

    @pl.when(first)
    def _():
        row = lax.broadcasted_iota(jnp.int32, (seq, LANES), 0)
        lane = lax.broadcasted_iota(jnp.int32, (seq, LANES), 1)
        ones_row = (lax.broadcasted_iota(jnp.int32, (nbp, seq), 0) == 0).astype(BF16)
        for hd in range(hp):
            kaug_ref[hd, :, dh:dh + LANES] = (lane == row // blk).astype(BF16)
            vaug_t_ref[hd, dh:dh + nbp, :] = ones_row

    kmeans = []
    for hd in range(hp):
        cols = slice(hd * dh, (hd + 1) * dh)
        kaug_ref[hd, :, 0:dh] = k_ref[:, cols]
        for j in range(nb):
            vaug_t_ref[hd, 0:dh, j * blk:(j + 1) * blk] = v_ref[j * blk:(j + 1) * blk, cols].astype(F32).T.astype(BF16)
        kmean = jnp.sum(k_ref[:, cols].astype(F32).reshape(nb, blk, dh), axis=1) * (1.0 / blk)
        if nbp > nb:
            kmean = jnp.concatenate([kmean, jnp.zeros((nbp - nb, dh), F32)], axis=0)
        kmeans.append(kmean.astype(BF16))

    c = (dh ** -0.5) * 1.4426950408889634
    blk_id = lax.broadcasted_iota(jnp.int32, (nbp, blk), 0)
    key_pos = lax.broadcasted_iota(jnp.int32, (blk, blk), 0)
    q_pos = lax.broadcasted_iota(jnp.int32, (blk, blk), 1)
    zeros_pad = jnp.zeros((LANES - nbp, blk), BF16)

    def scores(hd, i):
        q_t = q_ref[i * blk:(i + 1) * blk, hd * dh:(hd + 1) * dh].astype(F32).T.astype(BF16)
        gate_t = jnp.dot(kmeans[hd], q_t, preferred_element_type=F32)
        g = jnp.where(blk_id < i, gate_t, NEG_INF)
        sel = blk_id == i
        for _ in range(topk):
            m = jnp.max(g, axis=0, keepdims=True)
            first_max = jnp.min(jnp.where(g == m, blk_id, nbp), axis=0, keepdims=True)
            hit = blk_id == first_max
            sel = sel | (hit & (blk_id < i))
            g = jnp.where(hit, -jnp.inf, g)
        bias_t = jnp.where(sel, 0.0, NEG_INF).astype(BF16)
        q_aug_t = jnp.concatenate([q_t, bias_t, zeros_pad], axis=0)
        return jnp.dot(kaug_ref[hd, 0:(i + 1) * blk, :], q_aug_t, preferred_element_type=F32)

    def probs(i, s):
        s_own = jnp.where(key_pos <= q_pos, s[i * blk:, :], NEG_INF)
        m = jnp.max(s_own, axis=0, keepdims=True)
        if i > 0:
            s_past = s[:i * blk, :]
            m = jnp.maximum(m, jnp.max(s_past, axis=0, keepdims=True))
            p = jnp.concatenate([jnp.exp2((s_past - m) * c), jnp.exp2((s_own - m) * c)], axis=0)
        else:
            p = jnp.exp2((s_own - m) * c)
        return p.astype(BF16)

    def pv(hd, i, p):
        out_t = jnp.dot(vaug_t_ref[hd, :, 0:(i + 1) * blk], p, preferred_element_type=F32)
        o_t = out_t[0:dh, :] / out_t[dh:dh + 1, :]
        o_ref[i * blk:(i + 1) * blk, hd * dh:(hd + 1) * dh] = o_t.T.astype(o_ref.dtype)

    order = list(range(nb - 1, -1, -1))
    s_of, p_of = {}, {}
    for step in range(nb + 2):
        for hd in range(hp):
            if step < nb:
                s_of[hd, order[step]] = scores(hd, order[step])
        for hd in range(hp):
            if 1 <= step <= nb:
                t1 = order[step - 1]
                p_of[hd, t1] = probs(t1, s_of.pop((hd, t1)))
        for hd in range(hp):
            if step >= 2:
                t2 = order[step - 2]
                pv(hd, t2, p_of.pop((hd, t2)))


def _moba_attention(qkv, bsz, seq, d):
    t = qkv.shape[0]
    dh = d // N_HEADS
    blk = MOBA_BLOCK
    nb = seq // blk
    hp = ATTN_HEADS_PER_STEP
    assert seq % blk == 0 and nb <= BF16_SUBLANES and dh == LANES and N_HEADS % hp == 0
    n_hg = N_HEADS // hp
    return pl.pallas_call(
        functools.partial(_attn_kernel, nb=nb, blk=blk, dh=dh, topk=min(MOBA_TOPK, nb)),
        grid=(bsz, n_hg),
        in_specs=[pl.BlockSpec((seq, hp * dh), lambda b, h: (b, h)),
                  pl.BlockSpec((seq, hp * dh), lambda b, h: (b, n_hg + h)),
                  pl.BlockSpec((seq, hp * dh), lambda b, h: (b, 2 * n_hg + h))],
        out_specs=pl.BlockSpec((seq, hp * dh), lambda b, h: (b, h)),
        out_shape=jax.ShapeDtypeStruct((t, d), BF16),
        scratch_shapes=[pltpu.VMEM((hp, seq, dh + LANES), BF16),
                        pltpu.VMEM((hp, dh + BF16_SUBLANES, seq), BF16)],
        compiler_params=_cparams(("arbitrary", "arbitrary")),
        name="moba_attention",
    )(qkv, qkv, qkv)


def _pack_bf16_pairs(lo, hi):
    lo_bits = lax.bitcast_convert_type(lo.astype(BF16).astype(F32), jnp.uint32)
    hi_bits = lax.bitcast_convert_type(hi.astype(BF16).astype(F32), jnp.uint32)
    return (hi_bits & jnp.uint32(0xFFFF0000)) | (lo_bits >> 16)


def _unpack_bf16_pairs(u):
    lo = lax.bitcast_convert_type(u << 16, F32)
    hi = lax.bitcast_convert_type(u & jnp.uint32(0xFFFF0000), F32)
    return lo, hi


def _store_token_slabs(ref, packed, slab_rows, first_tile=0):
    m, width = packed.shape
    for j in range(width // LANES):
        ref[pl.ds(first_tile + j, m, stride=slab_rows), :] = packed[:, j * LANES:(j + 1) * LANES]


def _load_token_slabs(ref, m, slab_rows):
    return jnp.concatenate([ref[pl.ds(j, m, stride=slab_rows), :] for j in range(slab_rows)], axis=1)


def _route_rows(h, g, w_cat, b_cat, n_grp, epg):
    xn = _rmsnorm_rows(h, g)
    hd = xn.shape[1] // 2
    xn_packed = _pack_bf16_pairs(xn[:, :hd], xn[:, hd:])
    logits = jnp.dot(xn.astype(BF16), w_cat, preferred_element_type=F32) + b_cat
    lane = lax.broadcasted_iota(jnp.int32, logits.shape, 1)

    lg = jnp.where(lane < n_grp, logits, -jnp.inf)
    mg = jnp.max(lg, axis=1, keepdims=True)
    gidx = jnp.min(jnp.where(lg == mg, lane, LANES), axis=1, keepdims=True)
    pg_sel = 1.0 / jnp.sum(jnp.exp(lg - mg), axis=1, keepdims=True)

    lo = n_grp + gidx * epg
    in_grp = (lane >= lo) & (lane < lo + epg)
    le = jnp.where(in_grp, logits, -jnp.inf)
    e = jnp.exp(le - jnp.max(le, axis=1, keepdims=True))
    pe = jnp.where(in_grp, e / jnp.sum(e, axis=1, keepdims=True), -1.0)
    v1 = jnp.max(pe, axis=1, keepdims=True)
    i1 = jnp.min(jnp.where(pe == v1, lane, LANES), axis=1, keepdims=True)
    pe2 = jnp.where(lane == i1, -1.0, pe)
    v2 = jnp.max(pe2, axis=1, keepdims=True)
    i2 = jnp.min(jnp.where(pe2 == v2, lane, LANES), axis=1, keepdims=True)
    den = v1 + v2
    ids = jnp.where(lane == 0, i1 - n_grp, jnp.where(lane == 1, i2 - n_grp, 0)).astype(F32)
    et = ids.T[0:SUBLANES, :].astype(jnp.int32)
    rf = jnp.where(lane == 0, pg_sel * v1 / den, jnp.where(lane == 1, pg_sel * v2 / den, 0.0))
    return xn_packed, et, rf


def _mix_out_router_kernel(a_ref, w_ref, *rest, has_bias, n_grp, epg):
    if has_bias:
        b_ref, rest = rest[0], rest[1:]
    r_ref, g_ref, wc_ref, bc_ref, h_ref, xn_ref, et_ref, rf_ref, h_prev = rest

    @pl.when(pl.program_id(0) == 0)
    def _():
        h_prev[...] = jnp.zeros_like(h_prev)

    xn_packed, et, rf = _route_rows(h_prev[...], g_ref[...], wc_ref[...], bc_ref[...], n_grp, epg)
    _store_token_slabs(xn_ref, xn_packed, xn_packed.shape[1] // LANES)
    et_ref[...] = et
    rf_ref[...] = rf

    y = jnp.dot(a_ref[...], w_ref[...], preferred_element_type=F32)
    if has_bias:
        y = y + b_ref[...]
    h = r_ref[...] + y
    h_ref[...] = h
    h_prev[...] = h


def _mix_out_router(a, w, b, res, g, w_cat, b_cat, n_grp, epg):
    t, k = a.shape
    d = w.shape[1]
    tm = _tile(t, 256)
    n = t // tm
    cur = lambda i: (jnp.minimum(i, n - 1), 0)
    prev = lambda i: (jnp.maximum(i - 1, 0), 0)
    has_bias = b is not None
    in_specs = [pl.BlockSpec((tm, k), cur),
                pl.BlockSpec((k, d), lambda i: (0, 0))]
    args = [a, w]
    if has_bias:
        in_specs.append(pl.BlockSpec((1, d), lambda i: (0, 0)))
        args.append(b.reshape(1, d))
    in_specs += [pl.BlockSpec((tm, d), cur),
                 pl.BlockSpec((1, d), lambda i: (0, 0)),
                 pl.BlockSpec((d, LANES), lambda i: (0, 0)),
                 pl.BlockSpec((1, LANES), lambda i: (0, 0))]
    args += [res, g.reshape(1, d), w_cat, b_cat]
    return pl.pallas_call(
        functools.partial(_mix_out_router_kernel, has_bias=has_bias, n_grp=n_grp, epg=epg),
        grid=(n + 1,),
        in_specs=in_specs,
        out_specs=[pl.BlockSpec((tm, d), cur),
                   pl.BlockSpec((tm * (d // 2 // LANES), LANES), prev),
                   pl.BlockSpec((SUBLANES, tm), lambda i: (0, jnp.maximum(i - 1, 0))),
                   pl.BlockSpec((tm, LANES), prev)],
        out_shape=[jax.ShapeDtypeStruct((t, d), F32),
                   jax.ShapeDtypeStruct((t * (d // 2 // LANES), LANES), jnp.uint32),
                   jax.ShapeDtypeStruct((SUBLANES, t), jnp.int32),
                   jax.ShapeDtypeStruct((t, LANES), F32)],
        scratch_shapes=[pltpu.VMEM((tm, d), F32)],
        compiler_params=_cparams(("arbitrary",)),
        name="mix_out_router",
    )(*args)


DISPATCH_CHUNK = 512
DISPATCH_SMEM_CHUNK = 4096


def _dispatch_kernel(et_ref, tok_ref, dst_ref, blk_ref, rank_ref, dest_ref, dest_sm, rows_sm, rows_vm,
                     sem, *, n_exp, topk, rb, slab_rows):
    t = et_ref.shape[1]
    c = DISPATCH_CHUNK
    e_iota = lax.broadcasted_iota(jnp.int32, (n_exp, c), 0)
    earlier = (lax.broadcasted_iota(jnp.int32, (c, c), 0) < lax.broadcasted_iota(jnp.int32, (c, c), 1))
    earlier = jnp.where(earlier, 1.0, 0.0).astype(BF16)

    run = jnp.zeros((n_exp, 1), F32)
    for k in range(topk):
        def rank_chunk(j, run, k=k):
            off = pl.multiple_of(j * c, c)
            oh = e_iota == et_ref[k:k + 1, pl.ds(off, c)]
            ohf = jnp.where(oh, 1.0, 0.0)
            pre = jnp.dot(ohf.astype(BF16), earlier, preferred_element_type=F32)
            rank_ref[k:k + 1, pl.ds(off, c)] = jnp.sum(jnp.where(oh, pre + run, 0.0), axis=0, keepdims=True)
            return run + jnp.sum(ohf, axis=1, keepdims=True)
        run = lax.fori_loop(0, t // c, rank_chunk, run)

    cnt = run
    nblk = jnp.floor((cnt + (rb - 1)) * (1.0 / rb))
    before = (lax.broadcasted_iota(jnp.int32, (n_exp, n_exp), 1) < lax.broadcasted_iota(jnp.int32, (n_exp, n_exp), 0))
    before = jnp.where(before, 1.0, 0.0).astype(BF16)
    bstart = jnp.dot(before, jnp.broadcast_to(nblk, (n_exp, LANES)).astype(BF16),
                     preferred_element_type=F32)[:, 0:1]

    for k in range(topk):
        def dest_chunk(j, carry, k=k):
            off = pl.multiple_of(j * c, c)
            oh = e_iota == et_ref[k:k + 1, pl.ds(off, c)]
            base = jnp.sum(jnp.where(oh, bstart, 0.0), axis=0, keepdims=True) * rb
            dest_ref[k:k + 1, pl.ds(off, c)] = (base + rank_ref[k:k + 1, pl.ds(off, c)]).astype(jnp.int32)
            return carry
        lax.fori_loop(0, t // c, dest_chunk, 0)

    rows_vm[...] = jnp.full(rows_vm.shape, -1, jnp.int32)
    fill = pltpu.make_async_copy(rows_vm, rows_sm, sem)
    fill.start()
    fill.wait()
    sc = DISPATCH_SMEM_CHUNK
    for k in range(topk):
        def scatter_chunk(j, carry, k=k):
            off = pl.multiple_of(j * sc, sc)
            cp = pltpu.make_async_copy(dest_ref.at[pl.ds(k, 1), pl.ds(off, sc)], dest_sm, sem)
            cp.start()
            cp.wait()

            def one(a, carry2):
                d = dest_sm[0, a]
                rows_sm[lax.shift_right_logical(d, LANE_SHIFT), d & (LANES - 1)] = k * t + off + a
                return carry2
            return lax.fori_loop(0, sc, one, carry, unroll=8)
        lax.fori_loop(0, t // sc, scatter_chunk, 0)
    back = pltpu.make_async_copy(rows_sm, rows_vm, sem)
    back.start()
    back.wait()

    a = rows_vm[...]
    valid = a >= 0
    tok = a
    for k in range(1, topk):
        tok = jnp.where(a >= k * t, a - k * t, tok)
    flat = (lax.broadcasted_iota(jnp.int32, a.shape, 0) * LANES + lax.broadcasted_iota(jnp.int32, a.shape, 1))
    scratch_dst = topk * t + ((flat // rb) % 2) * rb + flat % rb
    tok_ref[...] = jnp.where(valid, tok, 0) * slab_rows
    dst_ref[...] = jnp.where(valid, a, scratch_dst) * slab_rows

    nbp = blk_ref.shape[1]
    b_row = lax.broadcasted_iota(jnp.int32, (1, nbp), 1).astype(F32)
    bend = bstart + nblk
    blk_e = jnp.minimum(jnp.sum(jnp.where(bend <= b_row, 1.0, 0.0), axis=0, keepdims=True), n_exp - 1.0)
    mine = lax.broadcasted_iota(jnp.int32, (n_exp, nbp), 0).astype(F32) == blk_e
    cnt_b = jnp.sum(jnp.where(mine, cnt, 0.0), axis=0, keepdims=True)
    bstart_b = jnp.sum(jnp.where(mine, bstart, 0.0), axis=0, keepdims=True)
    blk_n = jnp.clip(cnt_b - (b_row - bstart_b) * rb, 0.0, float(rb))
    row = lax.broadcasted_iota(jnp.int32, blk_ref.shape, 0)
    blk_ref[...] = jnp.where(row == 0, blk_e, jnp.where(row == 1, blk_n, 0.0)).astype(jnp.int32)


def _dispatch(et, n_exp, rb, slab_rows):
    t = et.shape[1]
    topk = EXPERT_TOPK
    n_asg = topk * t
    assert n_asg % rb == 0 and t % DISPATCH_SMEM_CHUNK == 0 and (rb & (rb - 1)) == 0 and rb % LANES == 0
    n_rb = n_asg // rb + n_exp
    n_rows = n_rb * rb
    nbp = -(-n_rb // LANES) * LANES
    tok, dst, blk = pl.pallas_call(
        functools.partial(_dispatch_kernel, n_exp=n_exp, topk=topk, rb=rb, slab_rows=slab_rows),
        in_specs=[pl.BlockSpec(memory_space=pltpu.VMEM)],
        out_specs=[pl.BlockSpec(memory_space=pltpu.VMEM)] * 3,
        out_shape=[jax.ShapeDtypeStruct((n_rows // LANES, LANES), jnp.int32),
                   jax.ShapeDtypeStruct((n_rows // LANES, LANES), jnp.int32),
                   jax.ShapeDtypeStruct((SUBLANES, nbp), jnp.int32)],
        scratch_shapes=[pltpu.VMEM((SUBLANES, t), F32),
                        pltpu.VMEM((SUBLANES, t), jnp.int32),
                        pltpu.SMEM((1, DISPATCH_SMEM_CHUNK), jnp.int32),
                        pltpu.SMEM((n_rows // LANES, LANES), jnp.int32),
                        pltpu.VMEM((n_rows // LANES, LANES), jnp.int32),
                        pltpu.SemaphoreType.DMA(())],
        compiler_params=pltpu.CompilerParams(vmem_limit_bytes=VMEM_LIMIT_BYTES),
        name="moe_dispatch",
    )(et)
    return tok.reshape(n_rb, rb), dst.reshape(n_rb, rb), blk[0, :n_rb], blk[1, :n_rb]


WEIGHT_CHUNK_ROWS = 512
WEIGHT_RING = 4


def _expert_kernel(blk_e, blk_n, tok_all, dst_all, xn_hbm, wg_hbm, wu_hbm, wd_hbm,
                   y_hbm, xbuf, obuf, wbuf_g, wbuf_u, wbuf_d, stg, run_end, next_e, run_start, wslot_of,
                   succ_e, st, gsem, ssem, wsem, *, n_rb, rb, layer):
    i = pl.program_id(0)
    slot = lax.rem(i, 2)
    n = blk_n[i]
    d, f = wbuf_g.shape[1:]
    cr = stg.shape[1]
    n_gu = d // cr
    n_dc = d // f
    n_chunks = 2 * n_gu + (f // cr) * n_dc
    ring = stg.shape[0]
    G_DONE, P_EXP, P_CHUNK = 0, 1, 2

    chunks = ([(wg_hbm, wbuf_g, r * cr, 0) for r in range(n_gu)]
              + [(wu_hbm, wbuf_u, r * cr, 0) for r in range(n_gu)]
              + [(wd_hbm, wbuf_d, r * cr, h * f) for r in range(f // cr) for h in range(n_dc)])
    assert len(chunks) == n_chunks

    def chunk_copy(e, c, k):
        for cid, (w_hbm, _, r0, c0) in enumerate(chunks):
            @pl.when(c == cid)
            def _(w_hbm=w_hbm, r0=r0, c0=c0):
                pltpu.make_async_copy(w_hbm.at[layer, e, pl.ds(r0, cr), pl.ds(c0, f)], stg.at[k],
                                      wsem.at[k]).start()

    def start_next_chunk(k):
        e = st[P_EXP]
        c = st[P_CHUNK]

        @pl.when(e >= 0)
        def _():
            chunk_copy(e, c, k)
            last = c + 1 == n_chunks
            st[P_CHUNK] = jnp.where(last, 0, c + 1)
            st[P_EXP] = jnp.where(last, succ_e[e], e)

    def convert_chunk(c, ws):
        g = st[G_DONE]
        k = lax.rem(g, ring)
        pltpu.make_async_copy(wg_hbm.at[layer, 0, pl.ds(0, cr), :], stg.at[k], wsem.at[k]).wait()
        start_next_chunk(lax.rem(g + ring - 1, ring))
        for cid, (_, wbuf, r0, c0) in enumerate(chunks):
            @pl.when(c == cid)
            def _(wbuf=wbuf, r0=r0, c0=c0):
                wbuf[ws, r0:r0 + cr, c0:c0 + f] = stg[k].astype(BF16)
        st[G_DONE] = g + 1

    @pl.when(i == 0)
    def _():
        def back(ii, carry):
            end_nb, next_nb = carry
            b = n_rb - 1 - ii
            nb = jnp.minimum(b + 1, n_rb - 1)
            nb_active = (b + 1 < n_rb) & (blk_n[nb] > 0)
            same = nb_active & (blk_e[nb] == blk_e[b])
            end_b = jnp.where(same, end_nb, b + 1)
            next_b = jnp.where(same, next_nb, jnp.where(nb_active, blk_e[nb], -1))
            run_end[b] = end_b
            next_e[b] = next_b

            @pl.when((blk_n[b] > 0) & jnp.logical_not(same))
            def _():
                succ_e[blk_e[b]] = next_b
            return end_b, next_b
        lax.fori_loop(0, n_rb, back, (jnp.int32(n_rb), jnp.int32(-1)))

        def fwd(b, carry):
            start_pb, wslot_pb = carry
            pb = jnp.maximum(b - 1, 0)
            same = (b > 0) & (blk_e[pb] == blk_e[b])
            start_b = jnp.where(same, start_pb, b)
            wslot_b = jnp.where(b == 0, 0, jnp.where(same, wslot_pb, 1 - wslot_pb))
            run_start[b] = start_b
            wslot_of[b] = wslot_b
            return start_b, wslot_b
        lax.fori_loop(0, n_rb, fwd, (jnp.int32(0), jnp.int32(0)))

        st[G_DONE] = 0
        st[P_CHUNK] = 0
        st[P_EXP] = jnp.where(n > 0, blk_e[0], -1)

        @pl.when(n > 0)
        def _():
            for k in range(ring - 1):
                start_next_chunk(k)

            def first(c, carry):
                convert_chunk(c, 0)
                return carry
            lax.fori_loop(0, n_chunks, first, 0)

    ws = wslot_of[i]

    @pl.when((n > 0) & (next_e[i] >= 0))
    def _():
        run_len = run_end[i] - run_start[i]
        per = lax.div(n_chunks + run_len - 1, run_len)
        lo = (i - run_start[i]) * per
        hi = jnp.minimum(lo + per, n_chunks)

        def nxt(c, carry):
            convert_chunk(c, 1 - ws)
            return carry
        lax.fori_loop(lo, hi, nxt, 0)

    ns = xbuf.shape[1] // rb

    def gather_row(blk, sl, r):
        return pltpu.make_async_copy(xn_hbm.at[pl.ds(tok_all[blk * rb + r], ns)], xbuf.at[sl, pl.ds(r * ns, ns)],
                                     gsem.at[sl])

    def start_gather(blk, sl):
        def body(r, c):
            gather_row(blk, sl, r).start()
            return c
        lax.fori_loop(0, rb, body, 0, unroll=8)

    def scatter_row(sl, r, dst):
        return pltpu.make_async_copy(obuf.at[sl, pl.ds(r * ns, ns)], y_hbm.at[pl.ds(dst, ns)], ssem.at[sl])

    def wait_scatter(sl):
        pltpu.make_async_copy(obuf.at[sl], y_hbm.at[pl.ds(0, rb * ns)], ssem.at[sl]).wait()

    @pl.when(i == 0)
    def _():
        obuf[...] = jnp.zeros_like(obuf)
        n_real = y_hbm.shape[0] - 2 * rb * ns
        for sl in range(2):
            init = pltpu.make_async_copy(obuf.at[sl], y_hbm.at[pl.ds(n_real + sl * rb * ns, rb * ns)],
                                         ssem.at[sl])
            init.start()
            init.wait()

    n_prev = blk_n[jnp.maximum(i - 1, 0)]
    n_next = blk_n[jnp.minimum(i + 1, n_rb - 1)]
    do_gather = (i + 1 < n_rb) & (n_next > 0)
    do_scatter = (i >= 1) & (n_prev > 0)

    @pl.when((i == 0) & (n > 0))
    def _():
        start_gather(0, 0)

    @pl.when(n > 0)
    def _():
        pltpu.make_async_copy(xn_hbm.at[pl.ds(0, rb * ns)], xbuf.at[slot], gsem.at[slot]).wait()

    hd = ns * LANES
    fc, dc = min(f, MXU_COLS), min(hd, MXU_COLS)

    def free_obuf():
        @pl.when((i >= 2) & (blk_n[jnp.maximum(i - 2, 0)] > 0))
        def _():
            wait_scatter(slot)

    def compute(interleave_dma):
        def issue(piece, n_piece, start_row):
            if interleave_dma:
                per = -(-rb // n_piece)
                for r in range(piece * per, min(rb, (piece + 1) * per)):
                    start_row(r)

        x_lo, x_hi = _unpack_bf16_pairs(_load_token_slabs(xbuf.at[slot], rb, ns))
        x = jnp.concatenate([x_lo.astype(BF16), x_hi.astype(BF16)], axis=1)
        hid = []
        for c in range(f // fc):
            cols = slice(c * fc, (c + 1) * fc)
            g = jnp.dot(x, wbuf_g[ws, :, cols], preferred_element_type=F32)
            u = jnp.dot(x, wbuf_u[ws, :, cols], preferred_element_type=F32)
            hid.append(((g * jax.nn.sigmoid(g)) * u).astype(BF16))
            issue(c, f // fc, lambda r: gather_row(i + 1, 1 - slot, r).start(priority=r % 2))
        hid = jnp.concatenate(hid, axis=1)
        free_obuf()
        for c in range(hd // dc):
            cols = slice(c * dc, (c + 1) * dc)
            cols_hi = slice(hd + c * dc, hd + (c + 1) * dc)
            y_lo = jnp.dot(hid, wbuf_d[ws, :, cols], preferred_element_type=F32)
            y_hi = jnp.dot(hid, wbuf_d[ws, :, cols_hi], preferred_element_type=F32)
            _store_token_slabs(obuf.at[slot], _pack_bf16_pairs(y_lo, y_hi), ns, first_tile=c * dc // LANES)
            issue(c, hd // dc,
                  lambda r: scatter_row(1 - slot, r, dst_all[(i - 1) * rb + r]).start(priority=r % 2))

    fast = (n > 0) & do_gather & do_scatter

    @pl.when(fast)
    def _():
        compute(True)

    @pl.when(jnp.logical_not(fast))
    def _():
        @pl.when(do_gather)
        def _():
            start_gather(i + 1, 1 - slot)

        @pl.when(do_scatter)
        def _():
            def body(r, c):
                scatter_row(1 - slot, r, dst_all[(i - 1) * rb + r]).start()
                return c
            lax.fori_loop(0, rb, body, 0, unroll=8)

        @pl.when(n > 0)
        def _():
            compute(False)

        @pl.when(n == 0)
        def _():
            free_obuf()

    @pl.when((i == n_rb - 1) & do_scatter)
    def _():
        wait_scatter(1 - slot)


def _experts(xn, row_tok, row_dst, blk_e, blk_n, w_gate, w_up, w_down, layer, n_slots):
    n_exp, d, f = w_gate.shape[1:]
    ns = d // 2 // LANES
    assert xn.shape[1] == LANES and d % (2 * LANES) == 0 and w_down.shape[1:] == (n_exp, f, d)
    n_rb, rb = row_tok.shape
    assert (n_slots - 2 * rb) % rb == 0
    cr = min(WEIGHT_CHUNK_ROWS, f)
    assert d % cr == 0 and f % cr == 0 and d % f == 0
    hbm = pl.BlockSpec(memory_space=pl.ANY)
    grid_spec = pltpu.PrefetchScalarGridSpec(
        num_scalar_prefetch=4,
        grid=(n_rb,),
        in_specs=[hbm, hbm, hbm, hbm],
        out_specs=hbm,
        scratch_shapes=[pltpu.VMEM((2, rb * ns, LANES), jnp.uint32),
                        pltpu.VMEM((2, rb * ns, LANES), jnp.uint32),
                        pltpu.VMEM((2, d, f), BF16),
                        pltpu.VMEM((2, d, f), BF16),
                        pltpu.VMEM((2, f, d), BF16),
                        pltpu.VMEM((WEIGHT_RING, cr, f), F32),
                        pltpu.SMEM((n_rb,), jnp.int32),
                        pltpu.SMEM((n_rb,), jnp.int32),
                        pltpu.SMEM((n_rb,), jnp.int32),
                        pltpu.SMEM((n_rb,), jnp.int32),
                        pltpu.SMEM((n_exp,), jnp.int32),
                        pltpu.SMEM((4,), jnp.int32),
                        pltpu.SemaphoreType.DMA((2,)),
                        pltpu.SemaphoreType.DMA((2,)),
                        pltpu.SemaphoreType.DMA((WEIGHT_RING,))],
    )
    return pl.pallas_call(
        functools.partial(_expert_kernel, n_rb=n_rb, rb=rb, layer=layer),
        grid_spec=grid_spec,
        out_shape=jax.ShapeDtypeStruct((n_slots * ns, LANES), jnp.uint32),
        compiler_params=_cparams(("arbitrary",)),
        name="moe_experts",
    )(blk_e, blk_n, row_tok.reshape(-1), row_dst.reshape(-1), xn, w_gate, w_up, w_down)


def _combine_kernel(h_ref, y0_ref, y1_ref, rf_ref, *rest, final):
    if final:
        g_ref, o_ref = rest
    else:
        (o_ref,) = rest
    w = rf_ref[...]
    m = h_ref.shape[0]
    ns = y0_ref.shape[0] // m
    y0_lo, y0_hi = _unpack_bf16_pairs(_load_token_slabs(y0_ref, m, ns))
    y1_lo, y1_hi = _unpack_bf16_pairs(_load_token_slabs(y1_ref, m, ns))
    y = jnp.concatenate([w[:, 0:1] * y0_lo + w[:, 1:2] * y1_lo, w[:, 0:1] * y0_hi + w[:, 1:2] * y1_hi], axis=1)
    out = h_ref[...] + y
    if final:
        out = _rmsnorm_rows(out, g_ref[...])
    o_ref[...] = out


def _combine(h, y, rf, final_g):
    t, d = h.shape
    tm = _tile(t, 512)
    final = final_g is not None
    in_specs = [pl.BlockSpec((tm, d), lambda i: (i, 0)),
                pl.BlockSpec((tm * (d // 2 // LANES), LANES), lambda i: (i, 0)),
                pl.BlockSpec((tm * (d // 2 // LANES), LANES), lambda i: (t // tm + i, 0)),
                pl.BlockSpec((tm, LANES), lambda i: (i, 0))]
    args = [h, y, y, rf]
    if final:
        in_specs.append(pl.BlockSpec((1, d), lambda i: (0, 0)))
        args.append(final_g.reshape(1, d))
    return pl.pallas_call(
        functools.partial(_combine_kernel, final=final),
        grid=(t // tm,),
        in_specs=in_specs,
        out_specs=pl.BlockSpec((tm, d), lambda i: (i, 0)),
        out_shape=jax.ShapeDtypeStruct((t, d), F32),
        compiler_params=_cparams(("parallel",)),
        name="moe_combine",
    )(*args)


def _mix_out_moe(act, w_out, b_out, h, norm_g, w_grp, b_grp, w_exp, b_exp, w_gate, w_up, w_down, layer, final_g):
    t, d = h.shape
    n_grp = w_grp.shape[1]
    n_exp = w_exp.shape[1]
    epg = n_exp // n_grp
    assert n_grp + n_exp <= LANES
    pad = LANES - n_grp - n_exp
    w_cat = jnp.concatenate([w_grp, w_exp, jnp.zeros((d, pad), F32)], axis=1).astype(BF16)
    b_cat = jnp.concatenate([b_grp, b_exp, jnp.zeros((pad,), F32)]).reshape(1, LANES)
    h, xn, et, rf = _mix_out_router(act, w_out, b_out, h, norm_g, w_cat, b_cat, n_grp, epg)
    row_tok, row_dst, blk_e, blk_n = _dispatch(et, n_exp, ROW_BLOCK, d // 2 // LANES)
    y = _experts(xn, row_tok, row_dst, blk_e, blk_n, w_gate, w_up, w_down, layer,
                 EXPERT_TOPK * t + 2 * ROW_BLOCK)
    return _combine(h, y, rf, final_g)


def _rope_tables(seq, dh):
    inv = 1.0 / (ROPE_THETA ** (jnp.arange(0, dh, 2, dtype=F32) / dh))
    ang = jnp.arange(seq, dtype=F32)[:, None] * inv[None, :]
    cos, sin = jnp.cos(ang), jnp.sin(ang)
    return jnp.concatenate([cos, cos], axis=1), jnp.concatenate([-sin, sin], axis=1)


def kernel(x, lru_norm, lru_w_in, lru_b_in, lru_conv_w, lru_conv_b, lru_w_r, lru_b_r, lru_w_i, lru_b_i, lru_lambda, lru_w_out, lru_b_out, att_norm, att_w_qkv, att_w_o, ffn_norm, moe_w_grp, moe_b_grp, moe_w_exp, moe_b_exp, moe_w_gate, moe_w_up, moe_w_down, final_norm):
    bsz, seq, d = x.shape
    depth = ffn_norm.shape[0]
    n_mixers = 2
    cc, ss = _rope_tables(seq, d // N_HEADS)
    h = x.reshape(bsz * seq, d)
    for layer in range(depth):
        j = layer // n_mixers
        if layer % n_mixers == 0:
            xb, gate = _lru_in(h, lru_norm[j], lru_w_in[j].astype(BF16), lru_b_in[j])
            act = _lru_scan(xb, gate, lru_conv_w[j], lru_conv_b[j], lru_w_r[j].astype(BF16), lru_b_r[j],
                            lru_w_i[j].astype(BF16), lru_b_i[j], lru_lambda[j], bsz, seq)
            w_out, b_out = lru_w_out[j].astype(BF16), lru_b_out[j]
        else:
            qkv = _qkv_rope(h, att_norm[j], att_w_qkv[j].astype(BF16), cc, ss, seq)
            act = _moba_attention(qkv, bsz, seq, d)
            w_out, b_out = att_w_o[j].astype(BF16), None
        final_g = final_norm if layer == depth - 1 else None
        h = _mix_out_moe(act, w_out, b_out, h, ffn_norm[layer], moe_w_grp[layer], moe_b_grp[layer],
                         moe_w_exp[layer], moe_b_exp[layer], moe_w_gate, moe_w_up, moe_w_down, layer, final_g)
    return h.reshape(bsz, seq, d)
```

```python
import functools

import jax
import jax.numpy as jnp
from jax import lax
from jax.experimental import pallas as pl
from jax.experimental.pallas import tpu as pltpu

F32 = jnp.float32
BF16 = jnp.bfloat16

NORM_EPS = 1e-6
NEG_INF = -1e30
LRU_C = 8.0
N_HEADS = 16
MOBA_BLOCK = 256
MOBA_TOPK = 3
ROPE_THETA = 10000.0
EXPERT_TOPK = 2
ROW_BLOCK = 256
LANES = 128
SUBLANES = 8
BF16_SUBLANES = 16
MXU_COLS = 256
VMEM_LIMIT_BYTES = 56 * 1024 * 1024


def _tile(n, pref):
    t = min(n, pref)
    while n % t:
        t //= 2
    return t


def _cparams(sem):
    return pltpu.CompilerParams(dimension_semantics=sem, vmem_limit_bytes=VMEM_LIMIT_BYTES)


def _sigmoid(z):
    return 0.5 * jnp.tanh(0.5 * z) + 0.5


def _rmsnorm_rows(x, g):
    ms = jnp.mean(x * x, axis=-1, keepdims=True)
    return x * lax.rsqrt(ms + NORM_EPS) * g


def _lru_in_kernel(x_ref, g_ref, wx_ref, wg_ref, bx_ref, bg_ref, ox_ref, og_ref, xn_ref):
    @pl.when(pl.program_id(1) == 0)
    def _():
        xn_ref[...] = _rmsnorm_rows(x_ref[...], g_ref[...]).astype(BF16)

    xn = xn_ref[...]
    ox_ref[...] = jnp.dot(xn, wx_ref[...], preferred_element_type=F32) + bx_ref[...]
    og_ref[...] = jax.nn.gelu(jnp.dot(xn, wg_ref[...], preferred_element_type=F32) + bg_ref[...],
                              approximate=True)


def _lru_in(h, g, w, b):
    t, d = h.shape
    wdt = w.shape[1] // 2
    tm, tn = _tile(t, 1024), _tile(wdt, 512)
    nj = wdt // tn
    b2 = b.reshape(1, 2 * wdt)
    return pl.pallas_call(
        _lru_in_kernel,
        grid=(t // tm, nj),
        in_specs=[pl.BlockSpec((tm, d), lambda i, j: (i, 0)),
                  pl.BlockSpec((1, d), lambda i, j: (0, 0)),
                  pl.BlockSpec((d, tn), lambda i, j: (0, j)),
                  pl.BlockSpec((d, tn), lambda i, j: (0, nj + j)),
                  pl.BlockSpec((1, tn), lambda i, j: (0, j)),
                  pl.BlockSpec((1, tn), lambda i, j: (0, nj + j))],
        out_specs=[pl.BlockSpec((tm, tn), lambda i, j: (i, j)),
                   pl.BlockSpec((tm, tn), lambda i, j: (i, j))],
        out_shape=[jax.ShapeDtypeStruct((t, wdt), F32),
                   jax.ShapeDtypeStruct((t, wdt), F32)],
        scratch_shapes=[pltpu.VMEM((tm, d), BF16)],
        compiler_params=_cparams(("parallel", "arbitrary")),
        name="lru_in",
    )(h, g.reshape(1, d), w, w, b2, b2)


def _qkv_kernel(x_ref, g_ref, w_ref, cc_ref, ss_ref, o_ref, xn_ref, *, n_rope, heads_per_tile, dh):
    j = pl.program_id(1)

    @pl.when(j == 0)
    def _():
        xn_ref[...] = _rmsnorm_rows(x_ref[...], g_ref[...]).astype(BF16)

    acc = jnp.dot(xn_ref[...], w_ref[...], preferred_element_type=F32)

    is_rope = j < n_rope
    cc = cc_ref[...]
    ss = ss_ref[...]
    for hh in range(heads_per_tile):
        a = acc[:, hh * dh:(hh + 1) * dh]
        rot = a * cc + pltpu.roll(a, dh // 2, 1) * ss
        o_ref[:, hh * dh:(hh + 1) * dh] = jnp.where(is_rope, rot, a).astype(o_ref.dtype)


def _qkv_rope(h, g, w, cc, ss, seq):
    t, d = h.shape
    n = w.shape[1]
    dh = d // N_HEADS
    tm, tn = _tile(seq, 1024), _tile(d, 1024)
    s_tiles = seq // tm
    return pl.pallas_call(
        functools.partial(_qkv_kernel, n_rope=(2 * d) // tn, heads_per_tile=tn // dh, dh=dh),
        grid=(t // tm, n // tn),
        in_specs=[pl.BlockSpec((tm, d), lambda i, j: (i, 0)),
                  pl.BlockSpec((1, d), lambda i, j: (0, 0)),
                  pl.BlockSpec((d, tn), lambda i, j: (0, j)),
                  pl.BlockSpec((tm, dh), lambda i, j: (i % s_tiles, 0)),
                  pl.BlockSpec((tm, dh), lambda i, j: (i % s_tiles, 0))],
        out_specs=pl.BlockSpec((tm, tn), lambda i, j: (i, j)),
        out_shape=jax.ShapeDtypeStruct((t, n), BF16),
        scratch_shapes=[pltpu.VMEM((tm, d), BF16)],
        compiler_params=_cparams(("parallel", "arbitrary")),
        name="qkv_rope",
    )(h, g.reshape(1, d), w, cc, ss)


def _lru_scan_kernel(xb_ref, gate_ref, cw_ref, cb_ref, wr_ref, br_ref, wi_ref, bi_ref, lam_ref,
                     o_ref, ext_ref, a_ref, b_ref, h_ref, *, ts, n_grp, gw, conv_w):
    s = pl.program_id(1)
    w = xb_ref.shape[1]

    @pl.when(s == 0)
    def _():
        ext_ref[0:SUBLANES, :] = jnp.zeros((SUBLANES, w), F32)
        h_ref[...] = jnp.zeros_like(h_ref)

    ext_ref[SUBLANES:SUBLANES + ts, :] = xb_ref[...]
    xc = cb_ref[...] + cw_ref[conv_w - 1:conv_w, :] * xb_ref[...]
    for k in range(1, conv_w):
        xc = xc + cw_ref[conv_w - 1 - k:conv_w - k, :] * ext_ref[SUBLANES - k:SUBLANES - k + ts, :]
    ext_ref[0:SUBLANES, :] = ext_ref[ts:ts + SUBLANES, :]

    z = -lam_ref[...]
    softplus = jnp.maximum(z, 0.0) + jnp.log1p(jnp.exp(-jnp.abs(z)))
    c = -LRU_C * softplus
    for g in range(n_grp):
        sl = slice(g * gw, (g + 1) * gw)
        xg = xc[:, sl]
        xg16 = xg.astype(BF16)
        r = _sigmoid(jnp.dot(xg16, wr_ref[g], preferred_element_type=F32) + br_ref[:, sl])
        ig = _sigmoid(jnp.dot(xg16, wi_ref[g], preferred_element_type=F32) + bi_ref[:, sl])
        log_a = c[:, sl] * r
        a_ref[:, sl] = jnp.exp(log_a)
        th = jnp.tanh(log_a)
        b_ref[:, sl] = jnp.sqrt(-2.0 * th / (1.0 - th)) * ig * xg

    rows = lax.broadcasted_iota(jnp.int32, (SUBLANES, w), 0)

    def chunk(ci, h):
        r0 = pl.multiple_of(ci * SUBLANES, SUBLANES)
        a = a_ref[pl.ds(r0, SUBLANES), :]
        b = b_ref[pl.ds(r0, SUBLANES), :]
        for sh in (1, 2, 4):
            keep = rows >= sh
            a_sh = jnp.where(keep, pltpu.roll(a, sh, 0), 1.0)
            b_sh = jnp.where(keep, pltpu.roll(b, sh, 0), 0.0)
            b = a * b_sh + b
            a = a * a_sh
        hc = a * h + b
        b_ref[pl.ds(r0, SUBLANES), :] = hc * gate_ref[pl.ds(r0, SUBLANES), :]
        return jnp.broadcast_to(hc[SUBLANES - 1:SUBLANES, :], (SUBLANES, w))

    h_ref[...] = lax.fori_loop(0, ts // SUBLANES, chunk, h_ref[...])
    o_ref[...] = b_ref[...].astype(o_ref.dtype)


def _lru_scan(xb, gate, conv_w, conv_b, w_r, b_r, w_i, b_i, lam, bsz, seq):
    t, w = xb.shape
    n_grp, gw = w_r.shape[0], w_r.shape[1]
    cw = conv_w.shape[0]
    ts = _tile(seq, 256)
    s_tiles = seq // ts
    row = lambda b, s: (b * s_tiles + s, 0)
    vec = pl.BlockSpec((1, w), lambda b, s: (0, 0))
    return pl.pallas_call(
        functools.partial(_lru_scan_kernel, ts=ts, n_grp=n_grp, gw=gw, conv_w=cw),
        grid=(bsz, s_tiles),
        in_specs=[pl.BlockSpec((ts, w), row),
                  pl.BlockSpec((ts, w), row),
                  pl.BlockSpec((cw, w), lambda b, s: (0, 0)),
                  vec,
                  pl.BlockSpec((n_grp, gw, gw), lambda b, s: (0, 0, 0)),
                  vec,
                  pl.BlockSpec((n_grp, gw, gw), lambda b, s: (0, 0, 0)),
                  vec,
                  vec],
        out_specs=pl.BlockSpec((ts, w), row),
        out_shape=jax.ShapeDtypeStruct((t, w), BF16),
        scratch_shapes=[pltpu.VMEM((ts + SUBLANES, w), F32),
                        pltpu.VMEM((ts, w), F32),
                        pltpu.VMEM((ts, w), F32),
                        pltpu.VMEM((SUBLANES, w), F32)],
        compiler_params=_cparams(("parallel", "arbitrary")),
        name="lru_scan",
    )(xb, gate, conv_w, conv_b.reshape(1, w), w_r, b_r.reshape(1, w), w_i, b_i.reshape(1, w),
      lam.reshape(1, w))


ATTN_HEADS_PER_STEP = 1


def _attn_kernel(q_ref, k_ref, v_ref, o_ref, kaug_ref, vaug_t_ref, *, nb, blk, dh, topk):
    seq = nb * blk
    hp = kaug_ref.shape[0]
    nbp = vaug_t_ref.shape[1] - dh
    first = (pl.program_id(0) == 0) & (pl.program_id(1) == 0)

    @pl.when(first)
    def _():
        row = lax.broadcasted_iota(jnp.int32, (seq, LANES), 0)
        lane = lax.broadcasted_iota(jnp.int32, (seq, LANES), 1)
        ones_row = (lax.broadcasted_iota(jnp.int32, (nbp, seq), 0) == 0).astype(BF16)
        for hd in range(hp):
            kaug_ref[hd, :, dh:dh + LANES] = (lane == row // blk).astype(BF16)
            vaug_t_ref[hd, dh:dh + nbp, :] = ones_row

    kmeans = []
    for hd in range(hp):
        cols = slice(hd * dh, (hd + 1) * dh)
        kaug_ref[hd, :, 0:dh] = k_ref[:, cols]
        for j in range(nb):
            vaug_t_ref[hd, 0:dh, j * blk:(j + 1) * blk] = v_ref[j * blk:(j + 1) * blk, cols].astype(F32).T.astype(BF16)
        kmean = jnp.sum(k_ref[:, cols].astype(F32).reshape(nb, blk, dh), axis=1) * (1.0 / blk)
        if nbp > nb:
            kmean = jnp.concatenate([kmean, jnp.zeros((nbp - nb, dh), F32)], axis=0)
        kmeans.append(kmean.astype(BF16))

    c = (dh ** -0.5) * 1.4426950408889634
    blk_id = lax.broadcasted_iota(jnp.int32, (nbp, blk), 0)
    key_pos = lax.broadcasted_iota(jnp.int32, (blk, blk), 0)
    q_pos = lax.broadcasted_iota(jnp.int32, (blk, blk), 1)
    zeros_pad = jnp.zeros((LANES - nbp, blk), BF16)

    def scores(hd, i):
        q_t = q_ref[i * blk:(i + 1) * blk, hd * dh:(hd + 1) * dh].astype(F32).T.astype(BF16)
        gate_t = jnp.dot(kmeans[hd], q_t, preferred_element_type=F32)
        g = jnp.where(blk_id < i, gate_t, NEG_INF)
        sel = blk_id == i
        for _ in range(topk):
            m = jnp.max(g, axis=0, keepdims=True)
            first_max = jnp.min(jnp.where(g == m, blk_id, nbp), axis=0, keepdims=True)
            hit = blk_id == first_max
            sel = sel | (hit & (blk_id < i))
            g = jnp.where(hit, -jnp.inf, g)
        bias_t = jnp.where(sel, 0.0, NEG_INF).astype(BF16)
        q_aug_t = jnp.concatenate([q_t, bias_t, zeros_pad], axis=0)
        return jnp.dot(kaug_ref[hd, 0:(i + 1) * blk, :], q_aug_t, preferred_element_type=F32)

    def probs(i, s):
        s_own = jnp.where(key_pos <= q_pos, s[i * blk:, :], NEG_INF)
        m = jnp.max(s_own, axis=0, keepdims=True)
        if i > 0:
            s_past = s[:i * blk, :]
            m = jnp.maximum(m, jnp.max(s_past, axis=0, keepdims=True))
            p = jnp.concatenate([jnp.exp2((s_past - m) * c), jnp.exp2((s_own - m) * c)], axis=0)
        else:
            p = jnp.exp2((s_own - m) * c)
        return p.astype(BF16)

    def pv(hd, i, p):
        out_t = jnp.dot(vaug_t_ref[hd, :, 0:(i + 1) * blk], p, preferred_element_type=F32)
        o_t = out_t[0:dh, :] / out_t[dh:dh + 1, :]
        o_ref[i * blk:(i + 1) * blk, hd * dh:(hd + 1) * dh] = o_t.T.astype(o_ref.dtype)

    order = list(range(nb - 1, -1, -1))
    s_of, p_of = {}, {}
    for step in range(nb + 2):
        for hd in range(hp):
            if step < nb:
                s_of[hd, order[step]] = scores(hd, order[step])
        for hd in range(hp):
            if 1 <= step <= nb:
                t1 = order[step - 1]
                p_of[hd, t1] = probs(t1, s_of.pop((hd, t1)))
        for hd in range(hp):
            if step >= 2:
                t2 = order[step - 2]
                pv(hd, t2, p_of.pop((hd, t2)))


def _moba_attention(qkv, bsz, seq, d):
    t = qkv.shape[0]
    dh = d // N_HEADS
    blk = MOBA_BLOCK
    nb = seq // blk
    hp = ATTN_HEADS_PER_STEP
    assert seq % blk == 0 and nb <= BF16_SUBLANES and dh == LANES and N_HEADS % hp == 0
    n_hg = N_HEADS // hp
    return pl.pallas_call(
        functools.partial(_attn_kernel, nb=nb, blk=blk, dh=dh, topk=min(MOBA_TOPK, nb)),
        grid=(bsz, n_hg),
        in_specs=[pl.BlockSpec((seq, hp * dh), lambda b, h: (b, h)),
                  pl.BlockSpec((seq, hp * dh), lambda b, h: (b, n_hg + h)),
                  pl.BlockSpec((seq, hp * dh), lambda b, h: (b, 2 * n_hg + h))],
        out_specs=pl.BlockSpec((seq, hp * dh), lambda b, h: (b, h)),
        out_shape=jax.ShapeDtypeStruct((t, d), BF16),
        scratch_shapes=[pltpu.VMEM((hp, seq, dh + LANES), BF16),
                        pltpu.VMEM((hp, dh + BF16_SUBLANES, seq), BF16)],
        compiler_params=_cparams(("arbitrary", "arbitrary")),
        name="moba_attention",
    )(qkv, qkv, qkv)


def _pack_bf16_pairs(lo, hi):
    lo_bits = lax.bitcast_convert_type(lo.astype(BF16).astype(F32), jnp.uint32)
    hi_bits = lax.bitcast_convert_type(hi.astype(BF16).astype(F32), jnp.uint32)
    return (hi_bits & jnp.uint32(0xFFFF0000)) | (lo_bits >> 16)


def _unpack_bf16_pairs(u):
    lo = lax.bitcast_convert_type(u << 16, F32)
    hi = lax.bitcast_convert_type(u & jnp.uint32(0xFFFF0000), F32)
    return lo, hi


def _store_token_slabs(ref, packed, slab_rows, first_tile=0):
    m, width = packed.shape
    for j in range(width // LANES):
        ref[pl.ds(first_tile + j, m, stride=slab_rows), :] = packed[:, j * LANES:(j + 1) * LANES]


def _load_token_slabs(ref, m, slab_rows):
    return jnp.concatenate([ref[pl.ds(j, m, stride=slab_rows), :] for j in range(slab_rows)], axis=1)


def _route_rows(h, g, w_cat, b_cat, n_grp, epg):
    xn = _rmsnorm_rows(h, g)
    hd = xn.shape[1] // 2
    xn_packed = _pack_bf16_pairs(xn[:, :hd], xn[:, hd:])
    logits = jnp.dot(xn.astype(BF16), w_cat, preferred_element_type=F32) + b_cat
    lane = lax.broadcasted_iota(jnp.int32, logits.shape, 1)

    lg = jnp.where(lane < n_grp, logits, -jnp.inf)
    mg = jnp.max(lg, axis=1, keepdims=True)
    gidx = jnp.min(jnp.where(lg == mg, lane, LANES), axis=1, keepdims=True)
    pg_sel = 1.0 / jnp.sum(jnp.exp(lg - mg), axis=1, keepdims=True)

    lo = n_grp + gidx * epg
    in_grp = (lane >= lo) & (lane < lo + epg)
    le = jnp.where(in_grp, logits, -jnp.inf)
    e = jnp.exp(le - jnp.max(le, axis=1, keepdims=True))
    pe = jnp.where(in_grp, e / jnp.sum(e, axis=1, keepdims=True), -1.0)
    v1 = jnp.max(pe, axis=1, keepdims=True)
    i1 = jnp.min(jnp.where(pe == v1, lane, LANES), axis=1, keepdims=True)
    pe2 = jnp.where(lane == i1, -1.0, pe)
    v2 = jnp.max(pe2, axis=1, keepdims=True)
    i2 = jnp.min(jnp.where(pe2 == v2, lane, LANES), axis=1, keepdims=True)
    den = v1 + v2
    ids = jnp.where(lane == 0, i1 - n_grp, jnp.where(lane == 1, i2 - n_grp, 0)).astype(F32)
    et = ids.T[0:SUBLANES, :].astype(jnp.int32)
    rf = jnp.where(lane == 0, pg_sel * v1 / den, jnp.where(lane == 1, pg_sel * v2 / den, 0.0))
    return xn_packed, et, rf


def _mix_out_router_kernel(a_ref, w_ref, *rest, has_bias, n_grp, epg):
    if has_bias:
        b_ref, rest = rest[0], rest[1:]
    r_ref, g_ref, wc_ref, bc_ref, h_ref, xn_ref, et_ref, rf_ref, h_prev = rest

    @pl.when(pl.program_id(0) == 0)
    def _():
        h_prev[...] = jnp.zeros_like(h_prev)

    xn_packed, et, rf = _route_rows(h_prev[...], g_ref[...], wc_ref[...], bc_ref[...], n_grp, epg)
    _store_token_slabs(xn_ref, xn_packed, xn_packed.shape[1] // LANES)
    et_ref[...] = et
    rf_ref[...] = rf

    y = jnp.dot(a_ref[...], w_ref[...], preferred_element_type=F32)
    if has_bias:
        y = y + b_ref[...]
    h = r_ref[...] + y
    h_ref[...] = h
    h_prev[...] = h


def _mix_out_router(a, w, b, res, g, w_cat, b_cat, n_grp, epg):
    t, k = a.shape
    d = w.shape[1]
    tm = _tile(t, 256)
    n = t // tm
    cur = lambda i: (jnp.minimum(i, n - 1), 0)
    prev = lambda i: (jnp.maximum(i - 1, 0), 0)
    has_bias = b is not None
    in_specs = [pl.BlockSpec((tm, k), cur),
                pl.BlockSpec((k, d), lambda i: (0, 0))]
    args = [a, w]
    if has_bias:
        in_specs.append(pl.BlockSpec((1, d), lambda i: (0, 0)))
        args.append(b.reshape(1, d))
    in_specs += [pl.BlockSpec((tm, d), cur),
                 pl.BlockSpec((1, d), lambda i: (0, 0)),
                 pl.BlockSpec((d, LANES), lambda i: (0, 0)),
                 pl.BlockSpec((1, LANES), lambda i: (0, 0))]
    args += [res, g.reshape(1, d), w_cat, b_cat]
    return pl.pallas_call(
        functools.partial(_mix_out_router_kernel, has_bias=has_bias, n_grp=n_grp, epg=epg),
        grid=(n + 1,),
        in_specs=in_specs,
        out_specs=[pl.BlockSpec((tm, d), cur),
                   pl.BlockSpec((tm * (d // 2 // LANES), LANES), prev),
                   pl.BlockSpec((SUBLANES, tm), lambda i: (0, jnp.maximum(i - 1, 0))),
                   pl.BlockSpec((tm, LANES), prev)],
        out_shape=[jax.ShapeDtypeStruct((t, d), F32),
                   jax.ShapeDtypeStruct((t * (d // 2 // LANES), LANES), jnp.uint32),
                   jax.ShapeDtypeStruct((SUBLANES, t), jnp.int32),
                   jax.ShapeDtypeStruct((t, LANES), F32)],
        scratch_shapes=[pltpu.VMEM((tm, d), F32)],
        compiler_params=_cparams(("arbitrary",)),
        name="mix_out_router",
    )(*args)


DISPATCH_CHUNK = 512
DISPATCH_SMEM_CHUNK = 4096


def _dispatch_kernel(et_ref, tok_ref, dst_ref, blk_ref, rank_ref, dest_ref, dest_sm, rows_sm, rows_vm,
                     sem, *, n_exp, topk, rb, slab_rows):
    t = et_ref.shape[1]
    c = DISPATCH_CHUNK
    e_iota = lax.broadcasted_iota(jnp.int32, (n_exp, c), 0)
    earlier = (lax.broadcasted_iota(jnp.int32, (c, c), 0) < lax.broadcasted_iota(jnp.int32, (c, c), 1))
    earlier = jnp.where(earlier, 1.0, 0.0).astype(BF16)

    run = jnp.zeros((n_exp, 1), F32)
    for k in range(topk):
        def rank_chunk(j, run, k=k):
            off = pl.multiple_of(j * c, c)
            oh = e_iota == et_ref[k:k + 1, pl.ds(off, c)]
            ohf = jnp.where(oh, 1.0, 0.0)
            pre = jnp.dot(ohf.astype(BF16), earlier, preferred_element_type=F32)
            rank_ref[k:k + 1, pl.ds(off, c)] = jnp.sum(jnp.where(oh, pre + run, 0.0), axis=0, keepdims=True)
            return run + jnp.sum(ohf, axis=1, keepdims=True)
        run = lax.fori_loop(0, t // c, rank_chunk, run)

    cnt = run
    nblk = jnp.floor((cnt + (rb - 1)) * (1.0 / rb))
    before = (lax.broadcasted_iota(jnp.int32, (n_exp, n_exp), 1) < lax.broadcasted_iota(jnp.int32, (n_exp, n_exp), 0))
    before = jnp.where(before, 1.0, 0.0).astype(BF16)
    bstart = jnp.dot(before, jnp.broadcast_to(nblk, (n_exp, LANES)).astype(BF16),
                     preferred_element_type=F32)[:, 0:1]

    for k in range(topk):
        def dest_chunk(j, carry, k=k):
            off = pl.multiple_of(j * c, c)
            oh = e_iota == et_ref[k:k + 1, pl.ds(off, c)]
            base = jnp.sum(jnp.where(oh, bstart, 0.0), axis=0, keepdims=True) * rb
            dest_ref[k:k + 1, pl.ds(off, c)] = (base + rank_ref[k:k + 1, pl.ds(off, c)]).astype(jnp.int32)
            return carry
        lax.fori_loop(0, t // c, dest_chunk, 0)

    rows_vm[...] = jnp.full(rows_vm.shape, -1, jnp.int32)
    fill = pltpu.make_async_copy(rows_vm, rows_sm, sem)
    fill.start()
    fill.wait()
    sc = DISPATCH_SMEM_CHUNK
    for k in range(topk):
        def scatter_chunk(j, carry, k=k):
            off = pl.multiple_of(j * sc, sc)
            cp = pltpu.make_async_copy(dest_ref.at[pl.ds(k, 1), pl.ds(off, sc)], dest_sm, sem)
            cp.start()
            cp.wait()

            def one(a, carry2):
                d = dest_sm[0, a]
                rows_sm[0, d] = k * t + off + a
                return carry2
            return lax.fori_loop(0, sc, one, carry, unroll=8)
        lax.fori_loop(0, t // sc, scatter_chunk, 0)
    back = pltpu.make_async_copy(rows_sm, rows_vm, sem)
    back.start()
    back.wait()

    a = rows_vm[...]
    valid = a >= 0
    tok = a
    for k in range(1, topk):
        tok = jnp.where(a >= k * t, a - k * t, tok)
    flat = lax.broadcasted_iota(jnp.int32, a.shape, 1)
    scratch_dst = topk * t + ((flat // rb) % 2) * rb + flat % rb
    tok_ref[...] = jnp.where(valid, tok, 0) * slab_rows
    dst_ref[...] = jnp.where(valid, a, scratch_dst) * slab_rows

    nbp = blk_ref.shape[1]
    b_row = lax.broadcasted_iota(jnp.int32, (1, nbp), 1).astype(F32)
    bend = bstart + nblk
    blk_e = jnp.minimum(jnp.sum(jnp.where(bend <= b_row, 1.0, 0.0), axis=0, keepdims=True), n_exp - 1.0)
    mine = lax.broadcasted_iota(jnp.int32, (n_exp, nbp), 0).astype(F32) == blk_e
    cnt_b = jnp.sum(jnp.where(mine, cnt, 0.0), axis=0, keepdims=True)
    bstart_b = jnp.sum(jnp.where(mine, bstart, 0.0), axis=0, keepdims=True)
    blk_n = jnp.clip(cnt_b - (b_row - bstart_b) * rb, 0.0, float(rb))
    row = lax.broadcasted_iota(jnp.int32, blk_ref.shape, 0)
    blk_ref[...] = jnp.where(row == 0, blk_e, jnp.where(row == 1, blk_n, 0.0)).astype(jnp.int32)


def _dispatch(et, n_exp, rb, slab_rows):
    t = et.shape[1]
    topk = EXPERT_TOPK
    n_asg = topk * t
    assert n_asg % rb == 0 and t % DISPATCH_SMEM_CHUNK == 0 and (rb & (rb - 1)) == 0 and rb % LANES == 0
    n_rb = n_asg // rb + n_exp
    n_rows = n_rb * rb
    nbp = -(-n_rb // LANES) * LANES
    tok, dst, blk = pl.pallas_call(
        functools.partial(_dispatch_kernel, n_exp=n_exp, topk=topk, rb=rb, slab_rows=slab_rows),
        in_specs=[pl.BlockSpec(memory_space=pltpu.VMEM)],
        out_specs=[pl.BlockSpec(memory_space=pltpu.VMEM)] * 3,
        out_shape=[jax.ShapeDtypeStruct((1, n_rows), jnp.int32),
                   jax.ShapeDtypeStruct((1, n_rows), jnp.int32),
                   jax.ShapeDtypeStruct((SUBLANES, nbp), jnp.int32)],
        scratch_shapes=[pltpu.VMEM((SUBLANES, t), F32),
                        pltpu.VMEM((SUBLANES, t), jnp.int32),
                        pltpu.SMEM((1, DISPATCH_SMEM_CHUNK), jnp.int32),
                        pltpu.SMEM((1, n_rows), jnp.int32),
                        pltpu.VMEM((1, n_rows), jnp.int32),
                        pltpu.SemaphoreType.DMA(())],
        compiler_params=pltpu.CompilerParams(vmem_limit_bytes=VMEM_LIMIT_BYTES),
        name="moe_dispatch",
    )(et)
    return tok.reshape(n_rb, rb), dst.reshape(n_rb, rb), blk[0, :n_rb], blk[1, :n_rb]


WEIGHT_CHUNK_ROWS = 512
WEIGHT_RING = 4


def _expert_kernel(blk_e, blk_n, tok_all, dst_all, xn_hbm, wg_hbm, wu_hbm, wd_hbm,
                   y_hbm, xbuf, obuf, wbuf_g, wbuf_u, wbuf_d, stg, run_end, next_e, run_start, wslot_of,
                   succ_e, st, gsem, ssem, wsem, *, n_rb, rb, layer):
    i = pl.program_id(0)
    slot = lax.rem(i, 2)
    n = blk_n[i]
    d, f = wbuf_g.shape[1:]
    cr = stg.shape[1]
    n_gu = d // cr
    n_dc = d // f
    n_chunks = 2 * n_gu + (f // cr) * n_dc
    ring = stg.shape[0]
    G_DONE, P_EXP, P_CHUNK = 0, 1, 2

    chunks = ([(wg_hbm, wbuf_g, r * cr, 0) for r in range(n_gu)]
              + [(wu_hbm, wbuf_u, r * cr, 0) for r in range(n_gu)]
              + [(wd_hbm, wbuf_d, r * cr, h * f) for r in range(f // cr) for h in range(n_dc)])
    assert len(chunks) == n_chunks

    def chunk_copy(e, c, k):
        for cid, (w_hbm, _, r0, c0) in enumerate(chunks):
            @pl.when(c == cid)
            def _(w_hbm=w_hbm, r0=r0, c0=c0):
                pltpu.make_async_copy(w_hbm.at[layer, e, pl.ds(r0, cr), pl.ds(c0, f)], stg.at[k],
                                      wsem.at[k]).start()

    def start_next_chunk(k):
        e = st[P_EXP]
        c = st[P_CHUNK]

        @pl.when(e >= 0)
        def _():
            chunk_copy(e, c, k)
            last = c + 1 == n_chunks
            st[P_CHUNK] = jnp.where(last, 0, c + 1)
            st[P_EXP] = jnp.where(last, succ_e[e], e)

    def convert_chunk(c, ws):
        g = st[G_DONE]
        k = lax.rem(g, ring)
        pltpu.make_async_copy(wg_hbm.at[layer, 0, pl.ds(0, cr), :], stg.at[k], wsem.at[k]).wait()
        start_next_chunk(lax.rem(g + ring - 1, ring))
        for cid, (_, wbuf, r0, c0) in enumerate(chunks):
            @pl.when(c == cid)
            def _(wbuf=wbuf, r0=r0, c0=c0):
                wbuf[ws, r0:r0 + cr, c0:c0 + f] = stg[k].astype(BF16)
        st[G_DONE] = g + 1

    @pl.when(i == 0)
    def _():
        def back(ii, carry):
            end_nb, next_nb = carry
            b = n_rb - 1 - ii
            nb = jnp.minimum(b + 1, n_rb - 1)
            nb_active = (b + 1 < n_rb) & (blk_n[nb] > 0)
            same = nb_active & (blk_e[nb] == blk_e[b])
            end_b = jnp.where(same, end_nb, b + 1)
            next_b = jnp.where(same, next_nb, jnp.where(nb_active, blk_e[nb], -1))
            run_end[b] = end_b
            next_e[b] = next_b

            @pl.when((blk_n[b] > 0) & jnp.logical_not(same))
            def _():
                succ_e[blk_e[b]] = next_b
            return end_b, next_b
        lax.fori_loop(0, n_rb, back, (jnp.int32(n_rb), jnp.int32(-1)))

        def fwd(b, carry):
            start_pb, wslot_pb = carry
            pb = jnp.maximum(b - 1, 0)
            same = (b > 0) & (blk_e[pb] == blk_e[b])
            start_b = jnp.where(same, start_pb, b)
            wslot_b = jnp.where(b == 0, 0, jnp.where(same, wslot_pb, 1 - wslot_pb))
            run_start[b] = start_b
            wslot_of[b] = wslot_b
            return start_b, wslot_b
        lax.fori_loop(0, n_rb, fwd, (jnp.int32(0), jnp.int32(0)))

        st[G_DONE] = 0
        st[P_CHUNK] = 0
        st[P_EXP] = jnp.where(n > 0, blk_e[0], -1)

        @pl.when(n > 0)
        def _():
            for k in range(ring - 1):
                start_next_chunk(k)

            def first(c, carry):
                convert_chunk(c, 0)
                return carry
            lax.fori_loop(0, n_chunks, first, 0)

    ws = wslot_of[i]

    @pl.when((n > 0) & (next_e[i] >= 0))
    def _():
        run_len = run_end[i] - run_start[i]
        per = lax.div(n_chunks + run_len - 1, run_len)
        lo = (i - run_start[i]) * per
        hi = jnp.minimum(lo + per, n_chunks)

        def nxt(c, carry):
            convert_chunk(c, 1 - ws)
            return carry
        lax.fori_loop(lo, hi, nxt, 0)

    ns = xbuf.shape[1] // rb

    def gather_row(blk, sl, r):
        return pltpu.make_async_copy(xn_hbm.at[pl.ds(tok_all[blk * rb + r], ns)], xbuf.at[sl, pl.ds(r * ns, ns)],
                                     gsem.at[sl])

    def start_gather(blk, sl):
        def body(r, c):
            gather_row(blk, sl, r).start()
            return c
        lax.fori_loop(0, rb, body, 0, unroll=8)

    def scatter_row(sl, r, dst):
        return pltpu.make_async_copy(obuf.at[sl, pl.ds(r * ns, ns)], y_hbm.at[pl.ds(dst, ns)], ssem.at[sl])

    def wait_scatter(sl):
        pltpu.make_async_copy(obuf.at[sl], y_hbm.at[pl.ds(0, rb * ns)], ssem.at[sl]).wait()

    @pl.when(i == 0)
    def _():
        obuf[...] = jnp.zeros_like(obuf)
        n_real = y_hbm.shape[0] - 2 * rb * ns
        for sl in range(2):
            init = pltpu.make_async_copy(obuf.at[sl], y_hbm.at[pl.ds(n_real + sl * rb * ns, rb * ns)],
                                         ssem.at[sl])
            init.start()
            init.wait()

    n_prev = blk_n[jnp.maximum(i - 1, 0)]
    n_next = blk_n[jnp.minimum(i + 1, n_rb - 1)]
    do_gather = (i + 1 < n_rb) & (n_next > 0)
    do_scatter = (i >= 1) & (n_prev > 0)

    @pl.when((i == 0) & (n > 0))
    def _():
        start_gather(0, 0)

    @pl.when(n > 0)
    def _():
        pltpu.make_async_copy(xn_hbm.at[pl.ds(0, rb * ns)], xbuf.at[slot], gsem.at[slot]).wait()

    hd = ns * LANES
    fc, dc = min(f, MXU_COLS), min(hd, MXU_COLS)

    def free_obuf():
        @pl.when((i >= 2) & (blk_n[jnp.maximum(i - 2, 0)] > 0))
        def _():
            wait_scatter(slot)

    def compute(interleave_dma):
        def issue(piece, n_piece, start_row):
            if interleave_dma:
                per = -(-rb // n_piece)
                for r in range(piece * per, min(rb, (piece + 1) * per)):
                    start_row(r)

        x_lo, x_hi = _unpack_bf16_pairs(_load_token_slabs(xbuf.at[slot], rb, ns))
        x = jnp.concatenate([x_lo.astype(BF16), x_hi.astype(BF16)], axis=1)
        hid = []
        for c in range(f // fc):
            cols = slice(c * fc, (c + 1) * fc)
            start_gather_row = lambda r: gather_row(i + 1, 1 - slot, r).start(priority=r % 2)
            g = jnp.dot(x, wbuf_g[ws, :, cols], preferred_element_type=F32)
            issue(2 * c, 2 * (f // fc), start_gather_row)
            u = jnp.dot(x, wbuf_u[ws, :, cols], preferred_element_type=F32)
            issue(2 * c + 1, 2 * (f // fc), start_gather_row)
            hid.append(((g * jax.nn.sigmoid(g)) * u).astype(BF16))
        hid = jnp.concatenate(hid, axis=1)
        free_obuf()
        for c in range(hd // dc):
            cols = slice(c * dc, (c + 1) * dc)
            cols_hi = slice(hd + c * dc, hd + (c + 1) * dc)
            start_scatter_row = lambda r: scatter_row(1 - slot, r, dst_all[(i - 1) * rb + r]).start(priority=r % 2)
            y_lo = jnp.dot(hid, wbuf_d[ws, :, cols], preferred_element_type=F32)
            issue(2 * c, 2 * (hd // dc), start_scatter_row)
            y_hi = jnp.dot(hid, wbuf_d[ws, :, cols_hi], preferred_element_type=F32)
            issue(2 * c + 1, 2 * (hd // dc), start_scatter_row)
            _store_token_slabs(obuf.at[slot], _pack_bf16_pairs(y_lo, y_hi), ns, first_tile=c * dc // LANES)

    fast = (n > 0) & do_gather & do_scatter

    @pl.when(fast)
    def _():
        compute(True)

    @pl.when(jnp.logical_not(fast))
    def _():
        @pl.when(do_gather)
        def _():
            start_gather(i + 1, 1 - slot)

        @pl.when(do_scatter)
        def _():
            def body(r, c):
                scatter_row(1 - slot, r, dst_all[(i - 1) * rb + r]).start()
                return c
            lax.fori_loop(0, rb, body, 0, unroll=8)

        @pl.when(n > 0)
        def _():
            compute(False)

        @pl.when(n == 0)
        def _():
            free_obuf()

    @pl.when((i == n_rb - 1) & do_scatter)
    def _():
        wait_scatter(1 - slot)


def _experts(xn, row_tok, row_dst, blk_e, blk_n, w_gate, w_up, w_down, layer, n_slots):
    n_exp, d, f = w_gate.shape[1:]
    ns = d // 2 // LANES
    assert xn.shape[1] == LANES and d % (2 * LANES) == 0 and w_down.shape[1:] == (n_exp, f, d)
    n_rb, rb = row_tok.shape
    assert (n_slots - 2 * rb) % rb == 0
    cr = min(WEIGHT_CHUNK_ROWS, f)
    assert d % cr == 0 and f % cr == 0 and d % f == 0
    hbm = pl.BlockSpec(memory_space=pl.ANY)
    grid_spec = pltpu.PrefetchScalarGridSpec(
        num_scalar_prefetch=4,
        grid=(n_rb,),
        in_specs=[hbm, hbm, hbm, hbm],
        out_specs=hbm,
        scratch_shapes=[pltpu.VMEM((2, rb * ns, LANES), jnp.uint32),
                        pltpu.VMEM((2, rb * ns, LANES), jnp.uint32),
                        pltpu.VMEM((2, d, f), BF16),
                        pltpu.VMEM((2, d, f), BF16),
                        pltpu.VMEM((2, f, d), BF16),
                        pltpu.VMEM((WEIGHT_RING, cr, f), F32),
                        pltpu.SMEM((n_rb,), jnp.int32),
                        pltpu.SMEM((n_rb,), jnp.int32),
                        pltpu.SMEM((n_rb,), jnp.int32),
                        pltpu.SMEM((n_rb,), jnp.int32),
                        pltpu.SMEM((n_exp,), jnp.int32),
                        pltpu.SMEM((4,), jnp.int32),
                        pltpu.SemaphoreType.DMA((2,)),
                        pltpu.SemaphoreType.DMA((2,)),
                        pltpu.SemaphoreType.DMA((WEIGHT_RING,))],
    )
    return pl.pallas_call(
        functools.partial(_expert_kernel, n_rb=n_rb, rb=rb, layer=layer),
        grid_spec=grid_spec,
        out_shape=jax.ShapeDtypeStruct((n_slots * ns, LANES), jnp.uint32),
        compiler_params=_cparams(("arbitrary",)),
        name="moe_experts",
    )(blk_e, blk_n, row_tok.reshape(-1), row_dst.reshape(-1), xn, w_gate, w_up, w_down)


def _combine_kernel(h_ref, y0_ref, y1_ref, rf_ref, *rest, final):
    if final:
        g_ref, o_ref = rest
    else:
        (o_ref,) = rest
    w = rf_ref[...]
    m = h_ref.shape[0]
    ns = y0_ref.shape[0] // m
    y0_lo, y0_hi = _unpack_bf16_pairs(_load_token_slabs(y0_ref, m, ns))
    y1_lo, y1_hi = _unpack_bf16_pairs(_load_token_slabs(y1_ref, m, ns))
    y = jnp.concatenate([w[:, 0:1] * y0_lo + w[:, 1:2] * y1_lo, w[:, 0:1] * y0_hi + w[:, 1:2] * y1_hi], axis=1)
    out = h_ref[...] + y
    if final:
        out = _rmsnorm_rows(out, g_ref[...])
    o_ref[...] = out


def _combine(h, y, rf, final_g):
    t, d = h.shape
    tm = _tile(t, 512)
    final = final_g is not None
    in_specs = [pl.BlockSpec((tm, d), lambda i: (i, 0)),
                pl.BlockSpec((tm * (d // 2 // LANES), LANES), lambda i: (i, 0)),
                pl.BlockSpec((tm * (d // 2 // LANES), LANES), lambda i: (t // tm + i, 0)),
                pl.BlockSpec((tm, LANES), lambda i: (i, 0))]
    args = [h, y, y, rf]
    if final:
        in_specs.append(pl.BlockSpec((1, d), lambda i: (0, 0)))
        args.append(final_g.reshape(1, d))
    return pl.pallas_call(
        functools.partial(_combine_kernel, final=final),
        grid=(t // tm,),
        in_specs=in_specs,
        out_specs=pl.BlockSpec((tm, d), lambda i: (i, 0)),
        out_shape=jax.ShapeDtypeStruct((t, d), F32),
        compiler_params=_cparams(("parallel",)),
        name="moe_combine",
    )(*args)


def _mix_out_moe(act, w_out, b_out, h, norm_g, w_grp, b_grp, w_exp, b_exp, w_gate, w_up, w_down, layer, final_g):
    t, d = h.shape
    n_grp = w_grp.shape[1]
    n_exp = w_exp.shape[1]
    epg = n_exp // n_grp
    assert n_grp + n_exp <= LANES
    pad = LANES - n_grp - n_exp
    w_cat = jnp.concatenate([w_grp, w_exp, jnp.zeros((d, pad), F32)], axis=1).astype(BF16)
    b_cat = jnp.concatenate([b_grp, b_exp, jnp.zeros((pad,), F32)]).reshape(1, LANES)
    h, xn, et, rf = _mix_out_router(act, w_out, b_out, h, norm_g, w_cat, b_cat, n_grp, epg)
    row_tok, row_dst, blk_e, blk_n = _dispatch(et, n_exp, ROW_BLOCK, d // 2 // LANES)
    y = _experts(xn, row_tok, row_dst, blk_e, blk_n, w_gate, w_up, w_down, layer,
                 EXPERT_TOPK * t + 2 * ROW_BLOCK)
    return _combine(h, y, rf, final_g)


def _rope_tables(seq, dh):
    inv = 1.0 / (ROPE_THETA ** (jnp.arange(0, dh, 2, dtype=F32) / dh))
    ang = jnp.arange(seq, dtype=F32)[:, None] * inv[None, :]
    cos, sin = jnp.cos(ang), jnp.sin(ang)
    return jnp.concatenate([cos, cos], axis=1), jnp.concatenate([-sin, sin], axis=1)


def kernel(x, lru_norm, lru_w_in, lru_b_in, lru_conv_w, lru_conv_b, lru_w_r, lru_b_r, lru_w_i, lru_b_i, lru_lambda, lru_w_out, lru_b_out, att_norm, att_w_qkv, att_w_o, ffn_norm, moe_w_grp, moe_b_grp, moe_w_exp, moe_b_exp, moe_w_gate, moe_w_up, moe_w_down, final_norm):
    bsz, seq, d = x.shape
    depth = ffn_norm.shape[0]
    n_mixers = 2
    cc, ss = _rope_tables(seq, d // N_HEADS)
    h = x.reshape(bsz * seq, d)
    for layer in range(depth):
        j = layer // n_mixers
        if layer % n_mixers == 0:
            xb, gate = _lru_in(h, lru_norm[j], lru_w_in[j].astype(BF16), lru_b_in[j])
            act = _lru_scan(xb, gate, lru_conv_w[j], lru_conv_b[j], lru_w_r[j].astype(BF16), lru_b_r[j],
                            lru_w_i[j].astype(BF16), lru_b_i[j], lru_lambda[j], bsz, seq)
            w_out, b_out = lru_w_out[j].astype(BF16), lru_b_out[j]
        else:
            qkv = _qkv_rope(h, att_norm[j], att_w_qkv[j].astype(BF16), cc, ss, seq)
            act = _moba_attention(qkv, bsz, seq, d)
            w_out, b_out = att_w_o[j].astype(BF16), None
        final_g = final_norm if layer == depth - 1 else None
        h = _mix_out_moe(act, w_out, b_out, h, ffn_norm[layer], moe_w_grp[layer], moe_b_grp[layer],
                         moe_w_exp[layer], moe_b_exp[layer], moe_w_gate, moe_w_up, moe_w_down, layer, final_g)
    return h.reshape(bsz, seq, d)
```

```python
import functools

import jax
import jax.numpy as jnp
from jax import lax
from jax.experimental import pallas as pl
from jax.experimental.pallas import tpu as pltpu

F32 = jnp.float32
BF16 = jnp.bfloat16

NORM_EPS = 1e-6
NEG_INF = -1e30
LRU_C = 8.0
N_HEADS = 16
MOBA_BLOCK = 256
MOBA_TOPK = 3
ROPE_THETA = 10000.0
EXPERT_TOPK = 2
ROW_BLOCK = 256
LANES = 128
SUBLANES = 8
BF16_SUBLANES = 16
MXU_COLS = 256
VMEM_LIMIT_BYTES = 56 * 1024 * 1024


def _tile(n, pref):
    t = min(n, pref)
    while n % t:
        t //= 2
    return t


def _cparams(sem):
    return pltpu.CompilerParams(dimension_semantics=sem, vmem_limit_bytes=VMEM_LIMIT_BYTES)


def _sigmoid(z):
    return 0.5 * jnp.tanh(0.5 * z) + 0.5


def _rmsnorm_rows(x, g):
    ms = jnp.mean(x * x, axis=-1, keepdims=True)
    return x * lax.rsqrt(ms + NORM_EPS) * g


def _lru_in_kernel(x_ref, g_ref, wx_ref, wg_ref, bx_ref, bg_ref, ox_ref, og_ref, xn_ref, yb_ref, *, n_j):
    s = pl.program_id(0)
    n_tiles = pl.num_programs(0) - 1

    @pl.when(s == 0)
    def _():
        yb_ref[...] = jnp.zeros(yb_ref.shape, F32)

    @pl.when(lax.rem(jnp.minimum(s, n_tiles - 1), n_j) == 0)
    def _():
        xn_ref[...] = _rmsnorm_rows(x_ref[...], g_ref[...]).astype(BF16)

    og_ref[...] = jax.nn.gelu(yb_ref[...], approximate=True)
    xn = xn_ref[...]
    ox_ref[...] = jnp.dot(xn, wx_ref[...], preferred_element_type=F32) + bx_ref[...]
    yb_ref[...] = jnp.dot(xn, wg_ref[...], preferred_element_type=F32) + bg_ref[...]


def _lru_in(h, g, w, b):
    t, d = h.shape
    wdt = w.shape[1] // 2
    tm, tn = _tile(t, 1024), _tile(wdt, 512)
    nj = wdt // tn
    n_tiles = (t // tm) * nj
    cur = lambda s: jnp.minimum(s, n_tiles - 1)
    prev = lambda s: jnp.maximum(s - 1, 0)
    b2 = b.reshape(1, 2 * wdt)
    return pl.pallas_call(
        functools.partial(_lru_in_kernel, n_j=nj),
        grid=(n_tiles + 1,),
        in_specs=[pl.BlockSpec((tm, d), lambda s: (cur(s) // nj, 0)),
                  pl.BlockSpec((1, d), lambda s: (0, 0)),
                  pl.BlockSpec((d, tn), lambda s: (0, cur(s) % nj)),
                  pl.BlockSpec((d, tn), lambda s: (0, nj + cur(s) % nj)),
                  pl.BlockSpec((1, tn), lambda s: (0, cur(s) % nj)),
                  pl.BlockSpec((1, tn), lambda s: (0, nj + cur(s) % nj))],
        out_specs=[pl.BlockSpec((tm, tn), lambda s: (cur(s) // nj, cur(s) % nj)),
                   pl.BlockSpec((tm, tn), lambda s: (prev(s) // nj, prev(s) % nj))],
        out_shape=[jax.ShapeDtypeStruct((t, wdt), F32),
                   jax.ShapeDtypeStruct((t, wdt), F32)],
        scratch_shapes=[pltpu.VMEM((tm, d), BF16),
                        pltpu.VMEM((tm, tn), F32)],
        compiler_params=_cparams(("arbitrary",)),
        name="lru_in",
    )(h, g.reshape(1, d), w, w, b2, b2)


def _qkv_kernel(x_ref, g_ref, w_ref, cc_ref, ss_ref, o_ref, xn_ref, acc_ref, *, n_j, n_rope, heads_per_tile, dh):
    s = pl.program_id(0)
    n_tiles = pl.num_programs(0) - 1
    j = lax.rem(jnp.minimum(s, n_tiles - 1), n_j)

    @pl.when(pl.program_id(0) == 0)
    def _():
        acc_ref[...] = jnp.zeros(acc_ref.shape, F32)

    @pl.when(j == 0)
    def _():
        xn_ref[...] = _rmsnorm_rows(x_ref[...], g_ref[...]).astype(BF16)

    acc = acc_ref[...]
    is_rope = lax.rem(jnp.maximum(s - 1, 0), n_j) < n_rope
    cc = cc_ref[...]
    ss = ss_ref[...]
    for hh in range(heads_per_tile):
        a = acc[:, hh * dh:(hh + 1) * dh]
        rot = a * cc + pltpu.roll(a, dh // 2, 1) * ss
        o_ref[:, hh * dh:(hh + 1) * dh] = jnp.where(is_rope, rot, a).astype(o_ref.dtype)

    acc_ref[...] = jnp.dot(xn_ref[...], w_ref[...], preferred_element_type=F32)


def _qkv_rope(h, g, w, cc, ss, seq):
    t, d = h.shape
    n = w.shape[1]
    dh = d // N_HEADS
    tm, tn = _tile(seq, 1024), _tile(d, 1024)
    s_tiles = seq // tm
    n_j = n // tn
    n_tiles = (t // tm) * n_j
    cur_i = lambda s: jnp.minimum(s, n_tiles - 1) // n_j
    prev = lambda s: jnp.maximum(s - 1, 0)
    return pl.pallas_call(
        functools.partial(_qkv_kernel, n_j=n_j, n_rope=(2 * d) // tn, heads_per_tile=tn // dh, dh=dh),
        grid=(n_tiles + 1,),
        in_specs=[pl.BlockSpec((tm, d), lambda s: (cur_i(s), 0)),
                  pl.BlockSpec((1, d), lambda s: (0, 0)),
                  pl.BlockSpec((d, tn), lambda s: (0, jnp.minimum(s, n_tiles - 1) % n_j)),
                  pl.BlockSpec((tm, dh), lambda s: ((prev(s) // n_j) % s_tiles, 0)),
                  pl.BlockSpec((tm, dh), lambda s: ((prev(s) // n_j) % s_tiles, 0))],
        out_specs=pl.BlockSpec((tm, tn), lambda s: (prev(s) // n_j, prev(s) % n_j)),
        out_shape=jax.ShapeDtypeStruct((t, n), BF16),
        scratch_shapes=[pltpu.VMEM((tm, d), BF16),
                        pltpu.VMEM((tm, tn), F32)],
        compiler_params=_cparams(("arbitrary",)),
        name="qkv_rope",
    )(h, g.reshape(1, d), w, cc, ss)


def _lru_scan_kernel(xb_ref, gate_ref, cw_ref, cb_ref, wr_ref, br_ref, wi_ref, bi_ref, lam_ref,
                     o_ref, ext_ref, a_ref, b_ref, h_ref, *, ts, n_grp, gw, conv_w):
    s = pl.program_id(1)
    w = xb_ref.shape[1]

    @pl.when(s == 0)
    def _():
        ext_ref[0:SUBLANES, :] = jnp.zeros((SUBLANES, w), F32)
        h_ref[...] = jnp.zeros_like(h_ref)

    ext_ref[SUBLANES:SUBLANES + ts, :] = xb_ref[...]
    xc = cb_ref[...] + cw_ref[conv_w - 1:conv_w, :] * xb_ref[...]
    for k in range(1, conv_w):
        xc = xc + cw_ref[conv_w - 1 - k:conv_w - k, :] * ext_ref[SUBLANES - k:SUBLANES - k + ts, :]
    ext_ref[0:SUBLANES, :] = ext_ref[ts:ts + SUBLANES, :]

    z = -lam_ref[...]
    softplus = jnp.maximum(z, 0.0) + jnp.log1p(jnp.exp(-jnp.abs(z)))
    c = -LRU_C * softplus
    for g in range(n_grp):
        sl = slice(g * gw, (g + 1) * gw)
        xg = xc[:, sl]
        xg16 = xg.astype(BF16)
        r = _sigmoid(jnp.dot(xg16, wr_ref[g], preferred_element_type=F32) + br_ref[:, sl])
        ig = _sigmoid(jnp.dot(xg16, wi_ref[g], preferred_element_type=F32) + bi_ref[:, sl])
        log_a = c[:, sl] * r
        a_ref[:, sl] = jnp.exp(log_a)
        th = jnp.tanh(log_a)
        b_ref[:, sl] = jnp.sqrt(-2.0 * th / (1.0 - th)) * ig * xg

    rows = lax.broadcasted_iota(jnp.int32, (SUBLANES, w), 0)

    def chunk(ci, h):
        r0 = pl.multiple_of(ci * SUBLANES, SUBLANES)
        a = a_ref[pl.ds(r0, SUBLANES), :]
        b = b_ref[pl.ds(r0, SUBLANES), :]
        for sh in (1, 2, 4):
            keep = rows >= sh
            a_sh = jnp.where(keep, pltpu.roll(a, sh, 0), 1.0)
            b_sh = jnp.where(keep, pltpu.roll(b, sh, 0), 0.0)
            b = a * b_sh + b
            a = a * a_sh
        hc = a * h + b
        b_ref[pl.ds(r0, SUBLANES), :] = hc * gate_ref[pl.ds(r0, SUBLANES), :]
        return jnp.broadcast_to(hc[SUBLANES - 1:SUBLANES, :], (SUBLANES, w))

    h_ref[...] = lax.fori_loop(0, ts // SUBLANES, chunk, h_ref[...])
    o_ref[...] = b_ref[...].astype(o_ref.dtype)


def _lru_scan(xb, gate, conv_w, conv_b, w_r, b_r, w_i, b_i, lam, bsz, seq):
    t, w = xb.shape
    n_grp, gw = w_r.shape[0], w_r.shape[1]
    cw = conv_w.shape[0]
    ts = _tile(seq, 256)
    s_tiles = seq // ts
    row = lambda b, s: (b * s_tiles + s, 0)
    vec = pl.BlockSpec((1, w), lambda b, s: (0, 0))
    return pl.pallas_call(
        functools.partial(_lru_scan_kernel, ts=ts, n_grp=n_grp, gw=gw, conv_w=cw),
        grid=(bsz, s_tiles),
        in_specs=[pl.BlockSpec((ts, w), row),
                  pl.BlockSpec((ts, w), row),
                  pl.BlockSpec((cw, w), lambda b, s: (0, 0)),
                  vec,
                  pl.BlockSpec((n_grp, gw, gw), lambda b, s: (0, 0, 0)),
                  vec,
                  pl.BlockSpec((n_grp, gw, gw), lambda b, s: (0, 0, 0)),
                  vec,
                  vec],
        out_specs=pl.BlockSpec((ts, w), row),
        out_shape=jax.ShapeDtypeStruct((t, w), BF16),
        scratch_shapes=[pltpu.VMEM((ts + SUBLANES, w), F32),
                        pltpu.VMEM((ts, w), F32),
                        pltpu.VMEM((ts, w), F32),
                        pltpu.VMEM((SUBLANES, w), F32)],
        compiler_params=_cparams(("parallel", "arbitrary")),
        name="lru_scan",
    )(xb, gate, conv_w, conv_b.reshape(1, w), w_r, b_r.reshape(1, w), w_i, b_i.reshape(1, w),
      lam.reshape(1, w))


ATTN_HEADS_PER_STEP = 1


def _attn_kernel(q_ref, k_ref, v_ref, o_ref, kaug_ref, vaug_t_ref, *, nb, blk, dh, topk):
    seq = nb * blk
    hp = kaug_ref.shape[0]
    nbp = vaug_t_ref.shape[1] - dh
    first = (pl.program_id(0) == 0) & (pl.program_id(1) == 0)

    @pl.when(first)
    def _():
        row = lax.broadcasted_iota(jnp.int32, (seq, LANES), 0)
        lane = lax.broadcasted_iota(jnp.int32, (seq, LANES), 1)
        ones_row = (lax.broadcasted_iota(jnp.int32, (nbp, seq), 0) == 0).astype(BF16)
        for hd in range(hp):
            kaug_ref[hd, :, dh:dh + LANES] = (lane == row // blk).astype(BF16)
            vaug_t_ref[hd, dh:dh + nbp, :] = ones_row

    kmeans = []
    for hd in range(hp):
        cols = slice(hd * dh, (hd + 1) * dh)
        kaug_ref[hd, :, 0:dh] = k_ref[:, cols]
        for j in range(nb):
            vaug_t_ref[hd, 0:dh, j * blk:(j + 1) * blk] = v_ref[j * blk:(j + 1) * blk, cols].astype(F32).T.astype(BF16)
        kmean = jnp.sum(k_ref[:, cols].astype(F32).reshape(nb, blk, dh), axis=1) * (1.0 / blk)
        if nbp > nb:
            kmean = jnp.concatenate([kmean, jnp.zeros((nbp - nb, dh), F32)], axis=0)
        kmeans.append(kmean.astype(BF16))

    c = (dh ** -0.5) * 1.4426950408889634
    blk_id = lax.broadcasted_iota(jnp.int32, (nbp, blk), 0)
    key_pos = lax.broadcasted_iota(jnp.int32, (blk, blk), 0)
    q_pos = lax.broadcasted_iota(jnp.int32, (blk, blk), 1)
    zeros_pad = jnp.zeros((LANES - nbp, blk), BF16)

    def scores(hd, i):
        q_t = q_ref[i * blk:(i + 1) * blk, hd * dh:(hd + 1) * dh].astype(F32).T.astype(BF16)
        gate_t = jnp.dot(kmeans[hd], q_t, preferred_element_type=F32)
        g = jnp.where(blk_id < i, gate_t, NEG_INF)
        sel = blk_id == i
        for _ in range(topk):
            m = jnp.max(g, axis=0, keepdims=True)
            first_max = jnp.min(jnp.where(g == m, blk_id, nbp), axis=0, keepdims=True)
            hit = blk_id == first_max
            sel = sel | (hit & (blk_id < i))
            g = jnp.where(hit, -jnp.inf, g)
        bias_t = jnp.where(sel, 0.0, NEG_INF).astype(BF16)
        q_aug_t = jnp.concatenate([q_t, bias_t, zeros_pad], axis=0)
        return jnp.dot(kaug_ref[hd, 0:(i + 1) * blk, :], q_aug_t, preferred_element_type=F32)

    def probs(i, s):
        s_own = jnp.where(key_pos <= q_pos, s[i * blk:, :], NEG_INF)
        m = jnp.max(s_own, axis=0, keepdims=True)
        if i > 0:
            s_past = s[:i * blk, :]
            m = jnp.maximum(m, jnp.max(s_past, axis=0, keepdims=True))
            p = jnp.concatenate([jnp.exp2((s_past - m) * c), jnp.exp2((s_own - m) * c)], axis=0)
        else:
            p = jnp.exp2((s_own - m) * c)
        return p.astype(BF16)

    def pv(hd, i, p):
        out_t = jnp.dot(vaug_t_ref[hd, :, 0:(i + 1) * blk], p, preferred_element_type=F32)
        o_t = out_t[0:dh, :] / out_t[dh:dh + 1, :]
        o_ref[i * blk:(i + 1) * blk, hd * dh:(hd + 1) * dh] = o_t.T.astype(o_ref.dtype)

    order = list(range(nb - 1, -1, -1))
    s_of, p_of = {}, {}
    for step in range(nb + 2):
        for hd in range(hp):
            if step < nb:
                s_of[hd, order[step]] = scores(hd, order[step])
        for hd in range(hp):
            if 1 <= step <= nb:
                t1 = order[step - 1]
                p_of[hd, t1] = probs(t1, s_of.pop((hd, t1)))
        for hd in range(hp):
            if step >= 2:
                t2 = order[step - 2]
                pv(hd, t2, p_of.pop((hd, t2)))


def _moba_attention(qkv, bsz, seq, d):
    t = qkv.shape[0]
    dh = d // N_HEADS
    blk = MOBA_BLOCK
    nb = seq // blk
    hp = ATTN_HEADS_PER_STEP
    assert seq % blk == 0 and nb <= BF16_SUBLANES and dh == LANES and N_HEADS % hp == 0
    n_hg = N_HEADS // hp
    return pl.pallas_call(
        functools.partial(_attn_kernel, nb=nb, blk=blk, dh=dh, topk=min(MOBA_TOPK, nb)),
        grid=(bsz, n_hg),
        in_specs=[pl.BlockSpec((seq, hp * dh), lambda b, h: (b, h)),
                  pl.BlockSpec((seq, hp * dh), lambda b, h: (b, n_hg + h)),
                  pl.BlockSpec((seq, hp * dh), lambda b, h: (b, 2 * n_hg + h))],
        out_specs=pl.BlockSpec((seq, hp * dh), lambda b, h: (b, h)),
        out_shape=jax.ShapeDtypeStruct((t, d), BF16),
        scratch_shapes=[pltpu.VMEM((hp, seq, dh + LANES), BF16),
                        pltpu.VMEM((hp, dh + BF16_SUBLANES, seq), BF16)],
        compiler_params=_cparams(("arbitrary", "arbitrary")),
        name="moba_attention",
    )(qkv, qkv, qkv)


def _pack_bf16_pairs(lo, hi):
    lo_bits = lax.bitcast_convert_type(lo.astype(BF16).astype(F32), jnp.uint32)
    hi_bits = lax.bitcast_convert_type(hi.astype(BF16).astype(F32), jnp.uint32)
    return (hi_bits & jnp.uint32(0xFFFF0000)) | (lo_bits >> 16)


def _unpack_bf16_pairs(u):
    lo = lax.bitcast_convert_type(u << 16, F32)
    hi = lax.bitcast_convert_type(u & jnp.uint32(0xFFFF0000), F32)
    return lo, hi


def _store_token_slabs(ref, packed, slab_rows, first_tile=0):
    m, width = packed.shape
    for j in range(width // LANES):
        ref[pl.ds(first_tile + j, m, stride=slab_rows), :] = packed[:, j * LANES:(j + 1) * LANES]


def _load_token_slabs(ref, m, slab_rows):
    return jnp.concatenate([ref[pl.ds(j, m, stride=slab_rows), :] for j in range(slab_rows)], axis=1)


def _route_rows(h, g, w_cat, b_cat, n_grp, epg):
    xn = _rmsnorm_rows(h, g)
    hd = xn.shape[1] // 2
    xn_packed = _pack_bf16_pairs(xn[:, :hd], xn[:, hd:])
    logits = jnp.dot(xn.astype(BF16), w_cat, preferred_element_type=F32) + b_cat
    lane = lax.broadcasted_iota(jnp.int32, logits.shape, 1)

    lg = jnp.where(lane < n_grp, logits, -jnp.inf)
    mg = jnp.max(lg, axis=1, keepdims=True)
    gidx = jnp.min(jnp.where(lg == mg, lane, LANES), axis=1, keepdims=True)
    pg_sel = 1.0 / jnp.sum(jnp.exp(lg - mg), axis=1, keepdims=True)

    lo = n_grp + gidx * epg
    in_grp = (lane >= lo) & (lane < lo + epg)
    le = jnp.where(in_grp, logits, -jnp.inf)
    e = jnp.exp(le - jnp.max(le, axis=1, keepdims=True))
    pe = jnp.where(in_grp, e / jnp.sum(e, axis=1, keepdims=True), -1.0)
    v1 = jnp.max(pe, axis=1, keepdims=True)
    i1 = jnp.min(jnp.where(pe == v1, lane, LANES), axis=1, keepdims=True)
    pe2 = jnp.where(lane == i1, -1.0, pe)
    v2 = jnp.max(pe2, axis=1, keepdims=True)
    i2 = jnp.min(jnp.where(pe2 == v2, lane, LANES), axis=1, keepdims=True)
    den = v1 + v2
    ids = jnp.where(lane == 0, i1 - n_grp, jnp.where(lane == 1, i2 - n_grp, 0)).astype(F32)
    et = ids.T[0:SUBLANES, :].astype(jnp.int32)
    rf = jnp.where(lane == 0, pg_sel * v1 / den, jnp.where(lane == 1, pg_sel * v2 / den, 0.0))
    return xn_packed, et, rf


def _mix_out_router_kernel(a_ref, w_ref, *rest, has_bias, n_grp, epg):
    if has_bias:
        b_ref, rest = rest[0], rest[1:]
    r_ref, g_ref, wc_ref, bc_ref, h_ref, xn_ref, et_ref, rf_ref, h_prev = rest

    @pl.when(pl.program_id(0) == 0)
    def _():
        h_prev[...] = jnp.zeros_like(h_prev)

    xn_packed, et, rf = _route_rows(h_prev[...], g_ref[...], wc_ref[...], bc_ref[...], n_grp, epg)
    _store_token_slabs(xn_ref, xn_packed, xn_packed.shape[1] // LANES)
    et_ref[...] = et
    rf_ref[...] = rf

    y = jnp.dot(a_ref[...], w_ref[...], preferred_element_type=F32)
    if has_bias:
        y = y + b_ref[...]
    h = r_ref[...] + y
    h_ref[...] = h
    h_prev[...] = h


def _mix_out_router(a, w, b, res, g, w_cat, b_cat, n_grp, epg):
    t, k = a.shape
    d = w.shape[1]
    tm = _tile(t, 256)
    n = t // tm
    cur = lambda i: (jnp.minimum(i, n - 1), 0)
    prev = lambda i: (jnp.maximum(i - 1, 0), 0)
    has_bias = b is not None
    in_specs = [pl.BlockSpec((tm, k), cur),
                pl.BlockSpec((k, d), lambda i: (0, 0))]
    args = [a, w]
    if has_bias:
        in_specs.append(pl.BlockSpec((1, d), lambda i: (0, 0)))
        args.append(b.reshape(1, d))
    in_specs += [pl.BlockSpec((tm, d), cur),
                 pl.BlockSpec((1, d), lambda i: (0, 0)),
                 pl.BlockSpec((d, LANES), lambda i: (0, 0)),
                 pl.BlockSpec((1, LANES), lambda i: (0, 0))]
    args += [res, g.reshape(1, d), w_cat, b_cat]
    return pl.pallas_call(
        functools.partial(_mix_out_router_kernel, has_bias=has_bias, n_grp=n_grp, epg=epg),
        grid=(n + 1,),
        in_specs=in_specs,
        out_specs=[pl.BlockSpec((tm, d), cur),
                   pl.BlockSpec((tm * (d // 2 // LANES), LANES), prev),
                   pl.BlockSpec((SUBLANES, tm), lambda i: (0, jnp.maximum(i - 1, 0))),
                   pl.BlockSpec((tm, LANES), prev)],
        out_shape=[jax.ShapeDtypeStruct((t, d), F32),
                   jax.ShapeDtypeStruct((t * (d // 2 // LANES), LANES), jnp.uint32),
                   jax.ShapeDtypeStruct((SUBLANES, t), jnp.int32),
                   jax.ShapeDtypeStruct((t, LANES), F32)],
        scratch_shapes=[pltpu.VMEM((tm, d), F32)],
        compiler_params=_cparams(("arbitrary",)),
        name="mix_out_router",
    )(*args)


DISPATCH_CHUNK = 512
DISPATCH_SMEM_CHUNK = 4096


def _dispatch_kernel(et_ref, tok_ref, dst_ref, blk_ref, rank_ref, dest_ref, dest_sm, rows_sm, rows_vm,
                     sem, *, n_exp, topk, rb, slab_rows):
    t = et_ref.shape[1]
    c = DISPATCH_CHUNK
    e_iota = lax.broadcasted_iota(jnp.int32, (n_exp, c), 0)
    earlier = (lax.broadcasted_iota(jnp.int32, (c, c), 0) < lax.broadcasted_iota(jnp.int32, (c, c), 1))
    earlier = jnp.where(earlier, 1.0, 0.0).astype(BF16)

    run = jnp.zeros((n_exp, 1), F32)
    for k in range(topk):
        def rank_chunk(j, run, k=k):
            off = pl.multiple_of(j * c, c)
            oh = e_iota == et_ref[k:k + 1, pl.ds(off, c)]
            ohf = jnp.where(oh, 1.0, 0.0)
            pre = jnp.dot(ohf.astype(BF16), earlier, preferred_element_type=F32)
            rank_ref[k:k + 1, pl.ds(off, c)] = jnp.sum(jnp.where(oh, pre + run, 0.0), axis=0, keepdims=True)
            return run + jnp.sum(ohf, axis=1, keepdims=True)
        run = lax.fori_loop(0, t // c, rank_chunk, run)

    cnt = run
    nblk = jnp.floor((cnt + (rb - 1)) * (1.0 / rb))
    before = (lax.broadcasted_iota(jnp.int32, (n_exp, n_exp), 1) < lax.broadcasted_iota(jnp.int32, (n_exp, n_exp), 0))
    before = jnp.where(before, 1.0, 0.0).astype(BF16)
    bstart = jnp.dot(before, jnp.broadcast_to(nblk, (n_exp, LANES)).astype(BF16),
                     preferred_element_type=F32)[:, 0:1]

    for k in range(topk):
        def dest_chunk(j, carry, k=k):
            off = pl.multiple_of(j * c, c)
            oh = e_iota == et_ref[k:k + 1, pl.ds(off, c)]
            base = jnp.sum(jnp.where(oh, bstart, 0.0), axis=0, keepdims=True) * rb
            dest_ref[k:k + 1, pl.ds(off, c)] = (base + rank_ref[k:k + 1, pl.ds(off, c)]).astype(jnp.int32)
            return carry
        lax.fori_loop(0, t // c, dest_chunk, 0)

    rows_vm[...] = jnp.full(rows_vm.shape, -1, jnp.int32)
    fill = pltpu.make_async_copy(rows_vm, rows_sm, sem)
    fill.start()
    fill.wait()
    sc = DISPATCH_SMEM_CHUNK
    for k in range(topk):
        def scatter_chunk(j, carry, k=k):
            off = pl.multiple_of(j * sc, sc)
            cp = pltpu.make_async_copy(dest_ref.at[pl.ds(k, 1), pl.ds(off, sc)], dest_sm, sem)
            cp.start()
            cp.wait()

            def one(a, carry2):
                d = dest_sm[0, a]
                rows_sm[0, d] = k * t + off + a
                return carry2
            return lax.fori_loop(0, sc, one, carry, unroll=8)
        lax.fori_loop(0, t // sc, scatter_chunk, 0)
    back = pltpu.make_async_copy(rows_sm, rows_vm, sem)
    back.start()
    back.wait()

    a = rows_vm[...]
    valid = a >= 0
    tok = a
    for k in range(1, topk):
        tok = jnp.where(a >= k * t, a - k * t, tok)
    flat = lax.broadcasted_iota(jnp.int32, a.shape, 1)
    scratch_dst = topk * t + ((flat // rb) % 2) * rb + flat % rb
    tok_ref[...] = jnp.where(valid, tok, 0) * slab_rows
    dst_ref[...] = jnp.where(valid, a, scratch_dst) * slab_rows

    nbp = blk_ref.shape[1]
    b_row = lax.broadcasted_iota(jnp.int32, (1, nbp), 1).astype(F32)
    bend = bstart + nblk
    blk_e = jnp.minimum(jnp.sum(jnp.where(bend <= b_row, 1.0, 0.0), axis=0, keepdims=True), n_exp - 1.0)
    mine = lax.broadcasted_iota(jnp.int32, (n_exp, nbp), 0).astype(F32) == blk_e
    cnt_b = jnp.sum(jnp.where(mine, cnt, 0.0), axis=0, keepdims=True)
    bstart_b = jnp.sum(jnp.where(mine, bstart, 0.0), axis=0, keepdims=True)
    blk_n = jnp.clip(cnt_b - (b_row - bstart_b) * rb, 0.0, float(rb))
    row = lax.broadcasted_iota(jnp.int32, blk_ref.shape, 0)
    blk_ref[...] = jnp.where(row == 0, blk_e, jnp.where(row == 1, blk_n, 0.0)).astype(jnp.int32)


def _dispatch(et, n_exp, rb, slab_rows):
    t = et.shape[1]
    topk = EXPERT_TOPK
    n_asg = topk * t
    assert n_asg % rb == 0 and t % DISPATCH_SMEM_CHUNK == 0 and (rb & (rb - 1)) == 0 and rb % LANES == 0
    n_rb = n_asg // rb + n_exp
    n_rows = n_rb * rb
    nbp = -(-n_rb // LANES) * LANES
    tok, dst, blk = pl.pallas_call(
        functools.partial(_dispatch_kernel, n_exp=n_exp, topk=topk, rb=rb, slab_rows=slab_rows),
        in_specs=[pl.BlockSpec(memory_space=pltpu.VMEM)],
        out_specs=[pl.BlockSpec(memory_space=pltpu.VMEM)] * 3,
        out_shape=[jax.ShapeDtypeStruct((1, n_rows), jnp.int32),
                   jax.ShapeDtypeStruct((1, n_rows), jnp.int32),
                   jax.ShapeDtypeStruct((SUBLANES, nbp), jnp.int32)],
        scratch_shapes=[pltpu.VMEM((SUBLANES, t), F32),
                        pltpu.VMEM((SUBLANES, t), jnp.int32),
                        pltpu.SMEM((1, DISPATCH_SMEM_CHUNK), jnp.int32),
                        pltpu.SMEM((1, n_rows), jnp.int32),
                        pltpu.VMEM((1, n_rows), jnp.int32),
                        pltpu.SemaphoreType.DMA(())],
        compiler_params=pltpu.CompilerParams(vmem_limit_bytes=VMEM_LIMIT_BYTES),
        name="moe_dispatch",
    )(et)
    return tok.reshape(n_rb, rb), dst.reshape(n_rb, rb), blk[0, :n_rb], blk[1, :n_rb]


WEIGHT_CHUNK_ROWS = 512
WEIGHT_RING = 4


def _expert_kernel(blk_e, blk_n, tok_all, dst_all, xn_hbm, wg_hbm, wu_hbm, wd_hbm,
                   y_hbm, xbuf, obuf, wbuf_g, wbuf_u, wbuf_d, stg, run_end, next_e, run_start, wslot_of,
                   succ_e, st, gsem, ssem, wsem, *, n_rb, rb, layer):
    i = pl.program_id(0)
    slot = lax.rem(i, 2)
    n = blk_n[i]
    d, f = wbuf_g.shape[1:]
    cr = stg.shape[1]
    n_gu = d // cr
    n_dc = d // f
    n_chunks = 2 * n_gu + (f // cr) * n_dc
    ring = stg.shape[0]
    G_DONE, P_EXP, P_CHUNK = 0, 1, 2

    chunks = ([(wg_hbm, wbuf_g, r * cr, 0) for r in range(n_gu)]
              + [(wu_hbm, wbuf_u, r * cr, 0) for r in range(n_gu)]
              + [(wd_hbm, wbuf_d, r * cr, h * f) for r in range(f // cr) for h in range(n_dc)])
    assert len(chunks) == n_chunks

    def chunk_copy(e, c, k):
        for cid, (w_hbm, _, r0, c0) in enumerate(chunks):
            @pl.when(c == cid)
            def _(w_hbm=w_hbm, r0=r0, c0=c0):
                pltpu.make_async_copy(w_hbm.at[layer, e, pl.ds(r0, cr), pl.ds(c0, f)], stg.at[k],
                                      wsem.at[k]).start()

    def start_next_chunk(k):
        e = st[P_EXP]
        c = st[P_CHUNK]

        @pl.when(e >= 0)
        def _():
            chunk_copy(e, c, k)
            last = c + 1 == n_chunks
            st[P_CHUNK] = jnp.where(last, 0, c + 1)
            st[P_EXP] = jnp.where(last, succ_e[e], e)

    def convert_chunk(c, ws):
        g = st[G_DONE]
        k = lax.rem(g, ring)
        pltpu.make_async_copy(wg_hbm.at[layer, 0, pl.ds(0, cr), :], stg.at[k], wsem.at[k]).wait()
        start_next_chunk(lax.rem(g + ring - 1, ring))
        for cid, (_, wbuf, r0, c0) in enumerate(chunks):
            @pl.when(c == cid)
            def _(wbuf=wbuf, r0=r0, c0=c0):
                wbuf[ws, r0:r0 + cr, c0:c0 + f] = stg[k].astype(BF16)
        st[G_DONE] = g + 1

    @pl.when(i == 0)
    def _():
        def back(ii, carry):
            end_nb, next_nb = carry
            b = n_rb - 1 - ii
            nb = jnp.minimum(b + 1, n_rb - 1)
            nb_active = (b + 1 < n_rb) & (blk_n[nb] > 0)
            same = nb_active & (blk_e[nb] == blk_e[b])
            end_b = jnp.where(same, end_nb, b + 1)
            next_b = jnp.where(same, next_nb, jnp.where(nb_active, blk_e[nb], -1))
            run_end[b] = end_b
            next_e[b] = next_b

            @pl.when((blk_n[b] > 0) & jnp.logical_not(same))
            def _():
                succ_e[blk_e[b]] = next_b
            return end_b, next_b
        lax.fori_loop(0, n_rb, back, (jnp.int32(n_rb), jnp.int32(-1)))

        def fwd(b, carry):
            start_pb, wslot_pb = carry
            pb = jnp.maximum(b - 1, 0)
            same = (b > 0) & (blk_e[pb] == blk_e[b])
            start_b = jnp.where(same, start_pb, b)
            wslot_b = jnp.where(b == 0, 0, jnp.where(same, wslot_pb, 1 - wslot_pb))
            run_start[b] = start_b
            wslot_of[b] = wslot_b
            return start_b, wslot_b
        lax.fori_loop(0, n_rb, fwd, (jnp.int32(0), jnp.int32(0)))

        st[G_DONE] = 0
        st[P_CHUNK] = 0
        st[P_EXP] = jnp.where(n > 0, blk_e[0], -1)

        @pl.when(n > 0)
        def _():
            for k in range(ring - 1):
                start_next_chunk(k)

            def first(c, carry):
                convert_chunk(c, 0)
                return carry
            lax.fori_loop(0, n_chunks, first, 0)

    ws = wslot_of[i]

    @pl.when((n > 0) & (next_e[i] >= 0))
    def _():
        run_len = run_end[i] - run_start[i]
        per = lax.div(n_chunks + run_len - 1, run_len)
        lo = (i - run_start[i]) * per
        hi = jnp.minimum(lo + per, n_chunks)

        def nxt(c, carry):
            convert_chunk(c, 1 - ws)
            return carry
        lax.fori_loop(lo, hi, nxt, 0)

    ns = xbuf.shape[1] // rb

    def gather_row(blk, sl, r):
        return pltpu.make_async_copy(xn_hbm.at[pl.ds(tok_all[blk * rb + r], ns)], xbuf.at[sl, pl.ds(r * ns, ns)],
                                     gsem.at[sl])

    def start_gather(blk, sl):
        def body(r, c):
            gather_row(blk, sl, r).start()
            return c
        lax.fori_loop(0, rb, body, 0, unroll=8)

    def scatter_row(sl, r, dst):
        return pltpu.make_async_copy(obuf.at[sl, pl.ds(r * ns, ns)], y_hbm.at[pl.ds(dst, ns)], ssem.at[sl])

    def wait_scatter(sl):
        pltpu.make_async_copy(obuf.at[sl], y_hbm.at[pl.ds(0, rb * ns)], ssem.at[sl]).wait()

    @pl.when(i == 0)
    def _():
        obuf[...] = jnp.zeros_like(obuf)
        n_real = y_hbm.shape[0] - 2 * rb * ns
        for sl in range(2):
            init = pltpu.make_async_copy(obuf.at[sl], y_hbm.at[pl.ds(n_real + sl * rb * ns, rb * ns)],
                                         ssem.at[sl])
            init.start()
            init.wait()

    n_prev = blk_n[jnp.maximum(i - 1, 0)]
    n_next = blk_n[jnp.minimum(i + 1, n_rb - 1)]
    do_gather = (i + 1 < n_rb) & (n_next > 0)
    do_scatter = (i >= 1) & (n_prev > 0)

    @pl.when((i == 0) & (n > 0))
    def _():
        start_gather(0, 0)

    @pl.when(n > 0)
    def _():
        pltpu.make_async_copy(xn_hbm.at[pl.ds(0, rb * ns)], xbuf.at[slot], gsem.at[slot]).wait()

    hd = ns * LANES
    fc, dc = min(f, MXU_COLS), min(hd, MXU_COLS)

    def free_obuf():
        @pl.when((i >= 2) & (blk_n[jnp.maximum(i - 2, 0)] > 0))
        def _():
            wait_scatter(slot)

    def compute(interleave_dma):
        def issue(piece, n_piece, start_row):
            if interleave_dma:
                per = -(-rb // n_piece)
                for r in range(piece * per, min(rb, (piece + 1) * per)):
                    start_row(r)

        x_lo, x_hi = _unpack_bf16_pairs(_load_token_slabs(xbuf.at[slot], rb, ns))
        x = jnp.concatenate([x_lo.astype(BF16), x_hi.astype(BF16)], axis=1)
        hid = []
        for c in range(f // fc):
            cols = slice(c * fc, (c + 1) * fc)
            start_gather_row = lambda r: gather_row(i + 1, 1 - slot, r).start(priority=r % 2)
            g = jnp.dot(x, wbuf_g[ws, :, cols], preferred_element_type=F32)
            issue(2 * c, 2 * (f // fc), start_gather_row)
            u = jnp.dot(x, wbuf_u[ws, :, cols], preferred_element_type=F32)
            issue(2 * c + 1, 2 * (f // fc), start_gather_row)
            hid.append(((g * jax.nn.sigmoid(g)) * u).astype(BF16))
        hid = jnp.concatenate(hid, axis=1)
        free_obuf()
        for c in range(hd // dc):
            cols = slice(c * dc, (c + 1) * dc)
            cols_hi = slice(hd + c * dc, hd + (c + 1) * dc)
            start_scatter_row = lambda r: scatter_row(1 - slot, r, dst_all[(i - 1) * rb + r]).start(priority=r % 2)
            y_lo = jnp.dot(hid, wbuf_d[ws, :, cols], preferred_element_type=F32)
            issue(2 * c, 2 * (hd // dc), start_scatter_row)
            y_hi = jnp.dot(hid, wbuf_d[ws, :, cols_hi], preferred_element_type=F32)
            issue(2 * c + 1, 2 * (hd // dc), start_scatter_row)
            _store_token_slabs(obuf.at[slot], _pack_bf16_pairs(y_lo, y_hi), ns, first_tile=c * dc // LANES)

    fast = (n > 0) & do_gather & do_scatter

    @pl.when(fast)
    def _():
        compute(True)

    @pl.when(jnp.logical_not(fast))
    def _():
        @pl.when(do_gather)
        def _():
            start_gather(i + 1, 1 - slot)

        @pl.when(do_scatter)
        def _():
            def body(r, c):
                scatter_row(1 - slot, r, dst_all[(i - 1) * rb + r]).start()
                return c
            lax.fori_loop(0, rb, body, 0, unroll=8)

        @pl.when(n > 0)
        def _():
            compute(False)

        @pl.when(n == 0)
        def _():
            free_obuf()

    @pl.when((i == n_rb - 1) & do_scatter)
    def _():
        wait_scatter(1 - slot)


def _experts(xn, row_tok, row_dst, blk_e, blk_n, w_gate, w_up, w_down, layer, n_slots):
    n_exp, d, f = w_gate.shape[1:]
    ns = d // 2 // LANES
    assert xn.shape[1] == LANES and d % (2 * LANES) == 0 and w_down.shape[1:] == (n_exp, f, d)
    n_rb, rb = row_tok.shape
    assert (n_slots - 2 * rb) % rb == 0
    cr = min(WEIGHT_CHUNK_ROWS, f)
    assert d % cr == 0 and f % cr == 0 and d % f == 0
    hbm = pl.BlockSpec(memory_space=pl.ANY)
    grid_spec = pltpu.PrefetchScalarGridSpec(
        num_scalar_prefetch=4,
        grid=(n_rb,),
        in_specs=[hbm, hbm, hbm, hbm],
        out_specs=hbm,
        scratch_shapes=[pltpu.VMEM((2, rb * ns, LANES), jnp.uint32),
                        pltpu.VMEM((2, rb * ns, LANES), jnp.uint32),
                        pltpu.VMEM((2, d, f), BF16),
                        pltpu.VMEM((2, d, f), BF16),
                        pltpu.VMEM((2, f, d), BF16),
                        pltpu.VMEM((WEIGHT_RING, cr, f), F32),
                        pltpu.SMEM((n_rb,), jnp.int32),
                        pltpu.SMEM((n_rb,), jnp.int32),
                        pltpu.SMEM((n_rb,), jnp.int32),
                        pltpu.SMEM((n_rb,), jnp.int32),
                        pltpu.SMEM((n_exp,), jnp.int32),
                        pltpu.SMEM((4,), jnp.int32),
                        pltpu.SemaphoreType.DMA((2,)),
                        pltpu.SemaphoreType.DMA((2,)),
                        pltpu.SemaphoreType.DMA((WEIGHT_RING,))],
    )
    return pl.pallas_call(
        functools.partial(_expert_kernel, n_rb=n_rb, rb=rb, layer=layer),
        grid_spec=grid_spec,
        out_shape=jax.ShapeDtypeStruct((n_slots * ns, LANES), jnp.uint32),
        compiler_params=_cparams(("arbitrary",)),
        name="moe_experts",
    )(blk_e, blk_n, row_tok.reshape(-1), row_dst.reshape(-1), xn, w_gate, w_up, w_down)


def _combine_kernel(h_ref, y0_ref, y1_ref, rf_ref, *rest, final):
    if final:
        g_ref, o_ref = rest
    else:
        (o_ref,) = rest
    w = rf_ref[...]
    m = h_ref.shape[0]
    ns = y0_ref.shape[0] // m
    y0_lo, y0_hi = _unpack_bf16_pairs(_load_token_slabs(y0_ref, m, ns))
    y1_lo, y1_hi = _unpack_bf16_pairs(_load_token_slabs(y1_ref, m, ns))
    y = jnp.concatenate([w[:, 0:1] * y0_lo + w[:, 1:2] * y1_lo, w[:, 0:1] * y0_hi + w[:, 1:2] * y1_hi], axis=1)
    out = h_ref[...] + y
    if final:
        out = _rmsnorm_rows(out, g_ref[...])
    o_ref[...] = out


def _combine(h, y, rf, final_g):
    t, d = h.shape
    tm = _tile(t, 512)
    final = final_g is not None
    in_specs = [pl.BlockSpec((tm, d), lambda i: (i, 0)),
                pl.BlockSpec((tm * (d // 2 // LANES), LANES), lambda i: (i, 0)),
                pl.BlockSpec((tm * (d // 2 // LANES), LANES), lambda i: (t // tm + i, 0)),
                pl.BlockSpec((tm, LANES), lambda i: (i, 0))]
    args = [h, y, y, rf]
    if final:
        in_specs.append(pl.BlockSpec((1, d), lambda i: (0, 0)))
        args.append(final_g.reshape(1, d))
    return pl.pallas_call(
        functools.partial(_combine_kernel, final=final),
        grid=(t // tm,),
        in_specs=in_specs,
        out_specs=pl.BlockSpec((tm, d), lambda i: (i, 0)),
        out_shape=jax.ShapeDtypeStruct((t, d), F32),
        compiler_params=_cparams(("parallel",)),
        name="moe_combine",
    )(*args)


def _mix_out_moe(act, w_out, b_out, h, norm_g, w_grp, b_grp, w_exp, b_exp, w_gate, w_up, w_down, layer, final_g):
    t, d = h.shape
    n_grp = w_grp.shape[1]
    n_exp = w_exp.shape[1]
    epg = n_exp // n_grp
    assert n_grp + n_exp <= LANES
    pad = LANES - n_grp - n_exp
    w_cat = jnp.concatenate([w_grp, w_exp, jnp.zeros((d, pad), F32)], axis=1).astype(BF16)
    b_cat = jnp.concatenate([b_grp, b_exp, jnp.zeros((pad,), F32)]).reshape(1, LANES)
    h, xn, et, rf = _mix_out_router(act, w_out, b_out, h, norm_g, w_cat, b_cat, n_grp, epg)
    row_tok, row_dst, blk_e, blk_n = _dispatch(et, n_exp, ROW_BLOCK, d // 2 // LANES)
    y = _experts(xn, row_tok, row_dst, blk_e, blk_n, w_gate, w_up, w_down, layer,
                 EXPERT_TOPK * t + 2 * ROW_BLOCK)
    return _combine(h, y, rf, final_g)


def _rope_tables(seq, dh):
    inv = 1.0 / (ROPE_THETA ** (jnp.arange(0, dh, 2, dtype=F32) / dh))
    ang = jnp.arange(seq, dtype=F32)[:, None] * inv[None, :]
    cos, sin = jnp.cos(ang), jnp.sin(ang)
    return jnp.concatenate([cos, cos], axis=1), jnp.concatenate([-sin, sin], axis=1)


def kernel(x, lru_norm, lru_w_in, lru_b_in, lru_conv_w, lru_conv_b, lru_w_r, lru_b_r, lru_w_i, lru_b_i, lru_lambda, lru_w_out, lru_b_out, att_norm, att_w_qkv, att_w_o, ffn_norm, moe_w_grp, moe_b_grp, moe_w_exp, moe_b_exp, moe_w_gate, moe_w_up, moe_w_down, final_norm):
    bsz, seq, d = x.shape
    depth = ffn_norm.shape[0]
    n_mixers = 2
    cc, ss = _rope_tables(seq, d // N_HEADS)
    h = x.reshape(bsz * seq, d)
    for layer in range(depth):
        j = layer // n_mixers
        if layer % n_mixers == 0:
            xb, gate = _lru_in(h, lru_norm[j], lru_w_in[j].astype(BF16), lru_b_in[j])
            act = _lru_scan(xb, gate, lru_conv_w[j], lru_conv_b[j], lru_w_r[j].astype(BF16), lru_b_r[j],
                            lru_w_i[j].astype(BF16), lru_b_i[j], lru_lambda[j], bsz, seq)
            w_out, b_out = lru_w_out[j].astype(BF16), lru_b_out[j]
        else:
            qkv = _qkv_rope(h, att_norm[j], att_w_qkv[j].astype(BF16), cc, ss, seq)
            act = _moba_attention(qkv, bsz, seq, d)
            w_out, b_out = att_w_o[j].astype(BF16), None
        final_g = final_norm if layer == depth - 1 else None
        h = _mix_out_moe(act, w_out, b_out, h, ffn_norm[layer], moe_w_grp[layer], moe_b_grp[layer],
                         moe_w_exp[layer], moe_b_exp[layer], moe_w_gate, moe_w_up, moe_w_down, layer, final_g)
    return h.reshape(bsz, seq, d)
```

```python
import functools

import jax
import jax.numpy as jnp
from jax import lax
from jax.experimental import pallas as pl
from jax.experimental.pallas import tpu as pltpu

F32 = jnp.float32
BF16 = jnp.bfloat16

NORM_EPS = 1e-6
NEG_INF = -1e30
LRU_C = 8.0
N_HEADS = 16
MOBA_BLOCK = 256
MOBA_TOPK = 3
ROPE_THETA = 10000.0
EXPERT_TOPK = 2
ROW_BLOCK = 256
LANES = 128
SUBLANES = 8
BF16_SUBLANES = 16
MXU_COLS = 256
VMEM_LIMIT_BYTES = 56 * 1024 * 1024


def _tile(n, pref):
    t = min(n, pref)
    while n % t:
        t //= 2
    return t


def _cparams(sem):
    return pltpu.CompilerParams(dimension_semantics=sem, vmem_limit_bytes=VMEM_LIMIT_BYTES)


def _sigmoid(z):
    return 0.5 * jnp.tanh(0.5 * z) + 0.5


def _rmsnorm_rows(x, g):
    ms = jnp.mean(x * x, axis=-1, keepdims=True)
    return x * lax.rsqrt(ms + NORM_EPS) * g


def _lru_in_kernel(x_ref, g_ref, wx_ref, wg_ref, bx_ref, bg_ref, ox_ref, og_ref, xn_ref):
    @pl.when(pl.program_id(1) == 0)
    def _():
        xn_ref[...] = _rmsnorm_rows(x_ref[...], g_ref[...]).astype(BF16)

    xn = xn_ref[...]
    ox_ref[...] = jnp.dot(xn, wx_ref[...], preferred_element_type=F32) + bx_ref[...]
    og_ref[...] = jax.nn.gelu(jnp.dot(xn, wg_ref[...], preferred_element_type=F32) + bg_ref[...],
                              approximate=True)


def _lru_in(h, g, w, b):
    t, d = h.shape
    wdt = w.shape[1] // 2
    tm, tn = _tile(t, 1024), _tile(wdt, 512)
    nj = wdt // tn
    b2 = b.reshape(1, 2 * wdt)
    return pl.pallas_call(
        _lru_in_kernel,
        grid=(t // tm, nj),
        in_specs=[pl.BlockSpec((tm, d), lambda i, j: (i, 0)),
                  pl.BlockSpec((1, d), lambda i, j: (0, 0)),
                  pl.BlockSpec((d, tn), lambda i, j: (0, j)),
                  pl.BlockSpec((d, tn), lambda i, j: (0, nj + j)),
                  pl.BlockSpec((1, tn), lambda i, j: (0, j)),
                  pl.BlockSpec((1, tn), lambda i, j: (0, nj + j))],
        out_specs=[pl.BlockSpec((tm, tn), lambda i, j: (i, j)),
                   pl.BlockSpec((tm, tn), lambda i, j: (i, j))],
        out_shape=[jax.ShapeDtypeStruct((t, wdt), F32),
                   jax.ShapeDtypeStruct((t, wdt), F32)],
        scratch_shapes=[pltpu.VMEM((tm, d), BF16)],
        compiler_params=_cparams(("parallel", "arbitrary")),
        name="lru_in",
    )(h, g.reshape(1, d), w, w, b2, b2)


def _qkv_kernel(x_ref, g_ref, w_ref, cc_ref, ss_ref, o_ref, xn_ref, *, n_rope, heads_per_tile, dh):
    j = pl.program_id(1)

    @pl.when(j == 0)
    def _():
        xn_ref[...] = _rmsnorm_rows(x_ref[...], g_ref[...]).astype(BF16)

    acc = jnp.dot(xn_ref[...], w_ref[...], preferred_element_type=F32)

    is_rope = j < n_rope
    cc = cc_ref[...]
    ss = ss_ref[...]
    for hh in range(heads_per_tile):
        a = acc[:, hh * dh:(hh + 1) * dh]
        rot = a * cc + pltpu.roll(a, dh // 2, 1) * ss
        o_ref[:, hh * dh:(hh + 1) * dh] = jnp.where(is_rope, rot, a).astype(o_ref.dtype)


def _qkv_rope(h, g, w, cc, ss, seq):
    t, d = h.shape
    n = w.shape[1]
    dh = d // N_HEADS
    tm, tn = _tile(seq, 1024), _tile(d, 1024)
    s_tiles = seq // tm
    return pl.pallas_call(
        functools.partial(_qkv_kernel, n_rope=(2 * d) // tn, heads_per_tile=tn // dh, dh=dh),
        grid=(t // tm, n // tn),
        in_specs=[pl.BlockSpec((tm, d), lambda i, j: (i, 0)),
                  pl.BlockSpec((1, d), lambda i, j: (0, 0)),
                  pl.BlockSpec((d, tn), lambda i, j: (0, j)),
                  pl.BlockSpec((tm, dh), lambda i, j: (i % s_tiles, 0)),
                  pl.BlockSpec((tm, dh), lambda i, j: (i % s_tiles, 0))],
        out_specs=pl.BlockSpec((tm, tn), lambda i, j: (i, j)),
        out_shape=jax.ShapeDtypeStruct((t, n), BF16),
        scratch_shapes=[pltpu.VMEM((tm, d), BF16)],
        compiler_params=_cparams(("parallel", "arbitrary")),
        name="qkv_rope",
    )(h, g.reshape(1, d), w, cc, ss)


def _lru_scan_kernel(xb_ref, gate_ref, cw_ref, cb_ref, wr_ref, br_ref, wi_ref, bi_ref, lam_ref,
                     o_ref, ext_ref, a_ref, b_ref, h_ref, *, ts, n_grp, gw, conv_w):
    s = pl.program_id(1)
    w = xb_ref.shape[1]

    @pl.when(s == 0)
    def _():
        ext_ref[0:SUBLANES, :] = jnp.zeros((SUBLANES, w), F32)
        h_ref[...] = jnp.zeros_like(h_ref)

    ext_ref[SUBLANES:SUBLANES + ts, :] = xb_ref[...]
    xc = cb_ref[...] + cw_ref[conv_w - 1:conv_w, :] * xb_ref[...]
    for k in range(1, conv_w):
        xc = xc + cw_ref[conv_w - 1 - k:conv_w - k, :] * ext_ref[SUBLANES - k:SUBLANES - k + ts, :]
    ext_ref[0:SUBLANES, :] = ext_ref[ts:ts + SUBLANES, :]

    z = -lam_ref[...]
    softplus = jnp.maximum(z, 0.0) + jnp.log1p(jnp.exp(-jnp.abs(z)))
    c = -LRU_C * softplus
    for g in range(n_grp):
        sl = slice(g * gw, (g + 1) * gw)
        xg = xc[:, sl]
        xg16 = xg.astype(BF16)
        r = _sigmoid(jnp.dot(xg16, wr_ref[g], preferred_element_type=F32) + br_ref[:, sl])
        ig = _sigmoid(jnp.dot(xg16, wi_ref[g], preferred_element_type=F32) + bi_ref[:, sl])
        log_a = c[:, sl] * r
        a_ref[:, sl] = jnp.exp(log_a)
        th = jnp.tanh(log_a)
        b_ref[:, sl] = jnp.sqrt(-2.0 * th / (1.0 - th)) * ig * xg

    rows = lax.broadcasted_iota(jnp.int32, (SUBLANES, w), 0)

    def chunk(ci, h):
        r0 = pl.multiple_of(ci * SUBLANES, SUBLANES)
        a = a_ref[pl.ds(r0, SUBLANES), :]
        b = b_ref[pl.ds(r0, SUBLANES), :]
        for sh in (1, 2, 4):
            keep = rows >= sh
            a_sh = jnp.where(keep, pltpu.roll(a, sh, 0), 1.0)
            b_sh = jnp.where(keep, pltpu.roll(b, sh, 0), 0.0)
            b = a * b_sh + b
            a = a * a_sh
        hc = a * h + b
        b_ref[pl.ds(r0, SUBLANES), :] = hc * gate_ref[pl.ds(r0, SUBLANES), :]
        return jnp.broadcast_to(hc[SUBLANES - 1:SUBLANES, :], (SUBLANES, w))

    h_ref[...] = lax.fori_loop(0, ts // SUBLANES, chunk, h_ref[...])
    o_ref[...] = b_ref[...].astype(o_ref.dtype)


def _lru_scan(xb, gate, conv_w, conv_b, w_r, b_r, w_i, b_i, lam, bsz, seq):
    t, w = xb.shape
    n_grp, gw = w_r.shape[0], w_r.shape[1]
    cw = conv_w.shape[0]
    ts = _tile(seq, 256)
    s_tiles = seq // ts
    row = lambda b, s: (b * s_tiles + s, 0)
    vec = pl.BlockSpec((1, w), lambda b, s: (0, 0))
    return pl.pallas_call(
        functools.partial(_lru_scan_kernel, ts=ts, n_grp=n_grp, gw=gw, conv_w=cw),
        grid=(bsz, s_tiles),
        in_specs=[pl.BlockSpec((ts, w), row),
                  pl.BlockSpec((ts, w), row),
                  pl.BlockSpec((cw, w), lambda b, s: (0, 0)),
                  vec,
                  pl.BlockSpec((n_grp, gw, gw), lambda b, s: (0, 0, 0)),
                  vec,
                  pl.BlockSpec((n_grp, gw, gw), lambda b, s: (0, 0, 0)),
                  vec,
                  vec],
        out_specs=pl.BlockSpec((ts, w), row),
        out_shape=jax.ShapeDtypeStruct((t, w), BF16),
        scratch_shapes=[pltpu.VMEM((ts + SUBLANES, w), F32),
                        pltpu.VMEM((ts, w), F32),
                        pltpu.VMEM((ts, w), F32),
                        pltpu.VMEM((SUBLANES, w), F32)],
        compiler_params=_cparams(("parallel", "arbitrary")),
        name="lru_scan",
    )(xb, gate, conv_w, conv_b.reshape(1, w), w_r, b_r.reshape(1, w), w_i, b_i.reshape(1, w),
      lam.reshape(1, w))


ATTN_HEADS_PER_STEP = 1


def _attn_kernel(q_ref, k_ref, v_ref, o_ref, kaug_ref, vaug_t_ref, *, nb, blk, dh, topk):
    seq = nb * blk
    hp = kaug_ref.shape[0]
    nbp = vaug_t_ref.shape[1] - dh
    first = (pl.program_id(0) == 0) & (pl.program_id(1) == 0)

    @pl.when(first)
    def _():
        row = lax.broadcasted_iota(jnp.int32, (seq, LANES), 0)
        lane = lax.broadcasted_iota(jnp.int32, (seq, LANES), 1)
        ones_row = (lax.broadcasted_iota(jnp.int32, (nbp, seq), 0) == 0).astype(BF16)
        for hd in range(hp):
            kaug_ref[hd, :, dh:dh + LANES] = (lane == row // blk).astype(BF16)
            vaug_t_ref[hd, dh:dh + nbp, :] = ones_row

    kmeans = []
    for hd in range(hp):
        cols = slice(hd * dh, (hd + 1) * dh)
        kaug_ref[hd, :, 0:dh] = k_ref[:, cols]
        for j in range(nb):
            vaug_t_ref[hd, 0:dh, j * blk:(j + 1) * blk] = v_ref[j * blk:(j + 1) * blk, cols].astype(F32).T.astype(BF16)
        kmean = jnp.sum(k_ref[:, cols].astype(F32).reshape(nb, blk, dh), axis=1) * (1.0 / blk)
        if nbp > nb:
            kmean = jnp.concatenate([kmean, jnp.zeros((nbp - nb, dh), F32)], axis=0)
        kmeans.append(kmean.astype(BF16))

    c = (dh ** -0.5) * 1.4426950408889634
    blk_id = lax.broadcasted_iota(jnp.int32, (nbp, blk), 0)
    key_pos = lax.broadcasted_iota(jnp.int32, (blk, blk), 0)
    q_pos = lax.broadcasted_iota(jnp.int32, (blk, blk), 1)
    zeros_pad = jnp.zeros((LANES - nbp, blk), BF16)

    def scores(hd, i):
        q_t = q_ref[i * blk:(i + 1) * blk, hd * dh:(hd + 1) * dh].astype(F32).T.astype(BF16)
        gate_t = jnp.dot(kmeans[hd], q_t, preferred_element_type=F32)
        g = jnp.where(blk_id < i, gate_t, NEG_INF)
        sel = blk_id == i
        for _ in range(topk):
            m = jnp.max(g, axis=0, keepdims=True)
            first_max = jnp.min(jnp.where(g == m, blk_id, nbp), axis=0, keepdims=True)
            hit = blk_id == first_max
            sel = sel | (hit & (blk_id < i))
            g = jnp.where(hit, -jnp.inf, g)
        bias_t = jnp.where(sel, 0.0, NEG_INF).astype(BF16)
        q_aug_t = jnp.concatenate([q_t, bias_t, zeros_pad], axis=0)
        return jnp.dot(kaug_ref[hd, 0:(i + 1) * blk, :], q_aug_t, preferred_element_type=F32)

    def probs(i, s):
        s_own = jnp.where(key_pos <= q_pos, s[i * blk:, :], NEG_INF)
        m = jnp.max(s_own, axis=0, keepdims=True)
        if i > 0:
            s_past = s[:i * blk, :]
            m = jnp.maximum(m, jnp.max(s_past, axis=0, keepdims=True))
            p = jnp.concatenate([jnp.exp2((s_past - m) * c), jnp.exp2((s_own - m) * c)], axis=0)
        else:
            p = jnp.exp2((s_own - m) * c)
        return p.astype(BF16)

    def pv(hd, i, p):
        out_t = jnp.dot(vaug_t_ref[hd, :, 0:(i + 1) * blk], p, preferred_element_type=F32)
        o_t = out_t[0:dh, :] / out_t[dh:dh + 1, :]
        o_ref[i * blk:(i + 1) * blk, hd * dh:(hd + 1) * dh] = o_t.T.astype(o_ref.dtype)

    order = list(range(nb - 1, -1, -1))
    s_of, p_of = {}, {}
    for step in range(nb + 2):
        for hd in range(hp):
            if step < nb:
                s_of[hd, order[step]] = scores(hd, order[step])
        for hd in range(hp):
            if 1 <= step <= nb:
                t1 = order[step - 1]
                p_of[hd, t1] = probs(t1, s_of.pop((hd, t1)))
        for hd in range(hp):
            if step >= 2:
                t2 = order[step - 2]
                pv(hd, t2, p_of.pop((hd, t2)))


def _moba_attention(qkv, bsz, seq, d):
    t = qkv.shape[0]
    dh = d // N_HEADS
    blk = MOBA_BLOCK
    nb = seq // blk
    hp = ATTN_HEADS_PER_STEP
    assert seq % blk == 0 and nb <= BF16_SUBLANES and dh == LANES and N_HEADS % hp == 0
    n_hg = N_HEADS // hp
    return pl.pallas_call(
        functools.partial(_attn_kernel, nb=nb, blk=blk, dh=dh, topk=min(MOBA_TOPK, nb)),
        grid=(bsz, n_hg),
        in_specs=[pl.BlockSpec((seq, hp * dh), lambda b, h: (b, h)),
                  pl.BlockSpec((seq, hp * dh), lambda b, h: (b, n_hg + h)),
                  pl.BlockSpec((seq, hp * dh), lambda b, h: (b, 2 * n_hg + h))],
        out_specs=pl.BlockSpec((seq, hp * dh), lambda b, h: (b, h)),
        out_shape=jax.ShapeDtypeStruct((t, d), BF16),
        scratch_shapes=[pltpu.VMEM((hp, seq, dh + LANES), BF16),
                        pltpu.VMEM((hp, dh + BF16_SUBLANES, seq), BF16)],
        compiler_params=_cparams(("arbitrary", "arbitrary")),
        name="moba_attention",
    )(qkv, qkv, qkv)


def _pack_bf16_pairs(lo, hi):
    lo_bits = lax.bitcast_convert_type(lo.astype(BF16).astype(F32), jnp.uint32)
    hi_bits = lax.bitcast_convert_type(hi.astype(BF16).astype(F32), jnp.uint32)
    return (hi_bits & jnp.uint32(0xFFFF0000)) | (lo_bits >> 16)


def _unpack_bf16_pairs(u):
    lo = lax.bitcast_convert_type(u << 16, F32)
    hi = lax.bitcast_convert_type(u & jnp.uint32(0xFFFF0000), F32)
    return lo, hi


def _store_token_slabs(ref, packed, slab_rows, first_tile=0):
    m, width = packed.shape
    for j in range(width // LANES):
        ref[pl.ds(first_tile + j, m, stride=slab_rows), :] = packed[:, j * LANES:(j + 1) * LANES]


def _load_token_slabs(ref, m, slab_rows):
    return jnp.concatenate([ref[pl.ds(j, m, stride=slab_rows), :] for j in range(slab_rows)], axis=1)


def _route_rows(h, g, w_cat, b_cat, n_grp, epg):
    xn = _rmsnorm_rows(h, g)
    hd = xn.shape[1] // 2
    xn_packed = _pack_bf16_pairs(xn[:, :hd], xn[:, hd:])
    logits = jnp.dot(xn.astype(BF16), w_cat, preferred_element_type=F32) + b_cat
    lane = lax.broadcasted_iota(jnp.int32, logits.shape, 1)

    lg = jnp.where(lane < n_grp, logits, -jnp.inf)
    mg = jnp.max(lg, axis=1, keepdims=True)
    gidx = jnp.min(jnp.where(lg == mg, lane, LANES), axis=1, keepdims=True)
    pg_sel = 1.0 / jnp.sum(jnp.exp(lg - mg), axis=1, keepdims=True)

    lo = n_grp + gidx * epg
    in_grp = (lane >= lo) & (lane < lo + epg)
    le = jnp.where(in_grp, logits, -jnp.inf)
    e = jnp.exp(le - jnp.max(le, axis=1, keepdims=True))
    pe = jnp.where(in_grp, e / jnp.sum(e, axis=1, keepdims=True), -1.0)
    v1 = jnp.max(pe, axis=1, keepdims=True)
    i1 = jnp.min(jnp.where(pe == v1, lane, LANES), axis=1, keepdims=True)
    pe2 = jnp.where(lane == i1, -1.0, pe)
    v2 = jnp.max(pe2, axis=1, keepdims=True)
    i2 = jnp.min(jnp.where(pe2 == v2, lane, LANES), axis=1, keepdims=True)
    den = v1 + v2
    ids = jnp.where(lane == 0, i1 - n_grp, jnp.where(lane == 1, i2 - n_grp, 0)).astype(F32)
    et = ids.T[0:SUBLANES, :].astype(jnp.int32)
    rf = jnp.where(lane == 0, pg_sel * v1 / den, jnp.where(lane == 1, pg_sel * v2 / den, 0.0))
    return xn_packed, et, rf


def _mix_out_router_kernel(a_ref, w_ref, *rest, has_bias, n_grp, epg):
    if has_bias:
        b_ref, rest = rest[0], rest[1:]
    r_ref, g_ref, wc_ref, bc_ref, h_ref, xn_ref, et_ref, rf_ref, h_prev = rest

    @pl.when(pl.program_id(0) == 0)
    def _():
        h_prev[...] = jnp.zeros_like(h_prev)

    xn_packed, et, rf = _route_rows(h_prev[...], g_ref[...], wc_ref[...], bc_ref[...], n_grp, epg)
    _store_token_slabs(xn_ref, xn_packed, xn_packed.shape[1] // LANES)
    et_ref[...] = et
    rf_ref[...] = rf

    y = jnp.dot(a_ref[...], w_ref[...], preferred_element_type=F32)
    if has_bias:
        y = y + b_ref[...]
    h = r_ref[...] + y
    h_ref[...] = h
    h_prev[...] = h


def _mix_out_router(a, w, b, res, g, w_cat, b_cat, n_grp, epg):
    t, k = a.shape
    d = w.shape[1]
    tm = _tile(t, 256)
    n = t // tm
    cur = lambda i: (jnp.minimum(i, n - 1), 0)
    prev = lambda i: (jnp.maximum(i - 1, 0), 0)
    has_bias = b is not None
    in_specs = [pl.BlockSpec((tm, k), cur),
                pl.BlockSpec((k, d), lambda i: (0, 0))]
    args = [a, w]
    if has_bias:
        in_specs.append(pl.BlockSpec((1, d), lambda i: (0, 0)))
        args.append(b.reshape(1, d))
    in_specs += [pl.BlockSpec((tm, d), cur),
                 pl.BlockSpec((1, d), lambda i: (0, 0)),
                 pl.BlockSpec((d, LANES), lambda i: (0, 0)),
                 pl.BlockSpec((1, LANES), lambda i: (0, 0))]
    args += [res, g.reshape(1, d), w_cat, b_cat]
    return pl.pallas_call(
        functools.partial(_mix_out_router_kernel, has_bias=has_bias, n_grp=n_grp, epg=epg),
        grid=(n + 1,),
        in_specs=in_specs,
        out_specs=[pl.BlockSpec((tm, d), cur),
                   pl.BlockSpec((tm * (d // 2 // LANES), LANES), prev),
                   pl.BlockSpec((SUBLANES, tm), lambda i: (0, jnp.maximum(i - 1, 0))),
                   pl.BlockSpec((tm, LANES), prev)],
        out_shape=[jax.ShapeDtypeStruct((t, d), F32),
                   jax.ShapeDtypeStruct((t * (d // 2 // LANES), LANES), jnp.uint32),
                   jax.ShapeDtypeStruct((SUBLANES, t), jnp.int32),
                   jax.ShapeDtypeStruct((t, LANES), F32)],
        scratch_shapes=[pltpu.VMEM((tm, d), F32)],
        compiler_params=_cparams(("arbitrary",)),
        name="mix_out_router",
    )(*args)


DISPATCH_CHUNK = 512
DISPATCH_SMEM_CHUNK = 4096


def _dispatch_kernel(et_ref, tok_ref, dst_ref, blk_ref, rank_ref, dest_ref, dest_sm, rows_sm, rows_vm,
                     sem, *, n_exp, topk, rb, slab_rows):
    t = et_ref.shape[1]
    c = DISPATCH_CHUNK
    e_iota = lax.broadcasted_iota(jnp.int32, (n_exp, c), 0)
    earlier = (lax.broadcasted_iota(jnp.int32, (c, c), 0) < lax.broadcasted_iota(jnp.int32, (c, c), 1))
    earlier = jnp.where(earlier, 1.0, 0.0).astype(BF16)

    run = jnp.zeros((n_exp, 1), F32)
    for k in range(topk):
        def rank_chunk(j, run, k=k):
            off = pl.multiple_of(j * c, c)
            oh = e_iota == et_ref[k:k + 1, pl.ds(off, c)]
            ohf = jnp.where(oh, 1.0, 0.0)
            pre = jnp.dot(ohf.astype(BF16), earlier, preferred_element_type=F32)
            rank_ref[k:k + 1, pl.ds(off, c)] = jnp.sum(jnp.where(oh, pre + run, 0.0), axis=0, keepdims=True)
            return run + jnp.sum(ohf, axis=1, keepdims=True)
        run = lax.fori_loop(0, t // c, rank_chunk, run)

    cnt = run
    nblk = jnp.floor((cnt + (rb - 1)) * (1.0 / rb))
    before = (lax.broadcasted_iota(jnp.int32, (n_exp, n_exp), 1) < lax.broadcasted_iota(jnp.int32, (n_exp, n_exp), 0))
    before = jnp.where(before, 1.0, 0.0).astype(BF16)
    bstart = jnp.dot(before, jnp.broadcast_to(nblk, (n_exp, LANES)).astype(BF16),
                     preferred_element_type=F32)[:, 0:1]

    for k in range(topk):
        def dest_chunk(j, carry, k=k):
            off = pl.multiple_of(j * c, c)
            oh = e_iota == et_ref[k:k + 1, pl.ds(off, c)]
            base = jnp.sum(jnp.where(oh, bstart, 0.0), axis=0, keepdims=True) * rb
            dest_ref[k:k + 1, pl.ds(off, c)] = (base + rank_ref[k:k + 1, pl.ds(off, c)]).astype(jnp.int32)
            return carry
        lax.fori_loop(0, t // c, dest_chunk, 0)

    rows_vm[...] = jnp.full(rows_vm.shape, -1, jnp.int32)
    fill = pltpu.make_async_copy(rows_vm, rows_sm, sem)
    fill.start()
    fill.wait()
    sc = DISPATCH_SMEM_CHUNK
    for k in range(topk):
        def scatter_chunk(j, carry, k=k):
            off = pl.multiple_of(j * sc, sc)
            cp = pltpu.make_async_copy(dest_ref.at[pl.ds(k, 1), pl.ds(off, sc)], dest_sm, sem)
            cp.start()
            cp.wait()

            def one(a, carry2):
                d = dest_sm[0, a]
                rows_sm[0, d] = k * t + off + a
                return carry2
            return lax.fori_loop(0, sc, one, carry, unroll=8)
        lax.fori_loop(0, t // sc, scatter_chunk, 0)
    back = pltpu.make_async_copy(rows_sm, rows_vm, sem)
    back.start()
    back.wait()

    a = rows_vm[...]
    valid = a >= 0
    tok = a
    for k in range(1, topk):
        tok = jnp.where(a >= k * t, a - k * t, tok)
    flat = lax.broadcasted_iota(jnp.int32, a.shape, 1)
    scratch_dst = topk * t + ((flat // rb) % 2) * rb + flat % rb
    tok_ref[...] = jnp.where(valid, tok, 0) * slab_rows
    dst_ref[...] = jnp.where(valid, a, scratch_dst) * slab_rows

    nbp = blk_ref.shape[1]
    b_row = lax.broadcasted_iota(jnp.int32, (1, nbp), 1).astype(F32)
    bend = bstart + nblk
    blk_e = jnp.minimum(jnp.sum(jnp.where(bend <= b_row, 1.0, 0.0), axis=0, keepdims=True), n_exp - 1.0)
    mine = lax.broadcasted_iota(jnp.int32, (n_exp, nbp), 0).astype(F32) == blk_e
    cnt_b = jnp.sum(jnp.where(mine, cnt, 0.0), axis=0, keepdims=True)
    bstart_b = jnp.sum(jnp.where(mine, bstart, 0.0), axis=0, keepdims=True)
    blk_n = jnp.clip(cnt_b - (b_row - bstart_b) * rb, 0.0, float(rb))
    row = lax.broadcasted_iota(jnp.int32, blk_ref.shape, 0)
    blk_ref[...] = jnp.where(row == 0, blk_e, jnp.where(row == 1, blk_n, 0.0)).astype(jnp.int32)


def _dispatch(et, n_exp, rb, slab_rows):
    t = et.shape[1]
    topk = EXPERT_TOPK
    n_asg = topk * t
    assert n_asg % rb == 0 and t % DISPATCH_SMEM_CHUNK == 0 and (rb & (rb - 1)) == 0 and rb % LANES == 0
    n_rb = n_asg // rb + n_exp
    n_rows = n_rb * rb
    nbp = -(-n_rb // LANES) * LANES
    tok, dst, blk = pl.pallas_call(
        functools.partial(_dispatch_kernel, n_exp=n_exp, topk=topk, rb=rb, slab_rows=slab_rows),
        in_specs=[pl.BlockSpec(memory_space=pltpu.VMEM)],
        out_specs=[pl.BlockSpec(memory_space=pltpu.VMEM)] * 3,
        out_shape=[jax.ShapeDtypeStruct((1, n_rows), jnp.int32),
                   jax.ShapeDtypeStruct((1, n_rows), jnp.int32),
                   jax.ShapeDtypeStruct((SUBLANES, nbp), jnp.int32)],
        scratch_shapes=[pltpu.VMEM((SUBLANES, t), F32),
                        pltpu.VMEM((SUBLANES, t), jnp.int32),
                        pltpu.SMEM((1, DISPATCH_SMEM_CHUNK), jnp.int32),
                        pltpu.SMEM((1, n_rows), jnp.int32),
                        pltpu.VMEM((1, n_rows), jnp.int32),
                        pltpu.SemaphoreType.DMA(())],
        compiler_params=pltpu.CompilerParams(vmem_limit_bytes=VMEM_LIMIT_BYTES),
        name="moe_dispatch",
    )(et)
    return tok.reshape(n_rb, rb), dst.reshape(n_rb, rb), blk[0, :n_rb], blk[1, :n_rb]


WEIGHT_CHUNK_ROWS = 512
WEIGHT_RING = 4


def _expert_kernel(blk_e, blk_n, tok_all, dst_all, xn_hbm, wg_hbm, wu_hbm, wd_hbm,
                   y_hbm, xbuf, obuf, wbuf_g, wbuf_u, wbuf_d, stg, run_end, next_e, run_start, wslot_of,
                   succ_e, st, gsem, ssem, wsem, *, n_rb, rb, layer):
    i = pl.program_id(0)
    slot = lax.rem(i, 2)
    n = blk_n[i]
    d, f = wbuf_g.shape[1:]
    cr = stg.shape[1]
    n_gu = d // cr
    n_dc = d // f
    n_chunks = 2 * n_gu + (f // cr) * n_dc
    ring = stg.shape[0]
    G_DONE, P_EXP, P_CHUNK = 0, 1, 2

    chunks = ([(wg_hbm, wbuf_g, r * cr, 0) for r in range(n_gu)]
              + [(wu_hbm, wbuf_u, r * cr, 0) for r in range(n_gu)]
              + [(wd_hbm, wbuf_d, r * cr, h * f) for r in range(f // cr) for h in range(n_dc)])
    assert len(chunks) == n_chunks

    def chunk_copy(e, c, k):
        for cid, (w_hbm, _, r0, c0) in enumerate(chunks):
            @pl.when(c == cid)
            def _(w_hbm=w_hbm, r0=r0, c0=c0):
                pltpu.make_async_copy(w_hbm.at[layer, e, pl.ds(r0, cr), pl.ds(c0, f)], stg.at[k],
                                      wsem.at[k]).start()

    def start_next_chunk(k):
        e = st[P_EXP]
        c = st[P_CHUNK]

        @pl.when(e >= 0)
        def _():
            chunk_copy(e, c, k)
            last = c + 1 == n_chunks
            st[P_CHUNK] = jnp.where(last, 0, c + 1)
            st[P_EXP] = jnp.where(last, succ_e[e], e)

    def convert_chunk(c, ws):
        g = st[G_DONE]
        k = lax.rem(g, ring)
        pltpu.make_async_copy(wg_hbm.at[layer, 0, pl.ds(0, cr), :], stg.at[k], wsem.at[k]).wait()
        start_next_chunk(lax.rem(g + ring - 1, ring))
        for cid, (_, wbuf, r0, c0) in enumerate(chunks):
            @pl.when(c == cid)
            def _(wbuf=wbuf, r0=r0, c0=c0):
                wbuf[ws, r0:r0 + cr, c0:c0 + f] = stg[k].astype(BF16)
        st[G_DONE] = g + 1

    @pl.when(i == 0)
    def _():
        def back(ii, carry):
            end_nb, next_nb = carry
            b = n_rb - 1 - ii
            nb = jnp.minimum(b + 1, n_rb - 1)
            nb_active = (b + 1 < n_rb) & (blk_n[nb] > 0)
            same = nb_active & (blk_e[nb] == blk_e[b])
            end_b = jnp.where(same, end_nb, b + 1)
            next_b = jnp.where(same, next_nb, jnp.where(nb_active, blk_e[nb], -1))
            run_end[b] = end_b
            next_e[b] = next_b

            @pl.when((blk_n[b] > 0) & jnp.logical_not(same))
            def _():
                succ_e[blk_e[b]] = next_b
            return end_b, next_b
        lax.fori_loop(0, n_rb, back, (jnp.int32(n_rb), jnp.int32(-1)))

        def fwd(b, carry):
            start_pb, wslot_pb = carry
            pb = jnp.maximum(b - 1, 0)
            same = (b > 0) & (blk_e[pb] == blk_e[b])
            start_b = jnp.where(same, start_pb, b)
            wslot_b = jnp.where(b == 0, 0, jnp.where(same, wslot_pb, 1 - wslot_pb))
            run_start[b] = start_b
            wslot_of[b] = wslot_b
            return start_b, wslot_b
        lax.fori_loop(0, n_rb, fwd, (jnp.int32(0), jnp.int32(0)))

        st[G_DONE] = 0
        st[P_CHUNK] = 0
        st[P_EXP] = jnp.where(n > 0, blk_e[0], -1)

        @pl.when(n > 0)
        def _():
            for k in range(ring - 1):
                start_next_chunk(k)

            def first(c, carry):
                convert_chunk(c, 0)
                return carry
            lax.fori_loop(0, n_chunks, first, 0)

    ws = wslot_of[i]

    @pl.when((n > 0) & (next_e[i] >= 0))
    def _():
        run_len = run_end[i] - run_start[i]
        per = lax.div(n_chunks + run_len - 1, run_len)
        lo = (i - run_start[i]) * per
        hi = jnp.minimum(lo + per, n_chunks)

        def nxt(c, carry):
            convert_chunk(c, 1 - ws)
            return carry
        lax.fori_loop(lo, hi, nxt, 0)

    ns = xbuf.shape[1] // rb

    def gather_row(blk, sl, r):
        return pltpu.make_async_copy(xn_hbm.at[pl.ds(tok_all[blk * rb + r], ns)], xbuf.at[sl, pl.ds(r * ns, ns)],
                                     gsem.at[sl])

    def start_gather(blk, sl):
        def body(r, c):
            gather_row(blk, sl, r).start()
            return c
        lax.fori_loop(0, rb, body, 0, unroll=8)

    def scatter_row(sl, r, dst):
        return pltpu.make_async_copy(obuf.at[sl, pl.ds(r * ns, ns)], y_hbm.at[pl.ds(dst, ns)], ssem.at[sl])

    def wait_scatter(sl):
        pltpu.make_async_copy(obuf.at[sl], y_hbm.at[pl.ds(0, rb * ns)], ssem.at[sl]).wait()

    @pl.when(i == 0)
    def _():
        obuf[...] = jnp.zeros_like(obuf)
        n_real = y_hbm.shape[0] - 2 * rb * ns
        for sl in range(2):
            init = pltpu.make_async_copy(obuf.at[sl], y_hbm.at[pl.ds(n_real + sl * rb * ns, rb * ns)],
                                         ssem.at[sl])
            init.start()
            init.wait()

    n_prev = blk_n[jnp.maximum(i - 1, 0)]
    n_next = blk_n[jnp.minimum(i + 1, n_rb - 1)]
    do_gather = (i + 1 < n_rb) & (n_next > 0)
    do_scatter = (i >= 1) & (n_prev > 0)

    @pl.when((i == 0) & (n > 0))
    def _():
        start_gather(0, 0)

    @pl.when(n > 0)
    def _():
        pltpu.make_async_copy(xn_hbm.at[pl.ds(0, rb * ns)], xbuf.at[slot], gsem.at[slot]).wait()

    hd = ns * LANES
    fc, dc = min(f, MXU_COLS), min(hd, MXU_COLS)

    def free_obuf():
        @pl.when((i >= 2) & (blk_n[jnp.maximum(i - 2, 0)] > 0))
        def _():
            wait_scatter(slot)

    def compute(interleave_dma):
        def issue(piece, n_piece, start_row):
            if interleave_dma:
                per = -(-rb // n_piece)
                for r in range(piece * per, min(rb, (piece + 1) * per)):
                    start_row(r)

        x_lo, x_hi = _unpack_bf16_pairs(_load_token_slabs(xbuf.at[slot], rb, ns))
        x = jnp.concatenate([x_lo.astype(BF16), x_hi.astype(BF16)], axis=1)
        hid = []
        for c in range(f // fc):
            cols = slice(c * fc, (c + 1) * fc)
            start_gather_row = lambda r: gather_row(i + 1, 1 - slot, r).start(priority=r % 2)
            g = jnp.dot(x, wbuf_g[ws, :, cols], preferred_element_type=F32)
            issue(2 * c, 2 * (f // fc), start_gather_row)
            u = jnp.dot(x, wbuf_u[ws, :, cols], preferred_element_type=F32)
            issue(2 * c + 1, 2 * (f // fc), start_gather_row)
            hid.append(((g * jax.nn.sigmoid(g)) * u).astype(BF16))
        hid = jnp.concatenate(hid, axis=1)
        free_obuf()
        for c in range(hd // dc):
            cols = slice(c * dc, (c + 1) * dc)
            cols_hi = slice(hd + c * dc, hd + (c + 1) * dc)
            start_scatter_row = lambda r: scatter_row(1 - slot, r, dst_all[(i - 1) * rb + r]).start(priority=r % 2)
            y_lo = jnp.dot(hid, wbuf_d[ws, :, cols], preferred_element_type=F32)
            issue(2 * c, 2 * (hd // dc), start_scatter_row)
            y_hi = jnp.dot(hid, wbuf_d[ws, :, cols_hi], preferred_element_type=F32)
            issue(2 * c + 1, 2 * (hd // dc), start_scatter_row)
            _store_token_slabs(obuf.at[slot], _pack_bf16_pairs(y_lo, y_hi), ns, first_tile=c * dc // LANES)

    fast = (n > 0) & do_gather & do_scatter

    @pl.when(fast)
    def _():
        compute(True)

    @pl.when(jnp.logical_not(fast))
    def _():
        @pl.when(do_gather)
        def _():
            start_gather(i + 1, 1 - slot)

        @pl.when(do_scatter)
        def _():
            def body(r, c):
                scatter_row(1 - slot, r, dst_all[(i - 1) * rb + r]).start()
                return c
            lax.fori_loop(0, rb, body, 0, unroll=8)

        @pl.when(n > 0)
        def _():
            compute(False)

        @pl.when(n == 0)
        def _():
            free_obuf()

    @pl.when((i == n_rb - 1) & do_scatter)
    def _():
        wait_scatter(1 - slot)


def _experts(xn, row_tok, row_dst, blk_e, blk_n, w_gate, w_up, w_down, layer, n_slots):
    n_exp, d, f = w_gate.shape[1:]
    ns = d // 2 // LANES
    assert xn.shape[1] == LANES and d % (2 * LANES) == 0 and w_down.shape[1:] == (n_exp, f, d)
    n_rb, rb = row_tok.shape
    assert (n_slots - 2 * rb) % rb == 0
    cr = min(WEIGHT_CHUNK_ROWS, f)
    assert d % cr == 0 and f % cr == 0 and d % f == 0
    hbm = pl.BlockSpec(memory_space=pl.ANY)
    grid_spec = pltpu.PrefetchScalarGridSpec(
        num_scalar_prefetch=4,
        grid=(n_rb,),
        in_specs=[hbm, hbm, hbm, hbm],
        out_specs=hbm,
        scratch_shapes=[pltpu.VMEM((2, rb * ns, LANES), jnp.uint32),
                        pltpu.VMEM((2, rb * ns, LANES), jnp.uint32),
                        pltpu.VMEM((2, d, f), BF16),
                        pltpu.VMEM((2, d, f), BF16),
                        pltpu.VMEM((2, f, d), BF16),
                        pltpu.VMEM((WEIGHT_RING, cr, f), F32),
                        pltpu.SMEM((n_rb,), jnp.int32),
                        pltpu.SMEM((n_rb,), jnp.int32),
                        pltpu.SMEM((n_rb,), jnp.int32),
                        pltpu.SMEM((n_rb,), jnp.int32),
                        pltpu.SMEM((n_exp,), jnp.int32),
                        pltpu.SMEM((4,), jnp.int32),
                        pltpu.SemaphoreType.DMA((2,)),
                        pltpu.SemaphoreType.DMA((2,)),
                        pltpu.SemaphoreType.DMA((WEIGHT_RING,))],
    )
    return pl.pallas_call(
        functools.partial(_expert_kernel, n_rb=n_rb, rb=rb, layer=layer),
        grid_spec=grid_spec,
        out_shape=jax.ShapeDtypeStruct((n_slots * ns, LANES), jnp.uint32),
        compiler_params=_cparams(("arbitrary",)),
        name="moe_experts",
    )(blk_e, blk_n, row_tok.reshape(-1), row_dst.reshape(-1), xn, w_gate, w_up, w_down)


def _combine_kernel(h_ref, y0_ref, y1_ref, rf_ref, *rest, final):
    if final:
        g_ref, o_ref = rest
    else:
        (o_ref,) = rest
    w = rf_ref[...]
    m = h_ref.shape[0]
    ns = y0_ref.shape[0] // m
    y0_lo, y0_hi = _unpack_bf16_pairs(_load_token_slabs(y0_ref, m, ns))
    y1_lo, y1_hi = _unpack_bf16_pairs(_load_token_slabs(y1_ref, m, ns))
    y = jnp.concatenate([w[:, 0:1] * y0_lo + w[:, 1:2] * y1_lo, w[:, 0:1] * y0_hi + w[:, 1:2] * y1_hi], axis=1)
    out = h_ref[...] + y
    if final:
        out = _rmsnorm_rows(out, g_ref[...])
    o_ref[...] = out


def _combine(h, y, rf, final_g):
    t, d = h.shape
    tm = _tile(t, 512)
    final = final_g is not None
    in_specs = [pl.BlockSpec((tm, d), lambda i: (i, 0)),
                pl.BlockSpec((tm * (d // 2 // LANES), LANES), lambda i: (i, 0)),
                pl.BlockSpec((tm * (d // 2 // LANES), LANES), lambda i: (t // tm + i, 0)),
                pl.BlockSpec((tm, LANES), lambda i: (i, 0))]
    args = [h, y, y, rf]
    if final:
        in_specs.append(pl.BlockSpec((1, d), lambda i: (0, 0)))
        args.append(final_g.reshape(1, d))
    return pl.pallas_call(
        functools.partial(_combine_kernel, final=final),
        grid=(t // tm,),
        in_specs=in_specs,
        out_specs=pl.BlockSpec((tm, d), lambda i: (i, 0)),
        out_shape=jax.ShapeDtypeStruct((t, d), F32),
        compiler_params=_cparams(("parallel",)),
        name="moe_combine",
    )(*args)


def _mix_out_moe(act, w_out, b_out, h, norm_g, w_grp, b_grp, w_exp, b_exp, w_gate, w_up, w_down, layer, final_g):
    t, d = h.shape
    n_grp = w_grp.shape[1]
    n_exp = w_exp.shape[1]
    epg = n_exp // n_grp
    assert n_grp + n_exp <= LANES
    pad = LANES - n_grp - n_exp
    w_cat = jnp.concatenate([w_grp, w_exp, jnp.zeros((d, pad), F32)], axis=1).astype(BF16)
    b_cat = jnp.concatenate([b_grp, b_exp, jnp.zeros((pad,), F32)]).reshape(1, LANES)
    h, xn, et, rf = _mix_out_router(act, w_out, b_out, h, norm_g, w_cat, b_cat, n_grp, epg)
    row_tok, row_dst, blk_e, blk_n = _dispatch(et, n_exp, ROW_BLOCK, d // 2 // LANES)
    y = _experts(xn, row_tok, row_dst, blk_e, blk_n, w_gate, w_up, w_down, layer,
                 EXPERT_TOPK * t + 2 * ROW_BLOCK)
    return _combine(h, y, rf, final_g)


def _rope_tables(seq, dh):
    inv = 1.0 / (ROPE_THETA ** (jnp.arange(0, dh, 2, dtype=F32) / dh))
    ang = jnp.arange(seq, dtype=F32)[:, None] * inv[None, :]
    cos, sin = jnp.cos(ang), jnp.sin(ang)
    return jnp.concatenate([cos, cos], axis=1), jnp.concatenate([-sin, sin], axis=1)


def kernel(x, lru_norm, lru_w_in, lru_b_in, lru_conv_w, lru_conv_b, lru_w_r, lru_b_r, lru_w_i, lru_b_i, lru_lambda, lru_w_out, lru_b_out, att_norm, att_w_qkv, att_w_o, ffn_norm, moe_w_grp, moe_b_grp, moe_w_exp, moe_b_exp, moe_w_gate, moe_w_up, moe_w_down, final_norm):
    bsz, seq, d = x.shape
    depth = ffn_norm.shape[0]
    n_mixers = 2
    cc, ss = _rope_tables(seq, d // N_HEADS)
    h = x.reshape(bsz * seq, d)
    for layer in range(depth):
        j = layer // n_mixers
        if layer % n_mixers == 0:
            xb, gate = _lru_in(h, lru_norm[j], lru_w_in[j].astype(BF16), lru_b_in[j])
            act = _lru_scan(xb, gate, lru_conv_w[j], lru_conv_b[j], lru_w_r[j].astype(BF16), lru_b_r[j],
                            lru_w_i[j].astype(BF16), lru_b_i[j], lru_lambda[j], bsz, seq)
            w_out, b_out = lru_w_out[j].astype(BF16), lru_b_out[j]
        else:
            qkv = _qkv_rope(h, att_norm[j], att_w_qkv[j].astype(BF16), cc, ss, seq)
            act = _moba_attention(qkv, bsz, seq, d)
            w_out, b_out = att_w_o[j].astype(BF16), None
        final_g = final_norm if layer == depth - 1 else None
        h = _mix_out_moe(act, w_out, b_out, h, ffn_norm[layer], moe_w_grp[layer], moe_b_grp[layer],
                         moe_w_exp[layer], moe_b_exp[layer], moe_w_gate, moe_w_up, moe_w_down, layer, final_g)
    return h.reshape(bsz, seq, d)
```

```python
import functools

import jax
import jax.numpy as jnp
from jax import lax
from jax.experimental import pallas as pl
from jax.experimental.pallas import tpu as pltpu

F32 = jnp.float32
BF16 = jnp.bfloat16

NORM_EPS = 1e-6
NEG_INF = -1e30
LRU_C = 8.0
N_HEADS = 16
MOBA_BLOCK = 256
MOBA_TOPK = 3
ROPE_THETA = 10000.0
EXPERT_TOPK = 2
ROW_BLOCK = 256
LANES = 128
SUBLANES = 8
BF16_SUBLANES = 16
MXU_COLS = 256
VMEM_LIMIT_BYTES = 56 * 1024 * 1024


def _tile(n, pref):
    t = min(n, pref)
    while n % t:
        t //= 2
    return t


def _cparams(sem):
    return pltpu.CompilerParams(dimension_semantics=sem, vmem_limit_bytes=VMEM_LIMIT_BYTES)


def _sigmoid(z):
    return 0.5 * jnp.tanh(0.5 * z) + 0.5


def _rmsnorm_rows(x, g):
    ms = jnp.mean(x * x, axis=-1, keepdims=True)
    return x * lax.rsqrt(ms + NORM_EPS) * g


def _lru_in_kernel(x_ref, g_ref, wx_ref, wg_ref, bx_ref, bg_ref, ox_ref, og_ref, xn_ref):
    @pl.when(pl.program_id(1) == 0)
    def _():
        xn_ref[...] = _rmsnorm_rows(x_ref[...], g_ref[...]).astype(BF16)

    xn = xn_ref[...]
    ox_ref[...] = jnp.dot(xn, wx_ref[...], preferred_element_type=F32) + bx_ref[...]
    og_ref[...] = jax.nn.gelu(jnp.dot(xn, wg_ref[...], preferred_element_type=F32) + bg_ref[...],
                              approximate=True)


def _lru_in(h, g, w, b):
    t, d = h.shape
    wdt = w.shape[1] // 2
    tm, tn = _tile(t, 1024), _tile(wdt, 512)
    nj = wdt // tn
    b2 = b.reshape(1, 2 * wdt)
    return pl.pallas_call(
        _lru_in_kernel,
        grid=(t // tm, nj),
        in_specs=[pl.BlockSpec((tm, d), lambda i, j: (i, 0)),
                  pl.BlockSpec((1, d), lambda i, j: (0, 0)),
                  pl.BlockSpec((d, tn), lambda i, j: (0, j)),
                  pl.BlockSpec((d, tn), lambda i, j: (0, nj + j)),
                  pl.BlockSpec((1, tn), lambda i, j: (0, j)),
                  pl.BlockSpec((1, tn), lambda i, j: (0, nj + j))],
        out_specs=[pl.BlockSpec((tm, tn), lambda i, j: (i, j)),
                   pl.BlockSpec((tm, tn), lambda i, j: (i, j))],
        out_shape=[jax.ShapeDtypeStruct((t, wdt), F32),
                   jax.ShapeDtypeStruct((t, wdt), F32)],
        scratch_shapes=[pltpu.VMEM((tm, d), BF16)],
        compiler_params=_cparams(("parallel", "arbitrary")),
        name="lru_in",
    )(h, g.reshape(1, d), w, w, b2, b2)


def _qkv_kernel(x_ref, g_ref, w_ref, cc_ref, ss_ref, o_ref, xn_ref, *, n_rope, heads_per_tile, dh):
    j = pl.program_id(1)

    @pl.when(j == 0)
    def _():
        xn_ref[...] = _rmsnorm_rows(x_ref[...], g_ref[...]).astype(BF16)

    acc = jnp.dot(xn_ref[...], w_ref[...], preferred_element_type=F32)

    is_rope = j < n_rope
    cc = cc_ref[...]
    ss = ss_ref[...]
    for hh in range(heads_per_tile):
        a = acc[:, hh * dh:(hh + 1) * dh]
        rot = a * cc + pltpu.roll(a, dh // 2, 1) * ss
        o_ref[:, hh * dh:(hh + 1) * dh] = jnp.where(is_rope, rot, a).astype(o_ref.dtype)


def _qkv_rope(h, g, w, cc, ss, seq):
    t, d = h.shape
    n = w.shape[1]
    dh = d // N_HEADS
    tm, tn = _tile(seq, 1024), _tile(d, 1024)
    s_tiles = seq // tm
    return pl.pallas_call(
        functools.partial(_qkv_kernel, n_rope=(2 * d) // tn, heads_per_tile=tn // dh, dh=dh),
        grid=(t // tm, n // tn),
        in_specs=[pl.BlockSpec((tm, d), lambda i, j: (i, 0)),
                  pl.BlockSpec((1, d), lambda i, j: (0, 0)),
                  pl.BlockSpec((d, tn), lambda i, j: (0, j)),
                  pl.BlockSpec((tm, dh), lambda i, j: (i % s_tiles, 0)),
                  pl.BlockSpec((tm, dh), lambda i, j: (i % s_tiles, 0))],
        out_specs=pl.BlockSpec((tm, tn), lambda i, j: (i, j)),
        out_shape=jax.ShapeDtypeStruct((t, n), BF16),
        scratch_shapes=[pltpu.VMEM((tm, d), BF16)],
        compiler_params=_cparams(("parallel", "arbitrary")),
        name="qkv_rope",
    )(h, g.reshape(1, d), w, cc, ss)


def _lru_scan_kernel(xb_ref, gate_ref, cw_ref, cb_ref, wr_ref, br_ref, wi_ref, bi_ref, lam_ref,
                     o_ref, ext_ref, a_ref, b_ref, h_ref, *, ts, n_grp, gw, conv_w):
    s = pl.program_id(1)
    w = xb_ref.shape[1]

    @pl.when(s == 0)
    def _():
        ext_ref[0:SUBLANES, :] = jnp.zeros((SUBLANES, w), F32)
        h_ref[...] = jnp.zeros_like(h_ref)

    ext_ref[SUBLANES:SUBLANES + ts, :] = xb_ref[...]
    xc = cb_ref[...] + cw_ref[conv_w - 1:conv_w, :] * xb_ref[...]
    for k in range(1, conv_w):
        xc = xc + cw_ref[conv_w - 1 - k:conv_w - k, :] * ext_ref[SUBLANES - k:SUBLANES - k + ts, :]
    ext_ref[0:SUBLANES, :] = ext_ref[ts:ts + SUBLANES, :]

    z = -lam_ref[...]
    softplus = jnp.maximum(z, 0.0) + jnp.log1p(jnp.exp(-jnp.abs(z)))
    c = -LRU_C * softplus
    for g in range(n_grp):
        sl = slice(g * gw, (g + 1) * gw)
        xg = xc[:, sl]
        xg16 = xg.astype(BF16)
        r = _sigmoid(jnp.dot(xg16, wr_ref[g], preferred_element_type=F32) + br_ref[:, sl])
        ig = _sigmoid(jnp.dot(xg16, wi_ref[g], preferred_element_type=F32) + bi_ref[:, sl])
        log_a = c[:, sl] * r
        a_ref[:, sl] = jnp.exp(log_a)
        th = jnp.tanh(log_a)
        b_ref[:, sl] = jnp.sqrt(-2.0 * th / (1.0 - th)) * ig * xg

    rows = lax.broadcasted_iota(jnp.int32, (SUBLANES, w), 0)

    def chunk(ci, h):
        r0 = pl.multiple_of(ci * SUBLANES, SUBLANES)
        a = a_ref[pl.ds(r0, SUBLANES), :]
        b = b_ref[pl.ds(r0, SUBLANES), :]
        for sh in (1, 2, 4):
            keep = rows >= sh
            a_sh = jnp.where(keep, pltpu.roll(a, sh, 0), 1.0)
            b_sh = jnp.where(keep, pltpu.roll(b, sh, 0), 0.0)
            b = a * b_sh + b
            a = a * a_sh
        hc = a * h + b
        b_ref[pl.ds(r0, SUBLANES), :] = hc * gate_ref[pl.ds(r0, SUBLANES), :]
        return jnp.broadcast_to(hc[SUBLANES - 1:SUBLANES, :], (SUBLANES, w))

    h_ref[...] = lax.fori_loop(0, ts // SUBLANES, chunk, h_ref[...])
    o_ref[...] = b_ref[...].astype(o_ref.dtype)


def _lru_scan(xb, gate, conv_w, conv_b, w_r, b_r, w_i, b_i, lam, bsz, seq):
    t, w = xb.shape
    n_grp, gw = w_r.shape[0], w_r.shape[1]
    cw = conv_w.shape[0]
    ts = _tile(seq, 512)
    s_tiles = seq // ts
    row = lambda b, s: (b * s_tiles + s, 0)
    vec = pl.BlockSpec((1, w), lambda b, s: (0, 0))
    return pl.pallas_call(
        functools.partial(_lru_scan_kernel, ts=ts, n_grp=n_grp, gw=gw, conv_w=cw),
        grid=(bsz, s_tiles),
        in_specs=[pl.BlockSpec((ts, w), row),
                  pl.BlockSpec((ts, w), row),
                  pl.BlockSpec((cw, w), lambda b, s: (0, 0)),
                  vec,
                  pl.BlockSpec((n_grp, gw, gw), lambda b, s: (0, 0, 0)),
                  vec,
                  pl.BlockSpec((n_grp, gw, gw), lambda b, s: (0, 0, 0)),
                  vec,
                  vec],
        out_specs=pl.BlockSpec((ts, w), row),
        out_shape=jax.ShapeDtypeStruct((t, w), BF16),
        scratch_shapes=[pltpu.VMEM((ts + SUBLANES, w), F32),
                        pltpu.VMEM((ts, w), F32),
                        pltpu.VMEM((ts, w), F32),
                        pltpu.VMEM((SUBLANES, w), F32)],
        compiler_params=_cparams(("parallel", "arbitrary")),
        name="lru_scan",
    )(xb, gate, conv_w, conv_b.reshape(1, w), w_r, b_r.reshape(1, w), w_i, b_i.reshape(1, w),
      lam.reshape(1, w))


ATTN_HEADS_PER_STEP = 1


def _attn_kernel(q_ref, k_ref, v_ref, o_ref, kaug_ref, vaug_t_ref, *, nb, blk, dh, topk):
    seq = nb * blk
    hp = kaug_ref.shape[0]
    nbp = vaug_t_ref.shape[1] - dh
    first = (pl.program_id(0) == 0) & (pl.program_id(1) == 0)

    @pl.when(first)
    def _():
        row = lax.broadcasted_iota(jnp.int32, (seq, LANES), 0)
        lane = lax.broadcasted_iota(jnp.int32, (seq, LANES), 1)
        ones_row = (lax.broadcasted_iota(jnp.int32, (nbp, seq), 0) == 0).astype(BF16)
        for hd in range(hp):
            kaug_ref[hd, :, dh:dh + LANES] = (lane == row // blk).astype(BF16)
            vaug_t_ref[hd, dh:dh + nbp, :] = ones_row

    kmeans = []
    for hd in range(hp):
        cols = slice(hd * dh, (hd + 1) * dh)
        kaug_ref[hd, :, 0:dh] = k_ref[:, cols]
        for j in range(nb):
            vaug_t_ref[hd, 0:dh, j * blk:(j + 1) * blk] = v_ref[j * blk:(j + 1) * blk, cols].astype(F32).T.astype(BF16)
        kmean = jnp.sum(k_ref[:, cols].astype(F32).reshape(nb, blk, dh), axis=1) * (1.0 / blk)
        if nbp > nb:
            kmean = jnp.concatenate([kmean, jnp.zeros((nbp - nb, dh), F32)], axis=0)
        kmeans.append(kmean.astype(BF16))

    c = (dh ** -0.5) * 1.4426950408889634
    blk_id = lax.broadcasted_iota(jnp.int32, (nbp, blk), 0)
    key_pos = lax.broadcasted_iota(jnp.int32, (blk, blk), 0)
    q_pos = lax.broadcasted_iota(jnp.int32, (blk, blk), 1)
    zeros_pad = jnp.zeros((LANES - nbp, blk), BF16)

    def scores(hd, i):
        q_t = q_ref[i * blk:(i + 1) * blk, hd * dh:(hd + 1) * dh].astype(F32).T.astype(BF16)
        gate_t = jnp.dot(kmeans[hd], q_t, preferred_element_type=F32)
        g = jnp.where(blk_id < i, gate_t, NEG_INF)
        sel = blk_id == i
        for _ in range(topk):
            m = jnp.max(g, axis=0, keepdims=True)
            first_max = jnp.min(jnp.where(g == m, blk_id, nbp), axis=0, keepdims=True)
            hit = blk_id == first_max
            sel = sel | (hit & (blk_id < i))
            g = jnp.where(hit, -jnp.inf, g)
        bias_t = jnp.where(sel, 0.0, NEG_INF).astype(BF16)
        q_aug_t = jnp.concatenate([q_t, bias_t, zeros_pad], axis=0)
        return jnp.dot(kaug_ref[hd, 0:(i + 1) * blk, :], q_aug_t, preferred_element_type=F32)

    def probs(i, s):
        s_own = jnp.where(key_pos <= q_pos, s[i * blk:, :], NEG_INF)
        m = jnp.max(s_own, axis=0, keepdims=True)
        if i > 0:
            s_past = s[:i * blk, :]
            m = jnp.maximum(m, jnp.max(s_past, axis=0, keepdims=True))
            p = jnp.concatenate([jnp.exp2((s_past - m) * c), jnp.exp2((s_own - m) * c)], axis=0)
        else:
            p = jnp.exp2((s_own - m) * c)
        return p.astype(BF16)

    def pv(hd, i, p):
        out_t = jnp.dot(vaug_t_ref[hd, :, 0:(i + 1) * blk], p, preferred_element_type=F32)
        o_t = out_t[0:dh, :] / out_t[dh:dh + 1, :]
        o_ref[i * blk:(i + 1) * blk, hd * dh:(hd + 1) * dh] = o_t.T.astype(o_ref.dtype)

    order = list(range(nb - 1, -1, -1))
    s_of, p_of = {}, {}
    for step in range(nb + 2):
        for hd in range(hp):
            if step < nb:
                s_of[hd, order[step]] = scores(hd, order[step])
        for hd in range(hp):
            if 1 <= step <= nb:
                t1 = order[step - 1]
                p_of[hd, t1] = probs(t1, s_of.pop((hd, t1)))
        for hd in range(hp):
            if step >= 2:
                t2 = order[step - 2]
                pv(hd, t2, p_of.pop((hd, t2)))


def _moba_attention(qkv, bsz, seq, d):
    t = qkv.shape[0]
    dh = d // N_HEADS
    blk = MOBA_BLOCK
    nb = seq // blk
    hp = ATTN_HEADS_PER_STEP
    assert seq % blk == 0 and nb <= BF16_SUBLANES and dh == LANES and N_HEADS % hp == 0
    n_hg = N_HEADS // hp
    return pl.pallas_call(
        functools.partial(_attn_kernel, nb=nb, blk=blk, dh=dh, topk=min(MOBA_TOPK, nb)),
        grid=(bsz, n_hg),
        in_specs=[pl.BlockSpec((seq, hp * dh), lambda b, h: (b, h)),
                  pl.BlockSpec((seq, hp * dh), lambda b, h: (b, n_hg + h)),
                  pl.BlockSpec((seq, hp * dh), lambda b, h: (b, 2 * n_hg + h))],
        out_specs=pl.BlockSpec((seq, hp * dh), lambda b, h: (b, h)),
        out_shape=jax.ShapeDtypeStruct((t, d), BF16),
        scratch_shapes=[pltpu.VMEM((hp, seq, dh + LANES), BF16),
                        pltpu.VMEM((hp, dh + BF16_SUBLANES, seq), BF16)],
        compiler_params=_cparams(("arbitrary", "arbitrary")),
        name="moba_attention",
    )(qkv, qkv, qkv)


def _pack_bf16_pairs(lo, hi):
    lo_bits = lax.bitcast_convert_type(lo.astype(BF16).astype(F32), jnp.uint32)
    hi_bits = lax.bitcast_convert_type(hi.astype(BF16).astype(F32), jnp.uint32)
    return (hi_bits & jnp.uint32(0xFFFF0000)) | (lo_bits >> 16)


def _unpack_bf16_pairs(u):
    lo = lax.bitcast_convert_type(u << 16, F32)
    hi = lax.bitcast_convert_type(u & jnp.uint32(0xFFFF0000), F32)
    return lo, hi


def _store_token_slabs(ref, packed, slab_rows, first_tile=0):
    m, width = packed.shape
    for j in range(width // LANES):
        ref[pl.ds(first_tile + j, m, stride=slab_rows), :] = packed[:, j * LANES:(j + 1) * LANES]


def _load_token_slabs(ref, m, slab_rows):
    return jnp.concatenate([ref[pl.ds(j, m, stride=slab_rows), :] for j in range(slab_rows)], axis=1)


def _route_rows(h, g, w_cat, b_cat, n_grp, epg):
    xn = _rmsnorm_rows(h, g)
    hd = xn.shape[1] // 2
    xn_packed = _pack_bf16_pairs(xn[:, :hd], xn[:, hd:])
    logits = jnp.dot(xn.astype(BF16), w_cat, preferred_element_type=F32) + b_cat
    lane = lax.broadcasted_iota(jnp.int32, logits.shape, 1)

    lg = jnp.where(lane < n_grp, logits, -jnp.inf)
    mg = jnp.max(lg, axis=1, keepdims=True)
    gidx = jnp.min(jnp.where(lg == mg, lane, LANES), axis=1, keepdims=True)
    pg_sel = 1.0 / jnp.sum(jnp.exp(lg - mg), axis=1, keepdims=True)

    lo = n_grp + gidx * epg
    in_grp = (lane >= lo) & (lane < lo + epg)
    le = jnp.where(in_grp, logits, -jnp.inf)
    e = jnp.exp(le - jnp.max(le, axis=1, keepdims=True))
    pe = jnp.where(in_grp, e / jnp.sum(e, axis=1, keepdims=True), -1.0)
    v1 = jnp.max(pe, axis=1, keepdims=True)
    i1 = jnp.min(jnp.where(pe == v1, lane, LANES), axis=1, keepdims=True)
    pe2 = jnp.where(lane == i1, -1.0, pe)
    v2 = jnp.max(pe2, axis=1, keepdims=True)
    i2 = jnp.min(jnp.where(pe2 == v2, lane, LANES), axis=1, keepdims=True)
    den = v1 + v2
    ids = jnp.where(lane == 0, i1 - n_grp, jnp.where(lane == 1, i2 - n_grp, 0)).astype(F32)
    et = ids.T[0:SUBLANES, :].astype(jnp.int32)
    rf = jnp.where(lane == 0, pg_sel * v1 / den, jnp.where(lane == 1, pg_sel * v2 / den, 0.0))
    return xn_packed, et, rf


def _mix_out_router_kernel(a_ref, w_ref, *rest, has_bias, n_grp, epg):
    if has_bias:
        b_ref, rest = rest[0], rest[1:]
    r_ref, g_ref, wc_ref, bc_ref, h_ref, xn_ref, et_ref, rf_ref, h_prev = rest

    @pl.when(pl.program_id(0) == 0)
    def _():
        h_prev[...] = jnp.zeros_like(h_prev)

    xn_packed, et, rf = _route_rows(h_prev[...], g_ref[...], wc_ref[...], bc_ref[...], n_grp, epg)
    _store_token_slabs(xn_ref, xn_packed, xn_packed.shape[1] // LANES)
    et_ref[...] = et
    rf_ref[...] = rf

    y = jnp.dot(a_ref[...], w_ref[...], preferred_element_type=F32)
    if has_bias:
        y = y + b_ref[...]
    h = r_ref[...] + y
    h_ref[...] = h
    h_prev[...] = h


def _mix_out_router(a, w, b, res, g, w_cat, b_cat, n_grp, epg):
    t, k = a.shape
    d = w.shape[1]
    tm = _tile(t, 256)
    n = t // tm
    cur = lambda i: (jnp.minimum(i, n - 1), 0)
    prev = lambda i: (jnp.maximum(i - 1, 0), 0)
    has_bias = b is not None
    in_specs = [pl.BlockSpec((tm, k), cur),
                pl.BlockSpec((k, d), lambda i: (0, 0))]
    args = [a, w]
    if has_bias:
        in_specs.append(pl.BlockSpec((1, d), lambda i: (0, 0)))
        args.append(b.reshape(1, d))
    in_specs += [pl.BlockSpec((tm, d), cur),
                 pl.BlockSpec((1, d), lambda i: (0, 0)),
                 pl.BlockSpec((d, LANES), lambda i: (0, 0)),
                 pl.BlockSpec((1, LANES), lambda i: (0, 0))]
    args += [res, g.reshape(1, d), w_cat, b_cat]
    return pl.pallas_call(
        functools.partial(_mix_out_router_kernel, has_bias=has_bias, n_grp=n_grp, epg=epg),
        grid=(n + 1,),
        in_specs=in_specs,
        out_specs=[pl.BlockSpec((tm, d), cur),
                   pl.BlockSpec((tm * (d // 2 // LANES), LANES), prev),
                   pl.BlockSpec((SUBLANES, tm), lambda i: (0, jnp.maximum(i - 1, 0))),
                   pl.BlockSpec((tm, LANES), prev)],
        out_shape=[jax.ShapeDtypeStruct((t, d), F32),
                   jax.ShapeDtypeStruct((t * (d // 2 // LANES), LANES), jnp.uint32),
                   jax.ShapeDtypeStruct((SUBLANES, t), jnp.int32),
                   jax.ShapeDtypeStruct((t, LANES), F32)],
        scratch_shapes=[pltpu.VMEM((tm, d), F32)],
        compiler_params=_cparams(("arbitrary",)),
        name="mix_out_router",
    )(*args)


DISPATCH_CHUNK = 512
DISPATCH_SMEM_CHUNK = 16384


def _dispatch_kernel(et_ref, tok_ref, dst_ref, blk_ref, rank_ref, dest_ref, dest_sm, rows_sm, rows_vm,
                     sem, *, n_exp, topk, rb, slab_rows):
    t = et_ref.shape[1]
    c = DISPATCH_CHUNK
    e_iota = lax.broadcasted_iota(jnp.int32, (n_exp, c), 0)
    earlier = (lax.broadcasted_iota(jnp.int32, (c, c), 0) < lax.broadcasted_iota(jnp.int32, (c, c), 1))
    earlier = jnp.where(earlier, 1.0, 0.0).astype(BF16)

    run = jnp.zeros((n_exp, 1), F32)
    for k in range(topk):
        def rank_chunk(j, run, k=k):
            off = pl.multiple_of(j * c, c)
            oh = e_iota == et_ref[k:k + 1, pl.ds(off, c)]
            ohf = jnp.where(oh, 1.0, 0.0)
            pre = jnp.dot(ohf.astype(BF16), earlier, preferred_element_type=F32)
            rank_ref[k:k + 1, pl.ds(off, c)] = jnp.sum(jnp.where(oh, pre + run, 0.0), axis=0, keepdims=True)
            return run + jnp.sum(ohf, axis=1, keepdims=True)
        run = lax.fori_loop(0, t // c, rank_chunk, run)

    cnt = run
    nblk = jnp.floor((cnt + (rb - 1)) * (1.0 / rb))
    before = (lax.broadcasted_iota(jnp.int32, (n_exp, n_exp), 1) < lax.broadcasted_iota(jnp.int32, (n_exp, n_exp), 0))
    before = jnp.where(before, 1.0, 0.0).astype(BF16)
    bstart = jnp.dot(before, jnp.broadcast_to(nblk, (n_exp, LANES)).astype(BF16),
                     preferred_element_type=F32)[:, 0:1]

    for k in range(topk):
        def dest_chunk(j, carry, k=k):
            off = pl.multiple_of(j * c, c)
            oh = e_iota == et_ref[k:k + 1, pl.ds(off, c)]
            base = jnp.sum(jnp.where(oh, bstart, 0.0), axis=0, keepdims=True) * rb
            dest_ref[k:k + 1, pl.ds(off, c)] = (base + rank_ref[k:k + 1, pl.ds(off, c)]).astype(jnp.int32)
            return carry
        lax.fori_loop(0, t // c, dest_chunk, 0)

    rows_vm[...] = jnp.full(rows_vm.shape, -1, jnp.int32)
    fill = pltpu.make_async_copy(rows_vm, rows_sm, sem)
    fill.start()
    fill.wait()
    sc = dest_sm.shape[1]
    for k in range(topk):
        def scatter_chunk(j, carry, k=k):
            off = pl.multiple_of(j * sc, sc)
            cp = pltpu.make_async_copy(dest_ref.at[pl.ds(k, 1), pl.ds(off, sc)], dest_sm, sem)
            cp.start()
            cp.wait()

            def one(a, carry2):
                d = dest_sm[0, a]
                rows_sm[0, d] = k * t + off + a
                return carry2
            return lax.fori_loop(0, sc, one, carry, unroll=8)
        lax.fori_loop(0, t // sc, scatter_chunk, 0)
    back = pltpu.make_async_copy(rows_sm, rows_vm, sem)
    back.start()
    back.wait()

    a = rows_vm[...]
    valid = a >= 0
    tok = a
    for k in range(1, topk):
        tok = jnp.where(a >= k * t, a - k * t, tok)
    flat = lax.broadcasted_iota(jnp.int32, a.shape, 1)
    scratch_dst = topk * t + ((flat // rb) % 2) * rb + flat % rb
    tok_ref[...] = jnp.where(valid, tok, 0) * slab_rows
    dst_ref[...] = jnp.where(valid, a, scratch_dst) * slab_rows

    nbp = blk_ref.shape[1]
    b_row = lax.broadcasted_iota(jnp.int32, (1, nbp), 1).astype(F32)
    bend = bstart + nblk
    blk_e = jnp.minimum(jnp.sum(jnp.where(bend <= b_row, 1.0, 0.0), axis=0, keepdims=True), n_exp - 1.0)
    mine = lax.broadcasted_iota(jnp.int32, (n_exp, nbp), 0).astype(F32) == blk_e
    cnt_b = jnp.sum(jnp.where(mine, cnt, 0.0), axis=0, keepdims=True)
    bstart_b = jnp.sum(jnp.where(mine, bstart, 0.0), axis=0, keepdims=True)
    blk_n = jnp.clip(cnt_b - (b_row - bstart_b) * rb, 0.0, float(rb))
    row = lax.broadcasted_iota(jnp.int32, blk_ref.shape, 0)
    blk_ref[...] = jnp.where(row == 0, blk_e, jnp.where(row == 1, blk_n, 0.0)).astype(jnp.int32)


def _dispatch(et, n_exp, rb, slab_rows):
    t = et.shape[1]
    topk = EXPERT_TOPK
    n_asg = topk * t
    sc = min(DISPATCH_SMEM_CHUNK, t)
    assert n_asg % rb == 0 and t % sc == 0 and t % DISPATCH_CHUNK == 0 and (rb & (rb - 1)) == 0 and rb % LANES == 0
    n_rb = n_asg // rb + n_exp
    n_rows = n_rb * rb
    nbp = -(-n_rb // LANES) * LANES
    tok, dst, blk = pl.pallas_call(
        functools.partial(_dispatch_kernel, n_exp=n_exp, topk=topk, rb=rb, slab_rows=slab_rows),
        in_specs=[pl.BlockSpec(memory_space=pltpu.VMEM)],
        out_specs=[pl.BlockSpec(memory_space=pltpu.VMEM)] * 3,
        out_shape=[jax.ShapeDtypeStruct((1, n_rows), jnp.int32),
                   jax.ShapeDtypeStruct((1, n_rows), jnp.int32),
                   jax.ShapeDtypeStruct((SUBLANES, nbp), jnp.int32)],
        scratch_shapes=[pltpu.VMEM((SUBLANES, t), F32),
                        pltpu.VMEM((SUBLANES, t), jnp.int32),
                        pltpu.SMEM((1, sc), jnp.int32),
                        pltpu.SMEM((1, n_rows), jnp.int32),
                        pltpu.VMEM((1, n_rows), jnp.int32),
                        pltpu.SemaphoreType.DMA(())],
        compiler_params=pltpu.CompilerParams(vmem_limit_bytes=VMEM_LIMIT_BYTES),
        name="moe_dispatch",
    )(et)
    return tok.reshape(n_rb, rb), dst.reshape(n_rb, rb), blk[0, :n_rb], blk[1, :n_rb]


WEIGHT_CHUNK_ROWS = 512
WEIGHT_RING = 4


def _expert_kernel(blk_e, blk_n, tok_all, dst_all, xn_hbm, wg_hbm, wu_hbm, wd_hbm,
                   y_hbm, xbuf, obuf, wbuf_g, wbuf_u, wbuf_d, stg, run_end, next_e, run_start, wslot_of,
                   succ_e, st, gsem, ssem, wsem, *, n_rb, rb, layer):
    i = pl.program_id(0)
    slot = lax.rem(i, 2)
    n = blk_n[i]
    d, f = wbuf_g.shape[1:]
    cr = stg.shape[1]
    n_gu = d // cr
    n_dc = d // f
    n_chunks = 2 * n_gu + (f // cr) * n_dc
    ring = stg.shape[0]
    G_DONE, P_EXP, P_CHUNK = 0, 1, 2

    chunks = ([(wg_hbm, wbuf_g, r * cr, 0) for r in range(n_gu)]
              + [(wu_hbm, wbuf_u, r * cr, 0) for r in range(n_gu)]
              + [(wd_hbm, wbuf_d, r * cr, h * f) for r in range(f // cr) for h in range(n_dc)])
    assert len(chunks) == n_chunks

    def chunk_copy(e, c, k):
        for cid, (w_hbm, _, r0, c0) in enumerate(chunks):
            @pl.when(c == cid)
            def _(w_hbm=w_hbm, r0=r0, c0=c0):
                pltpu.make_async_copy(w_hbm.at[layer, e, pl.ds(r0, cr), pl.ds(c0, f)], stg.at[k],
                                      wsem.at[k]).start()

    def start_next_chunk(k):
        e = st[P_EXP]
        c = st[P_CHUNK]

        @pl.when(e >= 0)
        def _():
            chunk_copy(e, c, k)
            last = c + 1 == n_chunks
            st[P_CHUNK] = jnp.where(last, 0, c + 1)
            st[P_EXP] = jnp.where(last, succ_e[e], e)

    def convert_chunk(c, ws):
        g = st[G_DONE]
        k = lax.rem(g, ring)
        pltpu.make_async_copy(wg_hbm.at[layer, 0, pl.ds(0, cr), :], stg.at[k], wsem.at[k]).wait()
        start_next_chunk(lax.rem(g + ring - 1, ring))
        for cid, (_, wbuf, r0, c0) in enumerate(chunks):
            @pl.when(c == cid)
            def _(wbuf=wbuf, r0=r0, c0=c0):
                wbuf[ws, r0:r0 + cr, c0:c0 + f] = stg[k].astype(BF16)
        st[G_DONE] = g + 1

    @pl.when(i == 0)
    def _():
        def back(ii, carry):
            end_nb, next_nb = carry
            b = n_rb - 1 - ii
            nb = jnp.minimum(b + 1, n_rb - 1)
            nb_active = (b + 1 < n_rb) & (blk_n[nb] > 0)
            same = nb_active & (blk_e[nb] == blk_e[b])
            end_b = jnp.where(same, end_nb, b + 1)
            next_b = jnp.where(same, next_nb, jnp.where(nb_active, blk_e[nb], -1))
            run_end[b] = end_b
            next_e[b] = next_b

            @pl.when((blk_n[b] > 0) & jnp.logical_not(same))
            def _():
                succ_e[blk_e[b]] = next_b
            return end_b, next_b
        lax.fori_loop(0, n_rb, back, (jnp.int32(n_rb), jnp.int32(-1)))

        def fwd(b, carry):
            start_pb, wslot_pb = carry
            pb = jnp.maximum(b - 1, 0)
            same = (b > 0) & (blk_e[pb] == blk_e[b])
            start_b = jnp.where(same, start_pb, b)
            wslot_b = jnp.where(b == 0, 0, jnp.where(same, wslot_pb, 1 - wslot_pb))
            run_start[b] = start_b
            wslot_of[b] = wslot_b
            return start_b, wslot_b
        lax.fori_loop(0, n_rb, fwd, (jnp.int32(0), jnp.int32(0)))

        st[G_DONE] = 0
        st[P_CHUNK] = 0
        st[P_EXP] = jnp.where(n > 0, blk_e[0], -1)

        @pl.when(n > 0)
        def _():
            for k in range(ring - 1):
                start_next_chunk(k)

            def first(c, carry):
                convert_chunk(c, 0)
                return carry
            lax.fori_loop(0, n_chunks, first, 0)

    ws = wslot_of[i]

    @pl.when((n > 0) & (next_e[i] >= 0))
    def _():
        run_len = run_end[i] - run_start[i]
        per = lax.div(n_chunks + run_len - 1, run_len)
        lo = (i - run_start[i]) * per
        hi = jnp.minimum(lo + per, n_chunks)

        def nxt(c, carry):
            convert_chunk(c, 1 - ws)
            return carry
        lax.fori_loop(lo, hi, nxt, 0)

    ns = xbuf.shape[1] // rb

    def gather_row(blk, sl, r):
        return pltpu.make_async_copy(xn_hbm.at[pl.ds(tok_all[blk * rb + r], ns)], xbuf.at[sl, pl.ds(r * ns, ns)],
                                     gsem.at[sl])

    def start_gather(blk, sl):
        def body(r, c):
            gather_row(blk, sl, r).start()
            return c
        lax.fori_loop(0, rb, body, 0, unroll=8)

    def scatter_row(sl, r, dst):
        return pltpu.make_async_copy(obuf.at[sl, pl.ds(r * ns, ns)], y_hbm.at[pl.ds(dst, ns)], ssem.at[sl])

    def wait_scatter(sl):
        pltpu.make_async_copy(obuf.at[sl], y_hbm.at[pl.ds(0, rb * ns)], ssem.at[sl]).wait()

    @pl.when(i == 0)
    def _():
        obuf[...] = jnp.zeros_like(obuf)
        n_real = y_hbm.shape[0] - 2 * rb * ns
        for sl in range(2):
            init = pltpu.make_async_copy(obuf.at[sl], y_hbm.at[pl.ds(n_real + sl * rb * ns, rb * ns)],
                                         ssem.at[sl])
            init.start()
            init.wait()

    n_prev = blk_n[jnp.maximum(i - 1, 0)]
    n_next = blk_n[jnp.minimum(i + 1, n_rb - 1)]
    do_gather = (i + 1 < n_rb) & (n_next > 0)
    do_scatter = (i >= 1) & (n_prev > 0)

    @pl.when((i == 0) & (n > 0))
    def _():
        start_gather(0, 0)

    @pl.when(n > 0)
    def _():
        pltpu.make_async_copy(xn_hbm.at[pl.ds(0, rb * ns)], xbuf.at[slot], gsem.at[slot]).wait()

    hd = ns * LANES
    fc, dc = min(f, MXU_COLS), min(hd, MXU_COLS)

    def free_obuf():
        @pl.when((i >= 2) & (blk_n[jnp.maximum(i - 2, 0)] > 0))
        def _():
            wait_scatter(slot)

    def compute(interleave_dma):
        def issue(piece, n_piece, start_row):
            if interleave_dma:
                per = -(-rb // n_piece)
                for r in range(piece * per, min(rb, (piece + 1) * per)):
                    start_row(r)

        x_lo, x_hi = _unpack_bf16_pairs(_load_token_slabs(xbuf.at[slot], rb, ns))
        x = jnp.concatenate([x_lo.astype(BF16), x_hi.astype(BF16)], axis=1)
        hid = []
        for c in range(f // fc):
            cols = slice(c * fc, (c + 1) * fc)
            start_gather_row = lambda r: gather_row(i + 1, 1 - slot, r).start(priority=r % 2)
            g = jnp.dot(x, wbuf_g[ws, :, cols], preferred_element_type=F32)
            issue(2 * c, 2 * (f // fc), start_gather_row)
            u = jnp.dot(x, wbuf_u[ws, :, cols], preferred_element_type=F32)
            issue(2 * c + 1, 2 * (f // fc), start_gather_row)
            hid.append(((g * jax.nn.sigmoid(g)) * u).astype(BF16))
        hid = jnp.concatenate(hid, axis=1)
        free_obuf()
        for c in range(hd // dc):
            cols = slice(c * dc, (c + 1) * dc)
            cols_hi = slice(hd + c * dc, hd + (c + 1) * dc)
            start_scatter_row = lambda r: scatter_row(1 - slot, r, dst_all[(i - 1) * rb + r]).start(priority=r % 2)
            y_lo = jnp.dot(hid, wbuf_d[ws, :, cols], preferred_element_type=F32)
            issue(2 * c, 2 * (hd // dc), start_scatter_row)
            y_hi = jnp.dot(hid, wbuf_d[ws, :, cols_hi], preferred_element_type=F32)
            issue(2 * c + 1, 2 * (hd // dc), start_scatter_row)
            _store_token_slabs(obuf.at[slot], _pack_bf16_pairs(y_lo, y_hi), ns, first_tile=c * dc // LANES)

    fast = (n > 0) & do_gather & do_scatter

    @pl.when(fast)
    def _():
        compute(True)

    @pl.when(jnp.logical_not(fast))
    def _():
        @pl.when(do_gather)
        def _():
            start_gather(i + 1, 1 - slot)

        @pl.when(do_scatter)
        def _():
            def body(r, c):
                scatter_row(1 - slot, r, dst_all[(i - 1) * rb + r]).start()
                return c
            lax.fori_loop(0, rb, body, 0, unroll=8)

        @pl.when(n > 0)
        def _():
            compute(False)

        @pl.when(n == 0)
        def _():
            free_obuf()

    @pl.when((i == n_rb - 1) & do_scatter)
    def _():
        wait_scatter(1 - slot)


def _experts(xn, row_tok, row_dst, blk_e, blk_n, w_gate, w_up, w_down, layer, n_slots):
    n_exp, d, f = w_gate.shape[1:]
    ns = d // 2 // LANES
    assert xn.shape[1] == LANES and d % (2 * LANES) == 0 and w_down.shape[1:] == (n_exp, f, d)
    n_rb, rb = row_tok.shape
    assert (n_slots - 2 * rb) % rb == 0
    cr = min(WEIGHT_CHUNK_ROWS, f)
    assert d % cr == 0 and f % cr == 0 and d % f == 0
    hbm = pl.BlockSpec(memory_space=pl.ANY)
    grid_spec = pltpu.PrefetchScalarGridSpec(
        num_scalar_prefetch=4,
        grid=(n_rb,),
        in_specs=[hbm, hbm, hbm, hbm],
        out_specs=hbm,
        scratch_shapes=[pltpu.VMEM((2, rb * ns, LANES), jnp.uint32),
                        pltpu.VMEM((2, rb * ns, LANES), jnp.uint32),
                        pltpu.VMEM((2, d, f), BF16),
                        pltpu.VMEM((2, d, f), BF16),
                        pltpu.VMEM((2, f, d), BF16),
                        pltpu.VMEM((WEIGHT_RING, cr, f), F32),
                        pltpu.SMEM((n_rb,), jnp.int32),
                        pltpu.SMEM((n_rb,), jnp.int32),
                        pltpu.SMEM((n_rb,), jnp.int32),
                        pltpu.SMEM((n_rb,), jnp.int32),
                        pltpu.SMEM((n_exp,), jnp.int32),
                        pltpu.SMEM((4,), jnp.int32),
                        pltpu.SemaphoreType.DMA((2,)),
                        pltpu.SemaphoreType.DMA((2,)),
                        pltpu.SemaphoreType.DMA((WEIGHT_RING,))],
    )
    return pl.pallas_call(
        functools.partial(_expert_kernel, n_rb=n_rb, rb=rb, layer=layer),
        grid_spec=grid_spec,
        out_shape=jax.ShapeDtypeStruct((n_slots * ns, LANES), jnp.uint32),
        compiler_params=_cparams(("arbitrary",)),
        name="moe_experts",
    )(blk_e, blk_n, row_tok.reshape(-1), row_dst.reshape(-1), xn, w_gate, w_up, w_down)


def _combine_kernel(h_ref, y0_ref, y1_ref, rf_ref, *rest, final):
    if final:
        g_ref, o_ref = rest
    else:
        (o_ref,) = rest
    w = rf_ref[...]
    m = h_ref.shape[0]
    ns = y0_ref.shape[0] // m
    y0_lo, y0_hi = _unpack_bf16_pairs(_load_token_slabs(y0_ref, m, ns))
    y1_lo, y1_hi = _unpack_bf16_pairs(_load_token_slabs(y1_ref, m, ns))
    y = jnp.concatenate([w[:, 0:1] * y0_lo + w[:, 1:2] * y1_lo, w[:, 0:1] * y0_hi + w[:, 1:2] * y1_hi], axis=1)
    out = h_ref[...] + y
    if final:
        out = _rmsnorm_rows(out, g_ref[...])
    o_ref[...] = out


def _combine(h, y, rf, final_g):
    t, d = h.shape
    tm = _tile(t, 512)
    final = final_g is not None
    in_specs = [pl.BlockSpec((tm, d), lambda i: (i, 0)),
                pl.BlockSpec((tm * (d // 2 // LANES), LANES), lambda i: (i, 0)),
                pl.BlockSpec((tm * (d // 2 // LANES), LANES), lambda i: (t // tm + i, 0)),
                pl.BlockSpec((tm, LANES), lambda i: (i, 0))]
    args = [h, y, y, rf]
    if final:
        in_specs.append(pl.BlockSpec((1, d), lambda i: (0, 0)))
        args.append(final_g.reshape(1, d))
    return pl.pallas_call(
        functools.partial(_combine_kernel, final=final),
        grid=(t // tm,),
        in_specs=in_specs,
        out_specs=pl.BlockSpec((tm, d), lambda i: (i, 0)),
        out_shape=jax.ShapeDtypeStruct((t, d), F32),
        compiler_params=_cparams(("parallel",)),
        name="moe_combine",
    )(*args)


def _mix_out_moe(act, w_out, b_out, h, norm_g, w_grp, b_grp, w_exp, b_exp, w_gate, w_up, w_down, layer, final_g):
    t, d = h.shape
    n_grp = w_grp.shape[1]
    n_exp = w_exp.shape[1]
    epg = n_exp // n_grp
    assert n_grp + n_exp <= LANES
    pad = LANES - n_grp - n_exp
    w_cat = jnp.concatenate([w_grp, w_exp, jnp.zeros((d, pad), F32)], axis=1).astype(BF16)
    b_cat = jnp.concatenate([b_grp, b_exp, jnp.zeros((pad,), F32)]).reshape(1, LANES)
    h, xn, et, rf = _mix_out_router(act, w_out, b_out, h, norm_g, w_cat, b_cat, n_grp, epg)
    row_tok, row_dst, blk_e, blk_n = _dispatch(et, n_exp, ROW_BLOCK, d // 2 // LANES)
    y = _experts(xn, row_tok, row_dst, blk_e, blk_n, w_gate, w_up, w_down, layer,
                 EXPERT_TOPK * t + 2 * ROW_BLOCK)
    return _combine(h, y, rf, final_g)


def _rope_tables(seq, dh):
    inv = 1.0 / (ROPE_THETA ** (jnp.arange(0, dh, 2, dtype=F32) / dh))
    ang = jnp.arange(seq, dtype=F32)[:, None] * inv[None, :]
    cos, sin = jnp.cos(ang), jnp.sin(ang)
    return jnp.concatenate([cos, cos], axis=1), jnp.concatenate([-sin, sin], axis=1)


def kernel(x, lru_norm, lru_w_in, lru_b_in, lru_conv_w, lru_conv_b, lru_w_r, lru_b_r, lru_w_i, lru_b_i, lru_lambda, lru_w_out, lru_b_out, att_norm, att_w_qkv, att_w_o, ffn_norm, moe_w_grp, moe_b_grp, moe_w_exp, moe_b_exp, moe_w_gate, moe_w_up, moe_w_down, final_norm):
    bsz, seq, d = x.shape
    depth = ffn_norm.shape[0]
    n_mixers = 2
    cc, ss = _rope_tables(seq, d // N_HEADS)
    h = x.reshape(bsz * seq, d)
    for layer in range(depth):
        j = layer // n_mixers
        if layer % n_mixers == 0:
            xb, gate = _lru_in(h, lru_norm[j], lru_w_in[j].astype(BF16), lru_b_in[j])
            act = _lru_scan(xb, gate, lru_conv_w[j], lru_conv_b[j], lru_w_r[j].astype(BF16), lru_b_r[j],
                            lru_w_i[j].astype(BF16), lru_b_i[j], lru_lambda[j], bsz, seq)
            w_out, b_out = lru_w_out[j].astype(BF16), lru_b_out[j]
        else:
            qkv = _qkv_rope(h, att_norm[j], att_w_qkv[j].astype(BF16), cc, ss, seq)
            act = _moba_attention(qkv, bsz, seq, d)
            w_out, b_out = att_w_o[j].astype(BF16), None
        final_g = final_norm if layer == depth - 1 else None
        h = _mix_out_moe(act, w_out, b_out, h, ffn_norm[layer], moe_w_grp[layer], moe_b_grp[layer],
                         moe_w_exp[layer], moe_b_exp[layer], moe_w_gate, moe_w_up, moe_w_down, layer, final_g)
    return h.reshape(bsz, seq, d)
```

```python
import functools

import jax
import jax.numpy as jnp
from jax import lax
from jax.experimental import pallas as pl
from jax.experimental.pallas import tpu as pltpu

F32 = jnp.float32
BF16 = jnp.bfloat16

NORM_EPS = 1e-6
NEG_INF = -1e30
LRU_C = 8.0
N_HEADS = 16
MOBA_BLOCK = 256
MOBA_TOPK = 3
ROPE_THETA = 10000.0
EXPERT_TOPK = 2
ROW_BLOCK = 256
LANES = 128
SUBLANES = 8
BF16_SUBLANES = 16
MXU_COLS = 256
VMEM_LIMIT_BYTES = 56 * 1024 * 1024


def _tile(n, pref):
    t = min(n, pref)
    while n % t:
        t //= 2
    return t


def _cparams(sem):
    return pltpu.CompilerParams(dimension_semantics=sem, vmem_limit_bytes=VMEM_LIMIT_BYTES)


def _sigmoid(z):
    return 0.5 * jnp.tanh(0.5 * z) + 0.5


def _rmsnorm_rows(x, g):
    ms = jnp.mean(x * x, axis=-1, keepdims=True)
    return x * lax.rsqrt(ms + NORM_EPS) * g


def _lru_in_kernel(x_ref, g_ref, wx_ref, wg_ref, bx_ref, bg_ref, ox_ref, og_ref, xn_ref):
    @pl.when(pl.program_id(1) == 0)
    def _():
        xn_ref[...] = _rmsnorm_rows(x_ref[...], g_ref[...]).astype(BF16)

    xn = xn_ref[...]
    ox_ref[...] = jnp.dot(xn, wx_ref[...], preferred_element_type=F32) + bx_ref[...]
    og_ref[...] = jax.nn.gelu(jnp.dot(xn, wg_ref[...], preferred_element_type=F32) + bg_ref[...],
                              approximate=True)


def _lru_in(h, g, w, b):
    t, d = h.shape
    wdt = w.shape[1] // 2
    tm, tn = _tile(t, 1024), _tile(wdt, 512)
    nj = wdt // tn
    b2 = b.reshape(1, 2 * wdt)
    return pl.pallas_call(
        _lru_in_kernel,
        grid=(t // tm, nj),
        in_specs=[pl.BlockSpec((tm, d), lambda i, j: (i, 0)),
                  pl.BlockSpec((1, d), lambda i, j: (0, 0)),
                  pl.BlockSpec((d, tn), lambda i, j: (0, j)),
                  pl.BlockSpec((d, tn), lambda i, j: (0, nj + j)),
                  pl.BlockSpec((1, tn), lambda i, j: (0, j)),
                  pl.BlockSpec((1, tn), lambda i, j: (0, nj + j))],
        out_specs=[pl.BlockSpec((tm, tn), lambda i, j: (i, j)),
                   pl.BlockSpec((tm, tn), lambda i, j: (i, j))],
        out_shape=[jax.ShapeDtypeStruct((t, wdt), F32),
                   jax.ShapeDtypeStruct((t, wdt), F32)],
        scratch_shapes=[pltpu.VMEM((tm, d), BF16)],
        compiler_params=_cparams(("parallel", "arbitrary")),
        name="lru_in",
    )(h, g.reshape(1, d), w, w, b2, b2)


def _qkv_kernel(x_ref, g_ref, w_ref, cc_ref, ss_ref, o_ref, xn_ref, *, n_rope, heads_per_tile, dh):
    j = pl.program_id(1)

    @pl.when(j == 0)
    def _():
        xn_ref[...] = _rmsnorm_rows(x_ref[...], g_ref[...]).astype(BF16)

    acc = jnp.dot(xn_ref[...], w_ref[...], preferred_element_type=F32)

    is_rope = j < n_rope
    cc = cc_ref[...]
    ss = ss_ref[...]
    for hh in range(heads_per_tile):
        a = acc[:, hh * dh:(hh + 1) * dh]
        rot = a * cc + pltpu.roll(a, dh // 2, 1) * ss
        o_ref[:, hh * dh:(hh + 1) * dh] = jnp.where(is_rope, rot, a).astype(o_ref.dtype)


def _qkv_rope(h, g, w, cc, ss, seq):
    t, d = h.shape
    n = w.shape[1]
    dh = d // N_HEADS
    tm, tn = _tile(seq, 1024), _tile(d, 1024)
    s_tiles = seq // tm
    return pl.pallas_call(
        functools.partial(_qkv_kernel, n_rope=(2 * d) // tn, heads_per_tile=tn // dh, dh=dh),
        grid=(t // tm, n // tn),
        in_specs=[pl.BlockSpec((tm, d), lambda i, j: (i, 0)),
                  pl.BlockSpec((1, d), lambda i, j: (0, 0)),
                  pl.BlockSpec((d, tn), lambda i, j: (0, j)),
                  pl.BlockSpec((tm, dh), lambda i, j: (i % s_tiles, 0)),
                  pl.BlockSpec((tm, dh), lambda i, j: (i % s_tiles, 0))],
        out_specs=pl.BlockSpec((tm, tn), lambda i, j: (i, j)),
        out_shape=jax.ShapeDtypeStruct((t, n), BF16),
        scratch_shapes=[pltpu.VMEM((tm, d), BF16)],
        compiler_params=_cparams(("parallel", "arbitrary")),
        name="qkv_rope",
    )(h, g.reshape(1, d), w, cc, ss)


def _lru_scan_kernel(xb_ref, gate_ref, cw_ref, cb_ref, wr_ref, br_ref, wi_ref, bi_ref, lam_ref,
                     o_ref, ext_ref, a_ref, b_ref, h_ref, *, ts, n_grp, gw, conv_w):
    s = pl.program_id(1)
    w = xb_ref.shape[1]

    @pl.when(s == 0)
    def _():
        ext_ref[0:SUBLANES, :] = jnp.zeros((SUBLANES, w), F32)
        h_ref[...] = jnp.zeros_like(h_ref)

    ext_ref[SUBLANES:SUBLANES + ts, :] = xb_ref[...]
    xc = cb_ref[...] + cw_ref[conv_w - 1:conv_w, :] * xb_ref[...]
    for k in range(1, conv_w):
        xc = xc + cw_ref[conv_w - 1 - k:conv_w - k, :] * ext_ref[SUBLANES - k:SUBLANES - k + ts, :]
    ext_ref[0:SUBLANES, :] = ext_ref[ts:ts + SUBLANES, :]

    z = -lam_ref[...]
    softplus = jnp.maximum(z, 0.0) + jnp.log1p(jnp.exp(-jnp.abs(z)))
    c = -LRU_C * softplus
    for g in range(n_grp):
        sl = slice(g * gw, (g + 1) * gw)
        xg = xc[:, sl]
        xg16 = xg.astype(BF16)
        r = _sigmoid(jnp.dot(xg16, wr_ref[g], preferred_element_type=F32) + br_ref[:, sl])
        ig = _sigmoid(jnp.dot(xg16, wi_ref[g], preferred_element_type=F32) + bi_ref[:, sl])
        log_a = c[:, sl] * r
        a_ref[:, sl] = jnp.exp(log_a)
        th = jnp.tanh(log_a)
        b_ref[:, sl] = jnp.sqrt(-2.0 * th / (1.0 - th)) * ig * xg

    rows = lax.broadcasted_iota(jnp.int32, (SUBLANES, w), 0)

    def chunk(ci, h):
        r0 = pl.multiple_of(ci * SUBLANES, SUBLANES)
        a = a_ref[pl.ds(r0, SUBLANES), :]
        b = b_ref[pl.ds(r0, SUBLANES), :]
        for sh in (1, 2, 4):
            keep = rows >= sh
            a_sh = jnp.where(keep, pltpu.roll(a, sh, 0), 1.0)
            b_sh = jnp.where(keep, pltpu.roll(b, sh, 0), 0.0)
            b = a * b_sh + b
            a = a * a_sh
        hc = a * h + b
        b_ref[pl.ds(r0, SUBLANES), :] = hc * gate_ref[pl.ds(r0, SUBLANES), :]
        return jnp.broadcast_to(hc[SUBLANES - 1:SUBLANES, :], (SUBLANES, w))

    h_ref[...] = lax.fori_loop(0, ts // SUBLANES, chunk, h_ref[...])
    o_ref[...] = b_ref[...].astype(o_ref.dtype)


def _lru_scan(xb, gate, conv_w, conv_b, w_r, b_r, w_i, b_i, lam, bsz, seq):
    t, w = xb.shape
    n_grp, gw = w_r.shape[0], w_r.shape[1]
    cw = conv_w.shape[0]
    ts = _tile(seq, 512)
    s_tiles = seq // ts
    row = lambda b, s: (b * s_tiles + s, 0)
    vec = pl.BlockSpec((1, w), lambda b, s: (0, 0))
    return pl.pallas_call(
        functools.partial(_lru_scan_kernel, ts=ts, n_grp=n_grp, gw=gw, conv_w=cw),
        grid=(bsz, s_tiles),
        in_specs=[pl.BlockSpec((ts, w), row),
                  pl.BlockSpec((ts, w), row),
                  pl.BlockSpec((cw, w), lambda b, s: (0, 0)),
                  vec,
                  pl.BlockSpec((n_grp, gw, gw), lambda b, s: (0, 0, 0)),
                  vec,
                  pl.BlockSpec((n_grp, gw, gw), lambda b, s: (0, 0, 0)),
                  vec,
                  vec],
        out_specs=pl.BlockSpec((ts, w), row),
        out_shape=jax.ShapeDtypeStruct((t, w), BF16),
        scratch_shapes=[pltpu.VMEM((ts + SUBLANES, w), F32),
                        pltpu.VMEM((ts, w), F32),
                        pltpu.VMEM((ts, w), F32),
                        pltpu.VMEM((SUBLANES, w), F32)],
        compiler_params=_cparams(("parallel", "arbitrary")),
        name="lru_scan",
    )(xb, gate, conv_w, conv_b.reshape(1, w), w_r, b_r.reshape(1, w), w_i, b_i.reshape(1, w),
      lam.reshape(1, w))


ATTN_HEADS_PER_STEP = 1


def _attn_kernel(q_ref, k_ref, v_ref, o_ref, kaug_ref, vaug_t_ref, *, nb, blk, dh, topk):
    seq = nb * blk
    hp = kaug_ref.shape[0]
    nbp = vaug_t_ref.shape[1] - dh
    first = (pl.program_id(0) == 0) & (pl.program_id(1) == 0)

    @pl.when(first)
    def _():
        row = lax.broadcasted_iota(jnp.int32, (seq, LANES), 0)
        lane = lax.broadcasted_iota(jnp.int32, (seq, LANES), 1)
        ones_row = (lax.broadcasted_iota(jnp.int32, (nbp, seq), 0) == 0).astype(BF16)
        for hd in range(hp):
            kaug_ref[hd, :, dh:dh + LANES] = (lane == row // blk).astype(BF16)
            vaug_t_ref[hd, dh:dh + nbp, :] = ones_row

    kmeans = []
    for hd in range(hp):
        cols = slice(hd * dh, (hd + 1) * dh)
        kaug_ref[hd, :, 0:dh] = k_ref[:, cols]
        for j in range(nb):
            vaug_t_ref[hd, 0:dh, j * blk:(j + 1) * blk] = v_ref[j * blk:(j + 1) * blk, cols].astype(F32).T.astype(BF16)
        kmean = jnp.sum(k_ref[:, cols].astype(F32).reshape(nb, blk, dh), axis=1) * (1.0 / blk)
        if nbp > nb:
            kmean = jnp.concatenate([kmean, jnp.zeros((nbp - nb, dh), F32)], axis=0)
        kmeans.append(kmean.astype(BF16))

    c = (dh ** -0.5) * 1.4426950408889634
    blk_id = lax.broadcasted_iota(jnp.int32, (nbp, blk), 0)
    key_pos = lax.broadcasted_iota(jnp.int32, (blk, blk), 0)
    q_pos = lax.broadcasted_iota(jnp.int32, (blk, blk), 1)
    zeros_pad = jnp.zeros((LANES - nbp, blk), BF16)

    def scores(hd, i):
        q_t = q_ref[i * blk:(i + 1) * blk, hd * dh:(hd + 1) * dh].astype(F32).T.astype(BF16)
        gate_t = jnp.dot(kmeans[hd], q_t, preferred_element_type=F32)
        g = jnp.where(blk_id < i, gate_t, NEG_INF)
        sel = blk_id == i
        for _ in range(topk):
            m = jnp.max(g, axis=0, keepdims=True)
            first_max = jnp.min(jnp.where(g == m, blk_id, nbp), axis=0, keepdims=True)
            hit = blk_id == first_max
            sel = sel | (hit & (blk_id < i))
            g = jnp.where(hit, -jnp.inf, g)
        bias_t = jnp.where(sel, 0.0, NEG_INF).astype(BF16)
        q_aug_t = jnp.concatenate([q_t, bias_t, zeros_pad], axis=0)
        return jnp.dot(kaug_ref[hd, 0:(i + 1) * blk, :], q_aug_t, preferred_element_type=F32)

    def probs(i, s):
        s_own = jnp.where(key_pos <= q_pos, s[i * blk:, :], NEG_INF)
        m = jnp.max(s_own, axis=0, keepdims=True)
        if i > 0:
            s_past = s[:i * blk, :]
            m = jnp.maximum(m, jnp.max(s_past, axis=0, keepdims=True))
            p = jnp.concatenate([jnp.exp2((s_past - m) * c), jnp.exp2((s_own - m) * c)], axis=0)
        else:
            p = jnp.exp2((s_own - m) * c)
        return p.astype(BF16)

    def pv(hd, i, p):
        out_t = jnp.dot(vaug_t_ref[hd, :, 0:(i + 1) * blk], p, preferred_element_type=F32)
        o_t = out_t[0:dh, :] / out_t[dh:dh + 1, :]
        o_ref[i * blk:(i + 1) * blk, hd * dh:(hd + 1) * dh] = o_t.T.astype(o_ref.dtype)

    order = list(range(nb))
    s_of, p_of = {}, {}
    for step in range(nb + 2):
        for hd in range(hp):
            if step < nb:
                s_of[hd, order[step]] = scores(hd, order[step])
        for hd in range(hp):
            if 1 <= step <= nb:
                t1 = order[step - 1]
                p_of[hd, t1] = probs(t1, s_of.pop((hd, t1)))
        for hd in range(hp):
            if step >= 2:
                t2 = order[step - 2]
                pv(hd, t2, p_of.pop((hd, t2)))


def _moba_attention(qkv, bsz, seq, d):
    t = qkv.shape[0]
    dh = d // N_HEADS
    blk = MOBA_BLOCK
    nb = seq // blk
    hp = ATTN_HEADS_PER_STEP
    assert seq % blk == 0 and nb <= BF16_SUBLANES and dh == LANES and N_HEADS % hp == 0
    n_hg = N_HEADS // hp
    return pl.pallas_call(
        functools.partial(_attn_kernel, nb=nb, blk=blk, dh=dh, topk=min(MOBA_TOPK, nb)),
        grid=(bsz, n_hg),
        in_specs=[pl.BlockSpec((seq, hp * dh), lambda b, h: (b, h)),
                  pl.BlockSpec((seq, hp * dh), lambda b, h: (b, n_hg + h)),
                  pl.BlockSpec((seq, hp * dh), lambda b, h: (b, 2 * n_hg + h))],
        out_specs=pl.BlockSpec((seq, hp * dh), lambda b, h: (b, h)),
        out_shape=jax.ShapeDtypeStruct((t, d), BF16),
        scratch_shapes=[pltpu.VMEM((hp, seq, dh + LANES), BF16),
                        pltpu.VMEM((hp, dh + BF16_SUBLANES, seq), BF16)],
        compiler_params=_cparams(("arbitrary", "arbitrary")),
        name="moba_attention",
    )(qkv, qkv, qkv)


def _pack_bf16_pairs(lo, hi):
    lo_bits = lax.bitcast_convert_type(lo.astype(BF16).astype(F32), jnp.uint32)
    hi_bits = lax.bitcast_convert_type(hi.astype(BF16).astype(F32), jnp.uint32)
    return (hi_bits & jnp.uint32(0xFFFF0000)) | (lo_bits >> 16)


def _unpack_bf16_pairs(u):
    lo = lax.bitcast_convert_type(u << 16, F32)
    hi = lax.bitcast_convert_type(u & jnp.uint32(0xFFFF0000), F32)
    return lo, hi


def _store_token_slabs(ref, packed, slab_rows, first_tile=0):
    m, width = packed.shape
    for j in range(width // LANES):
        ref[pl.ds(first_tile + j, m, stride=slab_rows), :] = packed[:, j * LANES:(j + 1) * LANES]


def _load_token_slabs(ref, m, slab_rows):
    return jnp.concatenate([ref[pl.ds(j, m, stride=slab_rows), :] for j in range(slab_rows)], axis=1)


def _route_rows(h, g, w_cat, b_cat, n_grp, epg):
    xn = _rmsnorm_rows(h, g)
    hd = xn.shape[1] // 2
    xn_packed = _pack_bf16_pairs(xn[:, :hd], xn[:, hd:])
    logits = jnp.dot(xn.astype(BF16), w_cat, preferred_element_type=F32) + b_cat
    lane = lax.broadcasted_iota(jnp.int32, logits.shape, 1)

    lg = jnp.where(lane < n_grp, logits, -jnp.inf)
    mg = jnp.max(lg, axis=1, keepdims=True)
    gidx = jnp.min(jnp.where(lg == mg, lane, LANES), axis=1, keepdims=True)
    pg_sel = 1.0 / jnp.sum(jnp.exp(lg - mg), axis=1, keepdims=True)

    lo = n_grp + gidx * epg
    in_grp = (lane >= lo) & (lane < lo + epg)
    le = jnp.where(in_grp, logits, -jnp.inf)
    e = jnp.exp(le - jnp.max(le, axis=1, keepdims=True))
    pe = jnp.where(in_grp, e / jnp.sum(e, axis=1, keepdims=True), -1.0)
    v1 = jnp.max(pe, axis=1, keepdims=True)
    i1 = jnp.min(jnp.where(pe == v1, lane, LANES), axis=1, keepdims=True)
    pe2 = jnp.where(lane == i1, -1.0, pe)
    v2 = jnp.max(pe2, axis=1, keepdims=True)
    i2 = jnp.min(jnp.where(pe2 == v2, lane, LANES), axis=1, keepdims=True)
    den = v1 + v2
    ids = jnp.where(lane == 0, i1 - n_grp, jnp.where(lane == 1, i2 - n_grp, 0)).astype(F32)
    et = ids.T[0:SUBLANES, :].astype(jnp.int32)
    rf = jnp.where(lane == 0, pg_sel * v1 / den, jnp.where(lane == 1, pg_sel * v2 / den, 0.0))
    return xn_packed, et, rf


def _mix_out_router_kernel(a_ref, w_ref, *rest, has_bias, n_grp, epg):
    if has_bias:
        b_ref, rest = rest[0], rest[1:]
    r_ref, g_ref, wc_ref, bc_ref, h_ref, xn_ref, et_ref, rf_ref, h_prev = rest

    @pl.when(pl.program_id(0) == 0)
    def _():
        h_prev[...] = jnp.zeros_like(h_prev)

    xn_packed, et, rf = _route_rows(h_prev[...], g_ref[...], wc_ref[...], bc_ref[...], n_grp, epg)
    _store_token_slabs(xn_ref, xn_packed, xn_packed.shape[1] // LANES)
    et_ref[...] = et
    rf_ref[...] = rf

    y = jnp.dot(a_ref[...], w_ref[...], preferred_element_type=F32)
    if has_bias:
        y = y + b_ref[...]
    h = r_ref[...] + y
    h_ref[...] = h
    h_prev[...] = h


def _mix_out_router(a, w, b, res, g, w_cat, b_cat, n_grp, epg):
    t, k = a.shape
    d = w.shape[1]
    tm = _tile(t, 256)
    n = t // tm
    cur = lambda i: (jnp.minimum(i, n - 1), 0)
    prev = lambda i: (jnp.maximum(i - 1, 0), 0)
    has_bias = b is not None
    in_specs = [pl.BlockSpec((tm, k), cur),
                pl.BlockSpec((k, d), lambda i: (0, 0))]
    args = [a, w]
    if has_bias:
        in_specs.append(pl.BlockSpec((1, d), lambda i: (0, 0)))
        args.append(b.reshape(1, d))
    in_specs += [pl.BlockSpec((tm, d), cur),
                 pl.BlockSpec((1, d), lambda i: (0, 0)),
                 pl.BlockSpec((d, LANES), lambda i: (0, 0)),
                 pl.BlockSpec((1, LANES), lambda i: (0, 0))]
    args += [res, g.reshape(1, d), w_cat, b_cat]
    return pl.pallas_call(
        functools.partial(_mix_out_router_kernel, has_bias=has_bias, n_grp=n_grp, epg=epg),
        grid=(n + 1,),
        in_specs=in_specs,
        out_specs=[pl.BlockSpec((tm, d), cur),
                   pl.BlockSpec((tm * (d // 2 // LANES), LANES), prev),
                   pl.BlockSpec((SUBLANES, tm), lambda i: (0, jnp.maximum(i - 1, 0))),
                   pl.BlockSpec((tm, LANES), prev)],
        out_shape=[jax.ShapeDtypeStruct((t, d), F32),
                   jax.ShapeDtypeStruct((t * (d // 2 // LANES), LANES), jnp.uint32),
                   jax.ShapeDtypeStruct((SUBLANES, t), jnp.int32),
                   jax.ShapeDtypeStruct((t, LANES), F32)],
        scratch_shapes=[pltpu.VMEM((tm, d), F32)],
        compiler_params=_cparams(("arbitrary",)),
        name="mix_out_router",
    )(*args)


DISPATCH_CHUNK = 512
DISPATCH_SMEM_CHUNK = 16384


def _dispatch_kernel(et_ref, tok_ref, dst_ref, blk_ref, rank_ref, dest_ref, dest_sm, rows_sm, rows_vm,
                     sem, *, n_exp, topk, rb, slab_rows):
    t = et_ref.shape[1]
    c = DISPATCH_CHUNK
    e_iota = lax.broadcasted_iota(jnp.int32, (n_exp, c), 0)
    earlier = (lax.broadcasted_iota(jnp.int32, (c, c), 0) < lax.broadcasted_iota(jnp.int32, (c, c), 1))
    earlier = jnp.where(earlier, 1.0, 0.0).astype(BF16)

    run = jnp.zeros((n_exp, 1), F32)
    for k in range(topk):
        def rank_chunk(j, run, k=k):
            off = pl.multiple_of(j * c, c)
            oh = e_iota == et_ref[k:k + 1, pl.ds(off, c)]
            ohf = jnp.where(oh, 1.0, 0.0)
            pre = jnp.dot(ohf.astype(BF16), earlier, preferred_element_type=F32)
            rank_ref[k:k + 1, pl.ds(off, c)] = jnp.sum(jnp.where(oh, pre + run, 0.0), axis=0, keepdims=True)
            return run + jnp.sum(ohf, axis=1, keepdims=True)
        run = lax.fori_loop(0, t // c, rank_chunk, run)

    cnt = run
    nblk = jnp.floor((cnt + (rb - 1)) * (1.0 / rb))
    before = (lax.broadcasted_iota(jnp.int32, (n_exp, n_exp), 1) < lax.broadcasted_iota(jnp.int32, (n_exp, n_exp), 0))
    before = jnp.where(before, 1.0, 0.0).astype(BF16)
    bstart = jnp.dot(before, jnp.broadcast_to(nblk, (n_exp, LANES)).astype(BF16),
                     preferred_element_type=F32)[:, 0:1]

    for k in range(topk):
        def dest_chunk(j, carry, k=k):
            off = pl.multiple_of(j * c, c)
            oh = e_iota == et_ref[k:k + 1, pl.ds(off, c)]
            base = jnp.sum(jnp.where(oh, bstart, 0.0), axis=0, keepdims=True) * rb
            dest_ref[k:k + 1, pl.ds(off, c)] = (base + rank_ref[k:k + 1, pl.ds(off, c)]).astype(jnp.int32)
            return carry
        lax.fori_loop(0, t // c, dest_chunk, 0)

    rows_vm[...] = jnp.full(rows_vm.shape, -1, jnp.int32)
    fill = pltpu.make_async_copy(rows_vm, rows_sm, sem)
    fill.start()
    fill.wait()
    sc = dest_sm.shape[1]
    for k in range(topk):
        def scatter_chunk(j, carry, k=k):
            off = pl.multiple_of(j * sc, sc)
            cp = pltpu.make_async_copy(dest_ref.at[pl.ds(k, 1), pl.ds(off, sc)], dest_sm, sem)
            cp.start()
            cp.wait()

            def one(a, carry2):
                d = dest_sm[0, a]
                rows_sm[0, d] = k * t + off + a
                return carry2
            return lax.fori_loop(0, sc, one, carry, unroll=8)
        lax.fori_loop(0, t // sc, scatter_chunk, 0)
    back = pltpu.make_async_copy(rows_sm, rows_vm, sem)
    back.start()
    back.wait()

    a = rows_vm[...]
    valid = a >= 0
    tok = a
    for k in range(1, topk):
        tok = jnp.where(a >= k * t, a - k * t, tok)
    flat = lax.broadcasted_iota(jnp.int32, a.shape, 1)
    scratch_dst = topk * t + ((flat // rb) % 2) * rb + flat % rb
    tok_ref[...] = jnp.where(valid, tok, 0) * slab_rows
    dst_ref[...] = jnp.where(valid, a, scratch_dst) * slab_rows

    nbp = blk_ref.shape[1]
    b_row = lax.broadcasted_iota(jnp.int32, (1, nbp), 1).astype(F32)
    bend = bstart + nblk
    blk_e = jnp.minimum(jnp.sum(jnp.where(bend <= b_row, 1.0, 0.0), axis=0, keepdims=True), n_exp - 1.0)
    mine = lax.broadcasted_iota(jnp.int32, (n_exp, nbp), 0).astype(F32) == blk_e
    cnt_b = jnp.sum(jnp.where(mine, cnt, 0.0), axis=0, keepdims=True)
    bstart_b = jnp.sum(jnp.where(mine, bstart, 0.0), axis=0, keepdims=True)
    blk_n = jnp.clip(cnt_b - (b_row - bstart_b) * rb, 0.0, float(rb))
    row = lax.broadcasted_iota(jnp.int32, blk_ref.shape, 0)
    blk_ref[...] = jnp.where(row == 0, blk_e, jnp.where(row == 1, blk_n, 0.0)).astype(jnp.int32)


def _dispatch(et, n_exp, rb, slab_rows):
    t = et.shape[1]
    topk = EXPERT_TOPK
    n_asg = topk * t
    sc = min(DISPATCH_SMEM_CHUNK, t)
    assert n_asg % rb == 0 and t % sc == 0 and t % DISPATCH_CHUNK == 0 and (rb & (rb - 1)) == 0 and rb % LANES == 0
    n_rb = n_asg // rb + n_exp
    n_rows = n_rb * rb
    nbp = -(-n_rb // LANES) * LANES
    tok, dst, blk = pl.pallas_call(
        functools.partial(_dispatch_kernel, n_exp=n_exp, topk=topk, rb=rb, slab_rows=slab_rows),
        in_specs=[pl.BlockSpec(memory_space=pltpu.VMEM)],
        out_specs=[pl.BlockSpec(memory_space=pltpu.VMEM)] * 3,
        out_shape=[jax.ShapeDtypeStruct((1, n_rows), jnp.int32),
                   jax.ShapeDtypeStruct((1, n_rows), jnp.int32),
                   jax.ShapeDtypeStruct((SUBLANES, nbp), jnp.int32)],
        scratch_shapes=[pltpu.VMEM((SUBLANES, t), F32),
                        pltpu.VMEM((SUBLANES, t), jnp.int32),
                        pltpu.SMEM((1, sc), jnp.int32),
                        pltpu.SMEM((1, n_rows), jnp.int32),
                        pltpu.VMEM((1, n_rows), jnp.int32),
                        pltpu.SemaphoreType.DMA(())],
        compiler_params=pltpu.CompilerParams(vmem_limit_bytes=VMEM_LIMIT_BYTES),
        name="moe_dispatch",
    )(et)
    return tok.reshape(n_rb, rb), dst.reshape(n_rb, rb), blk[0, :n_rb], blk[1, :n_rb]


WEIGHT_CHUNK_ROWS = 512
WEIGHT_RING = 4


def _expert_kernel(blk_e, blk_n, tok_all, dst_all, xn_hbm, wg_hbm, wu_hbm, wd_hbm,
                   y_hbm, xbuf, obuf, wbuf_g, wbuf_u, wbuf_d, stg, run_end, next_e, run_start, wslot_of,
                   succ_e, st, gsem, ssem, wsem, *, n_rb, rb, layer):
    i = pl.program_id(0)
    slot = lax.rem(i, 2)
    n = blk_n[i]
    d, f = wbuf_g.shape[1:]
    cr = stg.shape[1]
    n_gu = d // cr
    n_dc = d // f
    n_chunks = 2 * n_gu + (f // cr) * n_dc
    ring = stg.shape[0]
    G_DONE, P_EXP, P_CHUNK = 0, 1, 2

    chunks = ([(wg_hbm, wbuf_g, r * cr, 0) for r in range(n_gu)]
              + [(wu_hbm, wbuf_u, r * cr, 0) for r in range(n_gu)]
              + [(wd_hbm, wbuf_d, r * cr, h * f) for r in range(f // cr) for h in range(n_dc)])
    assert len(chunks) == n_chunks

    def chunk_copy(e, c, k):
        for cid, (w_hbm, _, r0, c0) in enumerate(chunks):
            @pl.when(c == cid)
            def _(w_hbm=w_hbm, r0=r0, c0=c0):
                pltpu.make_async_copy(w_hbm.at[layer, e, pl.ds(r0, cr), pl.ds(c0, f)], stg.at[k],
                                      wsem.at[k]).start()

    def start_next_chunk(k):
        e = st[P_EXP]
        c = st[P_CHUNK]

        @pl.when(e >= 0)
        def _():
            chunk_copy(e, c, k)
            last = c + 1 == n_chunks
            st[P_CHUNK] = jnp.where(last, 0, c + 1)
            st[P_EXP] = jnp.where(last, succ_e[e], e)

    def convert_chunk(c, ws):
        g = st[G_DONE]
        k = lax.rem(g, ring)
        pltpu.make_async_copy(wg_hbm.at[layer, 0, pl.ds(0, cr), :], stg.at[k], wsem.at[k]).wait()
        start_next_chunk(lax.rem(g + ring - 1, ring))
        for cid, (_, wbuf, r0, c0) in enumerate(chunks):
            @pl.when(c == cid)
            def _(wbuf=wbuf, r0=r0, c0=c0):
                wbuf[ws, r0:r0 + cr, c0:c0 + f] = stg[k].astype(BF16)
        st[G_DONE] = g + 1

    @pl.when(i == 0)
    def _():
        def back(ii, carry):
            end_nb, next_nb = carry
            b = n_rb - 1 - ii
            nb = jnp.minimum(b + 1, n_rb - 1)
            nb_active = (b + 1 < n_rb) & (blk_n[nb] > 0)
            same = nb_active & (blk_e[nb] == blk_e[b])
            end_b = jnp.where(same, end_nb, b + 1)
            next_b = jnp.where(same, next_nb, jnp.where(nb_active, blk_e[nb], -1))
            run_end[b] = end_b
            next_e[b] = next_b

            @pl.when((blk_n[b] > 0) & jnp.logical_not(same))
            def _():
                succ_e[blk_e[b]] = next_b
            return end_b, next_b
        lax.fori_loop(0, n_rb, back, (jnp.int32(n_rb), jnp.int32(-1)))

        def fwd(b, carry):
            start_pb, wslot_pb = carry
            pb = jnp.maximum(b - 1, 0)
            same = (b > 0) & (blk_e[pb] == blk_e[b])
            start_b = jnp.where(same, start_pb, b)
            wslot_b = jnp.where(b == 0, 0, jnp.where(same, wslot_pb, 1 - wslot_pb))
            run_start[b] = start_b
            wslot_of[b] = wslot_b
            return start_b, wslot_b
        lax.fori_loop(0, n_rb, fwd, (jnp.int32(0), jnp.int32(0)))

        st[G_DONE] = 0
        st[P_CHUNK] = 0
        st[P_EXP] = jnp.where(n > 0, blk_e[0], -1)

        @pl.when(n > 0)
        def _():
            for k in range(ring - 1):
                start_next_chunk(k)

            def first(c, carry):
                convert_chunk(c, 0)
                return carry
            lax.fori_loop(0, n_chunks, first, 0)

    ws = wslot_of[i]

    @pl.when((n > 0) & (next_e[i] >= 0))
    def _():
        run_len = run_end[i] - run_start[i]
        per = lax.div(n_chunks + run_len - 1, run_len)
        lo = (i - run_start[i]) * per
        hi = jnp.minimum(lo + per, n_chunks)

        def nxt(c, carry):
            convert_chunk(c, 1 - ws)
            return carry
        lax.fori_loop(lo, hi, nxt, 0)

    ns = xbuf.shape[1] // rb

    def gather_row(blk, sl, r):
        return pltpu.make_async_copy(xn_hbm.at[pl.ds(tok_all[blk * rb + r], ns)], xbuf.at[sl, pl.ds(r * ns, ns)],
                                     gsem.at[sl])

    def start_gather(blk, sl):
        def body(r, c):
            gather_row(blk, sl, r).start()
            return c
        lax.fori_loop(0, rb, body, 0, unroll=8)

    def scatter_row(sl, r, dst):
        return pltpu.make_async_copy(obuf.at[sl, pl.ds(r * ns, ns)], y_hbm.at[pl.ds(dst, ns)], ssem.at[sl])

    def wait_scatter(sl):
        pltpu.make_async_copy(obuf.at[sl], y_hbm.at[pl.ds(0, rb * ns)], ssem.at[sl]).wait()

    @pl.when(i == 0)
    def _():
        obuf[...] = jnp.zeros_like(obuf)
        n_real = y_hbm.shape[0] - 2 * rb * ns
        for sl in range(2):
            init = pltpu.make_async_copy(obuf.at[sl], y_hbm.at[pl.ds(n_real + sl * rb * ns, rb * ns)],
                                         ssem.at[sl])
            init.start()
            init.wait()

    n_prev = blk_n[jnp.maximum(i - 1, 0)]
    n_next = blk_n[jnp.minimum(i + 1, n_rb - 1)]
    do_gather = (i + 1 < n_rb) & (n_next > 0)
    do_scatter = (i >= 1) & (n_prev > 0)

    @pl.when((i == 0) & (n > 0))
    def _():
        start_gather(0, 0)

    @pl.when(n > 0)
    def _():
        pltpu.make_async_copy(xn_hbm.at[pl.ds(0, rb * ns)], xbuf.at[slot], gsem.at[slot]).wait()

    hd = ns * LANES
    fc, dc = min(f, MXU_COLS), min(hd, MXU_COLS)

    def free_obuf():
        @pl.when((i >= 2) & (blk_n[jnp.maximum(i - 2, 0)] > 0))
        def _():
            wait_scatter(slot)

    def compute(interleave_dma):
        def issue(piece, n_piece, start_row):
            if interleave_dma:
                per = -(-rb // n_piece)
                for r in range(piece * per, min(rb, (piece + 1) * per)):
                    start_row(r)

        x_lo, x_hi = _unpack_bf16_pairs(_load_token_slabs(xbuf.at[slot], rb, ns))
        x = jnp.concatenate([x_lo.astype(BF16), x_hi.astype(BF16)], axis=1)
        hid = []
        for c in range(f // fc):
            cols = slice(c * fc, (c + 1) * fc)
            start_gather_row = lambda r: gather_row(i + 1, 1 - slot, r).start(priority=r % 2)
            g = jnp.dot(x, wbuf_g[ws, :, cols], preferred_element_type=F32)
            issue(2 * c, 2 * (f // fc), start_gather_row)
            u = jnp.dot(x, wbuf_u[ws, :, cols], preferred_element_type=F32)
            issue(2 * c + 1, 2 * (f // fc), start_gather_row)
            hid.append(((g * jax.nn.sigmoid(g)) * u).astype(BF16))
        hid = jnp.concatenate(hid, axis=1)
        free_obuf()
        for c in range(hd // dc):
            cols = slice(c * dc, (c + 1) * dc)
            cols_hi = slice(hd + c * dc, hd + (c + 1) * dc)
            start_scatter_row = lambda r: scatter_row(1 - slot, r, dst_all[(i - 1) * rb + r]).start(priority=r % 2)
            y_lo = jnp.dot(hid, wbuf_d[ws, :, cols], preferred_element_type=F32)
            issue(2 * c, 2 * (hd // dc), start_scatter_row)
            y_hi = jnp.dot(hid, wbuf_d[ws, :, cols_hi], preferred_element_type=F32)
            issue(2 * c + 1, 2 * (hd // dc), start_scatter_row)
            _store_token_slabs(obuf.at[slot], _pack_bf16_pairs(y_lo, y_hi), ns, first_tile=c * dc // LANES)

    fast = (n > 0) & do_gather & do_scatter

    @pl.when(fast)
    def _():
        compute(True)

    @pl.when(jnp.logical_not(fast))
    def _():
        @pl.when(do_gather)
        def _():
            start_gather(i + 1, 1 - slot)

        @pl.when(do_scatter)
        def _():
            def body(r, c):
                scatter_row(1 - slot, r, dst_all[(i - 1) * rb + r]).start()
                return c
            lax.fori_loop(0, rb, body, 0, unroll=8)

        @pl.when(n > 0)
        def _():
            compute(False)

        @pl.when(n == 0)
        def _():
            free_obuf()

    @pl.when((i == n_rb - 1) & do_scatter)
    def _():
        wait_scatter(1 - slot)


def _experts(xn, row_tok, row_dst, blk_e, blk_n, w_gate, w_up, w_down, layer, n_slots):
    n_exp, d, f = w_gate.shape[1:]
    ns = d // 2 // LANES
    assert xn.shape[1] == LANES and d % (2 * LANES) == 0 and w_down.shape[1:] == (n_exp, f, d)
    n_rb, rb = row_tok.shape
    assert (n_slots - 2 * rb) % rb == 0
    cr = min(WEIGHT_CHUNK_ROWS, f)
    assert d % cr == 0 and f % cr == 0 and d % f == 0
    hbm = pl.BlockSpec(memory_space=pl.ANY)
    grid_spec = pltpu.PrefetchScalarGridSpec(
        num_scalar_prefetch=4,
        grid=(n_rb,),
        in_specs=[hbm, hbm, hbm, hbm],
        out_specs=hbm,
        scratch_shapes=[pltpu.VMEM((2, rb * ns, LANES), jnp.uint32),
                        pltpu.VMEM((2, rb * ns, LANES), jnp.uint32),
                        pltpu.VMEM((2, d, f), BF16),
                        pltpu.VMEM((2, d, f), BF16),
                        pltpu.VMEM((2, f, d), BF16),
                        pltpu.VMEM((WEIGHT_RING, cr, f), F32),
                        pltpu.SMEM((n_rb,), jnp.int32),
                        pltpu.SMEM((n_rb,), jnp.int32),
                        pltpu.SMEM((n_rb,), jnp.int32),
                        pltpu.SMEM((n_rb,), jnp.int32),
                        pltpu.SMEM((n_exp,), jnp.int32),
                        pltpu.SMEM((4,), jnp.int32),
                        pltpu.SemaphoreType.DMA((2,)),
                        pltpu.SemaphoreType.DMA((2,)),
                        pltpu.SemaphoreType.DMA((WEIGHT_RING,))],
    )
    return pl.pallas_call(
        functools.partial(_expert_kernel, n_rb=n_rb, rb=rb, layer=layer),
        grid_spec=grid_spec,
        out_shape=jax.ShapeDtypeStruct((n_slots * ns, LANES), jnp.uint32),
        compiler_params=_cparams(("arbitrary",)),
        name="moe_experts",
    )(blk_e, blk_n, row_tok.reshape(-1), row_dst.reshape(-1), xn, w_gate, w_up, w_down)


def _combine_kernel(h_ref, y0_ref, y1_ref, rf_ref, *rest, final):
    if final:
        g_ref, o_ref = rest
    else:
        (o_ref,) = rest
    w = rf_ref[...]
    m = h_ref.shape[0]
    ns = y0_ref.shape[0] // m
    y0_lo, y0_hi = _unpack_bf16_pairs(_load_token_slabs(y0_ref, m, ns))
    y1_lo, y1_hi = _unpack_bf16_pairs(_load_token_slabs(y1_ref, m, ns))
    y = jnp.concatenate([w[:, 0:1] * y0_lo + w[:, 1:2] * y1_lo, w[:, 0:1] * y0_hi + w[:, 1:2] * y1_hi], axis=1)
    out = h_ref[...] + y
    if final:
        out = _rmsnorm_rows(out, g_ref[...])
    o_ref[...] = out


def _combine(h, y, rf, final_g):
    t, d = h.shape
    tm = _tile(t, 512)
    final = final_g is not None
    in_specs = [pl.BlockSpec((tm, d), lambda i: (i, 0)),
                pl.BlockSpec((tm * (d // 2 // LANES), LANES), lambda i: (i, 0)),
                pl.BlockSpec((tm * (d // 2 // LANES), LANES), lambda i: (t // tm + i, 0)),
                pl.BlockSpec((tm, LANES), lambda i: (i, 0))]
    args = [h, y, y, rf]
    if final:
        in_specs.append(pl.BlockSpec((1, d), lambda i: (0, 0)))
        args.append(final_g.reshape(1, d))
    return pl.pallas_call(
        functools.partial(_combine_kernel, final=final),
        grid=(t // tm,),
        in_specs=in_specs,
        out_specs=pl.BlockSpec((tm, d), lambda i: (i, 0)),
        out_shape=jax.ShapeDtypeStruct((t, d), F32),
        compiler_params=_cparams(("parallel",)),
        name="moe_combine",
    )(*args)


def _mix_out_moe(act, w_out, b_out, h, norm_g, w_grp, b_grp, w_exp, b_exp, w_gate, w_up, w_down, layer, final_g):
    t, d = h.shape
    n_grp = w_grp.shape[1]
    n_exp = w_exp.shape[1]
    epg = n_exp // n_grp
    assert n_grp + n_exp <= LANES
    pad = LANES - n_grp - n_exp
    w_cat = jnp.concatenate([w_grp, w_exp, jnp.zeros((d, pad), F32)], axis=1).astype(BF16)
    b_cat = jnp.concatenate([b_grp, b_exp, jnp.zeros((pad,), F32)]).reshape(1, LANES)
    h, xn, et, rf = _mix_out_router(act, w_out, b_out, h, norm_g, w_cat, b_cat, n_grp, epg)
    row_tok, row_dst, blk_e, blk_n = _dispatch(et, n_exp, ROW_BLOCK, d // 2 // LANES)
    y = _experts(xn, row_tok, row_dst, blk_e, blk_n, w_gate, w_up, w_down, layer,
                 EXPERT_TOPK * t + 2 * ROW_BLOCK)
    return _combine(h, y, rf, final_g)


def _rope_tables(seq, dh):
    inv = 1.0 / (ROPE_THETA ** (jnp.arange(0, dh, 2, dtype=F32) / dh))
    ang = jnp.arange(seq, dtype=F32)[:, None] * inv[None, :]
    cos, sin = jnp.cos(ang), jnp.sin(ang)
    return jnp.concatenate([cos, cos], axis=1), jnp.concatenate([-sin, sin], axis=1)


def kernel(x, lru_norm, lru_w_in, lru_b_in, lru_conv_w, lru_conv_b, lru_w_r, lru_b_r, lru_w_i, lru_b_i, lru_lambda, lru_w_out, lru_b_out, att_norm, att_w_qkv, att_w_o, ffn_norm, moe_w_grp, moe_b_grp, moe_w_exp, moe_b_exp, moe_w_gate, moe_w_up, moe_w_down, final_norm):
    bsz, seq, d = x.shape
    depth = ffn_norm.shape[0]
    n_mixers = 2
    cc, ss = _rope_tables(seq, d // N_HEADS)
    h = x.reshape(bsz * seq, d)
    for layer in range(depth):
        j = layer // n_mixers
        if layer % n_mixers == 0:
            xb, gate = _lru_in(h, lru_norm[j], lru_w_in[j].astype(BF16), lru_b_in[j])
            act = _lru_scan(xb, gate, lru_conv_w[j], lru_conv_b[j], lru_w_r[j].astype(BF16), lru_b_r[j],
                            lru_w_i[j].astype(BF16), lru_b_i[j], lru_lambda[j], bsz, seq)
            w_out, b_out = lru_w_out[j].astype(BF16), lru_b_out[j]
        else:
            qkv = _qkv_rope(h, att_norm[j], att_w_qkv[j].astype(BF16), cc, ss, seq)
            act = _moba_attention(qkv, bsz, seq, d)
            w_out, b_out = att_w_o[j].astype(BF16), None
        final_g = final_norm if layer == depth - 1 else None
        h = _mix_out_moe(act, w_out, b_out, h, ffn_norm[layer], moe_w_grp[layer], moe_b_grp[layer],
                         moe_w_exp[layer], moe_b_exp[layer], moe_w_gate, moe_w_up, moe_w_down, layer, final_g)
    return h.reshape(bsz, seq, d)
```
